```python
import jax
import jax.numpy as jnp
from jax import lax
import numpy as np

D_MODEL = 1024
BATCH = 32
SEQ = 2048
DEPTH = 2

N_MIXERS = 2
MIX_WIDTH = D_MODEL
X_WIDTH = D_MODEL // 4
N_X_HEADS = 4
X_HEAD_DIM = X_WIDTH // N_X_HEADS
SEQ_MIX_WIDTH = MIX_WIDTH - X_WIDTH
LIN_HEAD_DIM = 128
N_LIN_HEADS = SEQ_MIX_WIDTH // LIN_HEAD_DIM
CONV_WIDTH = 4
CHUNK = 64
SB_HEAD_DIM = 64
N_SB_HEADS = SEQ_MIX_WIDTH // SB_HEAD_DIM
SB_BLOCK = 128
N_MEM = 256
D_FF = 4 * D_MODEL
EPS = 1e-6
N_A_LAYERS = (DEPTH + 1) // 2
N_B_LAYERS = DEPTH // 2
IN_A = 4 * SEQ_MIX_WIDTH + 2 * N_LIN_HEADS + X_WIDTH
IN_B = 3 * SEQ_MIX_WIDTH + X_WIDTH

kernel_name = 'hybrid_gdn_stickbreak_memory_trunk'


def rms_norm(x, g):
    xf = x.astype(jnp.float32)
    y = xf * lax.rsqrt(jnp.mean(xf * xf, axis=-1, keepdims=True) + EPS)
    return (y * g.astype(jnp.float32)).astype(x.dtype)


def l2norm(x):
    return x * lax.rsqrt(jnp.sum(x * x, axis=-1, keepdims=True) + EPS)


def causal_conv(x, w):
    c = x.shape[-1]
    return lax.conv_general_dilated(
        x, w[:, None, :].astype(x.dtype), window_strides=(1,),
        padding=[(CONV_WIDTH - 1, 0)], dimension_numbers=('NWC', 'WIO', 'NWC'),
        feature_group_count=c)


def gated_deltanet(p, conv_w, a_log, dt_bias, o_gain):
    B, S, _ = p.shape
    H, Dh, C, W = N_LIN_HEADS, LIN_HEAD_DIM, CHUNK, SEQ_MIX_WIDTH
    nc = S // C
    qkv = jax.nn.silu(causal_conv(p[..., :3 * W], conv_w)).astype(jnp.float32)
    gate = p[..., 3 * W:4 * W].astype(jnp.float32)
    beta = jax.nn.sigmoid(p[..., 4 * W:4 * W + H].astype(jnp.float32))
    g = -jnp.exp(a_log.astype(jnp.float32)) * jax.nn.softplus(
        p[..., 4 * W + H:].astype(jnp.float32) + dt_bias.astype(jnp.float32))
    q = l2norm(qkv[..., :W].reshape(B, S, H, Dh)) * (Dh ** -0.5)
    k = l2norm(qkv[..., W:2 * W].reshape(B, S, H, Dh))
    v = qkv[..., 2 * W:].reshape(B, S, H, Dh)

    def chunked(t):
        t = t.reshape((B, nc, C) + t.shape[2:])
        return jnp.moveaxis(jnp.moveaxis(t, 1, 0), 3, 2)

    q, k, v, beta = chunked(q), chunked(k), chunked(v), chunked(beta)
    gc = jnp.cumsum(chunked(g), axis=-1)
    idx = jnp.arange(C)
    causal = idx[:, None] >= idx[None, :]
    strict = idx[:, None] > idx[None, :]
    decay = jnp.exp(jnp.where(causal, gc[..., :, None] - gc[..., None, :], -jnp.inf))
    kb = k * beta[..., None]
    lower = jnp.where(strict, jnp.einsum('nbhcd,nbhsd->nbhcs', kb, k) * decay, 0.0)
    eye = jnp.eye(C, dtype=jnp.float32)
    rhs = jnp.concatenate([v * beta[..., None], kb * jnp.exp(gc)[..., None]], axis=-1)
    sol = lax.linalg.triangular_solve(eye + lower, rhs, left_side=True, lower=True,
                                      unit_diagonal=True)
    u, w = sol[..., :Dh], sol[..., Dh:]
    intra = jnp.einsum('nbhcd,nbhsd->nbhcs', q, k) * decay
    q_dec = q * jnp.exp(gc)[..., None]
    k_dec = k * jnp.exp(gc[..., -1:] - gc)[..., None]
    chunk_decay = jnp.exp(gc[..., -1])

    def step(state, inp):
        u_c, w_c, q_c, k_c, a_c, d_c = inp
        v_new = u_c - jnp.einsum('bhcd,bhde->bhce', w_c, state)
        o_c = jnp.einsum('bhcd,bhde->bhce', q_c, state) + jnp.einsum('bhcs,bhse->bhce', a_c, v_new)
        state = state * d_c[..., None, None] + jnp.einsum('bhcd,bhce->bhde', k_c, v_new)
        return state, o_c

    state0 = jnp.zeros((B, H, Dh, Dh), jnp.float32)
    _, o = lax.scan(step, state0, (u, w, q_dec, k_dec, intra, chunk_decay))
    o = jnp.swapaxes(jnp.moveaxis(o, 0, 1), 2, 3).reshape(B, S, H, Dh)
    o = o * lax.rsqrt(jnp.mean(o * o, axis=-1, keepdims=True) + EPS) * o_gain.astype(jnp.float32)
    o = o * jax.nn.silu(gate.reshape(B, S, H, Dh))
    return o.reshape(B, S, W)


def stick_breaking_attention(q, k, v):
    B, S, H, Dh = q.shape
    scale = Dh ** -0.5
    outs = []
    for blk in range(S // SB_BLOCK):
        t0 = blk * SB_BLOCK
        t1 = t0 + SB_BLOCK
        kb, vb = k[:, :t1], v[:, :t1]
        z = jnp.einsum('bthd,bshd->bhts', q[:, t0:t1], kb,
                       preferred_element_type=jnp.float32) * scale
        t_idx = t0 + jnp.arange(SB_BLOCK)[:, None]
        s_idx = jnp.arange(t1)[None, :]
        before = s_idx < t_idx
        log_beta = jax.nn.log_sigmoid(z)
        log_1m_beta = jnp.where(before, jax.nn.log_sigmoid(-z), 0.0)
        tail = lax.cumsum(log_1m_beta, axis=3, reverse=True) - log_1m_beta
        a = jnp.where(before, jnp.exp(log_beta + tail), 0.0)
        outs.append(jnp.einsum('bhts,bshd->bthd', a.astype(vb.dtype), vb))
    return jnp.concatenate(outs, axis=1).reshape(B, S, H * Dh)


def memory_attention(q, mem_kv):
    B, S, _ = q.shape
    q = q.reshape(B, S, N_X_HEADS, X_HEAD_DIM)
    k = mem_kv[..., :X_WIDTH].reshape(B, N_MEM, N_X_HEADS, X_HEAD_DIM)
    v = mem_kv[..., X_WIDTH:].reshape(B, N_MEM, N_X_HEADS, X_HEAD_DIM)
    s = jnp.einsum('bshd,bmhd->bhsm', q, k, preferred_element_type=jnp.float32) * (X_HEAD_DIM ** -0.5)
    p = jax.nn.softmax(s, axis=-1).astype(v.dtype)
    return jnp.einsum('bhsm,bmhd->bshd', p, v).reshape(B, S, X_WIDTH)


def _fwd_setup_inputs(seed: int = 0) -> dict:
    key = jax.random.key(seed)
    ks = jax.random.split(key, 18)
    f32 = jnp.float32
    nrm = lambda k, shape, scale: jax.random.normal(k, shape, f32) * scale
    gain = lambda k, shape: 1.0 + 0.1 * jax.random.normal(k, shape, f32)
    return {
        'x': nrm(ks[0], (BATCH, SEQ, D_MODEL), 1.0),
        'mem': nrm(ks[1], (BATCH, N_MEM, D_MODEL), 1.0),
        'mem_norm': gain(ks[2], (D_MODEL,)),
        'norm_pre_mix': gain(ks[3], (DEPTH, D_MODEL)),
        'norm_post_mix': gain(ks[4], (DEPTH, D_MODEL)),
        'norm_pre_mlp': gain(ks[5], (DEPTH, D_MODEL)),
        'norm_post_mlp': gain(ks[6], (DEPTH, D_MODEL)),
        'w_in_a': nrm(ks[7], (N_A_LAYERS, D_MODEL, IN_A), D_MODEL ** -0.5),
        'conv_w_a': nrm(ks[8], (N_A_LAYERS, CONV_WIDTH, 3 * SEQ_MIX_WIDTH), CONV_WIDTH ** -0.5),
        'a_log_a': jnp.log(jax.random.uniform(ks[9], (N_A_LAYERS, N_LIN_HEADS), f32, 0.01, 1.0)),
        'dt_bias_a': nrm(ks[10], (N_A_LAYERS, N_LIN_HEADS), 0.1),
        'onorm_a': gain(ks[11], (N_A_LAYERS, LIN_HEAD_DIM)),
        'w_in_b': nrm(ks[12], (N_B_LAYERS, D_MODEL, IN_B), D_MODEL ** -0.5),
        'w_mem_kv': nrm(ks[13], (DEPTH, D_MODEL, 2 * X_WIDTH), D_MODEL ** -0.5),
        'w_out': nrm(ks[14], (DEPTH, MIX_WIDTH, D_MODEL), MIX_WIDTH ** -0.5),
        'w_up': nrm(ks[15], (DEPTH, D_MODEL, D_FF), D_MODEL ** -0.5),
        'w_down': nrm(ks[16], (DEPTH, D_FF, D_MODEL), D_FF ** -0.5),
    }


def _fwd_reference(x, mem, mem_norm, norm_pre_mix, norm_post_mix, norm_pre_mlp, norm_post_mlp,
              w_in_a, conv_w_a, a_log_a, dt_bias_a, onorm_a, w_in_b, w_mem_kv, w_out,
              w_up, w_down):
    B, S, _ = x.shape
    mem_n = rms_norm(mem, mem_norm)
    for i in range(DEPTH):
        j = i // N_MIXERS
        h = rms_norm(x, norm_pre_mix[i])
        if i % N_MIXERS == 0:
            proj = h @ w_in_a[j]
            mix = gated_deltanet(proj[..., :IN_A - X_WIDTH], conv_w_a[j], a_log_a[j],
                                 dt_bias_a[j], onorm_a[j])
            mem_q = proj[..., IN_A - X_WIDTH:]
        else:
            proj = h @ w_in_b[j]
            W = SEQ_MIX_WIDTH
            q = proj[..., :W].reshape(B, S, N_SB_HEADS, SB_HEAD_DIM)
            k = proj[..., W:2 * W].reshape(B, S, N_SB_HEADS, SB_HEAD_DIM)
            v = proj[..., 2 * W:3 * W].reshape(B, S, N_SB_HEADS, SB_HEAD_DIM)
            mix = stick_breaking_attention(q, k, v)
            mem_q = proj[..., 3 * W:]
        cross = memory_attention(mem_q, mem_n @ w_mem_kv[i])
        y = jnp.concatenate([mix.astype(x.dtype), cross.astype(x.dtype)], axis=-1) @ w_out[i]
        x = x + rms_norm(y, norm_post_mix[i])
        h = rms_norm(x, norm_pre_mlp[i])
        y = jnp.square(jax.nn.relu(h @ w_up[i])) @ w_down[i]
        x = x + rms_norm(y, norm_post_mlp[i])
    return x


import jax as _jax
import jax.numpy as _jnp

TWIN_FORMAT = 'train_step'
FWD_PARAMS = ['x', 'mem', 'mem_norm', 'norm_pre_mix', 'norm_post_mix', 'norm_pre_mlp', 'norm_post_mlp', 'w_in_a', 'conv_w_a', 'a_log_a', 'dt_bias_a', 'onorm_a', 'w_in_b', 'w_mem_kv', 'w_out', 'w_up', 'w_down']
TWIN_WEIGHTS = ['mem_norm', 'norm_pre_mix', 'norm_post_mix', 'norm_pre_mlp', 'norm_post_mlp', 'w_in_a', 'conv_w_a', 'a_log_a', 'dt_bias_a', 'onorm_a', 'w_in_b', 'w_mem_kv', 'w_out', 'w_up', 'w_down']
TWIN_DIFF_INPUT = 'x'
TWIN_INPUTS = ['x', 'mem', 'mem_norm', 'norm_pre_mix', 'norm_post_mix', 'norm_pre_mlp', 'norm_post_mlp', 'w_in_a', 'conv_w_a', 'a_log_a', 'dt_bias_a', 'onorm_a', 'w_in_b', 'w_mem_kv', 'w_out', 'w_up', 'w_down', 'loss_target', 'm_mem_norm', 'm_norm_pre_mix', 'm_norm_post_mix', 'm_norm_pre_mlp', 'm_norm_post_mlp', 'm_w_in_a', 'm_conv_w_a', 'm_a_log_a', 'm_dt_bias_a', 'm_onorm_a', 'm_w_in_b', 'm_w_mem_kv', 'm_w_out', 'm_w_up', 'm_w_down', 'v_mem_norm', 'v_norm_pre_mix', 'v_norm_post_mix', 'v_norm_pre_mlp', 'v_norm_post_mlp', 'v_w_in_a', 'v_conv_w_a', 'v_a_log_a', 'v_dt_bias_a', 'v_onorm_a', 'v_w_in_b', 'v_w_mem_kv', 'v_w_out', 'v_w_up', 'v_w_down']
TWIN_OUTPUTS = ['loss', 'grad_x', 'grad_mem_norm', 'grad_norm_pre_mix', 'grad_norm_post_mix', 'grad_norm_pre_mlp', 'grad_norm_post_mlp', 'grad_w_in_a', 'grad_conv_w_a', 'grad_a_log_a', 'grad_dt_bias_a', 'grad_onorm_a', 'grad_w_in_b', 'grad_w_mem_kv', 'grad_w_out', 'grad_w_up', 'grad_w_down', 'delta_mem_norm', 'delta_norm_pre_mix', 'delta_norm_post_mix', 'delta_norm_pre_mlp', 'delta_norm_post_mlp', 'delta_w_in_a', 'delta_conv_w_a', 'delta_a_log_a', 'delta_dt_bias_a', 'delta_onorm_a', 'delta_w_in_b', 'delta_w_mem_kv', 'delta_w_out', 'delta_w_up', 'delta_w_down', 'new_m_mem_norm', 'new_m_norm_pre_mix', 'new_m_norm_post_mix', 'new_m_norm_pre_mlp', 'new_m_norm_post_mlp', 'new_m_w_in_a', 'new_m_conv_w_a', 'new_m_a_log_a', 'new_m_dt_bias_a', 'new_m_onorm_a', 'new_m_w_in_b', 'new_m_w_mem_kv', 'new_m_w_out', 'new_m_w_up', 'new_m_w_down', 'new_v_mem_norm', 'new_v_norm_pre_mix', 'new_v_norm_post_mix', 'new_v_norm_pre_mlp', 'new_v_norm_post_mlp', 'new_v_w_in_a', 'new_v_conv_w_a', 'new_v_a_log_a', 'new_v_dt_bias_a', 'new_v_onorm_a', 'new_v_w_in_b', 'new_v_w_mem_kv', 'new_v_w_out', 'new_v_w_up', 'new_v_w_down']
TWIN_LEAF_KINDS = {'loss': 'loss', 'grad_x': 'grad_x', 'grad_mem_norm': 'grad_w', 'grad_norm_pre_mix': 'grad_w', 'grad_norm_post_mix': 'grad_w', 'grad_norm_pre_mlp': 'grad_w', 'grad_norm_post_mlp': 'grad_w', 'grad_w_in_a': 'grad_w', 'grad_conv_w_a': 'grad_w', 'grad_a_log_a': 'grad_w', 'grad_dt_bias_a': 'grad_w', 'grad_onorm_a': 'grad_w', 'grad_w_in_b': 'grad_w', 'grad_w_mem_kv': 'grad_w', 'grad_w_out': 'grad_w', 'grad_w_up': 'grad_w', 'grad_w_down': 'grad_w', 'delta_mem_norm': 'delta_w', 'delta_norm_pre_mix': 'delta_w', 'delta_norm_post_mix': 'delta_w', 'delta_norm_pre_mlp': 'delta_w', 'delta_norm_post_mlp': 'delta_w', 'delta_w_in_a': 'delta_w', 'delta_conv_w_a': 'delta_w', 'delta_a_log_a': 'delta_w', 'delta_dt_bias_a': 'delta_w', 'delta_onorm_a': 'delta_w', 'delta_w_in_b': 'delta_w', 'delta_w_mem_kv': 'delta_w', 'delta_w_out': 'delta_w', 'delta_w_up': 'delta_w', 'delta_w_down': 'delta_w', 'new_m_mem_norm': 'new_m', 'new_m_norm_pre_mix': 'new_m', 'new_m_norm_post_mix': 'new_m', 'new_m_norm_pre_mlp': 'new_m', 'new_m_norm_post_mlp': 'new_m', 'new_m_w_in_a': 'new_m', 'new_m_conv_w_a': 'new_m', 'new_m_a_log_a': 'new_m', 'new_m_dt_bias_a': 'new_m', 'new_m_onorm_a': 'new_m', 'new_m_w_in_b': 'new_m', 'new_m_w_mem_kv': 'new_m', 'new_m_w_out': 'new_m', 'new_m_w_up': 'new_m', 'new_m_w_down': 'new_m', 'new_v_mem_norm': 'new_v', 'new_v_norm_pre_mix': 'new_v', 'new_v_norm_post_mix': 'new_v', 'new_v_norm_pre_mlp': 'new_v', 'new_v_norm_post_mlp': 'new_v', 'new_v_w_in_a': 'new_v', 'new_v_conv_w_a': 'new_v', 'new_v_a_log_a': 'new_v', 'new_v_dt_bias_a': 'new_v', 'new_v_onorm_a': 'new_v', 'new_v_w_in_b': 'new_v', 'new_v_w_mem_kv': 'new_v', 'new_v_w_out': 'new_v', 'new_v_w_up': 'new_v', 'new_v_w_down': 'new_v'}


def _forward(args):
    return _fwd_reference(*[args[k] for k in FWD_PARAMS])


def _output_shape():
    out = _jax.eval_shape(lambda: _forward(_fwd_setup_inputs(0)))
    return out.shape, out.dtype

N_MICROBATCH = 1
ADAM_LR = 0.001
ADAM_B1 = 0.9
ADAM_B2 = 0.999
ADAM_EPS = 1e-08
ADAM_WD = 0.01
ADAM_STEP = 10
PER_EXAMPLE_BATCH_AXIS = {'x': 0, 'mem': 0, 'loss_target': 0}
SHARED_INPUTS = []
_WEIGHT_DTYPES = {'mem_norm': _jnp.float32, 'norm_pre_mix': _jnp.float32, 'norm_post_mix': _jnp.float32, 'norm_pre_mlp': _jnp.float32, 'norm_post_mlp': _jnp.float32, 'w_in_a': _jnp.float32, 'conv_w_a': _jnp.float32, 'a_log_a': _jnp.float32, 'dt_bias_a': _jnp.float32, 'onorm_a': _jnp.float32, 'w_in_b': _jnp.float32, 'w_mem_kv': _jnp.float32, 'w_out': _jnp.float32, 'w_up': _jnp.float32, 'w_down': _jnp.float32}
MOMENT_SCALE = {'mem_norm': 2.055798e+00, 'norm_pre_mix': 9.039751e+00, 'norm_post_mix': 6.722050e+01, 'norm_pre_mlp': 6.518678e+00, 'norm_post_mlp': 7.044362e+01, 'w_in_a': 1.064124e+00, 'conv_w_a': 4.208733e+00, 'a_log_a': 3.731196e+01, 'dt_bias_a': 2.248029e+01, 'onorm_a': 3.193236e+01, 'w_in_b': 1.061096e+01, 'w_mem_kv': 2.175870e+00, 'w_out': 1.487127e+01, 'w_up': 3.174334e+00, 'w_down': 2.105424e+01}


def _to_microbatches(a, axis):
    t = _jnp.moveaxis(a, axis, 0)
    t = t.reshape((N_MICROBATCH, t.shape[0] // N_MICROBATCH) + t.shape[1:])
    return _jnp.moveaxis(t, 1, axis + 1)


def setup_inputs(seed: int = 0) -> dict:
    inp = _fwd_setup_inputs(seed)
    key = _jax.random.fold_in(_jax.random.key(seed), 7919)
    shape, _ = _output_shape()
    out = dict(inp)
    out["loss_target"] = _jax.random.normal(_jax.random.fold_in(key, 0), shape, _jnp.float32)
    for i, name in enumerate(TWIN_WEIGHTS):
        w = inp[name].astype(_jnp.float32)
        if MOMENT_SCALE is None:
            s = _jnp.sqrt(_jnp.mean(_jnp.square(w)) + 1e-30)
        else:
            s = MOMENT_SCALE[name]
        km, kv = _jax.random.split(_jax.random.fold_in(key, i + 1))
        out[name] = w
        out["m_" + name] = s * _jax.random.normal(km, w.shape, _jnp.float32)
        out["v_" + name] = (s * s) * _jax.random.uniform(kv, w.shape, _jnp.float32, 0.5, 1.5)
    if N_MICROBATCH > 1:
        for name, axis in PER_EXAMPLE_BATCH_AXIS.items():
            out[name] = _to_microbatches(out[name], axis)
    return {'x': out['x'], 'mem': out['mem'], 'mem_norm': out['mem_norm'], 'norm_pre_mix': out['norm_pre_mix'], 'norm_post_mix': out['norm_post_mix'], 'norm_pre_mlp': out['norm_pre_mlp'], 'norm_post_mlp': out['norm_post_mlp'], 'w_in_a': out['w_in_a'], 'conv_w_a': out['conv_w_a'], 'a_log_a': out['a_log_a'], 'dt_bias_a': out['dt_bias_a'], 'onorm_a': out['onorm_a'], 'w_in_b': out['w_in_b'], 'w_mem_kv': out['w_mem_kv'], 'w_out': out['w_out'], 'w_up': out['w_up'], 'w_down': out['w_down'], 'loss_target': out['loss_target'], 'm_mem_norm': out['m_mem_norm'], 'm_norm_pre_mix': out['m_norm_pre_mix'], 'm_norm_post_mix': out['m_norm_post_mix'], 'm_norm_pre_mlp': out['m_norm_pre_mlp'], 'm_norm_post_mlp': out['m_norm_post_mlp'], 'm_w_in_a': out['m_w_in_a'], 'm_conv_w_a': out['m_conv_w_a'], 'm_a_log_a': out['m_a_log_a'], 'm_dt_bias_a': out['m_dt_bias_a'], 'm_onorm_a': out['m_onorm_a'], 'm_w_in_b': out['m_w_in_b'], 'm_w_mem_kv': out['m_w_mem_kv'], 'm_w_out': out['m_w_out'], 'm_w_up': out['m_w_up'], 'm_w_down': out['m_w_down'], 'v_mem_norm': out['v_mem_norm'], 'v_norm_pre_mix': out['v_norm_pre_mix'], 'v_norm_post_mix': out['v_norm_post_mix'], 'v_norm_pre_mlp': out['v_norm_pre_mlp'], 'v_norm_post_mlp': out['v_norm_post_mlp'], 'v_w_in_a': out['v_w_in_a'], 'v_conv_w_a': out['v_conv_w_a'], 'v_a_log_a': out['v_a_log_a'], 'v_dt_bias_a': out['v_dt_bias_a'], 'v_onorm_a': out['v_onorm_a'], 'v_w_in_b': out['v_w_in_b'], 'v_w_mem_kv': out['v_w_mem_kv'], 'v_w_out': out['v_w_out'], 'v_w_up': out['v_w_up'], 'v_w_down': out['v_w_down']}


def _loss(weights, diff, rest, loss_target):
    with _jax.named_scope("forward"):
        args = {**rest, TWIN_DIFF_INPUT: diff, **{k: w.astype(_WEIGHT_DTYPES[k]) for k, w in weights.items()}}
        y = _forward(args)
    with _jax.named_scope("loss_head"):
        err = _jnp.square(y.astype(_jnp.float32) - loss_target)
        return 0.5 * _jnp.sum(_jnp.mean(err, axis=-1)) if err.ndim else 0.5 * err


def _adamw(w, g, m, v):
    m = ADAM_B1 * m + (1.0 - ADAM_B1) * g
    v = ADAM_B2 * v + (1.0 - ADAM_B2) * _jnp.square(g)
    m_hat = m / (1.0 - ADAM_B1 ** ADAM_STEP)
    v_hat = v / (1.0 - ADAM_B2 ** ADAM_STEP)
    delta = -ADAM_LR * (m_hat / (_jnp.sqrt(v_hat) + ADAM_EPS) + ADAM_WD * w)
    return delta, m, v


def reference(x, mem, mem_norm, norm_pre_mix, norm_post_mix, norm_pre_mlp, norm_post_mlp, w_in_a, conv_w_a, a_log_a, dt_bias_a, onorm_a, w_in_b, w_mem_kv, w_out, w_up, w_down, loss_target, m_mem_norm, m_norm_pre_mix, m_norm_post_mix, m_norm_pre_mlp, m_norm_post_mlp, m_w_in_a, m_conv_w_a, m_a_log_a, m_dt_bias_a, m_onorm_a, m_w_in_b, m_w_mem_kv, m_w_out, m_w_up, m_w_down, v_mem_norm, v_norm_pre_mix, v_norm_post_mix, v_norm_pre_mlp, v_norm_post_mlp, v_w_in_a, v_conv_w_a, v_a_log_a, v_dt_bias_a, v_onorm_a, v_w_in_b, v_w_mem_kv, v_w_out, v_w_up, v_w_down):
    given = dict(x=x, mem=mem, mem_norm=mem_norm, norm_pre_mix=norm_pre_mix, norm_post_mix=norm_post_mix, norm_pre_mlp=norm_pre_mlp, norm_post_mlp=norm_post_mlp, w_in_a=w_in_a, conv_w_a=conv_w_a, a_log_a=a_log_a, dt_bias_a=dt_bias_a, onorm_a=onorm_a, w_in_b=w_in_b, w_mem_kv=w_mem_kv, w_out=w_out, w_up=w_up, w_down=w_down, loss_target=loss_target, m_mem_norm=m_mem_norm, m_norm_pre_mix=m_norm_pre_mix, m_norm_post_mix=m_norm_post_mix, m_norm_pre_mlp=m_norm_pre_mlp, m_norm_post_mlp=m_norm_post_mlp, m_w_in_a=m_w_in_a, m_conv_w_a=m_conv_w_a, m_a_log_a=m_a_log_a, m_dt_bias_a=m_dt_bias_a, m_onorm_a=m_onorm_a, m_w_in_b=m_w_in_b, m_w_mem_kv=m_w_mem_kv, m_w_out=m_w_out, m_w_up=m_w_up, m_w_down=m_w_down, v_mem_norm=v_mem_norm, v_norm_pre_mix=v_norm_pre_mix, v_norm_post_mix=v_norm_post_mix, v_norm_pre_mlp=v_norm_pre_mlp, v_norm_post_mlp=v_norm_post_mlp, v_w_in_a=v_w_in_a, v_conv_w_a=v_conv_w_a, v_a_log_a=v_a_log_a, v_dt_bias_a=v_dt_bias_a, v_onorm_a=v_onorm_a, v_w_in_b=v_w_in_b, v_w_mem_kv=v_w_mem_kv, v_w_out=v_w_out, v_w_up=v_w_up, v_w_down=v_w_down)
    weights = {n: given[n] for n in TWIN_WEIGHTS}
    shared = {n: given[n] for n in SHARED_INPUTS}
    per_example = {n: given[n] for n in ['x', 'mem']}
    grad_fn = _jax.value_and_grad(_loss, argnums=(0, 1))

    def one_microbatch(ex, loss_target):
        ex = dict(ex)
        diff = ex.pop(TWIN_DIFF_INPUT)
        return grad_fn(weights, diff, {**shared, **ex}, loss_target)

    if N_MICROBATCH == 1:
        loss, (grad_w, grad_x) = one_microbatch(per_example, given["loss_target"])
    else:
        def body(carry, xs):
            loss_sum, grad_sum = carry
            l_k, (gw_k, gx_k) = one_microbatch(xs[0], xs[1])
            with _jax.named_scope("update"):
                return (loss_sum + l_k, _jax.tree.map(_jnp.add, grad_sum, gw_k)), gx_k

        init = (_jnp.zeros((), _jnp.float32), _jax.tree.map(_jnp.zeros_like, weights))
        (loss, grad_w), grad_x = _jax.lax.scan(body, init, (per_example, given["loss_target"]))
    with _jax.named_scope("update"):
        delta_w, new_m, new_v = {}, {}, {}
        for n in TWIN_WEIGHTS:
            delta_w[n], new_m[n], new_v[n] = _adamw(weights[n], grad_w[n], given["m_" + n], given["v_" + n])
    return (loss, grad_x, *[grad_w[n] for n in TWIN_WEIGHTS], *[delta_w[n] for n in TWIN_WEIGHTS],
            *[new_m[n] for n in TWIN_WEIGHTS], *[new_v[n] for n in TWIN_WEIGHTS])
```

```python
import functools

import jax
import jax.numpy as jnp
from jax import lax
from jax.experimental import pallas as pl
from jax.experimental.pallas import tpu as pltpu

F32 = jnp.float32
BF16 = jnp.bfloat16
HIGHEST = lax.Precision.HIGHEST
MESH = pl.DeviceIdType.MESH

D_MODEL = 1024
SEQ = 2048
DEPTH = 2
X_WIDTH = 256
N_X_HEADS = 4
X_HEAD_DIM = 64
MIX_W = 768
LIN_DH = 128
N_LIN = 6
CONV_K = 4
CHUNK = 64
SB_DH = 64
SB_PAIRS = 6
N_MEM = 256
D_FF = 4096
EPS = 1e-6
IN_A = 3340
IN_A_PAD = 3456
IN_B = 2560
SMALL_COL = 26
N_CHIPS = 4

ADAM_LR, ADAM_B1, ADAM_B2, ADAM_EPS, ADAM_WD, ADAM_STEP = 0.001, 0.9, 0.999, 1e-08, 0.01, 10

VMEM_LIMIT = 48 * 1024 * 1024

NN = (((1,), (0,)), ((), ()))
NT = (((1,), (1,)), ((), ()))
TN = (((0,), (0,)), ((), ()))


def _cparams(sem):
    return pltpu.CompilerParams(dimension_semantics=sem, vmem_limit_bytes=VMEM_LIMIT)


def _dotbf(a, b, dn=NN):
    return lax.dot_general(a.astype(BF16), b.astype(BF16), dn, preferred_element_type=F32)


def _split(a):
    hi = a.astype(BF16)
    lo = (a - hi.astype(F32)).astype(BF16)
    return hi, lo


def _dot3(a, b, dn=NN):
    ah, al = _split(a)
    bh, bl = _split(b)
    d = functools.partial(lax.dot_general, dimension_numbers=dn, preferred_element_type=F32)
    return d(ah, bh) + (d(ah, bl) + d(al, bh))


def _dot_mask(a, m01):
    ah, al = _split(a)
    d = functools.partial(lax.dot_general, dimension_numbers=NN, preferred_element_type=F32)
    return d(ah, m01) + d(al, m01)


def _iota(shape, dim):
    return lax.broadcasted_iota(jnp.int32, shape, dim)


def _softplus(x):
    return jnp.maximum(x, 0.0) + jnp.log(1.0 + jnp.exp(-jnp.abs(x)))


def _log_sigmoid(z):
    return jnp.minimum(z, 0.0) - jnp.log(1.0 + jnp.exp(-jnp.abs(z)))


def _rms(x, g):
    r = lax.rsqrt(jnp.mean(x * x, axis=-1, keepdims=True) + EPS)
    return (x * r) * g


def _matmul(a, b, *, mode, tm, tn, tk, name, out_dtypes=(F32,), epilogue=None, extras=()):
    if mode == "nn":
        (m, k), (k2, n) = a.shape, b.shape
        a_spec = pl.BlockSpec((tm, tk), lambda i, j, kk: (i, kk))
        b_spec = pl.BlockSpec((tk, tn), lambda i, j, kk: (kk, j))
        dn = NN
    elif mode == "nt":
        (m, k), (n, k2) = a.shape, b.shape
        a_spec = pl.BlockSpec((tm, tk), lambda i, j, kk: (i, kk))
        b_spec = pl.BlockSpec((tn, tk), lambda i, j, kk: (j, kk))
        dn = NT
    else:
        (k, m), (k2, n) = a.shape, b.shape
        a_spec = pl.BlockSpec((tk, tm), lambda i, j, kk: (kk, i))
        b_spec = pl.BlockSpec((tk, tn), lambda i, j, kk: (kk, j))
        dn = TN
    assert k == k2 and m % tm == 0 and n % tn == 0 and k % tk == 0, (name, a.shape, b.shape)
    assert a.dtype == BF16 and b.dtype == BF16, name
    nk = k // tk
    n_extra, n_out = len(extras), len(out_dtypes)

    def body(a_ref, b_ref, *rest):
        extra_refs, out_refs, acc_ref = rest[:n_extra], rest[n_extra:n_extra + n_out], rest[-1]
        kk = pl.program_id(2)

        @pl.when(kk == 0)
        def _():
            acc_ref[...] = jnp.zeros_like(acc_ref)

        acc_ref[...] += lax.dot_general(a_ref[...], b_ref[...], dn, preferred_element_type=F32)

        @pl.when(kk == nk - 1)
        def _():
            acc = acc_ref[...]
            outs = (acc,) if epilogue is None else epilogue(acc, *[r[...] for r in extra_refs])
            for o_ref, o in zip(out_refs, outs):
                o_ref[...] = o.astype(o_ref.dtype)

    mn_spec = pl.BlockSpec((tm, tn), lambda i, j, kk: (i, j))
    outs = pl.pallas_call(
        body,
        name=name,
        grid=(m // tm, n // tn, nk),
        in_specs=[a_spec, b_spec] + [mn_spec] * n_extra,
        out_specs=[mn_spec] * n_out,
        out_shape=[jax.ShapeDtypeStruct((m, n), dt) for dt in out_dtypes],
        scratch_shapes=[pltpu.VMEM((tm, tn), F32)],
        compiler_params=_cparams(("parallel", "parallel", "arbitrary")),
    )(a, b, *extras)
    return outs[0] if n_out == 1 else outs


ROW_TILE = 512


def _row_spec(width=D_MODEL, tile=ROW_TILE):
    return pl.BlockSpec((tile, width), lambda i: (i, 0))


def _vec_spec(width=D_MODEL):
    return pl.BlockSpec((1, width), lambda i: (0, 0))


def _rms_fwd(x, g, *, name, tile=ROW_TILE):
    t = x.shape[0]

    def body(x_ref, g_ref, h_ref):
        h_ref[...] = _rms(x_ref[...], g_ref[...]).astype(BF16)

    return pl.pallas_call(
        body, name=name, grid=(t // tile,),
        in_specs=[_row_spec(tile=tile), _vec_spec()], out_specs=_row_spec(tile=tile),
        out_shape=jax.ShapeDtypeStruct((t, D_MODEL), BF16),
        compiler_params=_cparams(("parallel",)),
    )(x, g.reshape(1, D_MODEL))


def _post_norm_add(xres, y, g_post, g_next, *, name):
    t = xres.shape[0]

    def body(x_ref, y_ref, gp_ref, gn_ref, xo_ref, h_ref):
        xo = x_ref[...] + _rms(y_ref[...], gp_ref[...])
        xo_ref[...] = xo
        h_ref[...] = _rms(xo, gn_ref[...]).astype(BF16)

    return pl.pallas_call(
        body, name=name, grid=(t // ROW_TILE,),
        in_specs=[_row_spec(), _row_spec(), _vec_spec(), _vec_spec()],
        out_specs=[_row_spec(), _row_spec()],
        out_shape=[jax.ShapeDtypeStruct((t, D_MODEL), F32), jax.ShapeDtypeStruct((t, D_MODEL), BF16)],
        compiler_params=_cparams(("parallel",)),
    )(xres, y, g_post.reshape(1, D_MODEL), g_next.reshape(1, D_MODEL))


def _post_norm_loss(xres, y, g_post, target, *, name):
    t = xres.shape[0]

    def body(x_ref, y_ref, gp_ref, t_ref, loss_ref, dx_ref):
        @pl.when(pl.program_id(0) == 0)
        def _():
            loss_ref[...] = jnp.zeros_like(loss_ref)

        err = (x_ref[...] + _rms(y_ref[...], gp_ref[...])) - t_ref[...]
        per_tok = jnp.mean(err * err, axis=-1, keepdims=True)
        loss_ref[...] += 0.5 * jnp.sum(per_tok, axis=0, keepdims=True)
        dx_ref[...] = err * (1.0 / D_MODEL)

    return pl.pallas_call(
        body, name=name, grid=(t // ROW_TILE,),
        in_specs=[_row_spec(), _row_spec(), _vec_spec(), _row_spec()],
        out_specs=[pl.BlockSpec((1, 128), lambda i: (0, 0)), _row_spec()],
        out_shape=[jax.ShapeDtypeStruct((1, 128), F32), jax.ShapeDtypeStruct((t, D_MODEL), F32)],
        compiler_params=_cparams(("arbitrary",)),
    )(xres, y, g_post.reshape(1, D_MODEL), target)


def _rms_bwd(dy, x, g, *, name, res=None, out_dtype=F32, tile=ROW_TILE):
    t = x.shape[0]
    has_res = res is not None

    def body(dy_ref, x_ref, g_ref, *rest):
        res_ref = rest[0] if has_res else None
        dx_ref, dg_ref = rest[-2], rest[-1]

        @pl.when(pl.program_id(0) == 0)
        def _():
            dg_ref[...] = jnp.zeros_like(dg_ref)

        xf = x_ref[...]
        dyf = dy_ref[...].astype(F32)
        r = lax.rsqrt(jnp.mean(xf * xf, axis=-1, keepdims=True) + EPS)
        xhat = xf * r
        dg_ref[...] += jnp.sum(dyf * xhat, axis=0, keepdims=True)
        dxh = dyf * g_ref[...]
        dx = r * (dxh - xhat * jnp.mean(dxh * xhat, axis=-1, keepdims=True))
        if has_res:
            dx = dx + res_ref[...]
        dx_ref[...] = dx.astype(dx_ref.dtype)

    args = [dy, x, g.reshape(1, D_MODEL)] + ([res] if has_res else [])
    return pl.pallas_call(
        body, name=name, grid=(t // tile,),
        in_specs=[_row_spec(tile=tile), _row_spec(tile=tile), _vec_spec()] + ([_row_spec(tile=tile)] if has_res else []),
        out_specs=[_row_spec(tile=tile), _vec_spec()],
        out_shape=[jax.ShapeDtypeStruct((t, D_MODEL), out_dtype), jax.ShapeDtypeStruct((1, D_MODEL), F32)],
        compiler_params=_cparams(("arbitrary",)),
    )(*args)


CONV_COLS = 256
N_CONV_BLOCKS = 3 * MIX_W // CONV_COLS


def _shift_down(x, k):
    if k == 0:
        return x
    return jnp.where(_iota(x.shape, 0) >= k, pltpu.roll(x, k, 0), 0.0)


def _shift_up(x, k):
    if k == 0:
        return x
    s = x.shape[0]
    return jnp.where(_iota(x.shape, 0) < s - k, pltpu.roll(x, s - k, 0), 0.0)


def _conv_pre(x, w_ref):
    c = w_ref[CONV_K - 1:CONV_K, :] * x
    for i in range(CONV_K - 1):
        c = c + w_ref[i:i + 1, :] * _shift_down(x, CONV_K - 1 - i)
    return c


def _conv_silu_fwd(proj, conv_w, n_batch):
    def body(x_ref, w_ref, y_ref):
        c = _conv_pre(x_ref[...], w_ref)
        y_ref[...] = c * jax.nn.sigmoid(c)

    return pl.pallas_call(
        body, name="conv_silu_fwd", grid=(n_batch, N_CONV_BLOCKS),
        in_specs=[pl.BlockSpec((SEQ, CONV_COLS), lambda b, j: (b, j)),
                  pl.BlockSpec((CONV_K, CONV_COLS), lambda b, j: (0, j))],
        out_specs=pl.BlockSpec((SEQ, CONV_COLS), lambda b, j: (b, j)),
        out_shape=jax.ShapeDtypeStruct((n_batch * SEQ, 3 * MIX_W), F32),
        compiler_params=_cparams(("parallel", "parallel")),
    )(proj, conv_w)


def _conv_silu_bwd(dy, proj, conv_w, n_batch):
    def body(dy_ref, x_ref, w_ref, dx_ref, dw_ref):
        @pl.when(pl.program_id(1) == 0)
        def _():
            dw_ref[...] = jnp.zeros_like(dw_ref)

        x = x_ref[...]
        c = _conv_pre(x, w_ref)
        sig = jax.nn.sigmoid(c)
        dc = dy_ref[...] * (sig * (1.0 + c * (1.0 - sig)))
        dx = w_ref[CONV_K - 1:CONV_K, :] * dc
        dw_ref[CONV_K - 1:CONV_K, :] += jnp.sum(dc * x, axis=0, keepdims=True)
        for i in range(CONV_K - 1):
            k = CONV_K - 1 - i
            dx = dx + w_ref[i:i + 1, :] * _shift_up(dc, k)
            dw_ref[i:i + 1, :] += jnp.sum(dc * _shift_down(x, k), axis=0, keepdims=True)
        dx_ref[...] = dx.astype(BF16)

    return pl.pallas_call(
        body, name="conv_silu_bwd", grid=(N_CONV_BLOCKS, n_batch),
        in_specs=[pl.BlockSpec((SEQ, CONV_COLS), lambda j, b: (b, j)),
                  pl.BlockSpec((SEQ, CONV_COLS), lambda j, b: (b, j)),
                  pl.BlockSpec((CONV_K, CONV_COLS), lambda j, b: (0, j))],
        out_specs=[pl.BlockSpec((SEQ, CONV_COLS), lambda j, b: (b, j)),
                   pl.BlockSpec((CONV_K, CONV_COLS), lambda j, b: (0, j))],
        out_shape=[jax.ShapeDtypeStruct((n_batch * SEQ, 3 * MIX_W), BF16),
                   jax.ShapeDtypeStruct((CONV_K, 3 * MIX_W), F32)],
        compiler_params=_cparams(("parallel", "arbitrary")),
    )(dy, proj, conv_w)


@jax.custom_vjp
def _solve_apply(low, rhs, tinv):
    return _dot3(tinv, rhs)


def _solve_apply_fwd(low, rhs, tinv):
    sol = _dot3(tinv, rhs)
    return sol, (tinv, sol)


def _solve_apply_bwd(resid, g):
    tinv, sol = resid
    y = _dot3(tinv, g, TN)
    return -_dot3(y, sol, NT), y, jnp.zeros_like(tinv)


_solve_apply.defvjp(_solve_apply_fwd, _solve_apply_bwd)


def _inv_unit_lower(low):
    c = low.shape[0]
    eye = (_iota((c, c), 0) == _iota((c, c), 1)).astype(F32)
    m = -low
    p = eye + m
    for _ in range(5):
        m = _dot3(m, m)
        p = p + _dot3(p, m)
    return p


def _gdn_decays(small, alog_row, dtb_row):
    c = small.shape[0]
    beta_all = jax.nn.sigmoid(small)
    g_all = -jnp.exp(alog_row) * _softplus(small + dtb_row)
    ltri = (_iota((c, c), 1) <= _iota((c, c), 0)).astype(F32)
    gc_all = lax.dot_general(ltri, g_all, NN, precision=HIGHEST, preferred_element_type=F32)
    return beta_all, gc_all


def _gdn_head(h, q, k, v, gate, state, beta_all, gc_all, gain_row, tinv):
    c = q.shape[0]
    lane = _iota((c, 128), 1)
    row, col = _iota((c, c), 0), _iota((c, c), 1)
    causal, strict = row >= col, row > col
    beta = jnp.sum(jnp.where(lane == h, beta_all, 0.0), axis=1, keepdims=True)
    gc = jnp.sum(jnp.where(lane == N_LIN + h, gc_all, 0.0), axis=1, keepdims=True)
    pick = (lane == N_LIN + h).astype(F32)
    gc_j = lax.dot_general(pick, gc_all, NT, precision=HIGHEST, preferred_element_type=F32)
    decay = jnp.where(causal, jnp.exp(jnp.where(causal, gc - gc_j, 0.0)), 0.0)
    gc_last = jnp.sum(jnp.where(_iota((c, 1), 0) == c - 1, gc, 0.0), axis=0, keepdims=True)

    qn = q * lax.rsqrt(jnp.sum(q * q, axis=-1, keepdims=True) + EPS) * (LIN_DH ** -0.5)
    kn = k * lax.rsqrt(jnp.sum(k * k, axis=-1, keepdims=True) + EPS)
    kb = kn * beta
    low = jnp.where(strict, _dotbf(kb, kn, NT) * decay, 0.0)
    if tinv is None:
        tinv = _inv_unit_lower(low)
    egc = jnp.exp(gc)
    u = _solve_apply(low, v * beta, tinv)
    w = _solve_apply(low, kb * egc, tinv)
    intra = _dotbf(qn, kn, NT) * decay
    v_new = u - _dotbf(w, state)
    o = _dotbf(qn * egc, state) + _dotbf(intra, v_new)
    new_state = state * jnp.exp(gc_last) + _dotbf(kn * jnp.exp(gc_last - gc), v_new, TN)
    o = o * lax.rsqrt(jnp.mean(o * o, axis=-1, keepdims=True) + EPS) * gain_row
    return o * (gate * jax.nn.sigmoid(gate)), new_state, tinv


def _gdn_param_rows(a_log, dt_bias, onorm):
    alog_row = jnp.zeros((1, 128), F32).at[0, N_LIN:2 * N_LIN].set(a_log)
    dtb_row = jnp.zeros((1, 128), F32).at[0, N_LIN:2 * N_LIN].set(dt_bias)
    return alog_row, dtb_row, onorm.reshape(1, LIN_DH)


def _head(ref_or_val, h):
    return ref_or_val[:, LIN_DH * h:LIN_DH * (h + 1)]


def _gdn_fwd(qkv, proj, alog_row, dtb_row, gain_row, n_batch):
    nc = SEQ // CHUNK
    t = n_batch * SEQ

    def body(qkv_ref, small_ref, gate_ref, al_ref, dt_ref, gn_ref, mix_ref, st_ref, ti_ref, s_scr):
        @pl.when(pl.program_id(1) == 0)
        def _():
            s_scr[...] = jnp.zeros_like(s_scr)

        st_ref[0, 0] = s_scr[...]
        beta_all, gc_all = _gdn_decays(small_ref[...], al_ref[...], dt_ref[...])
        outs = []
        for h in range(N_LIN):
            o, s_new, tinv = _gdn_head(
                h, _head(qkv_ref, h), _head(qkv_ref, N_LIN + h), _head(qkv_ref, 2 * N_LIN + h),
                _head(gate_ref, h), s_scr[h], beta_all, gc_all, gn_ref[...], None)
            outs.append(o)
            s_scr[h] = s_new
            ti_ref[0, 0, h] = tinv
        mix_ref[...] = jnp.concatenate(outs, axis=1).astype(BF16)

    row = lambda b, n: b * nc + n
    vec = pl.BlockSpec((1, 128), lambda b, n: (0, 0))
    return pl.pallas_call(
        body, name="gdn_fwd", grid=(n_batch, nc),
        in_specs=[pl.BlockSpec((CHUNK, 3 * MIX_W), lambda b, n: (row(b, n), 0)),
                  pl.BlockSpec((CHUNK, 128), lambda b, n: (row(b, n), SMALL_COL)),
                  pl.BlockSpec((CHUNK, MIX_W), lambda b, n: (row(b, n), 3)),
                  vec, vec, vec],
        out_specs=[pl.BlockSpec((CHUNK, MIX_W), lambda b, n: (row(b, n), 0)),
                   pl.BlockSpec((1, 1, N_LIN, LIN_DH, LIN_DH), lambda b, n: (b, n, 0, 0, 0)),
                   pl.BlockSpec((1, 1, N_LIN, CHUNK, CHUNK), lambda b, n: (b, n, 0, 0, 0))],
        out_shape=[jax.ShapeDtypeStruct((t, MIX_W), BF16),
                   jax.ShapeDtypeStruct((n_batch, nc, N_LIN, LIN_DH, LIN_DH), F32),
                   jax.ShapeDtypeStruct((n_batch, nc, N_LIN, CHUNK, CHUNK), F32)],
        scratch_shapes=[pltpu.VMEM((N_LIN, LIN_DH, LIN_DH), F32)],
        compiler_params=_cparams(("parallel", "arbitrary")),
    )(qkv, proj, proj, alog_row, dtb_row, gain_row)


def _gdn_bwd(dcat, qkv, proj, states, tinvs, alog_row, dtb_row, gain_row, n_batch):
    nc = SEQ // CHUNK
    t = n_batch * SEQ

    def body(dmix_ref, qkv_ref, small_ref, gate_ref, st_ref, ti_ref, al_ref, dt_ref, gn_ref,
             dqkv_ref, dgate_ref, dsmall_ref, dal_ref, ddt_ref, dgn_ref, ds_scr):
        @pl.when(pl.program_id(1) == 0)
        def _():
            ds_scr[...] = jnp.zeros_like(ds_scr)

        @pl.when((pl.program_id(0) == 0) & (pl.program_id(1) == 0))
        def _():
            dal_ref[...] = jnp.zeros_like(dal_ref)
            ddt_ref[...] = jnp.zeros_like(ddt_ref)
            dgn_ref[...] = jnp.zeros_like(dgn_ref)

        def chunk(qs, ks, vs, gates, states_in, small, al, dt, gn):
            beta_all, gc_all = _gdn_decays(small, al, dt)
            outs, new_states = [], []
            for h in range(N_LIN):
                o, s_new, _ = _gdn_head(h, qs[h], ks[h], vs[h], gates[h], states_in[h],
                                        beta_all, gc_all, gn, ti_ref[0, 0, h])
                outs.append(o)
                new_states.append(s_new)
            return tuple(outs), tuple(new_states)

        heads = range(N_LIN)
        prim = (tuple(_head(qkv_ref, h) for h in heads),
                tuple(_head(qkv_ref, N_LIN + h) for h in heads),
                tuple(_head(qkv_ref, 2 * N_LIN + h) for h in heads),
                tuple(_head(gate_ref, h) for h in heads),
                tuple(st_ref[0, 0, h] for h in heads),
                small_ref[...], al_ref[...], dt_ref[...], gn_ref[...])
        _, vjp = jax.vjp(chunk, *prim)
        cot = (tuple(_head(dmix_ref, h) for h in heads), tuple(ds_scr[h] for h in heads))
        dq, dk, dv, dgate, dstate, dsmall, dal, ddt, dgn = vjp(cot)
        dqkv_ref[...] = jnp.concatenate(list(dq) + list(dk) + list(dv), axis=1)
        dgate_ref[...] = jnp.concatenate(list(dgate), axis=1).astype(BF16)
        dsmall_ref[...] = dsmall.astype(BF16)
        for h in heads:
            ds_scr[h] = dstate[h]
        dal_ref[...] += dal
        ddt_ref[...] += ddt
        dgn_ref[...] += dgn

    row = lambda b, n: b * nc + (nc - 1 - n)
    vec = pl.BlockSpec((1, 128), lambda b, n: (0, 0))
    return pl.pallas_call(
        body, name="gdn_bwd", grid=(n_batch, nc),
        in_specs=[pl.BlockSpec((CHUNK, MIX_W), lambda b, n: (row(b, n), 0)),
                  pl.BlockSpec((CHUNK, 3 * MIX_W), lambda b, n: (row(b, n), 0)),
                  pl.BlockSpec((CHUNK, 128), lambda b, n: (row(b, n), SMALL_COL)),
                  pl.BlockSpec((CHUNK, MIX_W), lambda b, n: (row(b, n), 3)),
                  pl.BlockSpec((1, 1, N_LIN, LIN_DH, LIN_DH), lambda b, n: (b, nc - 1 - n, 0, 0, 0)),
                  pl.BlockSpec((1, 1, N_LIN, CHUNK, CHUNK), lambda b, n: (b, nc - 1 - n, 0, 0, 0)),
                  vec, vec, vec],
        out_specs=[pl.BlockSpec((CHUNK, 3 * MIX_W), lambda b, n: (row(b, n), 0)),
                   pl.BlockSpec((CHUNK, MIX_W), lambda b, n: (row(b, n), 0)),
                   pl.BlockSpec((CHUNK, 128), lambda b, n: (row(b, n), 0)),
                   vec, vec, vec],
        out_shape=[jax.ShapeDtypeStruct((t, 3 * MIX_W), F32),
                   jax.ShapeDtypeStruct((t, MIX_W), BF16),
                   jax.ShapeDtypeStruct((t, 128), BF16),
                   jax.ShapeDtypeStruct((1, 128), F32),
                   jax.ShapeDtypeStruct((1, 128), F32),
                   jax.ShapeDtypeStruct((1, 128), F32)],
        scratch_shapes=[pltpu.VMEM((N_LIN, LIN_DH, LIN_DH), F32)],
        compiler_params=_cparams(("arbitrary", "arbitrary")),
    )(dcat, qkv, proj, proj, states, tinvs, alog_row, dtb_row, gain_row)


SB_T = 256


def _sb_masks():
    r, c = _iota((SB_T, SB_T), 0), _iota((SB_T, SB_T), 1)
    return r, c


def _sb_fwd(proj, n_batch):
    nq = SEQ // SB_T
    t = n_batch * SEQ
    scale = SB_DH ** -0.5

    def body(q_ref, k_ref, v_ref, o_ref, tot_ref, acc_scr, run_scr):
        qi = pl.program_id(2)
        lane = _iota((SB_T, 128), 1)
        r, c = _sb_masks()
        upper = (r > c).astype(BF16)
        acc_scr[...] = jnp.zeros_like(acc_scr)
        run_scr[...] = jnp.zeros_like(run_scr)
        q = q_ref[...]
        qm = [jnp.where((lane < SB_DH) == (hh == 0), q, 0.0).astype(BF16) for hh in range(2)]

        def step(it, carry):
            kb = qi - it
            ks = pl.ds(pl.multiple_of(kb * SB_T, SB_T), SB_T)
            k_blk = k_ref[ks, :].astype(BF16)
            v_blk = v_ref[ks, :].astype(BF16)
            valid = (kb * SB_T + c) < (qi * SB_T + r)
            for hh in range(2):
                z = lax.dot_general(qm[hh], k_blk, NT, preferred_element_type=F32) * scale
                lb = _log_sigmoid(z)
                l1m = jnp.where(valid, lb - z, 0.0)
                run = run_scr[hh][:, 0:1]
                tail = run + _dot_mask(l1m, upper)
                a = jnp.where(valid, jnp.exp(lb + tail), 0.0)
                acc_scr[hh] += lax.dot_general(a.astype(BF16), v_blk, NN, preferred_element_type=F32)
                run_scr[hh] += jnp.sum(l1m, axis=1, keepdims=True)
            return carry

        lax.fori_loop(0, qi + 1, step, 0)
        first = lane < SB_DH
        o_ref[...] = jnp.where(first, acc_scr[0], acc_scr[1]).astype(BF16)
        tot_ref[...] = jnp.where(first, run_scr[0], run_scr[1])

    nq_blocks = lambda b, p, i: (b * nq + i, p)
    seq_spec = lambda off: pl.BlockSpec((SEQ, 128), lambda b, p, i: (b, off + p))
    return pl.pallas_call(
        body, name="sb_fwd", grid=(n_batch, SB_PAIRS, nq),
        in_specs=[pl.BlockSpec((SB_T, 128), nq_blocks), seq_spec(SB_PAIRS), seq_spec(2 * SB_PAIRS)],
        out_specs=[pl.BlockSpec((SB_T, 128), nq_blocks), pl.BlockSpec((SB_T, 128), nq_blocks)],
        out_shape=[jax.ShapeDtypeStruct((t, MIX_W), BF16), jax.ShapeDtypeStruct((t, MIX_W), F32)],
        scratch_shapes=[pltpu.VMEM((2, SB_T, 128), F32), pltpu.VMEM((2, SB_T, 128), F32)],
        compiler_params=_cparams(("parallel", "parallel", "arbitrary")),
    )(proj, proj, proj)


def _sb_bwd(dcat, proj, totals, n_batch):
    nq = SEQ // SB_T
    t = n_batch * SEQ
    scale = SB_DH ** -0.5

    def body(do_ref, q_ref, k_ref, v_ref, tot_ref, dq_ref, dk_ref, dv_ref, dq_scr, run_scr, grun_scr):
        qi = pl.program_id(2)

        @pl.when(qi == 0)
        def _():
            dk_ref[...] = jnp.zeros_like(dk_ref)
            dv_ref[...] = jnp.zeros_like(dv_ref)

        lane = _iota((SB_T, 128), 1)
        r, c = _sb_masks()
        incl = (r <= c).astype(BF16)
        before = (r < c).astype(BF16)
        dq_scr[...] = jnp.zeros_like(dq_scr)
        run_scr[...] = jnp.zeros_like(run_scr)
        grun_scr[...] = jnp.zeros_like(grun_scr)
        q, do, tot = q_ref[...], do_ref[...], tot_ref[...]
        sel = [(lane < SB_DH) == (hh == 0) for hh in range(2)]
        qm = [jnp.where(sel[hh], q, 0.0).astype(BF16) for hh in range(2)]
        dom = [jnp.where(sel[hh], do, 0.0).astype(BF16) for hh in range(2)]
        total = [jnp.sum(jnp.where(lane == hh * SB_DH, tot, 0.0), axis=1, keepdims=True) for hh in range(2)]

        def step(kb, carry):
            ks = pl.ds(pl.multiple_of(kb * SB_T, SB_T), SB_T)
            k_blk = k_ref[ks, :].astype(BF16)
            v_blk = v_ref[ks, :].astype(BF16)
            valid = (kb * SB_T + c) < (qi * SB_T + r)
            dk_acc = jnp.zeros((SB_T, 128), F32)
            dv_acc = jnp.zeros((SB_T, 128), F32)
            for hh in range(2):
                z = lax.dot_general(qm[hh], k_blk, NT, preferred_element_type=F32) * scale
                lb = _log_sigmoid(z)
                sig = jnp.exp(lb)
                l1m = jnp.where(valid, lb - z, 0.0)
                prefix = run_scr[hh][:, 0:1] + _dot_mask(l1m, incl)
                a = jnp.where(valid, jnp.exp(lb + (total[hh] - prefix)), 0.0)
                da = lax.dot_general(dom[hh], v_blk, NT, preferred_element_type=F32)
                de = a * da
                dv_acc += lax.dot_general(a.astype(BF16), dom[hh], TN, preferred_element_type=F32)
                dl1m = jnp.where(valid, grun_scr[hh][:, 0:1] + _dot_mask(de, before), 0.0)
                dz = ((de * (1.0 - sig) - dl1m * sig) * scale).astype(BF16)
                dq_scr[hh] += lax.dot_general(dz, k_blk, NN, preferred_element_type=F32)
                dk_acc += lax.dot_general(dz, qm[hh], TN, preferred_element_type=F32)
                run_scr[hh] += jnp.sum(l1m, axis=1, keepdims=True)
                grun_scr[hh] += jnp.sum(de, axis=1, keepdims=True)
            dk_ref[ks, :] += dk_acc
            dv_ref[ks, :] += dv_acc
            return carry

        lax.fori_loop(0, qi + 1, step, 0)
        dq_ref[...] = jnp.where(sel[0], dq_scr[0], dq_scr[1]).astype(BF16)

    q_blocks = lambda b, p, i: (b * nq + i, p)
    seq_spec = lambda off: pl.BlockSpec((SEQ, 128), lambda b, p, i: (b, off + p))
    dq, dk, dv = pl.pallas_call(
        body, name="sb_bwd", grid=(n_batch, SB_PAIRS, nq),
        in_specs=[pl.BlockSpec((SB_T, 128), q_blocks), pl.BlockSpec((SB_T, 128), q_blocks),
                  seq_spec(SB_PAIRS), seq_spec(2 * SB_PAIRS), pl.BlockSpec((SB_T, 128), q_blocks)],
        out_specs=[pl.BlockSpec((SB_T, 128), q_blocks), seq_spec(0), seq_spec(0)],
        out_shape=[jax.ShapeDtypeStruct((t, MIX_W), BF16), jax.ShapeDtypeStruct((t, MIX_W), F32),
                   jax.ShapeDtypeStruct((t, MIX_W), F32)],
        scratch_shapes=[pltpu.VMEM((2, SB_T, 128), F32), pltpu.VMEM((2, SB_T, 128), F32),
                        pltpu.VMEM((2, SB_T, 128), F32)],
        compiler_params=_cparams(("parallel", "parallel", "arbitrary")),
    )(dcat, proj, proj, proj, totals)
    return dq, dk, dv


MEM_TQ = 512


def _mem_heads(lane):
    return [(lane >= X_HEAD_DIM * h) & (lane < X_HEAD_DIM * (h + 1)) for h in range(N_X_HEADS)]


def _mem_attn_fwd(proj, q_col, memkv, n_batch):
    nq = SEQ // MEM_TQ
    t = n_batch * SEQ
    scale = X_HEAD_DIM ** -0.5

    def body(q_ref, kv_ref, o_ref):
        q = q_ref[...]
        k = kv_ref[:, :X_WIDTH].astype(BF16)
        v = kv_ref[:, X_WIDTH:].astype(BF16)
        out = jnp.zeros((MEM_TQ, X_WIDTH), F32)
        for sel in _mem_heads(_iota((MEM_TQ, X_WIDTH), 1)):
            s = lax.dot_general(jnp.where(sel, q, 0.0).astype(BF16), k, NT, preferred_element_type=F32) * scale
            e = jnp.exp(s - jnp.max(s, axis=-1, keepdims=True))
            p = e / jnp.sum(e, axis=-1, keepdims=True)
            out = out + jnp.where(sel, lax.dot_general(p.astype(BF16), v, NN, preferred_element_type=F32), 0.0)
        o_ref[...] = out.astype(BF16)

    return pl.pallas_call(
        body, name="mem_attn_fwd", grid=(n_batch, nq),
        in_specs=[pl.BlockSpec((MEM_TQ, X_WIDTH), lambda b, i: (b * nq + i, q_col)),
                  pl.BlockSpec((N_MEM, 2 * X_WIDTH), lambda b, i: (b, 0))],
        out_specs=pl.BlockSpec((MEM_TQ, X_WIDTH), lambda b, i: (b * nq + i, 0)),
        out_shape=jax.ShapeDtypeStruct((t, X_WIDTH), BF16),
        compiler_params=_cparams(("parallel", "parallel")),
    )(proj, memkv)


def _mem_attn_bwd(dcat, proj, q_col, memkv, n_batch):
    nq = SEQ // MEM_TQ
    t = n_batch * SEQ
    scale = X_HEAD_DIM ** -0.5

    def body(do_ref, q_ref, kv_ref, dq_ref, dkv_ref):
        @pl.when(pl.program_id(1) == 0)
        def _():
            dkv_ref[...] = jnp.zeros_like(dkv_ref)

        q, do = q_ref[...], do_ref[...]
        k = kv_ref[:, :X_WIDTH].astype(BF16)
        v = kv_ref[:, X_WIDTH:].astype(BF16)
        dq = jnp.zeros((MEM_TQ, X_WIDTH), F32)
        dk = jnp.zeros((N_MEM, X_WIDTH), F32)
        dv = jnp.zeros((N_MEM, X_WIDTH), F32)
        for sel in _mem_heads(_iota((MEM_TQ, X_WIDTH), 1)):
            qm = jnp.where(sel, q, 0.0).astype(BF16)
            dom = jnp.where(sel, do, 0.0).astype(BF16)
            s = lax.dot_general(qm, k, NT, preferred_element_type=F32) * scale
            e = jnp.exp(s - jnp.max(s, axis=-1, keepdims=True))
            p = e / jnp.sum(e, axis=-1, keepdims=True)
            dp = lax.dot_general(dom, v, NT, preferred_element_type=F32)
            ds = ((p * (dp - jnp.sum(dp * p, axis=-1, keepdims=True))) * scale).astype(BF16)
            dv = dv + lax.dot_general(p.astype(BF16), dom, TN, preferred_element_type=F32)
            dk = dk + lax.dot_general(ds, qm, TN, preferred_element_type=F32)
            dq = dq + jnp.where(sel, lax.dot_general(ds, k, NN, preferred_element_type=F32), 0.0)
        dq_ref[...] = dq.astype(BF16)
        dkv_ref[...] += jnp.concatenate([dk, dv], axis=1)

    return pl.pallas_call(
        body, name="mem_attn_bwd", grid=(n_batch, nq),
        in_specs=[pl.BlockSpec((MEM_TQ, X_WIDTH), lambda b, i: (b * nq + i, 3)),
                  pl.BlockSpec((MEM_TQ, X_WIDTH), lambda b, i: (b * nq + i, q_col)),
                  pl.BlockSpec((N_MEM, 2 * X_WIDTH), lambda b, i: (b, 0))],
        out_specs=[pl.BlockSpec((MEM_TQ, X_WIDTH), lambda b, i: (b * nq + i, 0)),
                   pl.BlockSpec((N_MEM, 2 * X_WIDTH), lambda b, i: (b, 0))],
        out_shape=[jax.ShapeDtypeStruct((t, X_WIDTH), BF16),
                   jax.ShapeDtypeStruct((n_batch * N_MEM, 2 * X_WIDTH), F32)],
        compiler_params=_cparams(("parallel", "arbitrary")),
    )(dcat, proj, memkv)


def _relu2_epilogue(acc):
    r = jnp.maximum(acc, 0.0)
    return r * r, acc


def _relu2_bwd_epilogue(acc, u):
    return (acc * (2.0 * jnp.maximum(u.astype(F32), 0.0)),)


def _pad_in_a(w_in_a):
    w = 3 * MIX_W
    parts = [w_in_a[:, :w], w_in_a[:, w:w + MIX_W], w_in_a[:, IN_A - X_WIDTH:],
             w_in_a[:, w + MIX_W:w + MIX_W + 2 * N_LIN]]
    pad = jnp.zeros((D_MODEL, IN_A_PAD - IN_A), w_in_a.dtype)
    return jnp.concatenate(parts + [pad], axis=1)


def _unpad_in_a(g):
    w = 3 * MIX_W
    return jnp.concatenate([g[:, :w + MIX_W], g[:, w + MIX_W + X_WIDTH:w + MIX_W + X_WIDTH + 2 * N_LIN],
                            g[:, w + MIX_W:w + MIX_W + X_WIDTH]], axis=1)


def _local_step(x, mem, target, wts, small):
    t = x.shape[0]
    nb = t // SEQ
    npre, npost, mpre, mpost = small["norm_pre_mix"], small["norm_post_mix"], small["norm_pre_mlp"], small["norm_post_mlp"]
    alog_row, dtb_row, gain_row = _gdn_param_rows(small["a_log_a"][0], small["dt_bias_a"][0], small["onorm_a"][0])
    conv_w = small["conv_w"]

    mem_n = _rms_fwd(mem, small["mem_norm"], name="mem_norm_fwd", tile=256)
    saved = []
    h = _rms_fwd(x, npre[0], name="pre_mix_norm0")
    for i in range(DEPTH):
        s = {"x_in": x, "h1": h}
        if i == 0:
            proj = _matmul(h, wts["in_a_pad"], mode="nn", tm=512, tn=1152, tk=1024, name="proj_a")
            qkv = _conv_silu_fwd(proj, conv_w, nb)
            mix, states, tinvs = _gdn_fwd(qkv, proj, alog_row, dtb_row, gain_row, nb)
            s.update(qkv=qkv, states=states, tinvs=tinvs)
            q_col = (3 * MIX_W + MIX_W) // X_WIDTH
        else:
            proj = _matmul(h, wts["in_b"], mode="nn", tm=512, tn=1280, tk=1024, name="proj_b")
            mix, totals = _sb_fwd(proj, nb)
            s.update(totals=totals)
            q_col = 3 * MIX_W // X_WIDTH
        memkv = _matmul(mem_n, wts["mem_kv"][i], mode="nn", tm=256, tn=512, tk=1024, name=f"memkv{i}")
        cross = _mem_attn_fwd(proj, q_col, memkv, nb)
        cat = jnp.concatenate([mix, cross], axis=1)
        y = _matmul(cat, wts["out"][i], mode="nn", tm=512, tn=1024, tk=1024, name=f"out_proj{i}")
        x2, h2 = _post_norm_add(x, y, npost[i], mpre[i], name=f"post_mix{i}")
        a, u = _matmul(h2, wts["up"][i], mode="nn", tm=512, tn=1024, tk=1024, name=f"up{i}",
                       out_dtypes=(BF16, BF16), epilogue=_relu2_epilogue)
        y2 = _matmul(a, wts["down"][i], mode="nn", tm=512, tn=1024, tk=1024, name=f"down{i}")
        s.update(proj=proj, q_col=q_col, memkv=memkv, cat=cat, y=y, x2=x2, h2=h2, a=a, u=u, y2=y2)
        saved.append(s)
        if i + 1 < DEPTH:
            x, h = _post_norm_add(x2, y2, mpost[i], npre[i + 1], name=f"post_mlp{i}")
        else:
            loss_row, dx = _post_norm_loss(x2, y2, mpost[i], target, name="loss_head")

    gw = {"mem_kv": [None] * DEPTH, "out": [None] * DEPTH, "up": [None] * DEPTH, "down": [None] * DEPTH}
    gs = {k: [None] * DEPTH for k in ("norm_pre_mix", "norm_post_mix", "norm_pre_mlp", "norm_post_mlp")}
    dmem_n = None
    for i in reversed(range(DEPTH)):
        s = saved[i]
        dy2, gs["norm_post_mlp"][i] = _rms_bwd(dx, s["y2"], mpost[i], name=f"post_mlp_bwd{i}", out_dtype=BF16)
        du = _matmul(dy2, wts["down"][i], mode="nt", tm=512, tn=1024, tk=1024, name=f"down_dx{i}",
                     out_dtypes=(BF16,), epilogue=_relu2_bwd_epilogue, extras=(s["u"],))
        gw["down"][i] = _matmul(s["a"], dy2, mode="tn", tm=1024, tn=1024, tk=512, name=f"down_dw{i}")
        dh2 = _matmul(du, wts["up"][i], mode="nt", tm=512, tn=1024, tk=1024, name=f"up_dx{i}")
        gw["up"][i] = _matmul(s["h2"], du, mode="tn", tm=1024, tn=1024, tk=512, name=f"up_dw{i}")
        dx2, gs["norm_pre_mlp"][i] = _rms_bwd(dh2, s["x2"], mpre[i], name=f"pre_mlp_bwd{i}", res=dx)
        dy, gs["norm_post_mix"][i] = _rms_bwd(dx2, s["y"], npost[i], name=f"post_mix_bwd{i}", out_dtype=BF16)
        dcat = _matmul(dy, wts["out"][i], mode="nt", tm=512, tn=1024, tk=1024, name=f"out_dx{i}")
        gw["out"][i] = _matmul(s["cat"], dy, mode="tn", tm=1024, tn=1024, tk=512, name=f"out_dw{i}")
        dmemq, dmemkv = _mem_attn_bwd(dcat, s["proj"], s["q_col"], s["memkv"], nb)
        dmemkv = dmemkv.astype(BF16)
        gw["mem_kv"][i] = _matmul(mem_n, dmemkv, mode="tn", tm=1024, tn=512, tk=256, name=f"memkv_dw{i}")
        dmn = _matmul(dmemkv, wts["mem_kv"][i], mode="nt", tm=256, tn=1024, tk=512, name=f"memkv_dx{i}")
        dmem_n = dmn if dmem_n is None else dmem_n + dmn
        if i == 0:
            dqkv, dgate, dsmall, dalog, ddtb, dgain = _gdn_bwd(
                dcat, s["qkv"], s["proj"], s["states"], s["tinvs"], alog_row, dtb_row, gain_row, nb)
            dqkv_pre, dconv = _conv_silu_bwd(dqkv, s["proj"], conv_w, nb)
            dproj = jnp.concatenate([dqkv_pre, dgate, dmemq, dsmall], axis=1)
            w_in, tile = wts["in_a_pad"], 1152
        else:
            dq, dk, dv = _sb_bwd(dcat, s["proj"], s["totals"], nb)
            dproj = jnp.concatenate([dq, dk.astype(BF16), dv.astype(BF16), dmemq], axis=1)
            w_in, tile = wts["in_b"], 1280
        dh1 = _matmul(dproj, w_in, mode="nt", tm=512, tn=1024, tk=tile, name=f"proj_dx{i}")
        g_in = _matmul(s["h1"], dproj, mode="tn", tm=1024, tn=tile, tk=512, name=f"proj_dw{i}")
        if i == 0:
            gw["in_a"] = _unpad_in_a(g_in)
        else:
            gw["in_b"] = g_in
        dx, gs["norm_pre_mix"][i] = _rms_bwd(dh1, s["x_in"], npre[i], name=f"pre_mix_bwd{i}", res=dx2)

    _, g_mem_norm = _rms_bwd(dmem_n, mem, small["mem_norm"], name="mem_norm_bwd", tile=256)
    gsmall = {k: jnp.concatenate(v, axis=0) for k, v in gs.items()}
    gsmall.update(mem_norm=g_mem_norm[0], a_log_a=dalog[:, N_LIN:2 * N_LIN], dt_bias_a=ddtb[:, N_LIN:2 * N_LIN],
                  onorm_a=dgain, conv_w=dconv)
    gw = {k: (jnp.stack(v) if isinstance(v, list) else v) for k, v in gw.items()}
    return loss_row[0, 0], dx, gw, gsmall


PACK_ROWS = (("in_a", 835), ("in_b", 640), ("mem_kv", 256), ("out", 512), ("up", 2048), ("down", 2048))
R_PACK = 6400
R_HALF = R_PACK // 2
SUM_TILE = 640
ANY = pl.BlockSpec(memory_space=pl.ANY)


def _position():
    x, y, c = lax.axis_index("x"), lax.axis_index("y"), lax.axis_index("c")
    others = [(1 - x, y), (x, 1 - y), (1 - x, 1 - y)]
    return x, y, c, others


def _gather_chips(wflat):
    def body(w_ref, out_ref, send_sems, recv_sems, local_sem):
        x, y, c, others = _position()
        me = 2 * x + y

        def copy(k, src, dst, to):
            return pltpu.make_async_remote_copy(src_ref=src, dst_ref=dst, send_sem=send_sems.at[k],
                                                recv_sem=recv_sems.at[k], device_id=to, device_id_type=MESH)

        mine = pltpu.make_async_copy(w_ref, out_ref.at[me], local_sem)
        mine.start()
        first = [copy(j, w_ref.at[c], out_ref.at[me, c], (ox, oy, c)) for j, (ox, oy) in enumerate(others)]
        for cp in first:
            cp.start()
        passed = [copy(3 + j, out_ref.at[2 * ox + oy, c], out_ref.at[2 * ox + oy, c], (x, y, 1 - c))
                  for j, (ox, oy) in enumerate(others)]
        for j, (ox, oy) in enumerate(others):
            copy(j, w_ref.at[c], out_ref.at[2 * ox + oy, c], (x, y, c)).wait_recv()
            passed[j].start()
        for j, (ox, oy) in enumerate(others):
            copy(3 + j, w_ref.at[c], out_ref.at[2 * ox + oy, 1 - c], (x, y, c)).wait_recv()
        for cp in first + passed:
            cp.wait_send()
        mine.wait()

    return pl.pallas_call(
        body, name="gather_weights",
        in_specs=[ANY], out_specs=ANY,
        out_shape=jax.ShapeDtypeStruct((N_CHIPS, 2, R_HALF, D_MODEL), wflat.dtype),
        scratch_shapes=[pltpu.SemaphoreType.DMA((6,)), pltpu.SemaphoreType.DMA((6,)), pltpu.SemaphoreType.DMA],
    )(wflat)


def _gather_all(v, *, name):
    rows, n = v.shape

    def body(x_ref, out_ref, send_sems, recv_sems, local_sem):
        x, y, c, others = _position()
        me, sibling = (x, y, c), (x, y, 1 - c)

        def blk(px, py, pc):
            return out_ref.at[pl.ds((4 * px + 2 * py + pc) * rows, rows), :]

        def copy(k, block, to, src=None):
            return pltpu.make_async_remote_copy(src_ref=blk(*block) if src is None else src, dst_ref=blk(*block),
                                                send_sem=send_sems.at[k], recv_sem=recv_sems.at[k],
                                                device_id=to, device_id_type=MESH)

        mine = pltpu.make_async_copy(x_ref, blk(*me), local_sem)
        mine.start()
        first = [copy(0, me, sibling, src=x_ref)]
        first += [copy(1 + j, me, (*chip, c), src=x_ref) for j, chip in enumerate(others)]
        for cp in first:
            cp.start()
        passed = [copy(4 + j, (*chip, c), sibling) for j, chip in enumerate(others)]
        for j, chip in enumerate(others):
            copy(1 + j, (*chip, c), me).wait_recv()
            passed[j].start()
        copy(0, sibling, me).wait_recv()
        for j, chip in enumerate(others):
            copy(4 + j, (*chip, 1 - c), me).wait_recv()
        for cp in first + passed:
            cp.wait_send()
        mine.wait()

    vmem = pl.BlockSpec(memory_space=pltpu.VMEM)
    return pl.pallas_call(
        body, name=name, in_specs=[vmem], out_specs=vmem,
        out_shape=jax.ShapeDtypeStruct((8 * rows, n), v.dtype),
        scratch_shapes=[pltpu.SemaphoreType.DMA((7,)), pltpu.SemaphoreType.DMA((7,)), pltpu.SemaphoreType.DMA],
    )(v)


def _swap_halves(g5):
    def body(g_ref, out_ref, send_sem, recv_sem):
        x, y, c, _ = _position()
        cp = pltpu.make_async_remote_copy(src_ref=g_ref.at[:, 1 - c], dst_ref=out_ref, send_sem=send_sem,
                                          recv_sem=recv_sem, device_id=(x, y, 1 - c), device_id_type=MESH)
        cp.start()
        cp.wait()

    return pl.pallas_call(
        body, name="grad_swap_halves", in_specs=[ANY], out_specs=ANY,
        out_shape=jax.ShapeDtypeStruct((N_CHIPS, R_HALF, D_MODEL), g5.dtype),
        scratch_shapes=[pltpu.SemaphoreType.DMA, pltpu.SemaphoreType.DMA],
    )(g5)


def _add_halves(core, g5, got):
    def body(c_ref, a_ref, b_ref, o_ref):
        o_ref[...] = a_ref[0] + b_ref[...]

    nt = R_HALF // SUM_TILE
    return pl.pallas_call(
        body, name="grad_add_halves",
        grid_spec=pltpu.PrefetchScalarGridSpec(
            num_scalar_prefetch=1, grid=(N_CHIPS, nt),
            in_specs=[pl.BlockSpec((1, 1, SUM_TILE, D_MODEL), lambda s, i, c_ref: (s, c_ref[0], i, 0)),
                      pl.BlockSpec((1, SUM_TILE, D_MODEL), lambda s, i, c_ref: (s, i, 0))],
            out_specs=pl.BlockSpec((1, SUM_TILE, D_MODEL), lambda s, i, c_ref: (s, i, 0))),
        out_shape=jax.ShapeDtypeStruct((N_CHIPS, R_HALF, D_MODEL), F32),
        compiler_params=_cparams(("parallel", "parallel")),
    )(core, g5, got)


def _exchange_chips(p):
    def body(p_ref, q_ref, send_sems, recv_sems):
        x, y, c, others = _position()
        copies = [pltpu.make_async_remote_copy(src_ref=p_ref.at[2 * ox + oy], dst_ref=q_ref.at[j],
                                               send_sem=send_sems.at[j], recv_sem=recv_sems.at[j],
                                               device_id=(ox, oy, c), device_id_type=MESH)
                  for j, (ox, oy) in enumerate(others)]
        for cp in copies:
            cp.start()
        for cp in copies:
            cp.wait()

    return pl.pallas_call(
        body, name="grad_exchange_chips", in_specs=[ANY], out_specs=ANY,
        out_shape=jax.ShapeDtypeStruct((3, R_HALF, D_MODEL), p.dtype),
        scratch_shapes=[pltpu.SemaphoreType.DMA((3,)), pltpu.SemaphoreType.DMA((3,))],
    )(p)


def _add_chips(chip, p, q):
    def body(k_ref, p_ref, q_ref, o_ref):
        o_ref[...] = ((p_ref[0] + q_ref[0]) + q_ref[1]) + q_ref[2]

    nt = R_HALF // SUM_TILE
    return pl.pallas_call(
        body, name="grad_add_chips",
        grid_spec=pltpu.PrefetchScalarGridSpec(
            num_scalar_prefetch=1, grid=(nt,),
            in_specs=[pl.BlockSpec((1, SUM_TILE, D_MODEL), lambda i, k_ref: (k_ref[0], i, 0)),
                      pl.BlockSpec((3, SUM_TILE, D_MODEL), lambda i, k_ref: (0, i, 0))],
            out_specs=pl.BlockSpec((SUM_TILE, D_MODEL), lambda i, k_ref: (i, 0))),
        out_shape=jax.ShapeDtypeStruct((R_HALF, D_MODEL), F32),
        compiler_params=_cparams(("parallel",)),
    )(chip, p, q)


def _share_halves(half):
    def body(h_ref, out_ref, send_sem, recv_sem, local_sem):
        x, y, c, _ = _position()
        mine = pltpu.make_async_copy(h_ref, out_ref.at[c], local_sem)
        mine.start()
        cp = pltpu.make_async_remote_copy(src_ref=h_ref, dst_ref=out_ref.at[c], send_sem=send_sem,
                                          recv_sem=recv_sem, device_id=(x, y, 1 - c), device_id_type=MESH)
        cp.start()
        pltpu.make_async_remote_copy(src_ref=h_ref, dst_ref=out_ref.at[1 - c], send_sem=send_sem,
                                     recv_sem=recv_sem, device_id=(x, y, c), device_id_type=MESH).wait_recv()
        cp.wait_send()
        mine.wait()

    return pl.pallas_call(
        body, name="grad_share_halves", in_specs=[ANY], out_specs=ANY,
        out_shape=jax.ShapeDtypeStruct((2, R_HALF, D_MODEL), half.dtype),
        scratch_shapes=[pltpu.SemaphoreType.DMA, pltpu.SemaphoreType.DMA, pltpu.SemaphoreType.DMA],
    )(half)


def _reduce_scatter(g_packed, chip, core):
    g5 = g_packed.reshape(N_CHIPS, 2, R_HALF, D_MODEL)
    p = _add_halves(core, g5, _swap_halves(g5))
    half = _add_chips(chip, p, _exchange_chips(p))
    return _share_halves(half).reshape(R_PACK, D_MODEL)


def _pack_shard(parts):
    rows = [parts[name].reshape(n, D_MODEL) for name, n in PACK_ROWS]
    used = sum(n for _, n in PACK_ROWS)
    return jnp.concatenate(rows + [jnp.zeros((R_PACK - used, D_MODEL), rows[0].dtype)], axis=0)


def _unpack_shard(flat):
    shapes = {"in_a": (1, D_MODEL, IN_A // N_CHIPS), "in_b": (1, D_MODEL, IN_B // N_CHIPS),
              "mem_kv": (DEPTH, D_MODEL // N_CHIPS, 2 * X_WIDTH), "out": (DEPTH, D_MODEL // N_CHIPS, D_MODEL),
              "up": (DEPTH, D_MODEL, D_FF // N_CHIPS), "down": (DEPTH, D_FF // N_CHIPS, D_MODEL)}
    out, off = {}, 0
    for name, n in PACK_ROWS:
        out[name] = flat[off:off + n].reshape(shapes[name])
        off += n
    return out


def _unpack_gathered(g):
    out, off = {}, 0
    for name, n in PACK_ROWS:
        out[name] = g[:, off:off + n]
        off += n
    s = N_CHIPS
    full = {
        "in_a": out["in_a"].reshape(s, D_MODEL, IN_A // s).transpose(1, 0, 2).reshape(D_MODEL, IN_A),
        "in_b": out["in_b"].reshape(s, D_MODEL, IN_B // s).transpose(1, 0, 2).reshape(D_MODEL, IN_B),
        "mem_kv": out["mem_kv"].reshape(s, DEPTH, D_MODEL // s, 2 * X_WIDTH).transpose(1, 0, 2, 3).reshape(DEPTH, D_MODEL, 2 * X_WIDTH),
        "out": out["out"].reshape(s, DEPTH, D_MODEL // s, D_MODEL).transpose(1, 0, 2, 3).reshape(DEPTH, D_MODEL, D_MODEL),
        "up": out["up"].reshape(s, DEPTH, D_MODEL, D_FF // s).transpose(1, 2, 0, 3).reshape(DEPTH, D_MODEL, D_FF),
        "down": out["down"].reshape(s, DEPTH, D_FF // s, D_MODEL).transpose(1, 0, 2, 3).reshape(DEPTH, D_FF, D_MODEL),
    }
    full["in_a_pad"] = _pad_in_a(full.pop("in_a"))
    return full


def _pack_full_grads(gw):
    s = N_CHIPS
    parts = [
        gw["in_a"].reshape(D_MODEL, s, IN_A // s).transpose(1, 0, 2).reshape(s, -1, D_MODEL),
        gw["in_b"].reshape(D_MODEL, s, IN_B // s).transpose(1, 0, 2).reshape(s, -1, D_MODEL),
        gw["mem_kv"].reshape(DEPTH, s, D_MODEL // s, 2 * X_WIDTH).transpose(1, 0, 2, 3).reshape(s, -1, D_MODEL),
        gw["out"].reshape(DEPTH, s, D_MODEL // s, D_MODEL).transpose(1, 0, 2, 3).reshape(s, -1, D_MODEL),
        gw["up"].reshape(DEPTH, D_MODEL, s, D_FF // s).transpose(2, 0, 1, 3).reshape(s, -1, D_MODEL),
        gw["down"].reshape(DEPTH, s, D_FF // s, D_MODEL).transpose(1, 0, 2, 3).reshape(s, -1, D_MODEL),
    ]
    used = sum(n for _, n in PACK_ROWS)
    return jnp.concatenate(parts + [jnp.zeros((s, R_PACK - used, D_MODEL), F32)], axis=1)


def _adamw_math(w, g, m, v):
    m = ADAM_B1 * m + (1.0 - ADAM_B1) * g
    v = ADAM_B2 * v + (1.0 - ADAM_B2) * (g * g)
    m_hat = m / (1.0 - ADAM_B1 ** ADAM_STEP)
    v_hat = v / (1.0 - ADAM_B2 ** ADAM_STEP)
    delta = -ADAM_LR * (m_hat / (jnp.sqrt(v_hat) + ADAM_EPS) + ADAM_WD * w)
    return delta, m, v


ADAM_TILE = 256


def _adamw(w, g, m, v, *, name):
    shape = w.shape
    cols = shape[-1]
    rows = w.size // cols
    tile = min(rows, ADAM_TILE)
    assert rows % tile == 0, (name, shape)

    def body(w_ref, g_ref, m_ref, v_ref, d_ref, nm_ref, nv_ref):
        d_ref[...], nm_ref[...], nv_ref[...] = _adamw_math(w_ref[...], g_ref[...], m_ref[...], v_ref[...])

    spec = pl.BlockSpec((tile, cols), lambda i: (i, 0))
    outs = pl.pallas_call(
        body, name=name, grid=(rows // tile,), in_specs=[spec] * 4, out_specs=[spec] * 3,
        out_shape=[jax.ShapeDtypeStruct((rows, cols), F32)] * 3,
        compiler_params=_cparams(("parallel",)),
    )(*[a.reshape(rows, cols) for a in (w, g, m, v)])
    return [o.reshape(shape) for o in outs]


SMALL_NAMES = (("mem_norm", 8), ("norm_pre_mix", 16), ("norm_post_mix", 16), ("norm_pre_mlp", 16),
               ("norm_post_mlp", 16), ("a_log_a", 1), ("dt_bias_a", 1), ("onorm_a", 1))
SMALL_ROWS = 80
CONV_ROWS = CONV_K * 3 * MIX_W // 128
SMALL_GRAD_ROWS = SMALL_ROWS + CONV_ROWS


def _pack_small(vals):
    rows = []
    for name, n in SMALL_NAMES:
        flat = vals[name].reshape(-1)
        rows.append(jnp.pad(flat, (0, n * 128 - flat.size)).reshape(n, 128))
    used = sum(n for _, n in SMALL_NAMES)
    return jnp.concatenate(rows + [jnp.zeros((SMALL_ROWS - used, 128), F32)], axis=0)


def _unpack_small(packed, like):
    out, off = {}, 0
    for name, n in SMALL_NAMES:
        size = like[name].size
        out[name] = packed[off:off + n].reshape(-1)[:size].reshape(like[name].shape)
        off += n
    return out


def _small_update(gathered, w, m, v):
    def body(g_ref, w_ref, m_ref, v_ref, gs_ref, d_ref, nm_ref, nv_ref):
        g = g_ref[0]
        for dev in range(1, 8):
            g = g + g_ref[dev]
        gs_ref[...] = g
        d_ref[...], nm_ref[...], nv_ref[...] = _adamw_math(w_ref[...], g[:SMALL_ROWS], m_ref[...], v_ref[...])

    small = jax.ShapeDtypeStruct((SMALL_ROWS, 128), F32)
    return pl.pallas_call(
        body, name="small_update",
        out_shape=[jax.ShapeDtypeStruct((SMALL_GRAD_ROWS, 128), F32), small, small, small],
    )(gathered.reshape(8, SMALL_GRAD_ROWS, 128), w, m, v)


def kernel(x, mem, mem_norm, norm_pre_mix, norm_post_mix, norm_pre_mlp, norm_post_mlp, w_in_a, conv_w_a, a_log_a, dt_bias_a, onorm_a, w_in_b, w_mem_kv, w_out, w_up, w_down, loss_target, m_mem_norm, m_norm_pre_mix, m_norm_post_mix, m_norm_pre_mlp, m_norm_post_mlp, m_w_in_a, m_conv_w_a, m_a_log_a, m_dt_bias_a, m_onorm_a, m_w_in_b, m_w_mem_kv, m_w_out, m_w_up, m_w_down, v_mem_norm, v_norm_pre_mix, v_norm_post_mix, v_norm_pre_mlp, v_norm_post_mlp, v_w_in_a, v_conv_w_a, v_a_log_a, v_dt_bias_a, v_onorm_a, v_w_in_b, v_w_mem_kv, v_w_out, v_w_up, v_w_down):
    nb = x.shape[0]
    chip = (2 * lax.axis_index("x") + lax.axis_index("y")).astype(jnp.int32)
    core = lax.axis_index("c").astype(jnp.int32)
    shards = {"in_a": w_in_a, "in_b": w_in_b, "mem_kv": w_mem_kv, "out": w_out, "up": w_up, "down": w_down}
    moments_m = {"in_a": m_w_in_a, "in_b": m_w_in_b, "mem_kv": m_w_mem_kv, "out": m_w_out, "up": m_w_up, "down": m_w_down}
    moments_v = {"in_a": v_w_in_a, "in_b": v_w_in_b, "mem_kv": v_w_mem_kv, "out": v_w_out, "up": v_w_up, "down": v_w_down}
    small_w = {"mem_norm": mem_norm, "norm_pre_mix": norm_pre_mix, "norm_post_mix": norm_post_mix,
               "norm_pre_mlp": norm_pre_mlp, "norm_post_mlp": norm_post_mlp, "a_log_a": a_log_a,
               "dt_bias_a": dt_bias_a, "onorm_a": onorm_a}
    small_m = {"mem_norm": m_mem_norm, "norm_pre_mix": m_norm_pre_mix, "norm_post_mix": m_norm_post_mix,
               "norm_pre_mlp": m_norm_pre_mlp, "norm_post_mlp": m_norm_post_mlp, "a_log_a": m_a_log_a,
               "dt_bias_a": m_dt_bias_a, "onorm_a": m_onorm_a}
    small_v = {"mem_norm": v_mem_norm, "norm_pre_mix": v_norm_pre_mix, "norm_post_mix": v_norm_post_mix,
               "norm_pre_mlp": v_norm_pre_mlp, "norm_post_mlp": v_norm_post_mlp, "a_log_a": v_a_log_a,
               "dt_bias_a": v_dt_bias_a, "onorm_a": v_onorm_a}

    packed = _pack_shard({k: w.astype(BF16) for k, w in shards.items()})
    gathered = _gather_chips(packed.reshape(2, R_HALF, D_MODEL)).reshape(N_CHIPS, R_PACK, D_MODEL)
    wts = _unpack_gathered(gathered)
    conv_rows = CONV_ROWS // N_CHIPS
    conv_blk = jnp.pad(conv_w_a.reshape(conv_rows, 128), ((0, 24 - conv_rows), (0, 0)))
    conv_all = _gather_all(conv_blk, name="gather_conv").reshape(N_CHIPS, 2, 24, 128)[:, 0, :conv_rows]
    conv_full = conv_all.reshape(N_CHIPS, CONV_K, 3 * MIX_W // N_CHIPS).transpose(1, 0, 2).reshape(CONV_K, 3 * MIX_W)

    loss_local, dx, gw, gsmall = _local_step(
        x.reshape(nb * SEQ, D_MODEL), mem.reshape(nb * N_MEM, D_MODEL), loss_target.reshape(nb * SEQ, D_MODEL),
        wts, dict(small_w, conv_w=conv_full))
    loss = lax.psum(loss_local, ("x", "y", "c"))
    grad_x = dx.reshape(nb, SEQ, D_MODEL)

    g_shard = _unpack_shard(_reduce_scatter(_pack_full_grads(gw), chip.reshape(1), core.reshape(1)))
    upd = {k: _adamw(shards[k], g_shard[k], moments_m[k], moments_v[k], name=f"adamw_{k}") for k in shards}

    g_rows = jnp.concatenate([_pack_small(gsmall), gsmall["conv_w"].reshape(CONV_ROWS, 128)], axis=0)
    g_all = _gather_all(g_rows, name="gather_small_grads")
    g_sum, d_small, nm_small, nv_small = _small_update(g_all, _pack_small(small_w), _pack_small(small_m), _pack_small(small_v))
    gs = _unpack_small(g_sum, small_w)
    ds, nms, nvs = (_unpack_small(p, small_w) for p in (d_small, nm_small, nv_small))
    cw = 3 * MIX_W // N_CHIPS
    g_conv = lax.dynamic_slice(g_sum[SMALL_ROWS:].reshape(CONV_K, 3 * MIX_W), (0, chip * cw), (CONV_K, cw)).reshape(conv_w_a.shape)
    d_conv, nm_conv, nv_conv = _adamw(conv_w_a, g_conv, m_conv_w_a, v_conv_w_a, name="adamw_conv")

    order = ("mem_norm", "norm_pre_mix", "norm_post_mix", "norm_pre_mlp", "norm_post_mlp", "in_a", "conv", "a_log_a",
             "dt_bias_a", "onorm_a", "in_b", "mem_kv", "out", "up", "down")
    grads = dict(gs, conv=g_conv, **g_shard)
    deltas = dict(ds, conv=d_conv, **{k: u[0] for k, u in upd.items()})
    new_m = dict(nms, conv=nm_conv, **{k: u[1] for k, u in upd.items()})
    new_v = dict(nvs, conv=nv_conv, **{k: u[2] for k, u in upd.items()})
    return (loss, grad_x, *[grads[k] for k in order], *[deltas[k] for k in order],
            *[new_m[k] for k in order], *[new_v[k] for k in order])
```

```python
import functools

import jax
import jax.numpy as jnp
from jax import lax
from jax.experimental import pallas as pl
from jax.experimental.pallas import tpu as pltpu

F32 = jnp.float32
BF16 = jnp.bfloat16
HIGHEST = lax.Precision.HIGHEST
MESH = pl.DeviceIdType.MESH

D_MODEL = 1024
SEQ = 2048
DEPTH = 2
X_WIDTH = 256
N_X_HEADS = 4
X_HEAD_DIM = 64
MIX_W = 768
LIN_DH = 128
N_LIN = 6
CONV_K = 4
CHUNK = 64
SB_DH = 64
SB_PAIRS = 6
N_MEM = 256
D_FF = 4096
EPS = 1e-6
IN_A = 3340
IN_A_PAD = 3456
IN_B = 2560
SMALL_COL = 26
N_CHIPS = 4

ADAM_LR, ADAM_B1, ADAM_B2, ADAM_EPS, ADAM_WD, ADAM_STEP = 0.001, 0.9, 0.999, 1e-08, 0.01, 10

VMEM_LIMIT = 48 * 1024 * 1024

NN = (((1,), (0,)), ((), ()))
NT = (((1,), (1,)), ((), ()))
TN = (((0,), (0,)), ((), ()))


def _cparams(sem):
    return pltpu.CompilerParams(dimension_semantics=sem, vmem_limit_bytes=VMEM_LIMIT)


def _dotbf(a, b, dn=NN):
    return lax.dot_general(a.astype(BF16), b.astype(BF16), dn, preferred_element_type=F32)


def _split(a):
    hi = a.astype(BF16)
    lo = (a - hi.astype(F32)).astype(BF16)
    return hi, lo


def _dot3(a, b, dn=NN):
    ah, al = _split(a)
    bh, bl = _split(b)
    d = functools.partial(lax.dot_general, dimension_numbers=dn, preferred_element_type=F32)
    return d(ah, bh) + (d(ah, bl) + d(al, bh))


def _dot_mask(a, m01):
    ah, al = _split(a)
    d = functools.partial(lax.dot_general, dimension_numbers=NN, preferred_element_type=F32)
    return d(ah, m01) + d(al, m01)


def _iota(shape, dim):
    return lax.broadcasted_iota(jnp.int32, shape, dim)


def _softplus(x):
    return jnp.maximum(x, 0.0) + jnp.log(1.0 + jnp.exp(-jnp.abs(x)))


def _log_sigmoid(z):
    return jnp.minimum(z, 0.0) - jnp.log(1.0 + jnp.exp(-jnp.abs(z)))


def _rms(x, g):
    r = lax.rsqrt(jnp.mean(x * x, axis=-1, keepdims=True) + EPS)
    return (x * r) * g


def _matmul(a, b, *, mode, tm, tn, tk, name, out_dtypes=(F32,), epilogue=None, extras=()):
    if mode == "nn":
        (m, k), (k2, n) = a.shape, b.shape
        a_spec = pl.BlockSpec((tm, tk), lambda i, j, kk: (i, kk))
        b_spec = pl.BlockSpec((tk, tn), lambda i, j, kk: (kk, j))
        dn = NN
    elif mode == "nt":
        (m, k), (n, k2) = a.shape, b.shape
        a_spec = pl.BlockSpec((tm, tk), lambda i, j, kk: (i, kk))
        b_spec = pl.BlockSpec((tn, tk), lambda i, j, kk: (j, kk))
        dn = NT
    else:
        (k, m), (k2, n) = a.shape, b.shape
        a_spec = pl.BlockSpec((tk, tm), lambda i, j, kk: (kk, i))
        b_spec = pl.BlockSpec((tk, tn), lambda i, j, kk: (kk, j))
        dn = TN
    assert k == k2 and m % tm == 0 and n % tn == 0 and k % tk == 0, (name, a.shape, b.shape)
    assert a.dtype == BF16 and b.dtype == BF16, name
    nk = k // tk
    n_extra, n_out = len(extras), len(out_dtypes)

    def body(a_ref, b_ref, *rest):
        extra_refs, out_refs, acc_ref = rest[:n_extra], rest[n_extra:n_extra + n_out], rest[-1]
        kk = pl.program_id(2)

        @pl.when(kk == 0)
        def _():
            acc_ref[...] = jnp.zeros_like(acc_ref)

        acc_ref[...] += lax.dot_general(a_ref[...], b_ref[...], dn, preferred_element_type=F32)

        @pl.when(kk == nk - 1)
        def _():
            acc = acc_ref[...]
            outs = (acc,) if epilogue is None else epilogue(acc, *[r[...] for r in extra_refs])
            for o_ref, o in zip(out_refs, outs):
                o_ref[...] = o.astype(o_ref.dtype)

    mn_spec = pl.BlockSpec((tm, tn), lambda i, j, kk: (i, j))
    outs = pl.pallas_call(
        body,
        name=name,
        grid=(m // tm, n // tn, nk),
        in_specs=[a_spec, b_spec] + [mn_spec] * n_extra,
        out_specs=[mn_spec] * n_out,
        out_shape=[jax.ShapeDtypeStruct((m, n), dt) for dt in out_dtypes],
        scratch_shapes=[pltpu.VMEM((tm, tn), F32)],
        compiler_params=_cparams(("parallel", "parallel", "arbitrary")),
    )(a, b, *extras)
    return outs[0] if n_out == 1 else outs


ROW_TILE = 512


def _row_spec(width=D_MODEL, tile=ROW_TILE):
    return pl.BlockSpec((tile, width), lambda i: (i, 0))


def _vec_spec(width=D_MODEL):
    return pl.BlockSpec((1, width), lambda i: (0, 0))


def _rms_fwd(x, g, *, name, tile=ROW_TILE):
    t = x.shape[0]

    def body(x_ref, g_ref, h_ref):
        h_ref[...] = _rms(x_ref[...], g_ref[...]).astype(BF16)

    return pl.pallas_call(
        body, name=name, grid=(t // tile,),
        in_specs=[_row_spec(tile=tile), _vec_spec()], out_specs=_row_spec(tile=tile),
        out_shape=jax.ShapeDtypeStruct((t, D_MODEL), BF16),
        compiler_params=_cparams(("parallel",)),
    )(x, g.reshape(1, D_MODEL))


def _post_norm_add(xres, y, g_post, g_next, *, name):
    t = xres.shape[0]

    def body(x_ref, y_ref, gp_ref, gn_ref, xo_ref, h_ref):
        xo = x_ref[...] + _rms(y_ref[...], gp_ref[...])
        xo_ref[...] = xo
        h_ref[...] = _rms(xo, gn_ref[...]).astype(BF16)

    return pl.pallas_call(
        body, name=name, grid=(t // ROW_TILE,),
        in_specs=[_row_spec(), _row_spec(), _vec_spec(), _vec_spec()],
        out_specs=[_row_spec(), _row_spec()],
        out_shape=[jax.ShapeDtypeStruct((t, D_MODEL), F32), jax.ShapeDtypeStruct((t, D_MODEL), BF16)],
        compiler_params=_cparams(("parallel",)),
    )(xres, y, g_post.reshape(1, D_MODEL), g_next.reshape(1, D_MODEL))


def _post_norm_loss(xres, y, g_post, target, *, name):
    t = xres.shape[0]

    def body(x_ref, y_ref, gp_ref, t_ref, loss_ref, dx_ref):
        @pl.when(pl.program_id(0) == 0)
        def _():
            loss_ref[...] = jnp.zeros_like(loss_ref)

        err = (x_ref[...] + _rms(y_ref[...], gp_ref[...])) - t_ref[...]
        per_tok = jnp.mean(err * err, axis=-1, keepdims=True)
        loss_ref[...] += 0.5 * jnp.sum(per_tok, axis=0, keepdims=True)
        dx_ref[...] = err * (1.0 / D_MODEL)

    return pl.pallas_call(
        body, name=name, grid=(t // ROW_TILE,),
        in_specs=[_row_spec(), _row_spec(), _vec_spec(), _row_spec()],
        out_specs=[pl.BlockSpec((1, 128), lambda i: (0, 0)), _row_spec()],
        out_shape=[jax.ShapeDtypeStruct((1, 128), F32), jax.ShapeDtypeStruct((t, D_MODEL), F32)],
        compiler_params=_cparams(("arbitrary",)),
    )(xres, y, g_post.reshape(1, D_MODEL), target)


def _rms_bwd(dy, x, g, *, name, res=None, out_dtype=F32, tile=ROW_TILE):
    t = x.shape[0]
    has_res = res is not None

    def body(dy_ref, x_ref, g_ref, *rest):
        res_ref = rest[0] if has_res else None
        dx_ref, dg_ref = rest[-2], rest[-1]

        @pl.when(pl.program_id(0) == 0)
        def _():
            dg_ref[...] = jnp.zeros_like(dg_ref)

        xf = x_ref[...]
        dyf = dy_ref[...].astype(F32)
        r = lax.rsqrt(jnp.mean(xf * xf, axis=-1, keepdims=True) + EPS)
        xhat = xf * r
        dg_ref[...] += jnp.sum(dyf * xhat, axis=0, keepdims=True)
        dxh = dyf * g_ref[...]
        dx = r * (dxh - xhat * jnp.mean(dxh * xhat, axis=-1, keepdims=True))
        if has_res:
            dx = dx + res_ref[...]
        dx_ref[...] = dx.astype(dx_ref.dtype)

    args = [dy, x, g.reshape(1, D_MODEL)] + ([res] if has_res else [])
    return pl.pallas_call(
        body, name=name, grid=(t // tile,),
        in_specs=[_row_spec(tile=tile), _row_spec(tile=tile), _vec_spec()] + ([_row_spec(tile=tile)] if has_res else []),
        out_specs=[_row_spec(tile=tile), _vec_spec()],
        out_shape=[jax.ShapeDtypeStruct((t, D_MODEL), out_dtype), jax.ShapeDtypeStruct((1, D_MODEL), F32)],
        compiler_params=_cparams(("arbitrary",)),
    )(*args)


CONV_COLS = 256
N_CONV_BLOCKS = 3 * MIX_W // CONV_COLS


def _shift_down(x, k):
    if k == 0:
        return x
    return jnp.where(_iota(x.shape, 0) >= k, pltpu.roll(x, k, 0), 0.0)


def _shift_up(x, k):
    if k == 0:
        return x
    s = x.shape[0]
    return jnp.where(_iota(x.shape, 0) < s - k, pltpu.roll(x, s - k, 0), 0.0)


def _conv_pre(x, w_ref):
    c = w_ref[CONV_K - 1:CONV_K, :] * x
    for i in range(CONV_K - 1):
        c = c + w_ref[i:i + 1, :] * _shift_down(x, CONV_K - 1 - i)
    return c


def _conv_silu_fwd(proj, conv_w, n_batch):
    def body(x_ref, w_ref, y_ref):
        c = _conv_pre(x_ref[...], w_ref)
        y_ref[...] = c * jax.nn.sigmoid(c)

    return pl.pallas_call(
        body, name="conv_silu_fwd", grid=(n_batch, N_CONV_BLOCKS),
        in_specs=[pl.BlockSpec((SEQ, CONV_COLS), lambda b, j: (b, j)),
                  pl.BlockSpec((CONV_K, CONV_COLS), lambda b, j: (0, j))],
        out_specs=pl.BlockSpec((SEQ, CONV_COLS), lambda b, j: (b, j)),
        out_shape=jax.ShapeDtypeStruct((n_batch * SEQ, 3 * MIX_W), F32),
        compiler_params=_cparams(("parallel", "parallel")),
    )(proj, conv_w)


def _conv_silu_bwd(dy, proj, conv_w, n_batch):
    def body(dy_ref, x_ref, w_ref, dx_ref, dw_ref):
        @pl.when(pl.program_id(1) == 0)
        def _():
            dw_ref[...] = jnp.zeros_like(dw_ref)

        x = x_ref[...]
        c = _conv_pre(x, w_ref)
        sig = jax.nn.sigmoid(c)
        dc = dy_ref[...] * (sig * (1.0 + c * (1.0 - sig)))
        dx = w_ref[CONV_K - 1:CONV_K, :] * dc
        dw_ref[CONV_K - 1:CONV_K, :] += jnp.sum(dc * x, axis=0, keepdims=True)
        for i in range(CONV_K - 1):
            k = CONV_K - 1 - i
            dx = dx + w_ref[i:i + 1, :] * _shift_up(dc, k)
            dw_ref[i:i + 1, :] += jnp.sum(dc * _shift_down(x, k), axis=0, keepdims=True)
        dx_ref[...] = dx.astype(BF16)

    return pl.pallas_call(
        body, name="conv_silu_bwd", grid=(N_CONV_BLOCKS, n_batch),
        in_specs=[pl.BlockSpec((SEQ, CONV_COLS), lambda j, b: (b, j)),
                  pl.BlockSpec((SEQ, CONV_COLS), lambda j, b: (b, j)),
                  pl.BlockSpec((CONV_K, CONV_COLS), lambda j, b: (0, j))],
        out_specs=[pl.BlockSpec((SEQ, CONV_COLS), lambda j, b: (b, j)),
                   pl.BlockSpec((CONV_K, CONV_COLS), lambda j, b: (0, j))],
        out_shape=[jax.ShapeDtypeStruct((n_batch * SEQ, 3 * MIX_W), BF16),
                   jax.ShapeDtypeStruct((CONV_K, 3 * MIX_W), F32)],
        compiler_params=_cparams(("parallel", "arbitrary")),
    )(dy, proj, conv_w)


@jax.custom_vjp
def _solve_apply(low, rhs, tinv):
    return _dot3(tinv, rhs)


def _solve_apply_fwd(low, rhs, tinv):
    sol = _dot3(tinv, rhs)
    return sol, (tinv, sol)


def _solve_apply_bwd(resid, g):
    tinv, sol = resid
    y = _dot3(tinv, g, TN)
    return -_dot3(y, sol, NT), y, jnp.zeros_like(tinv)


_solve_apply.defvjp(_solve_apply_fwd, _solve_apply_bwd)


def _inv_unit_lower(lows):
    c = lows[0].shape[0]
    eye = (_iota((c, c), 0) == _iota((c, c), 1)).astype(F32)
    ms = [-low for low in lows]
    ps = [eye + m for m in ms]
    for _ in range(5):
        ms = [_dot3(m, m) for m in ms]
        ps = [p + _dot3(p, m) for p, m in zip(ps, ms)]
    return ps


def _gdn_chunk(qs, ks, vs, gates, states, small, alog_row, dtb_row, gain_row, tinvs):
    c = small.shape[0]
    heads = range(N_LIN)
    lane = _iota((c, 128), 1)
    row, col = _iota((c, c), 0), _iota((c, c), 1)
    causal, strict = row >= col, row > col
    last = _iota((c, 1), 0) == c - 1

    beta_all = jax.nn.sigmoid(small)
    g_all = -jnp.exp(alog_row) * _softplus(small + dtb_row)
    ltri = (col <= row).astype(F32)
    gc_all = lax.dot_general(ltri, g_all, NN, precision=HIGHEST, preferred_element_type=F32)

    beta = [jnp.sum(jnp.where(lane == h, beta_all, 0.0), axis=1, keepdims=True) for h in heads]
    gc = [jnp.sum(jnp.where(lane == N_LIN + h, gc_all, 0.0), axis=1, keepdims=True) for h in heads]
    gc_j = [lax.dot_general((lane == N_LIN + h).astype(F32), gc_all, NT, precision=HIGHEST,
                            preferred_element_type=F32) for h in heads]
    decay = [jnp.where(causal, jnp.exp(jnp.where(causal, gc[h] - gc_j[h], 0.0)), 0.0) for h in heads]
    gc_last = [jnp.sum(jnp.where(last, gc[h], 0.0), axis=0, keepdims=True) for h in heads]
    egc = [jnp.exp(g) for g in gc]
    qn = [q * lax.rsqrt(jnp.sum(q * q, axis=-1, keepdims=True) + EPS) * (LIN_DH ** -0.5) for q in qs]
    kn = [k * lax.rsqrt(jnp.sum(k * k, axis=-1, keepdims=True) + EPS) for k in ks]
    kb = [kn[h] * beta[h] for h in heads]
    low = [jnp.where(strict, _dotbf(kb[h], kn[h], NT) * decay[h], 0.0) for h in heads]
    if tinvs is None:
        tinvs = _inv_unit_lower(low)
    u = [_solve_apply(low[h], vs[h] * beta[h], tinvs[h]) for h in heads]
    w = [_solve_apply(low[h], kb[h] * egc[h], tinvs[h]) for h in heads]
    intra = [_dotbf(qn[h], kn[h], NT) * decay[h] for h in heads]
    v_new = [u[h] - _dotbf(w[h], states[h]) for h in heads]
    o = [_dotbf(qn[h] * egc[h], states[h]) + _dotbf(intra[h], v_new[h]) for h in heads]
    new_states = [states[h] * jnp.exp(gc_last[h]) + _dotbf(kn[h] * jnp.exp(gc_last[h] - gc[h]), v_new[h], TN)
                  for h in heads]
    o = [x * lax.rsqrt(jnp.mean(x * x, axis=-1, keepdims=True) + EPS) * gain_row for x in o]
    outs = [o[h] * (gates[h] * jax.nn.sigmoid(gates[h])) for h in heads]
    return outs, new_states, tinvs


def _gdn_param_rows(a_log, dt_bias, onorm):
    alog_row = jnp.zeros((1, 128), F32).at[0, N_LIN:2 * N_LIN].set(a_log)
    dtb_row = jnp.zeros((1, 128), F32).at[0, N_LIN:2 * N_LIN].set(dt_bias)
    return alog_row, dtb_row, onorm.reshape(1, LIN_DH)


def _head(ref_or_val, h):
    return ref_or_val[:, LIN_DH * h:LIN_DH * (h + 1)]


def _gdn_fwd(qkv, proj, alog_row, dtb_row, gain_row, n_batch):
    nc = SEQ // CHUNK
    t = n_batch * SEQ

    def body(qkv_ref, small_ref, gate_ref, al_ref, dt_ref, gn_ref, mix_ref, st_ref, ti_ref, s_scr):
        @pl.when(pl.program_id(1) == 0)
        def _():
            s_scr[...] = jnp.zeros_like(s_scr)

        heads = range(N_LIN)
        states = [s_scr[h] for h in heads]
        outs, new_states, tinvs = _gdn_chunk(
            [_head(qkv_ref, h) for h in heads], [_head(qkv_ref, N_LIN + h) for h in heads],
            [_head(qkv_ref, 2 * N_LIN + h) for h in heads], [_head(gate_ref, h) for h in heads],
            states, small_ref[...], al_ref[...], dt_ref[...], gn_ref[...], None)
        mix_ref[...] = jnp.concatenate(outs, axis=1).astype(BF16)
        for h in heads:
            st_ref[0, 0, h] = states[h]
            s_scr[h] = new_states[h]
            ti_ref[0, 0, h] = tinvs[h]

    row = lambda b, n: b * nc + n
    vec = pl.BlockSpec((1, 128), lambda b, n: (0, 0))
    return pl.pallas_call(
        body, name="gdn_fwd", grid=(n_batch, nc),
        in_specs=[pl.BlockSpec((CHUNK, 3 * MIX_W), lambda b, n: (row(b, n), 0)),
                  pl.BlockSpec((CHUNK, 128), lambda b, n: (row(b, n), SMALL_COL)),
                  pl.BlockSpec((CHUNK, MIX_W), lambda b, n: (row(b, n), 3)),
                  vec, vec, vec],
        out_specs=[pl.BlockSpec((CHUNK, MIX_W), lambda b, n: (row(b, n), 0)),
                   pl.BlockSpec((1, 1, N_LIN, LIN_DH, LIN_DH), lambda b, n: (b, n, 0, 0, 0)),
                   pl.BlockSpec((1, 1, N_LIN, CHUNK, CHUNK), lambda b, n: (b, n, 0, 0, 0))],
        out_shape=[jax.ShapeDtypeStruct((t, MIX_W), BF16),
                   jax.ShapeDtypeStruct((n_batch, nc, N_LIN, LIN_DH, LIN_DH), F32),
                   jax.ShapeDtypeStruct((n_batch, nc, N_LIN, CHUNK, CHUNK), F32)],
        scratch_shapes=[pltpu.VMEM((N_LIN, LIN_DH, LIN_DH), F32)],
        compiler_params=_cparams(("parallel", "arbitrary")),
    )(qkv, proj, proj, alog_row, dtb_row, gain_row)


def _gdn_bwd(dcat, qkv, proj, states, tinvs, alog_row, dtb_row, gain_row, n_batch):
    nc = SEQ // CHUNK
    t = n_batch * SEQ

    def body(dmix_ref, qkv_ref, small_ref, gate_ref, st_ref, ti_ref, al_ref, dt_ref, gn_ref,
             dqkv_ref, dgate_ref, dsmall_ref, dal_ref, ddt_ref, dgn_ref, ds_scr):
        @pl.when(pl.program_id(1) == 0)
        def _():
            ds_scr[...] = jnp.zeros_like(ds_scr)

        @pl.when((pl.program_id(0) == 0) & (pl.program_id(1) == 0))
        def _():
            dal_ref[...] = jnp.zeros_like(dal_ref)
            ddt_ref[...] = jnp.zeros_like(ddt_ref)
            dgn_ref[...] = jnp.zeros_like(dgn_ref)

        heads = range(N_LIN)
        tinvs = [ti_ref[0, 0, h] for h in heads]

        def chunk(qs, ks, vs, gates, states_in, small, al, dt, gn):
            outs, new_states, _ = _gdn_chunk(qs, ks, vs, gates, states_in, small, al, dt, gn, tinvs)
            return tuple(outs), tuple(new_states)

        prim = (tuple(_head(qkv_ref, h) for h in heads),
                tuple(_head(qkv_ref, N_LIN + h) for h in heads),
                tuple(_head(qkv_ref, 2 * N_LIN + h) for h in heads),
                tuple(_head(gate_ref, h) for h in heads),
                tuple(st_ref[0, 0, h] for h in heads),
                small_ref[...], al_ref[...], dt_ref[...], gn_ref[...])
        _, vjp = jax.vjp(chunk, *prim)
        cot = (tuple(_head(dmix_ref, h) for h in heads), tuple(ds_scr[h] for h in heads))
        dq, dk, dv, dgate, dstate, dsmall, dal, ddt, dgn = vjp(cot)
        dqkv_ref[...] = jnp.concatenate(list(dq) + list(dk) + list(dv), axis=1)
        dgate_ref[...] = jnp.concatenate(list(dgate), axis=1).astype(BF16)
        dsmall_ref[...] = dsmall.astype(BF16)
        for h in heads:
            ds_scr[h] = dstate[h]
        dal_ref[...] += dal
        ddt_ref[...] += ddt
        dgn_ref[...] += dgn

    row = lambda b, n: b * nc + (nc - 1 - n)
    vec = pl.BlockSpec((1, 128), lambda b, n: (0, 0))
    return pl.pallas_call(
        body, name="gdn_bwd", grid=(n_batch, nc),
        in_specs=[pl.BlockSpec((CHUNK, MIX_W), lambda b, n: (row(b, n), 0)),
                  pl.BlockSpec((CHUNK, 3 * MIX_W), lambda b, n: (row(b, n), 0)),
                  pl.BlockSpec((CHUNK, 128), lambda b, n: (row(b, n), SMALL_COL)),
                  pl.BlockSpec((CHUNK, MIX_W), lambda b, n: (row(b, n), 3)),
                  pl.BlockSpec((1, 1, N_LIN, LIN_DH, LIN_DH), lambda b, n: (b, nc - 1 - n, 0, 0, 0)),
                  pl.BlockSpec((1, 1, N_LIN, CHUNK, CHUNK), lambda b, n: (b, nc - 1 - n, 0, 0, 0)),
                  vec, vec, vec],
        out_specs=[pl.BlockSpec((CHUNK, 3 * MIX_W), lambda b, n: (row(b, n), 0)),
                   pl.BlockSpec((CHUNK, MIX_W), lambda b, n: (row(b, n), 0)),
                   pl.BlockSpec((CHUNK, 128), lambda b, n: (row(b, n), 0)),
                   vec, vec, vec],
        out_shape=[jax.ShapeDtypeStruct((t, 3 * MIX_W), F32),
                   jax.ShapeDtypeStruct((t, MIX_W), BF16),
                   jax.ShapeDtypeStruct((t, 128), BF16),
                   jax.ShapeDtypeStruct((1, 128), F32),
                   jax.ShapeDtypeStruct((1, 128), F32),
                   jax.ShapeDtypeStruct((1, 128), F32)],
        scratch_shapes=[pltpu.VMEM((N_LIN, LIN_DH, LIN_DH), F32)],
        compiler_params=_cparams(("arbitrary", "arbitrary")),
    )(dcat, qkv, proj, proj, states, tinvs, alog_row, dtb_row, gain_row)


SB_T = 256


def _sb_masks():
    r, c = _iota((SB_T, SB_T), 0), _iota((SB_T, SB_T), 1)
    return r, c


def _sb_fwd(proj, n_batch):
    nq = SEQ // SB_T
    t = n_batch * SEQ
    scale = SB_DH ** -0.5
    both = range(2)

    def body(q_ref, k_ref, v_ref, o_ref, tot_ref, acc_scr, run_scr):
        qi = pl.program_id(2)
        lane = _iota((SB_T, 128), 1)
        r, c = _sb_masks()
        upper = (r > c).astype(BF16)
        q = q_ref[...] * scale
        qm = [jnp.where((lane < SB_DH) == (hh == 0), q, jnp.zeros_like(q)) for hh in both]

        def block(kb, diagonal):
            ks = pl.ds(pl.multiple_of(kb * SB_T, SB_T), SB_T)
            k_blk, v_blk = k_ref[ks, :], v_ref[ks, :]
            z = [lax.dot_general(qm[hh], k_blk, NT, preferred_element_type=F32) for hh in both]
            ld = [jnp.log(1.0 + jnp.exp(-jnp.abs(x))) for x in z]
            lb = [jnp.minimum(z[hh], 0.0) - ld[hh] for hh in both]
            l1m = [lb[hh] - z[hh] for hh in both]
            if diagonal:
                l1m = [jnp.where(r > c, x, 0.0) for x in l1m]
            tail = [_dot_mask(x, upper) for x in l1m]
            if not diagonal:
                tail = [run_scr[hh][:, 0:1] + tail[hh] for hh in both]
            a = [jnp.exp(lb[hh] + tail[hh]) for hh in both]
            if diagonal:
                a = [jnp.where(r > c, x, 0.0) for x in a]
            pv = [lax.dot_general(a[hh].astype(BF16), v_blk, NN, preferred_element_type=F32) for hh in both]
            rowsum = [jnp.sum(x, axis=1, keepdims=True) for x in l1m]
            for hh in both:
                if diagonal:
                    acc_scr[hh] = pv[hh]
                    run_scr[hh] = jnp.broadcast_to(rowsum[hh], (SB_T, 128))
                else:
                    acc_scr[hh] += pv[hh]
                    run_scr[hh] += rowsum[hh]

        block(qi, True)

        def step(it, carry):
            block(qi - 1 - it, False)
            return carry

        lax.fori_loop(0, qi, step, 0)
        first = lane < SB_DH
        o_ref[...] = jnp.where(first, acc_scr[0], acc_scr[1]).astype(BF16)
        tot_ref[...] = jnp.where(first, run_scr[0], run_scr[1])

    nq_blocks = lambda b, p, i: (b * nq + i, p)
    seq_spec = lambda off: pl.BlockSpec((SEQ, 128), lambda b, p, i: (b, off + p))
    return pl.pallas_call(
        body, name="sb_fwd", grid=(n_batch, SB_PAIRS, nq),
        in_specs=[pl.BlockSpec((SB_T, 128), nq_blocks), seq_spec(SB_PAIRS), seq_spec(2 * SB_PAIRS)],
        out_specs=[pl.BlockSpec((SB_T, 128), nq_blocks), pl.BlockSpec((SB_T, 128), nq_blocks)],
        out_shape=[jax.ShapeDtypeStruct((t, MIX_W), BF16), jax.ShapeDtypeStruct((t, MIX_W), F32)],
        scratch_shapes=[pltpu.VMEM((2, SB_T, 128), F32), pltpu.VMEM((2, SB_T, 128), F32)],
        compiler_params=_cparams(("parallel", "parallel", "arbitrary")),
    )(proj, proj, proj)


def _sb_bwd(dcat, proj, totals, n_batch):
    nq = SEQ // SB_T
    t = n_batch * SEQ
    scale = SB_DH ** -0.5
    both = range(2)

    def body(do_ref, q_ref, k_ref, v_ref, tot_ref, dq_ref, dk_ref, dv_ref, dq_scr, run_scr, grun_scr):
        qi = pl.program_id(2)

        @pl.when(qi == 0)
        def _():
            dk_ref[...] = jnp.zeros_like(dk_ref)
            dv_ref[...] = jnp.zeros_like(dv_ref)

        lane = _iota((SB_T, 128), 1)
        r, c = _sb_masks()
        incl = (r <= c).astype(BF16)
        before = (r < c).astype(BF16)
        dq_scr[...] = jnp.zeros_like(dq_scr)
        run_scr[...] = jnp.zeros_like(run_scr)
        grun_scr[...] = jnp.zeros_like(grun_scr)
        q, do, tot = q_ref[...] * scale, do_ref[...], tot_ref[...]
        sel = [(lane < SB_DH) == (hh == 0) for hh in both]
        qm = [jnp.where(sel[hh], q, jnp.zeros_like(q)) for hh in both]
        dom = [jnp.where(sel[hh], do, 0.0).astype(BF16) for hh in both]
        total = [jnp.sum(jnp.where(lane == hh * SB_DH, tot, 0.0), axis=1, keepdims=True) for hh in both]

        def block(kb, diagonal):
            ks = pl.ds(pl.multiple_of(kb * SB_T, SB_T), SB_T)
            k_blk, v_blk = k_ref[ks, :], v_ref[ks, :]
            z = [lax.dot_general(qm[hh], k_blk, NT, preferred_element_type=F32) for hh in both]
            da = [lax.dot_general(dom[hh], v_blk, NT, preferred_element_type=F32) for hh in both]
            ld = [jnp.log(1.0 + jnp.exp(-jnp.abs(x))) for x in z]
            lb = [jnp.minimum(z[hh], 0.0) - ld[hh] for hh in both]
            sig = [jnp.exp(x) for x in lb]
            l1m = [lb[hh] - z[hh] for hh in both]
            if diagonal:
                l1m = [jnp.where(r > c, x, 0.0) for x in l1m]
            prefix = [run_scr[hh][:, 0:1] + _dot_mask(l1m[hh], incl) for hh in both]
            a = [jnp.exp(lb[hh] + (total[hh] - prefix[hh])) for hh in both]
            if diagonal:
                a = [jnp.where(r > c, x, 0.0) for x in a]
            de = [a[hh] * da[hh] for hh in both]
            dv = [lax.dot_general(a[hh].astype(BF16), dom[hh], TN, preferred_element_type=F32) for hh in both]
            dl1m = [grun_scr[hh][:, 0:1] + _dot_mask(de[hh], before) for hh in both]
            if diagonal:
                dl1m = [jnp.where(r > c, x, 0.0) for x in dl1m]
            dz = [(de[hh] * (1.0 - sig[hh]) - dl1m[hh] * sig[hh]).astype(BF16) for hh in both]
            dqp = [lax.dot_general(dz[hh], k_blk, NN, preferred_element_type=F32) for hh in both]
            dk = [lax.dot_general(dz[hh], qm[hh], TN, preferred_element_type=F32) for hh in both]
            rs_l = [jnp.sum(x, axis=1, keepdims=True) for x in l1m]
            rs_e = [jnp.sum(x, axis=1, keepdims=True) for x in de]
            for hh in both:
                dq_scr[hh] += dqp[hh]
                run_scr[hh] += rs_l[hh]
                grun_scr[hh] += rs_e[hh]
            dk_ref[ks, :] += dk[0] + dk[1]
            dv_ref[ks, :] += dv[0] + dv[1]

        def step(kb, carry):
            block(kb, False)
            return carry

        lax.fori_loop(0, qi, step, 0)
        block(qi, True)
        dq_ref[...] = (jnp.where(sel[0], dq_scr[0], dq_scr[1]) * scale).astype(BF16)

    q_blocks = lambda b, p, i: (b * nq + i, p)
    seq_spec = lambda off: pl.BlockSpec((SEQ, 128), lambda b, p, i: (b, off + p))
    dq, dk, dv = pl.pallas_call(
        body, name="sb_bwd", grid=(n_batch, SB_PAIRS, nq),
        in_specs=[pl.BlockSpec((SB_T, 128), q_blocks), pl.BlockSpec((SB_T, 128), q_blocks),
                  seq_spec(SB_PAIRS), seq_spec(2 * SB_PAIRS), pl.BlockSpec((SB_T, 128), q_blocks)],
        out_specs=[pl.BlockSpec((SB_T, 128), q_blocks), seq_spec(0), seq_spec(0)],
        out_shape=[jax.ShapeDtypeStruct((t, MIX_W), BF16), jax.ShapeDtypeStruct((t, MIX_W), F32),
                   jax.ShapeDtypeStruct((t, MIX_W), F32)],
        scratch_shapes=[pltpu.VMEM((2, SB_T, 128), F32), pltpu.VMEM((2, SB_T, 128), F32),
                        pltpu.VMEM((2, SB_T, 128), F32)],
        compiler_params=_cparams(("parallel", "parallel", "arbitrary")),
    )(dcat, proj, proj, proj, totals)
    return dq, dk, dv


MEM_TQ = 512


def _mem_heads(lane):
    return [(lane >= X_HEAD_DIM * h) & (lane < X_HEAD_DIM * (h + 1)) for h in range(N_X_HEADS)]


def _mem_attn_fwd(proj, q_col, memkv, n_batch):
    nq = SEQ // MEM_TQ
    t = n_batch * SEQ
    scale = X_HEAD_DIM ** -0.5

    def body(q_ref, kv_ref, o_ref):
        q = q_ref[...]
        k = kv_ref[:, :X_WIDTH].astype(BF16)
        v = kv_ref[:, X_WIDTH:].astype(BF16)
        out = jnp.zeros((MEM_TQ, X_WIDTH), F32)
        for sel in _mem_heads(_iota((MEM_TQ, X_WIDTH), 1)):
            s = lax.dot_general(jnp.where(sel, q, 0.0).astype(BF16), k, NT, preferred_element_type=F32) * scale
            e = jnp.exp(s - jnp.max(s, axis=-1, keepdims=True))
            p = e / jnp.sum(e, axis=-1, keepdims=True)
            out = out + jnp.where(sel, lax.dot_general(p.astype(BF16), v, NN, preferred_element_type=F32), 0.0)
        o_ref[...] = out.astype(BF16)

    return pl.pallas_call(
        body, name="mem_attn_fwd", grid=(n_batch, nq),
        in_specs=[pl.BlockSpec((MEM_TQ, X_WIDTH), lambda b, i: (b * nq + i, q_col)),
                  pl.BlockSpec((N_MEM, 2 * X_WIDTH), lambda b, i: (b, 0))],
        out_specs=pl.BlockSpec((MEM_TQ, X_WIDTH), lambda b, i: (b * nq + i, 0)),
        out_shape=jax.ShapeDtypeStruct((t, X_WIDTH), BF16),
        compiler_params=_cparams(("parallel", "parallel")),
    )(proj, memkv)


def _mem_attn_bwd(dcat, proj, q_col, memkv, n_batch):
    nq = SEQ // MEM_TQ
    t = n_batch * SEQ
    scale = X_HEAD_DIM ** -0.5

    def body(do_ref, q_ref, kv_ref, dq_ref, dkv_ref):
        @pl.when(pl.program_id(1) == 0)
        def _():
            dkv_ref[...] = jnp.zeros_like(dkv_ref)

        q, do = q_ref[...], do_ref[...]
        k = kv_ref[:, :X_WIDTH].astype(BF16)
        v = kv_ref[:, X_WIDTH:].astype(BF16)
        dq = jnp.zeros((MEM_TQ, X_WIDTH), F32)
        dk = jnp.zeros((N_MEM, X_WIDTH), F32)
        dv = jnp.zeros((N_MEM, X_WIDTH), F32)
        for sel in _mem_heads(_iota((MEM_TQ, X_WIDTH), 1)):
            qm = jnp.where(sel, q, 0.0).astype(BF16)
            dom = jnp.where(sel, do, 0.0).astype(BF16)
            s = lax.dot_general(qm, k, NT, preferred_element_type=F32) * scale
            e = jnp.exp(s - jnp.max(s, axis=-1, keepdims=True))
            p = e / jnp.sum(e, axis=-1, keepdims=True)
            dp = lax.dot_general(dom, v, NT, preferred_element_type=F32)
            ds = ((p * (dp - jnp.sum(dp * p, axis=-1, keepdims=True))) * scale).astype(BF16)
            dv = dv + lax.dot_general(p.astype(BF16), dom, TN, preferred_element_type=F32)
            dk = dk + lax.dot_general(ds, qm, TN, preferred_element_type=F32)
            dq = dq + jnp.where(sel, lax.dot_general(ds, k, NN, preferred_element_type=F32), 0.0)
        dq_ref[...] = dq.astype(BF16)
        dkv_ref[...] += jnp.concatenate([dk, dv], axis=1)

    return pl.pallas_call(
        body, name="mem_attn_bwd", grid=(n_batch, nq),
        in_specs=[pl.BlockSpec((MEM_TQ, X_WIDTH), lambda b, i: (b * nq + i, 3)),
                  pl.BlockSpec((MEM_TQ, X_WIDTH), lambda b, i: (b * nq + i, q_col)),
                  pl.BlockSpec((N_MEM, 2 * X_WIDTH), lambda b, i: (b, 0))],
        out_specs=[pl.BlockSpec((MEM_TQ, X_WIDTH), lambda b, i: (b * nq + i, 0)),
                   pl.BlockSpec((N_MEM, 2 * X_WIDTH), lambda b, i: (b, 0))],
        out_shape=[jax.ShapeDtypeStruct((t, X_WIDTH), BF16),
                   jax.ShapeDtypeStruct((n_batch * N_MEM, 2 * X_WIDTH), F32)],
        compiler_params=_cparams(("parallel", "arbitrary")),
    )(dcat, proj, memkv)


def _relu2_epilogue(acc):
    r = jnp.maximum(acc, 0.0)
    return r * r, acc


def _relu2_bwd_epilogue(acc, u):
    return (acc * (2.0 * jnp.maximum(u.astype(F32), 0.0)),)


def _pad_in_a(w_in_a):
    w = 3 * MIX_W
    parts = [w_in_a[:, :w], w_in_a[:, w:w + MIX_W], w_in_a[:, IN_A - X_WIDTH:],
             w_in_a[:, w + MIX_W:w + MIX_W + 2 * N_LIN]]
    pad = jnp.zeros((D_MODEL, IN_A_PAD - IN_A), w_in_a.dtype)
    return jnp.concatenate(parts + [pad], axis=1)


def _unpad_in_a(g):
    w = 3 * MIX_W
    return jnp.concatenate([g[:, :w + MIX_W], g[:, w + MIX_W + X_WIDTH:w + MIX_W + X_WIDTH + 2 * N_LIN],
                            g[:, w + MIX_W:w + MIX_W + X_WIDTH]], axis=1)


def _local_step(x, mem, target, wts, small):
    t = x.shape[0]
    nb = t // SEQ
    npre, npost, mpre, mpost = small["norm_pre_mix"], small["norm_post_mix"], small["norm_pre_mlp"], small["norm_post_mlp"]
    alog_row, dtb_row, gain_row = _gdn_param_rows(small["a_log_a"][0], small["dt_bias_a"][0], small["onorm_a"][0])
    conv_w = small["conv_w"]

    mem_n = _rms_fwd(mem, small["mem_norm"], name="mem_norm_fwd", tile=256)
    saved = []
    h = _rms_fwd(x, npre[0], name="pre_mix_norm0")
    for i in range(DEPTH):
        s = {"x_in": x, "h1": h}
        if i == 0:
            proj = _matmul(h, wts["in_a_pad"], mode="nn", tm=512, tn=1152, tk=1024, name="proj_a")
            qkv = _conv_silu_fwd(proj, conv_w, nb)
            mix, states, tinvs = _gdn_fwd(qkv, proj, alog_row, dtb_row, gain_row, nb)
            s.update(qkv=qkv, states=states, tinvs=tinvs)
            q_col = (3 * MIX_W + MIX_W) // X_WIDTH
        else:
            proj = _matmul(h, wts["in_b"], mode="nn", tm=512, tn=1280, tk=1024, name="proj_b", out_dtypes=(BF16,))
            mix, totals = _sb_fwd(proj, nb)
            s.update(totals=totals)
            q_col = 3 * MIX_W // X_WIDTH
        memkv = _matmul(mem_n, wts["mem_kv"][i], mode="nn", tm=256, tn=512, tk=1024, name=f"memkv{i}")
        cross = _mem_attn_fwd(proj, q_col, memkv, nb)
        cat = jnp.concatenate([mix, cross], axis=1)
        y = _matmul(cat, wts["out"][i], mode="nn", tm=512, tn=1024, tk=1024, name=f"out_proj{i}")
        x2, h2 = _post_norm_add(x, y, npost[i], mpre[i], name=f"post_mix{i}")
        a, u = _matmul(h2, wts["up"][i], mode="nn", tm=512, tn=1024, tk=1024, name=f"up{i}",
                       out_dtypes=(BF16, BF16), epilogue=_relu2_epilogue)
        y2 = _matmul(a, wts["down"][i], mode="nn", tm=512, tn=1024, tk=1024, name=f"down{i}")
        s.update(proj=proj, q_col=q_col, memkv=memkv, cat=cat, y=y, x2=x2, h2=h2, a=a, u=u, y2=y2)
        saved.append(s)
        if i + 1 < DEPTH:
            x, h = _post_norm_add(x2, y2, mpost[i], npre[i + 1], name=f"post_mlp{i}")
        else:
            loss_row, dx = _post_norm_loss(x2, y2, mpost[i], target, name="loss_head")

    gw = {"mem_kv": [None] * DEPTH, "out": [None] * DEPTH, "up": [None] * DEPTH, "down": [None] * DEPTH}
    gs = {k: [None] * DEPTH for k in ("norm_pre_mix", "norm_post_mix", "norm_pre_mlp", "norm_post_mlp")}
    dmem_n = None
    for i in reversed(range(DEPTH)):
        s = saved[i]
        dy2, gs["norm_post_mlp"][i] = _rms_bwd(dx, s["y2"], mpost[i], name=f"post_mlp_bwd{i}", out_dtype=BF16)
        du = _matmul(dy2, wts["down"][i], mode="nt", tm=512, tn=1024, tk=1024, name=f"down_dx{i}",
                     out_dtypes=(BF16,), epilogue=_relu2_bwd_epilogue, extras=(s["u"],))
        gw["down"][i] = _matmul(s["a"], dy2, mode="tn", tm=1024, tn=1024, tk=512, name=f"down_dw{i}")
        dh2 = _matmul(du, wts["up"][i], mode="nt", tm=512, tn=1024, tk=1024, name=f"up_dx{i}")
        gw["up"][i] = _matmul(s["h2"], du, mode="tn", tm=1024, tn=1024, tk=512, name=f"up_dw{i}")
        dx2, gs["norm_pre_mlp"][i] = _rms_bwd(dh2, s["x2"], mpre[i], name=f"pre_mlp_bwd{i}", res=dx)
        dy, gs["norm_post_mix"][i] = _rms_bwd(dx2, s["y"], npost[i], name=f"post_mix_bwd{i}", out_dtype=BF16)
        dcat = _matmul(dy, wts["out"][i], mode="nt", tm=512, tn=1024, tk=1024, name=f"out_dx{i}")
        gw["out"][i] = _matmul(s["cat"], dy, mode="tn", tm=1024, tn=1024, tk=512, name=f"out_dw{i}")
        dmemq, dmemkv = _mem_attn_bwd(dcat, s["proj"], s["q_col"], s["memkv"], nb)
        dmemkv = dmemkv.astype(BF16)
        gw["mem_kv"][i] = _matmul(mem_n, dmemkv, mode="tn", tm=1024, tn=512, tk=256, name=f"memkv_dw{i}")
        dmn = _matmul(dmemkv, wts["mem_kv"][i], mode="nt", tm=256, tn=1024, tk=512, name=f"memkv_dx{i}")
        dmem_n = dmn if dmem_n is None else dmem_n + dmn
        if i == 0:
            dqkv, dgate, dsmall, dalog, ddtb, dgain = _gdn_bwd(
                dcat, s["qkv"], s["proj"], s["states"], s["tinvs"], alog_row, dtb_row, gain_row, nb)
            dqkv_pre, dconv = _conv_silu_bwd(dqkv, s["proj"], conv_w, nb)
            dproj = jnp.concatenate([dqkv_pre, dgate, dmemq, dsmall], axis=1)
            w_in, tile = wts["in_a_pad"], 1152
        else:
            dq, dk, dv = _sb_bwd(dcat, s["proj"], s["totals"], nb)
            dproj = jnp.concatenate([dq, dk.astype(BF16), dv.astype(BF16), dmemq], axis=1)
            w_in, tile = wts["in_b"], 1280
        dh1 = _matmul(dproj, w_in, mode="nt", tm=512, tn=1024, tk=tile, name=f"proj_dx{i}")
        g_in = _matmul(s["h1"], dproj, mode="tn", tm=1024, tn=tile, tk=512, name=f"proj_dw{i}")
        if i == 0:
            gw["in_a"] = _unpad_in_a(g_in)
        else:
            gw["in_b"] = g_in
        dx, gs["norm_pre_mix"][i] = _rms_bwd(dh1, s["x_in"], npre[i], name=f"pre_mix_bwd{i}", res=dx2)

    _, g_mem_norm = _rms_bwd(dmem_n, mem, small["mem_norm"], name="mem_norm_bwd", tile=256)
    gsmall = {k: jnp.concatenate(v, axis=0) for k, v in gs.items()}
    gsmall.update(mem_norm=g_mem_norm[0], a_log_a=dalog[:, N_LIN:2 * N_LIN], dt_bias_a=ddtb[:, N_LIN:2 * N_LIN],
                  onorm_a=dgain, conv_w=dconv)
    gw = {k: (jnp.stack(v) if isinstance(v, list) else v) for k, v in gw.items()}
    return loss_row[0, 0], dx, gw, gsmall


PACK_ROWS = (("in_a", 835), ("in_b", 640), ("mem_kv", 256), ("out", 512), ("up", 2048), ("down", 2048))
R_PACK = 6400
R_HALF = R_PACK // 2
SUM_TILE = 640
ANY = pl.BlockSpec(memory_space=pl.ANY)


def _position():
    x, y, c = lax.axis_index("x"), lax.axis_index("y"), lax.axis_index("c")
    others = [(1 - x, y), (x, 1 - y), (1 - x, 1 - y)]
    return x, y, c, others


def _gather_chips(wflat):
    def body(w_ref, out_ref, send_sems, recv_sems, local_sem):
        x, y, c, others = _position()
        me = 2 * x + y

        def copy(k, src, dst, to):
            return pltpu.make_async_remote_copy(src_ref=src, dst_ref=dst, send_sem=send_sems.at[k],
                                                recv_sem=recv_sems.at[k], device_id=to, device_id_type=MESH)

        mine = pltpu.make_async_copy(w_ref, out_ref.at[me], local_sem)
        mine.start()
        first = [copy(j, w_ref.at[c], out_ref.at[me, c], (ox, oy, c)) for j, (ox, oy) in enumerate(others)]
        for cp in first:
            cp.start()
        passed = [copy(3 + j, out_ref.at[2 * ox + oy, c], out_ref.at[2 * ox + oy, c], (x, y, 1 - c))
                  for j, (ox, oy) in enumerate(others)]
        for j, (ox, oy) in enumerate(others):
            copy(j, w_ref.at[c], out_ref.at[2 * ox + oy, c], (x, y, c)).wait_recv()
            passed[j].start()
        for j, (ox, oy) in enumerate(others):
            copy(3 + j, w_ref.at[c], out_ref.at[2 * ox + oy, 1 - c], (x, y, c)).wait_recv()
        for cp in first + passed:
            cp.wait_send()
        mine.wait()

    return pl.pallas_call(
        body, name="gather_weights",
        in_specs=[ANY], out_specs=ANY,
        out_shape=jax.ShapeDtypeStruct((N_CHIPS, 2, R_HALF, D_MODEL), wflat.dtype),
        scratch_shapes=[pltpu.SemaphoreType.DMA((6,)), pltpu.SemaphoreType.DMA((6,)), pltpu.SemaphoreType.DMA],
    )(wflat)


def _gather_all(v, *, name):
    rows, n = v.shape

    def body(x_ref, out_ref, send_sems, recv_sems, local_sem):
        x, y, c, others = _position()
        me, sibling = (x, y, c), (x, y, 1 - c)

        def blk(px, py, pc):
            return out_ref.at[pl.ds((4 * px + 2 * py + pc) * rows, rows), :]

        def copy(k, block, to, src=None):
            return pltpu.make_async_remote_copy(src_ref=blk(*block) if src is None else src, dst_ref=blk(*block),
                                                send_sem=send_sems.at[k], recv_sem=recv_sems.at[k],
                                                device_id=to, device_id_type=MESH)

        mine = pltpu.make_async_copy(x_ref, blk(*me), local_sem)
        mine.start()
        first = [copy(0, me, sibling, src=x_ref)]
        first += [copy(1 + j, me, (*chip, c), src=x_ref) for j, chip in enumerate(others)]
        for cp in first:
            cp.start()
        passed = [copy(4 + j, (*chip, c), sibling) for j, chip in enumerate(others)]
        for j, chip in enumerate(others):
            copy(1 + j, (*chip, c), me).wait_recv()
            passed[j].start()
        copy(0, sibling, me).wait_recv()
        for j, chip in enumerate(others):
            copy(4 + j, (*chip, 1 - c), me).wait_recv()
        for cp in first + passed:
            cp.wait_send()
        mine.wait()

    vmem = pl.BlockSpec(memory_space=pltpu.VMEM)
    return pl.pallas_call(
        body, name=name, in_specs=[vmem], out_specs=vmem,
        out_shape=jax.ShapeDtypeStruct((8 * rows, n), v.dtype),
        scratch_shapes=[pltpu.SemaphoreType.DMA((7,)), pltpu.SemaphoreType.DMA((7,)), pltpu.SemaphoreType.DMA],
    )(v)


def _swap_halves(g5):
    def body(g_ref, out_ref, send_sem, recv_sem):
        x, y, c, _ = _position()
        cp = pltpu.make_async_remote_copy(src_ref=g_ref.at[:, 1 - c], dst_ref=out_ref, send_sem=send_sem,
                                          recv_sem=recv_sem, device_id=(x, y, 1 - c), device_id_type=MESH)
        cp.start()
        cp.wait()

    return pl.pallas_call(
        body, name="grad_swap_halves", in_specs=[ANY], out_specs=ANY,
        out_shape=jax.ShapeDtypeStruct((N_CHIPS, R_HALF, D_MODEL), g5.dtype),
        scratch_shapes=[pltpu.SemaphoreType.DMA, pltpu.SemaphoreType.DMA],
    )(g5)


def _add_halves(core, g5, got):
    def body(c_ref, a_ref, b_ref, o_ref, ob_ref):
        s = a_ref[0] + b_ref[...]
        o_ref[...] = s
        ob_ref[...] = s.astype(BF16)

    nt = R_HALF // SUM_TILE
    spec = pl.BlockSpec((1, SUM_TILE, D_MODEL), lambda s, i, c_ref: (s, i, 0))
    return pl.pallas_call(
        body, name="grad_add_halves",
        grid_spec=pltpu.PrefetchScalarGridSpec(
            num_scalar_prefetch=1, grid=(N_CHIPS, nt),
            in_specs=[pl.BlockSpec((1, 1, SUM_TILE, D_MODEL), lambda s, i, c_ref: (s, c_ref[0], i, 0)), spec],
            out_specs=[spec, spec]),
        out_shape=[jax.ShapeDtypeStruct((N_CHIPS, R_HALF, D_MODEL), F32),
                   jax.ShapeDtypeStruct((N_CHIPS, R_HALF, D_MODEL), BF16)],
        compiler_params=_cparams(("parallel", "parallel")),
    )(core, g5, got)


def _exchange_chips(p):
    def body(p_ref, q_ref, send_sems, recv_sems):
        x, y, c, others = _position()
        copies = [pltpu.make_async_remote_copy(src_ref=p_ref.at[2 * ox + oy], dst_ref=q_ref.at[j],
                                               send_sem=send_sems.at[j], recv_sem=recv_sems.at[j],
                                               device_id=(ox, oy, c), device_id_type=MESH)
                  for j, (ox, oy) in enumerate(others)]
        for cp in copies:
            cp.start()
        for cp in copies:
            cp.wait()

    return pl.pallas_call(
        body, name="grad_exchange_chips", in_specs=[ANY], out_specs=ANY,
        out_shape=jax.ShapeDtypeStruct((3, R_HALF, D_MODEL), p.dtype),
        scratch_shapes=[pltpu.SemaphoreType.DMA((3,)), pltpu.SemaphoreType.DMA((3,))],
    )(p)


def _add_chips(chip, p, q):
    def body(k_ref, p_ref, q_ref, o_ref):
        o_ref[...] = ((p_ref[0] + q_ref[0].astype(F32)) + q_ref[1].astype(F32)) + q_ref[2].astype(F32)

    nt = R_HALF // SUM_TILE
    return pl.pallas_call(
        body, name="grad_add_chips",
        grid_spec=pltpu.PrefetchScalarGridSpec(
            num_scalar_prefetch=1, grid=(nt,),
            in_specs=[pl.BlockSpec((1, SUM_TILE, D_MODEL), lambda i, k_ref: (k_ref[0], i, 0)),
                      pl.BlockSpec((3, SUM_TILE, D_MODEL), lambda i, k_ref: (0, i, 0))],
            out_specs=pl.BlockSpec((SUM_TILE, D_MODEL), lambda i, k_ref: (i, 0))),
        out_shape=jax.ShapeDtypeStruct((R_HALF, D_MODEL), F32),
        compiler_params=_cparams(("parallel",)),
    )(chip, p, q)


def _share_halves(half):
    def body(h_ref, out_ref, send_sem, recv_sem, local_sem):
        x, y, c, _ = _position()
        mine = pltpu.make_async_copy(h_ref, out_ref.at[c], local_sem)
        mine.start()
        cp = pltpu.make_async_remote_copy(src_ref=h_ref, dst_ref=out_ref.at[c], send_sem=send_sem,
                                          recv_sem=recv_sem, device_id=(x, y, 1 - c), device_id_type=MESH)
        cp.start()
        pltpu.make_async_remote_copy(src_ref=h_ref, dst_ref=out_ref.at[1 - c], send_sem=send_sem,
                                     recv_sem=recv_sem, device_id=(x, y, c), device_id_type=MESH).wait_recv()
        cp.wait_send()
        mine.wait()

    return pl.pallas_call(
        body, name="grad_share_halves", in_specs=[ANY], out_specs=ANY,
        out_shape=jax.ShapeDtypeStruct((2, R_HALF, D_MODEL), half.dtype),
        scratch_shapes=[pltpu.SemaphoreType.DMA, pltpu.SemaphoreType.DMA, pltpu.SemaphoreType.DMA],
    )(half)


def _reduce_scatter(g_packed, chip, core):
    g5 = g_packed.reshape(N_CHIPS, 2, R_HALF, D_MODEL)
    p, p_bf = _add_halves(core, g5, _swap_halves(g5))
    half = _add_chips(chip, p, _exchange_chips(p_bf))
    return _share_halves(half).reshape(R_PACK, D_MODEL)


def _slot(n):
    return -(-n // 16) * 16


def _pad_rows(a, axis):
    n = a.shape[axis]
    widths = [(0, 0)] * a.ndim
    widths[axis] = (0, _slot(n) - n)
    return jnp.pad(a, widths) if _slot(n) != n else a


def _pack_shard(parts):
    rows = [_pad_rows(parts[name].reshape(n, D_MODEL), 0) for name, n in PACK_ROWS]
    used = sum(_slot(n) for _, n in PACK_ROWS)
    return jnp.concatenate(rows + [jnp.zeros((R_PACK - used, D_MODEL), rows[0].dtype)], axis=0)


def _unpack_shard(flat):
    shapes = {"in_a": (1, D_MODEL, IN_A // N_CHIPS), "in_b": (1, D_MODEL, IN_B // N_CHIPS),
              "mem_kv": (DEPTH, D_MODEL // N_CHIPS, 2 * X_WIDTH), "out": (DEPTH, D_MODEL // N_CHIPS, D_MODEL),
              "up": (DEPTH, D_MODEL, D_FF // N_CHIPS), "down": (DEPTH, D_FF // N_CHIPS, D_MODEL)}
    out, off = {}, 0
    for name, n in PACK_ROWS:
        out[name] = flat[off:off + n].reshape(shapes[name])
        off += _slot(n)
    return out


def _unpack_gathered(g):
    out, off = {}, 0
    for name, n in PACK_ROWS:
        out[name] = g[:, off:off + n]
        off += _slot(n)
    s = N_CHIPS
    full = {
        "in_a": out["in_a"].reshape(s, D_MODEL, IN_A // s).transpose(1, 0, 2).reshape(D_MODEL, IN_A),
        "in_b": out["in_b"].reshape(s, D_MODEL, IN_B // s).transpose(1, 0, 2).reshape(D_MODEL, IN_B),
        "mem_kv": out["mem_kv"].reshape(s, DEPTH, D_MODEL // s, 2 * X_WIDTH).transpose(1, 0, 2, 3).reshape(DEPTH, D_MODEL, 2 * X_WIDTH),
        "out": out["out"].reshape(s, DEPTH, D_MODEL // s, D_MODEL).transpose(1, 0, 2, 3).reshape(DEPTH, D_MODEL, D_MODEL),
        "up": out["up"].reshape(s, DEPTH, D_MODEL, D_FF // s).transpose(1, 2, 0, 3).reshape(DEPTH, D_MODEL, D_FF),
        "down": out["down"].reshape(s, DEPTH, D_FF // s, D_MODEL).transpose(1, 0, 2, 3).reshape(DEPTH, D_FF, D_MODEL),
    }
    full["in_a_pad"] = _pad_in_a(full.pop("in_a"))
    return full


def _pack_full_grads(gw):
    s = N_CHIPS
    parts = [
        gw["in_a"].reshape(D_MODEL, s, IN_A // s).transpose(1, 0, 2).reshape(s, -1, D_MODEL),
        gw["in_b"].reshape(D_MODEL, s, IN_B // s).transpose(1, 0, 2).reshape(s, -1, D_MODEL),
        gw["mem_kv"].reshape(DEPTH, s, D_MODEL // s, 2 * X_WIDTH).transpose(1, 0, 2, 3).reshape(s, -1, D_MODEL),
        gw["out"].reshape(DEPTH, s, D_MODEL // s, D_MODEL).transpose(1, 0, 2, 3).reshape(s, -1, D_MODEL),
        gw["up"].reshape(DEPTH, D_MODEL, s, D_FF // s).transpose(2, 0, 1, 3).reshape(s, -1, D_MODEL),
        gw["down"].reshape(DEPTH, s, D_FF // s, D_MODEL).transpose(1, 0, 2, 3).reshape(s, -1, D_MODEL),
    ]
    parts = [_pad_rows(p, 1) for p in parts]
    used = sum(_slot(n) for _, n in PACK_ROWS)
    return jnp.concatenate(parts + [jnp.zeros((s, R_PACK - used, D_MODEL), F32)], axis=1)


def _adamw_math(w, g, m, v):
    m = ADAM_B1 * m + (1.0 - ADAM_B1) * g
    v = ADAM_B2 * v + (1.0 - ADAM_B2) * (g * g)
    m_hat = m / (1.0 - ADAM_B1 ** ADAM_STEP)
    v_hat = v / (1.0 - ADAM_B2 ** ADAM_STEP)
    delta = -ADAM_LR * (m_hat / (jnp.sqrt(v_hat) + ADAM_EPS) + ADAM_WD * w)
    return delta, m, v


ADAM_TILE = 256


def _adamw(w, g, m, v, *, name):
    shape = w.shape
    cols = shape[-1]
    rows = w.size // cols
    tile = min(rows, ADAM_TILE)
    assert rows % tile == 0, (name, shape)

    def body(w_ref, g_ref, m_ref, v_ref, d_ref, nm_ref, nv_ref):
        d_ref[...], nm_ref[...], nv_ref[...] = _adamw_math(w_ref[...], g_ref[...], m_ref[...], v_ref[...])

    spec = pl.BlockSpec((tile, cols), lambda i: (i, 0))
    outs = pl.pallas_call(
        body, name=name, grid=(rows // tile,), in_specs=[spec] * 4, out_specs=[spec] * 3,
        out_shape=[jax.ShapeDtypeStruct((rows, cols), F32)] * 3,
        compiler_params=_cparams(("parallel",)),
    )(*[a.reshape(rows, cols) for a in (w, g, m, v)])
    return [o.reshape(shape) for o in outs]


SMALL_NAMES = (("mem_norm", 8), ("norm_pre_mix", 16), ("norm_post_mix", 16), ("norm_pre_mlp", 16),
               ("norm_post_mlp", 16), ("a_log_a", 1), ("dt_bias_a", 1), ("onorm_a", 1))
SMALL_ROWS = 80
CONV_ROWS = CONV_K * 3 * MIX_W // 128
SMALL_GRAD_ROWS = SMALL_ROWS + CONV_ROWS


def _pack_small(vals):
    rows = []
    for name, n in SMALL_NAMES:
        flat = vals[name].reshape(-1)
        rows.append(jnp.pad(flat, (0, n * 128 - flat.size)).reshape(n, 128))
    used = sum(n for _, n in SMALL_NAMES)
    return jnp.concatenate(rows + [jnp.zeros((SMALL_ROWS - used, 128), F32)], axis=0)


def _unpack_small(packed, like):
    out, off = {}, 0
    for name, n in SMALL_NAMES:
        size = like[name].size
        out[name] = packed[off:off + n].reshape(-1)[:size].reshape(like[name].shape)
        off += n
    return out


def _small_update(gathered, w, m, v):
    def body(g_ref, w_ref, m_ref, v_ref, gs_ref, d_ref, nm_ref, nv_ref):
        g = g_ref[0]
        for dev in range(1, 8):
            g = g + g_ref[dev]
        gs_ref[...] = g
        d_ref[...], nm_ref[...], nv_ref[...] = _adamw_math(w_ref[...], g[:SMALL_ROWS], m_ref[...], v_ref[...])

    small = jax.ShapeDtypeStruct((SMALL_ROWS, 128), F32)
    return pl.pallas_call(
        body, name="small_update",
        out_shape=[jax.ShapeDtypeStruct((SMALL_GRAD_ROWS, 128), F32), small, small, small],
    )(gathered.reshape(8, SMALL_GRAD_ROWS, 128), w, m, v)


def kernel(x, mem, mem_norm, norm_pre_mix, norm_post_mix, norm_pre_mlp, norm_post_mlp, w_in_a, conv_w_a, a_log_a, dt_bias_a, onorm_a, w_in_b, w_mem_kv, w_out, w_up, w_down, loss_target, m_mem_norm, m_norm_pre_mix, m_norm_post_mix, m_norm_pre_mlp, m_norm_post_mlp, m_w_in_a, m_conv_w_a, m_a_log_a, m_dt_bias_a, m_onorm_a, m_w_in_b, m_w_mem_kv, m_w_out, m_w_up, m_w_down, v_mem_norm, v_norm_pre_mix, v_norm_post_mix, v_norm_pre_mlp, v_norm_post_mlp, v_w_in_a, v_conv_w_a, v_a_log_a, v_dt_bias_a, v_onorm_a, v_w_in_b, v_w_mem_kv, v_w_out, v_w_up, v_w_down):
    nb = x.shape[0]
    chip = (2 * lax.axis_index("x") + lax.axis_index("y")).astype(jnp.int32)
    core = lax.axis_index("c").astype(jnp.int32)
    shards = {"in_a": w_in_a, "in_b": w_in_b, "mem_kv": w_mem_kv, "out": w_out, "up": w_up, "down": w_down}
    moments_m = {"in_a": m_w_in_a, "in_b": m_w_in_b, "mem_kv": m_w_mem_kv, "out": m_w_out, "up": m_w_up, "down": m_w_down}
    moments_v = {"in_a": v_w_in_a, "in_b": v_w_in_b, "mem_kv": v_w_mem_kv, "out": v_w_out, "up": v_w_up, "down": v_w_down}
    small_w = {"mem_norm": mem_norm, "norm_pre_mix": norm_pre_mix, "norm_post_mix": norm_post_mix,
               "norm_pre_mlp": norm_pre_mlp, "norm_post_mlp": norm_post_mlp, "a_log_a": a_log_a,
               "dt_bias_a": dt_bias_a, "onorm_a": onorm_a}
    small_m = {"mem_norm": m_mem_norm, "norm_pre_mix": m_norm_pre_mix, "norm_post_mix": m_norm_post_mix,
               "norm_pre_mlp": m_norm_pre_mlp, "norm_post_mlp": m_norm_post_mlp, "a_log_a": m_a_log_a,
               "dt_bias_a": m_dt_bias_a, "onorm_a": m_onorm_a}
    small_v = {"mem_norm": v_mem_norm, "norm_pre_mix": v_norm_pre_mix, "norm_post_mix": v_norm_post_mix,
               "norm_pre_mlp": v_norm_pre_mlp, "norm_post_mlp": v_norm_post_mlp, "a_log_a": v_a_log_a,
               "dt_bias_a": v_dt_bias_a, "onorm_a": v_onorm_a}

    packed = _pack_shard({k: w.astype(BF16) for k, w in shards.items()})
    gathered = _gather_chips(packed.reshape(2, R_HALF, D_MODEL)).reshape(N_CHIPS, R_PACK, D_MODEL)
    wts = _unpack_gathered(gathered)
    conv_rows = CONV_ROWS // N_CHIPS
    conv_blk = jnp.pad(conv_w_a.reshape(conv_rows, 128), ((0, 24 - conv_rows), (0, 0)))
    conv_all = _gather_all(conv_blk, name="gather_conv").reshape(N_CHIPS, 2, 24, 128)[:, 0, :conv_rows]
    conv_full = conv_all.reshape(N_CHIPS, CONV_K, 3 * MIX_W // N_CHIPS).transpose(1, 0, 2).reshape(CONV_K, 3 * MIX_W)

    loss_local, dx, gw, gsmall = _local_step(
        x.reshape(nb * SEQ, D_MODEL), mem.reshape(nb * N_MEM, D_MODEL), loss_target.reshape(nb * SEQ, D_MODEL),
        wts, dict(small_w, conv_w=conv_full))
    loss = lax.psum(loss_local, ("x", "y", "c"))
    grad_x = dx.reshape(nb, SEQ, D_MODEL)

    g_shard = _unpack_shard(_reduce_scatter(_pack_full_grads(gw), chip.reshape(1), core.reshape(1)))
    upd = {k: _adamw(shards[k], g_shard[k], moments_m[k], moments_v[k], name=f"adamw_{k}") for k in shards}

    g_rows = jnp.concatenate([_pack_small(gsmall), gsmall["conv_w"].reshape(CONV_ROWS, 128)], axis=0)
    g_all = _gather_all(g_rows, name="gather_small_grads")
    g_sum, d_small, nm_small, nv_small = _small_update(g_all, _pack_small(small_w), _pack_small(small_m), _pack_small(small_v))
    gs = _unpack_small(g_sum, small_w)
    ds, nms, nvs = (_unpack_small(p, small_w) for p in (d_small, nm_small, nv_small))
    cw = 3 * MIX_W // N_CHIPS
    g_conv = lax.dynamic_slice(g_sum[SMALL_ROWS:].reshape(CONV_K, 3 * MIX_W), (0, chip * cw), (CONV_K, cw)).reshape(conv_w_a.shape)
    d_conv, nm_conv, nv_conv = _adamw(conv_w_a, g_conv, m_conv_w_a, v_conv_w_a, name="adamw_conv")

    order = ("mem_norm", "norm_pre_mix", "norm_post_mix", "norm_pre_mlp", "norm_post_mlp", "in_a", "conv", "a_log_a",
             "dt_bias_a", "onorm_a", "in_b", "mem_kv", "out", "up", "down")
    grads = dict(gs, conv=g_conv, **g_shard)
    deltas = dict(ds, conv=d_conv, **{k: u[0] for k, u in upd.items()})
    new_m = dict(nms, conv=nm_conv, **{k: u[1] for k, u in upd.items()})
    new_v = dict(nvs, conv=nv_conv, **{k: u[2] for k, u in upd.items()})
    return (loss, grad_x, *[grads[k] for k in order], *[deltas[k] for k in order],
            *[new_m[k] for k in order], *[new_v[k] for k in order])
```

```python
import functools

import jax
import jax.numpy as jnp
from jax import lax
from jax.experimental import pallas as pl
from jax.experimental.pallas import tpu as pltpu

F32 = jnp.float32
BF16 = jnp.bfloat16
HIGHEST = lax.Precision.HIGHEST
MESH = pl.DeviceIdType.MESH

D_MODEL = 1024
SEQ = 2048
DEPTH = 2
X_WIDTH = 256
N_X_HEADS = 4
X_HEAD_DIM = 64
MIX_W = 768
LIN_DH = 128
N_LIN = 6
CONV_K = 4
CHUNK = 64
SB_DH = 64
SB_PAIRS = 6
N_MEM = 256
D_FF = 4096
EPS = 1e-6
IN_A = 3340
IN_A_PAD = 3456
IN_B = 2560
SMALL_COL = 26
N_CHIPS = 4

ADAM_LR, ADAM_B1, ADAM_B2, ADAM_EPS, ADAM_WD, ADAM_STEP = 0.001, 0.9, 0.999, 1e-08, 0.01, 10

VMEM_LIMIT = 48 * 1024 * 1024

NN = (((1,), (0,)), ((), ()))
NT = (((1,), (1,)), ((), ()))
TN = (((0,), (0,)), ((), ()))


def _cparams(sem):
    return pltpu.CompilerParams(dimension_semantics=sem, vmem_limit_bytes=VMEM_LIMIT)


def _dotbf(a, b, dn=NN):
    return lax.dot_general(a.astype(BF16), b.astype(BF16), dn, preferred_element_type=F32)


def _split(a):
    hi = a.astype(BF16)
    lo = (a - hi.astype(F32)).astype(BF16)
    return hi, lo


def _dot3(a, b, dn=NN):
    ah, al = _split(a)
    bh, bl = _split(b)
    d = functools.partial(lax.dot_general, dimension_numbers=dn, preferred_element_type=F32)
    return d(ah, bh) + (d(ah, bl) + d(al, bh))


def _dot_mask(a, m01):
    ah, al = _split(a)
    d = functools.partial(lax.dot_general, dimension_numbers=NN, preferred_element_type=F32)
    return d(ah, m01) + d(al, m01)


def _iota(shape, dim):
    return lax.broadcasted_iota(jnp.int32, shape, dim)


def _softplus(x):
    return jnp.maximum(x, 0.0) + jnp.log(1.0 + jnp.exp(-jnp.abs(x)))


def _log_sigmoid(z):
    return jnp.minimum(z, 0.0) - jnp.log(1.0 + jnp.exp(-jnp.abs(z)))


def _rms(x, g):
    r = lax.rsqrt(jnp.mean(x * x, axis=-1, keepdims=True) + EPS)
    return (x * r) * g


def _matmul(a, b, *, mode, tm, tn, tk, name, out_dtypes=(F32,), epilogue=None, extras=(), n_outer=False):
    if n_outer:
        ix = lambda f: (lambda j, i, kk: f(i, j, kk))
    else:
        ix = lambda f: f
    if mode == "nn":
        (m, k), (k2, n) = a.shape, b.shape
        a_spec = pl.BlockSpec((tm, tk), ix(lambda i, j, kk: (i, kk)))
        b_spec = pl.BlockSpec((tk, tn), ix(lambda i, j, kk: (kk, j)))
        dn = NN
    elif mode == "nt":
        (m, k), (n, k2) = a.shape, b.shape
        a_spec = pl.BlockSpec((tm, tk), ix(lambda i, j, kk: (i, kk)))
        b_spec = pl.BlockSpec((tn, tk), ix(lambda i, j, kk: (j, kk)))
        dn = NT
    else:
        (k, m), (k2, n) = a.shape, b.shape
        a_spec = pl.BlockSpec((tk, tm), ix(lambda i, j, kk: (kk, i)))
        b_spec = pl.BlockSpec((tk, tn), ix(lambda i, j, kk: (kk, j)))
        dn = TN
    assert k == k2 and m % tm == 0 and n % tn == 0 and k % tk == 0, (name, a.shape, b.shape)
    assert a.dtype == BF16 and b.dtype == BF16, name
    nk = k // tk
    n_extra, n_out = len(extras), len(out_dtypes)

    def finish(acc, extra_refs, out_refs):
        outs = (acc,) if epilogue is None else epilogue(acc, *[r[...] for r in extra_refs])
        for o_ref, o in zip(out_refs, outs):
            o_ref[...] = o.astype(o_ref.dtype)

    def body_single(a_ref, b_ref, *rest):
        acc = lax.dot_general(a_ref[...], b_ref[...], dn, preferred_element_type=F32)
        finish(acc, rest[:n_extra], rest[n_extra:n_extra + n_out])

    def body_tiled(a_ref, b_ref, *rest):
        extra_refs, out_refs, acc_ref = rest[:n_extra], rest[n_extra:n_extra + n_out], rest[-1]
        kk = pl.program_id(2)

        @pl.when(kk == 0)
        def _():
            acc_ref[...] = jnp.zeros_like(acc_ref)

        acc_ref[...] += lax.dot_general(a_ref[...], b_ref[...], dn, preferred_element_type=F32)

        @pl.when(kk == nk - 1)
        def _():
            finish(acc_ref[...], extra_refs, out_refs)

    mn_spec = pl.BlockSpec((tm, tn), ix(lambda i, j, kk: (i, j)))
    grid = (n // tn, m // tm, nk) if n_outer else (m // tm, n // tn, nk)
    outs = pl.pallas_call(
        body_single if nk == 1 else body_tiled,
        name=name,
        grid=grid,
        in_specs=[a_spec, b_spec] + [mn_spec] * n_extra,
        out_specs=[mn_spec] * n_out,
        out_shape=[jax.ShapeDtypeStruct((m, n), dt) for dt in out_dtypes],
        scratch_shapes=[] if nk == 1 else [pltpu.VMEM((tm, tn), F32)],
        compiler_params=_cparams(("parallel", "parallel", "arbitrary")),
    )(a, b, *extras)
    return outs[0] if n_out == 1 else outs


ROW_TILE = 512


def _row_spec(width=D_MODEL, tile=ROW_TILE):
    return pl.BlockSpec((tile, width), lambda i: (i, 0))


def _vec_spec(width=D_MODEL):
    return pl.BlockSpec((1, width), lambda i: (0, 0))


def _rms_fwd(x, g, *, name, tile=ROW_TILE):
    t = x.shape[0]

    def body(x_ref, g_ref, h_ref):
        h_ref[...] = _rms(x_ref[...], g_ref[...]).astype(BF16)

    return pl.pallas_call(
        body, name=name, grid=(t // tile,),
        in_specs=[_row_spec(tile=tile), _vec_spec()], out_specs=_row_spec(tile=tile),
        out_shape=jax.ShapeDtypeStruct((t, D_MODEL), BF16),
        compiler_params=_cparams(("parallel",)),
    )(x, g.reshape(1, D_MODEL))


def _post_norm_add(xres, y, g_post, g_next, *, name):
    t = xres.shape[0]

    def body(x_ref, y_ref, gp_ref, gn_ref, xo_ref, h_ref):
        xo = x_ref[...] + _rms(y_ref[...], gp_ref[...])
        xo_ref[...] = xo
        h_ref[...] = _rms(xo, gn_ref[...]).astype(BF16)

    return pl.pallas_call(
        body, name=name, grid=(t // ROW_TILE,),
        in_specs=[_row_spec(), _row_spec(), _vec_spec(), _vec_spec()],
        out_specs=[_row_spec(), _row_spec()],
        out_shape=[jax.ShapeDtypeStruct((t, D_MODEL), F32), jax.ShapeDtypeStruct((t, D_MODEL), BF16)],
        compiler_params=_cparams(("parallel",)),
    )(xres, y, g_post.reshape(1, D_MODEL), g_next.reshape(1, D_MODEL))


def _post_norm_loss(xres, y, g_post, target, *, name):
    t = xres.shape[0]

    def body(x_ref, y_ref, gp_ref, t_ref, loss_ref, dx_ref):
        @pl.when(pl.program_id(0) == 0)
        def _():
            loss_ref[...] = jnp.zeros_like(loss_ref)

        err = (x_ref[...] + _rms(y_ref[...], gp_ref[...])) - t_ref[...]
        per_tok = jnp.mean(err * err, axis=-1, keepdims=True)
        loss_ref[...] += 0.5 * jnp.sum(per_tok, axis=0, keepdims=True)
        dx_ref[...] = err * (1.0 / D_MODEL)

    return pl.pallas_call(
        body, name=name, grid=(t // ROW_TILE,),
        in_specs=[_row_spec(), _row_spec(), _vec_spec(), _row_spec()],
        out_specs=[pl.BlockSpec((1, 128), lambda i: (0, 0)), _row_spec()],
        out_shape=[jax.ShapeDtypeStruct((1, 128), F32), jax.ShapeDtypeStruct((t, D_MODEL), F32)],
        compiler_params=_cparams(("arbitrary",)),
    )(xres, y, g_post.reshape(1, D_MODEL), target)


def _rms_bwd(dy, x, g, *, name, res=None, out_dtype=F32, tile=ROW_TILE):
    t = x.shape[0]
    has_res = res is not None

    def body(dy_ref, x_ref, g_ref, *rest):
        res_ref = rest[0] if has_res else None
        dx_ref, dg_ref = rest[-2], rest[-1]

        @pl.when(pl.program_id(0) == 0)
        def _():
            dg_ref[...] = jnp.zeros_like(dg_ref)

        xf = x_ref[...]
        dyf = dy_ref[...].astype(F32)
        r = lax.rsqrt(jnp.mean(xf * xf, axis=-1, keepdims=True) + EPS)
        xhat = xf * r
        dg_ref[...] += jnp.sum(dyf * xhat, axis=0, keepdims=True)
        dxh = dyf * g_ref[...]
        dx = r * (dxh - xhat * jnp.mean(dxh * xhat, axis=-1, keepdims=True))
        if has_res:
            dx = dx + res_ref[...]
        dx_ref[...] = dx.astype(dx_ref.dtype)

    args = [dy, x, g.reshape(1, D_MODEL)] + ([res] if has_res else [])
    return pl.pallas_call(
        body, name=name, grid=(t // tile,),
        in_specs=[_row_spec(tile=tile), _row_spec(tile=tile), _vec_spec()] + ([_row_spec(tile=tile)] if has_res else []),
        out_specs=[_row_spec(tile=tile), _vec_spec()],
        out_shape=[jax.ShapeDtypeStruct((t, D_MODEL), out_dtype), jax.ShapeDtypeStruct((1, D_MODEL), F32)],
        compiler_params=_cparams(("arbitrary",)),
    )(*args)


CONV_COLS = 256
N_CONV_BLOCKS = 3 * MIX_W // CONV_COLS


def _shift_down(x, k):
    if k == 0:
        return x
    return jnp.where(_iota(x.shape, 0) >= k, pltpu.roll(x, k, 0), 0.0)


def _shift_up(x, k):
    if k == 0:
        return x
    s = x.shape[0]
    return jnp.where(_iota(x.shape, 0) < s - k, pltpu.roll(x, s - k, 0), 0.0)


def _conv_pre(x, w_ref):
    c = w_ref[CONV_K - 1:CONV_K, :] * x
    for i in range(CONV_K - 1):
        c = c + w_ref[i:i + 1, :] * _shift_down(x, CONV_K - 1 - i)
    return c


def _conv_silu_fwd(proj, conv_w, n_batch):
    def body(x_ref, w_ref, y_ref):
        c = _conv_pre(x_ref[...], w_ref)
        y_ref[...] = c * jax.nn.sigmoid(c)

    return pl.pallas_call(
        body, name="conv_silu_fwd", grid=(n_batch, N_CONV_BLOCKS),
        in_specs=[pl.BlockSpec((SEQ, CONV_COLS), lambda b, j: (b, j)),
                  pl.BlockSpec((CONV_K, CONV_COLS), lambda b, j: (0, j))],
        out_specs=pl.BlockSpec((SEQ, CONV_COLS), lambda b, j: (b, j)),
        out_shape=jax.ShapeDtypeStruct((n_batch * SEQ, 3 * MIX_W), F32),
        compiler_params=_cparams(("parallel", "parallel")),
    )(proj, conv_w)


def _conv_silu_bwd(dy, proj, conv_w, n_batch):
    def body(dy_ref, x_ref, w_ref, dx_ref, dw_ref):
        @pl.when(pl.program_id(1) == 0)
        def _():
            dw_ref[...] = jnp.zeros_like(dw_ref)

        x = x_ref[...]
        c = _conv_pre(x, w_ref)
        sig = jax.nn.sigmoid(c)
        dc = dy_ref[...] * (sig * (1.0 + c * (1.0 - sig)))
        dx = w_ref[CONV_K - 1:CONV_K, :] * dc
        dw_ref[CONV_K - 1:CONV_K, :] += jnp.sum(dc * x, axis=0, keepdims=True)
        for i in range(CONV_K - 1):
            k = CONV_K - 1 - i
            dx = dx + w_ref[i:i + 1, :] * _shift_up(dc, k)
            dw_ref[i:i + 1, :] += jnp.sum(dc * _shift_down(x, k), axis=0, keepdims=True)
        dx_ref[...] = dx.astype(BF16)

    return pl.pallas_call(
        body, name="conv_silu_bwd", grid=(N_CONV_BLOCKS, n_batch),
        in_specs=[pl.BlockSpec((SEQ, CONV_COLS), lambda j, b: (b, j)),
                  pl.BlockSpec((SEQ, CONV_COLS), lambda j, b: (b, j)),
                  pl.BlockSpec((CONV_K, CONV_COLS), lambda j, b: (0, j))],
        out_specs=[pl.BlockSpec((SEQ, CONV_COLS), lambda j, b: (b, j)),
                   pl.BlockSpec((CONV_K, CONV_COLS), lambda j, b: (0, j))],
        out_shape=[jax.ShapeDtypeStruct((n_batch * SEQ, 3 * MIX_W), BF16),
                   jax.ShapeDtypeStruct((CONV_K, 3 * MIX_W), F32)],
        compiler_params=_cparams(("parallel", "arbitrary")),
    )(dy, proj, conv_w)


@jax.custom_vjp
def _solve_apply(low, rhs, tinv):
    return _dot3(tinv, rhs)


def _solve_apply_fwd(low, rhs, tinv):
    sol = _dot3(tinv, rhs)
    return sol, (tinv, sol)


def _solve_apply_bwd(resid, g):
    tinv, sol = resid
    y = _dot3(tinv, g, TN)
    return -_dot3(y, sol, NT), y, jnp.zeros_like(tinv)


_solve_apply.defvjp(_solve_apply_fwd, _solve_apply_bwd)


def _inv_unit_lower(lows):
    c = lows[0].shape[0]
    eye = (_iota((c, c), 0) == _iota((c, c), 1)).astype(F32)
    ms = [-low for low in lows]
    ps = [eye + m for m in ms]
    for _ in range(5):
        ms = [_dot3(m, m) for m in ms]
        ps = [p + _dot3(p, m) for p, m in zip(ps, ms)]
    return ps


def _gdn_chunk(qs, ks, vs, gates, states, small, alog_row, dtb_row, gain_row, tinvs):
    c = small.shape[0]
    heads = range(N_LIN)
    lane = _iota((c, 128), 1)
    row, col = _iota((c, c), 0), _iota((c, c), 1)
    causal, strict = row >= col, row > col
    last = _iota((c, 1), 0) == c - 1

    beta_all = jax.nn.sigmoid(small)
    g_all = -jnp.exp(alog_row) * _softplus(small + dtb_row)
    ltri = (col <= row).astype(F32)
    gc_all = lax.dot_general(ltri, g_all, NN, precision=HIGHEST, preferred_element_type=F32)

    beta = [jnp.sum(jnp.where(lane == h, beta_all, 0.0), axis=1, keepdims=True) for h in heads]
    gc = [jnp.sum(jnp.where(lane == N_LIN + h, gc_all, 0.0), axis=1, keepdims=True) for h in heads]
    gc_j = [lax.dot_general((lane == N_LIN + h).astype(F32), gc_all, NT, precision=HIGHEST,
                            preferred_element_type=F32) for h in heads]
    decay = [jnp.where(causal, jnp.exp(jnp.where(causal, gc[h] - gc_j[h], 0.0)), 0.0) for h in heads]
    gc_last = [jnp.sum(jnp.where(last, gc[h], 0.0), axis=0, keepdims=True) for h in heads]
    egc = [jnp.exp(g) for g in gc]
    qn = [q * lax.rsqrt(jnp.sum(q * q, axis=-1, keepdims=True) + EPS) * (LIN_DH ** -0.5) for q in qs]
    kn = [k * lax.rsqrt(jnp.sum(k * k, axis=-1, keepdims=True) + EPS) for k in ks]
    kb = [kn[h] * beta[h] for h in heads]
    low = [jnp.where(strict, _dotbf(kb[h], kn[h], NT) * decay[h], 0.0) for h in heads]
    if tinvs is None:
        tinvs = _inv_unit_lower(low)
    u = [_solve_apply(low[h], vs[h] * beta[h], tinvs[h]) for h in heads]
    w = [_solve_apply(low[h], kb[h] * egc[h], tinvs[h]) for h in heads]
    intra = [_dotbf(qn[h], kn[h], NT) * decay[h] for h in heads]
    v_new = [u[h] - _dotbf(w[h], states[h]) for h in heads]
    o = [_dotbf(qn[h] * egc[h], states[h]) + _dotbf(intra[h], v_new[h]) for h in heads]
    new_states = [states[h] * jnp.exp(gc_last[h]) + _dotbf(kn[h] * jnp.exp(gc_last[h] - gc[h]), v_new[h], TN)
                  for h in heads]
    o = [x * lax.rsqrt(jnp.mean(x * x, axis=-1, keepdims=True) + EPS) * gain_row for x in o]
    outs = [o[h] * (gates[h] * jax.nn.sigmoid(gates[h])) for h in heads]
    return outs, new_states, tinvs


def _gdn_param_rows(a_log, dt_bias, onorm):
    alog_row = jnp.zeros((1, 128), F32).at[0, N_LIN:2 * N_LIN].set(a_log)
    dtb_row = jnp.zeros((1, 128), F32).at[0, N_LIN:2 * N_LIN].set(dt_bias)
    return alog_row, dtb_row, onorm.reshape(1, LIN_DH)


def _head(ref_or_val, h):
    return ref_or_val[:, LIN_DH * h:LIN_DH * (h + 1)]


def _gdn_fwd(qkv, proj, alog_row, dtb_row, gain_row, n_batch):
    nc = SEQ // CHUNK
    t = n_batch * SEQ

    def body(qkv_ref, small_ref, gate_ref, al_ref, dt_ref, gn_ref, mix_ref, st_ref, ti_ref, s_scr):
        @pl.when(pl.program_id(1) == 0)
        def _():
            s_scr[...] = jnp.zeros_like(s_scr)

        heads = range(N_LIN)
        states = [s_scr[h] for h in heads]
        outs, new_states, tinvs = _gdn_chunk(
            [_head(qkv_ref, h) for h in heads], [_head(qkv_ref, N_LIN + h) for h in heads],
            [_head(qkv_ref, 2 * N_LIN + h) for h in heads], [_head(gate_ref, h) for h in heads],
            states, small_ref[...], al_ref[...], dt_ref[...], gn_ref[...], None)
        mix_ref[...] = jnp.concatenate(outs, axis=1).astype(BF16)
        for h in heads:
            st_ref[0, 0, h] = states[h]
            s_scr[h] = new_states[h]
            ti_ref[0, 0, h] = tinvs[h]

    row = lambda b, n: b * nc + n
    vec = pl.BlockSpec((1, 128), lambda b, n: (0, 0))
    return pl.pallas_call(
        body, name="gdn_fwd", grid=(n_batch, nc),
        in_specs=[pl.BlockSpec((CHUNK, 3 * MIX_W), lambda b, n: (row(b, n), 0)),
                  pl.BlockSpec((CHUNK, 128), lambda b, n: (row(b, n), SMALL_COL)),
                  pl.BlockSpec((CHUNK, MIX_W), lambda b, n: (row(b, n), 3)),
                  vec, vec, vec],
        out_specs=[pl.BlockSpec((CHUNK, MIX_W), lambda b, n: (row(b, n), 0)),
                   pl.BlockSpec((1, 1, N_LIN, LIN_DH, LIN_DH), lambda b, n: (b, n, 0, 0, 0)),
                   pl.BlockSpec((1, 1, N_LIN, CHUNK, CHUNK), lambda b, n: (b, n, 0, 0, 0))],
        out_shape=[jax.ShapeDtypeStruct((t, MIX_W), BF16),
                   jax.ShapeDtypeStruct((n_batch, nc, N_LIN, LIN_DH, LIN_DH), F32),
                   jax.ShapeDtypeStruct((n_batch, nc, N_LIN, CHUNK, CHUNK), F32)],
        scratch_shapes=[pltpu.VMEM((N_LIN, LIN_DH, LIN_DH), F32)],
        compiler_params=_cparams(("parallel", "arbitrary")),
    )(qkv, proj, proj, alog_row, dtb_row, gain_row)


def _gdn_bwd(dcat, qkv, proj, states, tinvs, alog_row, dtb_row, gain_row, n_batch):
    nc = SEQ // CHUNK
    t = n_batch * SEQ

    def body(dmix_ref, qkv_ref, small_ref, gate_ref, st_ref, ti_ref, al_ref, dt_ref, gn_ref,
             dqkv_ref, dgate_ref, dsmall_ref, dal_ref, ddt_ref, dgn_ref, ds_scr):
        @pl.when(pl.program_id(1) == 0)
        def _():
            ds_scr[...] = jnp.zeros_like(ds_scr)

        @pl.when((pl.program_id(0) == 0) & (pl.program_id(1) == 0))
        def _():
            dal_ref[...] = jnp.zeros_like(dal_ref)
            ddt_ref[...] = jnp.zeros_like(ddt_ref)
            dgn_ref[...] = jnp.zeros_like(dgn_ref)

        heads = range(N_LIN)
        tinvs = [ti_ref[0, 0, h] for h in heads]

        def chunk(qs, ks, vs, gates, states_in, small, al, dt, gn):
            outs, new_states, _ = _gdn_chunk(qs, ks, vs, gates, states_in, small, al, dt, gn, tinvs)
            return tuple(outs), tuple(new_states)

        prim = (tuple(_head(qkv_ref, h) for h in heads),
                tuple(_head(qkv_ref, N_LIN + h) for h in heads),
                tuple(_head(qkv_ref, 2 * N_LIN + h) for h in heads),
                tuple(_head(gate_ref, h) for h in heads),
                tuple(st_ref[0, 0, h] for h in heads),
                small_ref[...], al_ref[...], dt_ref[...], gn_ref[...])
        _, vjp = jax.vjp(chunk, *prim)
        cot = (tuple(_head(dmix_ref, h) for h in heads), tuple(ds_scr[h] for h in heads))
        dq, dk, dv, dgate, dstate, dsmall, dal, ddt, dgn = vjp(cot)
        dqkv_ref[...] = jnp.concatenate(list(dq) + list(dk) + list(dv), axis=1)
        dgate_ref[...] = jnp.concatenate(list(dgate), axis=1).astype(BF16)
        dsmall_ref[...] = dsmall.astype(BF16)
        for h in heads:
            ds_scr[h] = dstate[h]
        dal_ref[...] += dal
        ddt_ref[...] += ddt
        dgn_ref[...] += dgn

    row = lambda b, n: b * nc + (nc - 1 - n)
    vec = pl.BlockSpec((1, 128), lambda b, n: (0, 0))
    return pl.pallas_call(
        body, name="gdn_bwd", grid=(n_batch, nc),
        in_specs=[pl.BlockSpec((CHUNK, MIX_W), lambda b, n: (row(b, n), 0)),
                  pl.BlockSpec((CHUNK, 3 * MIX_W), lambda b, n: (row(b, n), 0)),
                  pl.BlockSpec((CHUNK, 128), lambda b, n: (row(b, n), SMALL_COL)),
                  pl.BlockSpec((CHUNK, MIX_W), lambda b, n: (row(b, n), 3)),
                  pl.BlockSpec((1, 1, N_LIN, LIN_DH, LIN_DH), lambda b, n: (b, nc - 1 - n, 0, 0, 0)),
                  pl.BlockSpec((1, 1, N_LIN, CHUNK, CHUNK), lambda b, n: (b, nc - 1 - n, 0, 0, 0)),
                  vec, vec, vec],
        out_specs=[pl.BlockSpec((CHUNK, 3 * MIX_W), lambda b, n: (row(b, n), 0)),
                   pl.BlockSpec((CHUNK, MIX_W), lambda b, n: (row(b, n), 0)),
                   pl.BlockSpec((CHUNK, 128), lambda b, n: (row(b, n), 0)),
                   vec, vec, vec],
        out_shape=[jax.ShapeDtypeStruct((t, 3 * MIX_W), F32),
                   jax.ShapeDtypeStruct((t, MIX_W), BF16),
                   jax.ShapeDtypeStruct((t, 128), BF16),
                   jax.ShapeDtypeStruct((1, 128), F32),
                   jax.ShapeDtypeStruct((1, 128), F32),
                   jax.ShapeDtypeStruct((1, 128), F32)],
        scratch_shapes=[pltpu.VMEM((N_LIN, LIN_DH, LIN_DH), F32)],
        compiler_params=_cparams(("arbitrary", "arbitrary")),
    )(dcat, qkv, proj, proj, states, tinvs, alog_row, dtb_row, gain_row)


SB_T = 256


def _sb_masks():
    r, c = _iota((SB_T, SB_T), 0), _iota((SB_T, SB_T), 1)
    return r, c


def _sb_rows(kb):
    start = kb * SB_T
    return pl.ds(start if isinstance(kb, int) else pl.multiple_of(start, SB_T), SB_T)


def _sb_fwd(proj, n_batch):
    nq = SEQ // SB_T
    t = n_batch * SEQ
    scale = SB_DH ** -0.5
    both = range(2)

    def body(q_ref, k_ref, v_ref, o_ref, tot_ref, acc_scr, run_scr):
        qi = pl.program_id(2)
        lane = _iota((SB_T, 128), 1)
        r, c = _sb_masks()
        upper = (r > c).astype(BF16)
        q = q_ref[...] * scale
        qm = [jnp.where((lane < SB_DH) == (hh == 0), q, jnp.zeros_like(q)) for hh in both]

        def blocks(kbs, diagonal):
            ch = [(n, hh) for n in range(len(kbs)) for hh in both]
            k_blk = [k_ref[_sb_rows(kb), :] for kb in kbs]
            v_blk = [v_ref[_sb_rows(kb), :] for kb in kbs]
            z = [lax.dot_general(qm[hh], k_blk[kb], NT, preferred_element_type=F32) for kb, hh in ch]
            ld = [jnp.log(1.0 + jnp.exp(-jnp.abs(x))) for x in z]
            lb = [jnp.minimum(x, 0.0) - y for x, y in zip(z, ld)]
            l1m = [x - y for x, y in zip(lb, z)]
            if diagonal:
                l1m = [jnp.where(r > c, x, 0.0) for x in l1m]
            tail = [_dot_mask(x, upper) for x in l1m]
            rowsum = [jnp.sum(x, axis=1, keepdims=True) for x in l1m]
            if not diagonal:
                run = [run_scr[hh][:, 0:1] for hh in both]
                for n, (kb, hh) in enumerate(ch):
                    tail[n] = run[hh] + tail[n]
                    run[hh] = run[hh] + rowsum[n]
            a = [jnp.exp(x + y) for x, y in zip(lb, tail)]
            if diagonal:
                a = [jnp.where(r > c, x, 0.0) for x in a]
            pv = [lax.dot_general(a[n].astype(BF16), v_blk[kb], NN, preferred_element_type=F32)
                  for n, (kb, hh) in enumerate(ch)]
            for hh in both:
                mine = [n for n, (kb, h2) in enumerate(ch) if h2 == hh]
                if diagonal:
                    acc_scr[hh] = pv[mine[0]]
                    run_scr[hh] = jnp.broadcast_to(rowsum[mine[0]], (SB_T, 128))
                else:
                    acc_scr[hh] += sum(pv[n] for n in mine[1:]) + pv[mine[0]]
                    run_scr[hh] += sum(rowsum[n] for n in mine[1:]) + rowsum[mine[0]]

        blocks([qi], True)

        def step(it, carry):
            kb = qi - 1 - 2 * it
            blocks([kb, kb - 1], False)
            return carry

        lax.fori_loop(0, qi >> 1, step, 0)

        @pl.when((qi & 1) == 1)
        def _():
            blocks([0], False)
        first = lane < SB_DH
        o_ref[...] = jnp.where(first, acc_scr[0], acc_scr[1]).astype(BF16)
        tot_ref[...] = jnp.where(first, run_scr[0], run_scr[1])

    nq_blocks = lambda b, p, i: (b * nq + i, p)
    seq_spec = lambda off: pl.BlockSpec((SEQ, 128), lambda b, p, i: (b, off + p))
    return pl.pallas_call(
        body, name="sb_fwd", grid=(n_batch, SB_PAIRS, nq),
        in_specs=[pl.BlockSpec((SB_T, 128), nq_blocks), seq_spec(SB_PAIRS), seq_spec(2 * SB_PAIRS)],
        out_specs=[pl.BlockSpec((SB_T, 128), nq_blocks), pl.BlockSpec((SB_T, 128), nq_blocks)],
        out_shape=[jax.ShapeDtypeStruct((t, MIX_W), BF16), jax.ShapeDtypeStruct((t, MIX_W), F32)],
        scratch_shapes=[pltpu.VMEM((2, SB_T, 128), F32), pltpu.VMEM((2, SB_T, 128), F32)],
        compiler_params=_cparams(("parallel", "parallel", "arbitrary")),
    )(proj, proj, proj)


def _sb_bwd(dcat, proj, totals, n_batch):
    nq = SEQ // SB_T
    t = n_batch * SEQ
    scale = SB_DH ** -0.5
    both = range(2)

    def body(do_ref, q_ref, k_ref, v_ref, tot_ref, dq_ref, dk_ref, dv_ref, dq_scr, run_scr, grun_scr):
        qi = pl.program_id(2)

        @pl.when(qi == 0)
        def _():
            dk_ref[...] = jnp.zeros_like(dk_ref)
            dv_ref[...] = jnp.zeros_like(dv_ref)

        lane = _iota((SB_T, 128), 1)
        r, c = _sb_masks()
        incl = (r <= c).astype(BF16)
        before = (r < c).astype(BF16)
        dq_scr[...] = jnp.zeros_like(dq_scr)
        run_scr[...] = jnp.zeros_like(run_scr)
        grun_scr[...] = jnp.zeros_like(grun_scr)
        q, do, tot = q_ref[...] * scale, do_ref[...], tot_ref[...]
        sel = [(lane < SB_DH) == (hh == 0) for hh in both]
        qm = [jnp.where(sel[hh], q, jnp.zeros_like(q)) for hh in both]
        dom = [jnp.where(sel[hh], do, 0.0).astype(BF16) for hh in both]
        total = [jnp.sum(jnp.where(lane == hh * SB_DH, tot, 0.0), axis=1, keepdims=True) for hh in both]

        def block(kb, diagonal):
            ks = pl.ds(pl.multiple_of(kb * SB_T, SB_T), SB_T)
            k_blk, v_blk = k_ref[ks, :], v_ref[ks, :]
            z = [lax.dot_general(qm[hh], k_blk, NT, preferred_element_type=F32) for hh in both]
            da = [lax.dot_general(dom[hh], v_blk, NT, preferred_element_type=F32) for hh in both]
            ld = [jnp.log(1.0 + jnp.exp(-jnp.abs(x))) for x in z]
            lb = [jnp.minimum(z[hh], 0.0) - ld[hh] for hh in both]
            sig = [jnp.exp(x) for x in lb]
            l1m = [lb[hh] - z[hh] for hh in both]
            if diagonal:
                l1m = [jnp.where(r > c, x, 0.0) for x in l1m]
            prefix = [run_scr[hh][:, 0:1] + _dot_mask(l1m[hh], incl) for hh in both]
            a = [jnp.exp(lb[hh] + (total[hh] - prefix[hh])) for hh in both]
            if diagonal:
                a = [jnp.where(r > c, x, 0.0) for x in a]
            de = [a[hh] * da[hh] for hh in both]
            dv = [lax.dot_general(a[hh].astype(BF16), dom[hh], TN, preferred_element_type=F32) for hh in both]
            dl1m = [grun_scr[hh][:, 0:1] + _dot_mask(de[hh], before) for hh in both]
            if diagonal:
                dl1m = [jnp.where(r > c, x, 0.0) for x in dl1m]
            dz = [(de[hh] * (1.0 - sig[hh]) - dl1m[hh] * sig[hh]).astype(BF16) for hh in both]
            dqp = [lax.dot_general(dz[hh], k_blk, NN, preferred_element_type=F32) for hh in both]
            dk = [lax.dot_general(dz[hh], qm[hh], TN, preferred_element_type=F32) for hh in both]
            rs_l = [jnp.sum(x, axis=1, keepdims=True) for x in l1m]
            rs_e = [jnp.sum(x, axis=1, keepdims=True) for x in de]
            for hh in both:
                dq_scr[hh] += dqp[hh]
                run_scr[hh] += rs_l[hh]
                grun_scr[hh] += rs_e[hh]
            dk_ref[ks, :] += dk[0] + dk[1]
            dv_ref[ks, :] += dv[0] + dv[1]

        def step(kb, carry):
            block(kb, False)
            return carry

        lax.fori_loop(0, qi, step, 0)
        block(qi, True)
        dq_ref[...] = (jnp.where(sel[0], dq_scr[0], dq_scr[1]) * scale).astype(BF16)

    q_blocks = lambda b, p, i: (b * nq + i, p)
    seq_spec = lambda off: pl.BlockSpec((SEQ, 128), lambda b, p, i: (b, off + p))
    dq, dk, dv = pl.pallas_call(
        body, name="sb_bwd", grid=(n_batch, SB_PAIRS, nq),
        in_specs=[pl.BlockSpec((SB_T, 128), q_blocks), pl.BlockSpec((SB_T, 128), q_blocks),
                  seq_spec(SB_PAIRS), seq_spec(2 * SB_PAIRS), pl.BlockSpec((SB_T, 128), q_blocks)],
        out_specs=[pl.BlockSpec((SB_T, 128), q_blocks), seq_spec(0), seq_spec(0)],
        out_shape=[jax.ShapeDtypeStruct((t, MIX_W), BF16), jax.ShapeDtypeStruct((t, MIX_W), F32),
                   jax.ShapeDtypeStruct((t, MIX_W), F32)],
        scratch_shapes=[pltpu.VMEM((2, SB_T, 128), F32), pltpu.VMEM((2, SB_T, 128), F32),
                        pltpu.VMEM((2, SB_T, 128), F32)],
        compiler_params=_cparams(("parallel", "parallel", "arbitrary")),
    )(dcat, proj, proj, proj, totals)
    return dq, dk, dv


MEM_TQ = 512


def _mem_heads(lane):
    return [(lane >= X_HEAD_DIM * h) & (lane < X_HEAD_DIM * (h + 1)) for h in range(N_X_HEADS)]


def _mem_attn_fwd(proj, q_col, memkv, n_batch):
    nq = SEQ // MEM_TQ
    t = n_batch * SEQ
    scale = X_HEAD_DIM ** -0.5

    def body(q_ref, kv_ref, o_ref):
        q = q_ref[...]
        k = kv_ref[:, :X_WIDTH].astype(BF16)
        v = kv_ref[:, X_WIDTH:].astype(BF16)
        out = jnp.zeros((MEM_TQ, X_WIDTH), F32)
        for sel in _mem_heads(_iota((MEM_TQ, X_WIDTH), 1)):
            s = lax.dot_general(jnp.where(sel, q, 0.0).astype(BF16), k, NT, preferred_element_type=F32) * scale
            e = jnp.exp(s - jnp.max(s, axis=-1, keepdims=True))
            p = e / jnp.sum(e, axis=-1, keepdims=True)
            out = out + jnp.where(sel, lax.dot_general(p.astype(BF16), v, NN, preferred_element_type=F32), 0.0)
        o_ref[...] = out.astype(BF16)

    return pl.pallas_call(
        body, name="mem_attn_fwd", grid=(n_batch, nq),
        in_specs=[pl.BlockSpec((MEM_TQ, X_WIDTH), lambda b, i: (b * nq + i, q_col)),
                  pl.BlockSpec((N_MEM, 2 * X_WIDTH), lambda b, i: (b, 0))],
        out_specs=pl.BlockSpec((MEM_TQ, X_WIDTH), lambda b, i: (b * nq + i, 0)),
        out_shape=jax.ShapeDtypeStruct((t, X_WIDTH), BF16),
        compiler_params=_cparams(("parallel", "parallel")),
    )(proj, memkv)


def _mem_attn_bwd(dcat, proj, q_col, memkv, n_batch):
    nq = SEQ // MEM_TQ
    t = n_batch * SEQ
    scale = X_HEAD_DIM ** -0.5

    def body(do_ref, q_ref, kv_ref, dq_ref, dkv_ref):
        @pl.when(pl.program_id(1) == 0)
        def _():
            dkv_ref[...] = jnp.zeros_like(dkv_ref)

        q, do = q_ref[...], do_ref[...]
        k = kv_ref[:, :X_WIDTH].astype(BF16)
        v = kv_ref[:, X_WIDTH:].astype(BF16)
        dq = jnp.zeros((MEM_TQ, X_WIDTH), F32)
        dk = jnp.zeros((N_MEM, X_WIDTH), F32)
        dv = jnp.zeros((N_MEM, X_WIDTH), F32)
        for sel in _mem_heads(_iota((MEM_TQ, X_WIDTH), 1)):
            qm = jnp.where(sel, q, 0.0).astype(BF16)
            dom = jnp.where(sel, do, 0.0).astype(BF16)
            s = lax.dot_general(qm, k, NT, preferred_element_type=F32) * scale
            e = jnp.exp(s - jnp.max(s, axis=-1, keepdims=True))
            p = e / jnp.sum(e, axis=-1, keepdims=True)
            dp = lax.dot_general(dom, v, NT, preferred_element_type=F32)
            ds = ((p * (dp - jnp.sum(dp * p, axis=-1, keepdims=True))) * scale).astype(BF16)
            dv = dv + lax.dot_general(p.astype(BF16), dom, TN, preferred_element_type=F32)
            dk = dk + lax.dot_general(ds, qm, TN, preferred_element_type=F32)
            dq = dq + jnp.where(sel, lax.dot_general(ds, k, NN, preferred_element_type=F32), 0.0)
        dq_ref[...] = dq.astype(BF16)
        dkv_ref[...] += jnp.concatenate([dk, dv], axis=1)

    return pl.pallas_call(
        body, name="mem_attn_bwd", grid=(n_batch, nq),
        in_specs=[pl.BlockSpec((MEM_TQ, X_WIDTH), lambda b, i: (b * nq + i, 3)),
                  pl.BlockSpec((MEM_TQ, X_WIDTH), lambda b, i: (b * nq + i, q_col)),
                  pl.BlockSpec((N_MEM, 2 * X_WIDTH), lambda b, i: (b, 0))],
        out_specs=[pl.BlockSpec((MEM_TQ, X_WIDTH), lambda b, i: (b * nq + i, 0)),
                   pl.BlockSpec((N_MEM, 2 * X_WIDTH), lambda b, i: (b, 0))],
        out_shape=[jax.ShapeDtypeStruct((t, X_WIDTH), BF16),
                   jax.ShapeDtypeStruct((n_batch * N_MEM, 2 * X_WIDTH), F32)],
        compiler_params=_cparams(("parallel", "arbitrary")),
    )(dcat, proj, memkv)


def _relu2_epilogue(acc):
    r = jnp.maximum(acc, 0.0)
    return (r * r,)


def _relu2_bwd_epilogue(acc, a):
    return (acc * (2.0 * jnp.sqrt(a.astype(F32))),)


def _pad_in_a(w_in_a):
    w = 3 * MIX_W
    parts = [w_in_a[:, :w], w_in_a[:, w:w + MIX_W], w_in_a[:, IN_A - X_WIDTH:],
             w_in_a[:, w + MIX_W:w + MIX_W + 2 * N_LIN]]
    pad = jnp.zeros((D_MODEL, IN_A_PAD - IN_A), w_in_a.dtype)
    return jnp.concatenate(parts + [pad], axis=1)


def _unpad_in_a(g):
    w = 3 * MIX_W
    return jnp.concatenate([g[:, :w + MIX_W], g[:, w + MIX_W + X_WIDTH:w + MIX_W + X_WIDTH + 2 * N_LIN],
                            g[:, w + MIX_W:w + MIX_W + X_WIDTH]], axis=1)


def _local_step(x, mem, target, wts, small):
    t = x.shape[0]
    nb = t // SEQ
    npre, npost, mpre, mpost = small["norm_pre_mix"], small["norm_post_mix"], small["norm_pre_mlp"], small["norm_post_mlp"]
    alog_row, dtb_row, gain_row = _gdn_param_rows(small["a_log_a"][0], small["dt_bias_a"][0], small["onorm_a"][0])
    conv_w = small["conv_w"]

    mem_n = _rms_fwd(mem, small["mem_norm"], name="mem_norm_fwd", tile=256)
    saved = []
    h = _rms_fwd(x, npre[0], name="pre_mix_norm0")
    big = min(1024, t)
    for i in range(DEPTH):
        s = {"x_in": x, "h1": h}
        if i == 0:
            proj = _matmul(h, wts["in_a_pad"], mode="nn", tm=big, tn=1152, tk=1024, name="proj_a")
            qkv = _conv_silu_fwd(proj, conv_w, nb)
            mix, states, tinvs = _gdn_fwd(qkv, proj, alog_row, dtb_row, gain_row, nb)
            s.update(qkv=qkv, states=states, tinvs=tinvs)
            q_col = (3 * MIX_W + MIX_W) // X_WIDTH
        else:
            proj = _matmul(h, wts["in_b"], mode="nn", tm=big, tn=1280, tk=1024, name="proj_b", out_dtypes=(BF16,))
            mix, totals = _sb_fwd(proj, nb)
            s.update(totals=totals)
            q_col = 3 * MIX_W // X_WIDTH
        memkv = _matmul(mem_n, wts["mem_kv"][i], mode="nn", tm=256, tn=512, tk=1024, name=f"memkv{i}")
        cross = _mem_attn_fwd(proj, q_col, memkv, nb)
        cat = jnp.concatenate([mix, cross], axis=1)
        y = _matmul(cat, wts["out"][i], mode="nn", tm=big, tn=1024, tk=1024, name=f"out_proj{i}")
        x2, h2 = _post_norm_add(x, y, npost[i], mpre[i], name=f"post_mix{i}")
        a = _matmul(h2, wts["up"][i], mode="nn", tm=big, tn=2048, tk=1024, name=f"up{i}",
                    out_dtypes=(BF16,), epilogue=_relu2_epilogue, n_outer=True)
        y2 = _matmul(a, wts["down"][i], mode="nn", tm=big, tn=1024, tk=2048, name=f"down{i}")
        s.update(proj=proj, q_col=q_col, memkv=memkv, cat=cat, y=y, x2=x2, h2=h2, a=a, y2=y2)
        saved.append(s)
        if i + 1 < DEPTH:
            x, h = _post_norm_add(x2, y2, mpost[i], npre[i + 1], name=f"post_mlp{i}")
        else:
            loss_row, dx = _post_norm_loss(x2, y2, mpost[i], target, name="loss_head")

    gw = {"mem_kv": [None] * DEPTH, "out": [None] * DEPTH, "up": [None] * DEPTH, "down": [None] * DEPTH}
    gs = {k: [None] * DEPTH for k in ("norm_pre_mix", "norm_post_mix", "norm_pre_mlp", "norm_post_mlp")}
    dmem_n = None
    for i in reversed(range(DEPTH)):
        s = saved[i]
        dy2, gs["norm_post_mlp"][i] = _rms_bwd(dx, s["y2"], mpost[i], name=f"post_mlp_bwd{i}", out_dtype=BF16)
        du = _matmul(dy2, wts["down"][i], mode="nt", tm=big, tn=2048, tk=1024, name=f"down_dx{i}",
                     out_dtypes=(BF16,), epilogue=_relu2_bwd_epilogue, extras=(s["a"],), n_outer=True)
        gw["down"][i] = _matmul(s["a"], dy2, mode="tn", tm=1024, tn=1024, tk=big, name=f"down_dw{i}")
        dh2 = _matmul(du, wts["up"][i], mode="nt", tm=big, tn=1024, tk=2048, name=f"up_dx{i}")
        gw["up"][i] = _matmul(s["h2"], du, mode="tn", tm=1024, tn=2048, tk=512, name=f"up_dw{i}")
        dx2, gs["norm_pre_mlp"][i] = _rms_bwd(dh2, s["x2"], mpre[i], name=f"pre_mlp_bwd{i}", res=dx)
        dy, gs["norm_post_mix"][i] = _rms_bwd(dx2, s["y"], npost[i], name=f"post_mix_bwd{i}", out_dtype=BF16)
        dcat = _matmul(dy, wts["out"][i], mode="nt", tm=big, tn=1024, tk=1024, name=f"out_dx{i}")
        gw["out"][i] = _matmul(s["cat"], dy, mode="tn", tm=1024, tn=1024, tk=big, name=f"out_dw{i}")
        dmemq, dmemkv = _mem_attn_bwd(dcat, s["proj"], s["q_col"], s["memkv"], nb)
        dmemkv = dmemkv.astype(BF16)
        gw["mem_kv"][i] = _matmul(mem_n, dmemkv, mode="tn", tm=1024, tn=512, tk=256, name=f"memkv_dw{i}")
        dmn = _matmul(dmemkv, wts["mem_kv"][i], mode="nt", tm=256, tn=1024, tk=512, name=f"memkv_dx{i}")
        dmem_n = dmn if dmem_n is None else dmem_n + dmn
        if i == 0:
            dqkv, dgate, dsmall, dalog, ddtb, dgain = _gdn_bwd(
                dcat, s["qkv"], s["proj"], s["states"], s["tinvs"], alog_row, dtb_row, gain_row, nb)
            dqkv_pre, dconv = _conv_silu_bwd(dqkv, s["proj"], conv_w, nb)
            dproj = jnp.concatenate([dqkv_pre, dgate, dmemq, dsmall], axis=1)
            w_in, tile = wts["in_a_pad"], 1152
        else:
            dq, dk, dv = _sb_bwd(dcat, s["proj"], s["totals"], nb)
            dproj = jnp.concatenate([dq, dk.astype(BF16), dv.astype(BF16), dmemq], axis=1)
            w_in, tile = wts["in_b"], 1280
        dh1 = _matmul(dproj, w_in, mode="nt", tm=big, tn=1024, tk=tile, name=f"proj_dx{i}")
        g_in = _matmul(s["h1"], dproj, mode="tn", tm=1024, tn=tile, tk=big, name=f"proj_dw{i}")
        if i == 0:
            gw["in_a"] = _unpad_in_a(g_in)
        else:
            gw["in_b"] = g_in
        dx, gs["norm_pre_mix"][i] = _rms_bwd(dh1, s["x_in"], npre[i], name=f"pre_mix_bwd{i}", res=dx2)

    _, g_mem_norm = _rms_bwd(dmem_n, mem, small["mem_norm"], name="mem_norm_bwd", tile=256)
    gsmall = {k: jnp.concatenate(v, axis=0) for k, v in gs.items()}
    gsmall.update(mem_norm=g_mem_norm[0], a_log_a=dalog[:, N_LIN:2 * N_LIN], dt_bias_a=ddtb[:, N_LIN:2 * N_LIN],
                  onorm_a=dgain, conv_w=dconv)
    return loss_row[0, 0], dx, gw, gsmall


PACK_ROWS = (("in_a", 835), ("in_b", 640), ("mem_kv", 256), ("out", 512), ("up", 2048), ("down", 2048))
R_PACK = 6400
R_HALF = R_PACK // 2
SUM_TILE = 640
ANY = pl.BlockSpec(memory_space=pl.ANY)


def _position():
    x, y, c = lax.axis_index("x"), lax.axis_index("y"), lax.axis_index("c")
    others = [(1 - x, y), (x, 1 - y), (1 - x, 1 - y)]
    return x, y, c, others


def _gather_chips(wflat):
    def body(w_ref, out_ref, send_sems, recv_sems):
        x, y, c, others = _position()
        me = 2 * x + y

        def copy(k, src, dst, to):
            return pltpu.make_async_remote_copy(src_ref=src, dst_ref=dst, send_sem=send_sems.at[k],
                                                recv_sem=recv_sems.at[k], device_id=to, device_id_type=MESH)

        first = [copy(j, w_ref.at[me, c], out_ref.at[me, c], (ox, oy, c)) for j, (ox, oy) in enumerate(others)]
        for cp in first:
            cp.start()
        passed = [copy(3 + j, out_ref.at[2 * ox + oy, c], out_ref.at[2 * ox + oy, c], (x, y, 1 - c))
                  for j, (ox, oy) in enumerate(others)]
        for j, (ox, oy) in enumerate(others):
            copy(j, w_ref.at[me, c], out_ref.at[2 * ox + oy, c], (x, y, c)).wait_recv()
            passed[j].start()
        for j, (ox, oy) in enumerate(others):
            copy(3 + j, w_ref.at[me, c], out_ref.at[2 * ox + oy, 1 - c], (x, y, c)).wait_recv()
        for cp in first + passed:
            cp.wait_send()

    return pl.pallas_call(
        body, name="gather_weights",
        in_specs=[ANY], out_specs=ANY, input_output_aliases={0: 0},
        out_shape=jax.ShapeDtypeStruct((N_CHIPS, 2, R_HALF, D_MODEL), wflat.dtype),
        scratch_shapes=[pltpu.SemaphoreType.DMA((6,)), pltpu.SemaphoreType.DMA((6,))],
    )(wflat)


def _gather_all(v, *, name):
    rows, n = v.shape

    def body(x_ref, out_ref, send_sems, recv_sems, local_sem):
        x, y, c, others = _position()
        me, sibling = (x, y, c), (x, y, 1 - c)

        def blk(px, py, pc):
            return out_ref.at[pl.ds((4 * px + 2 * py + pc) * rows, rows), :]

        def copy(k, block, to, src=None):
            return pltpu.make_async_remote_copy(src_ref=blk(*block) if src is None else src, dst_ref=blk(*block),
                                                send_sem=send_sems.at[k], recv_sem=recv_sems.at[k],
                                                device_id=to, device_id_type=MESH)

        mine = pltpu.make_async_copy(x_ref, blk(*me), local_sem)
        mine.start()
        first = [copy(0, me, sibling, src=x_ref)]
        first += [copy(1 + j, me, (*chip, c), src=x_ref) for j, chip in enumerate(others)]
        for cp in first:
            cp.start()
        passed = [copy(4 + j, (*chip, c), sibling) for j, chip in enumerate(others)]
        for j, chip in enumerate(others):
            copy(1 + j, (*chip, c), me).wait_recv()
            passed[j].start()
        copy(0, sibling, me).wait_recv()
        for j, chip in enumerate(others):
            copy(4 + j, (*chip, 1 - c), me).wait_recv()
        for cp in first + passed:
            cp.wait_send()
        mine.wait()

    vmem = pl.BlockSpec(memory_space=pltpu.VMEM)
    return pl.pallas_call(
        body, name=name, in_specs=[vmem], out_specs=vmem,
        out_shape=jax.ShapeDtypeStruct((8 * rows, n), v.dtype),
        scratch_shapes=[pltpu.SemaphoreType.DMA((7,)), pltpu.SemaphoreType.DMA((7,)), pltpu.SemaphoreType.DMA],
    )(v)


def _swap_halves(g5):
    def body(g_ref, out_ref, send_sem, recv_sem):
        x, y, c, _ = _position()
        cp = pltpu.make_async_remote_copy(src_ref=g_ref.at[:, 1 - c], dst_ref=out_ref, send_sem=send_sem,
                                          recv_sem=recv_sem, device_id=(x, y, 1 - c), device_id_type=MESH)
        cp.start()
        cp.wait()

    return pl.pallas_call(
        body, name="grad_swap_halves", in_specs=[ANY], out_specs=ANY,
        out_shape=jax.ShapeDtypeStruct((N_CHIPS, R_HALF, D_MODEL), g5.dtype),
        scratch_shapes=[pltpu.SemaphoreType.DMA, pltpu.SemaphoreType.DMA],
    )(g5)


def _add_halves(core, g5, got):
    def body(c_ref, a_ref, b_ref, o_ref, ob_ref):
        s = a_ref[0] + b_ref[...]
        o_ref[...] = s
        ob_ref[...] = s.astype(BF16)

    nt = R_HALF // SUM_TILE
    spec = pl.BlockSpec((1, SUM_TILE, D_MODEL), lambda s, i, c_ref: (s, i, 0))
    return pl.pallas_call(
        body, name="grad_add_halves",
        grid_spec=pltpu.PrefetchScalarGridSpec(
            num_scalar_prefetch=1, grid=(N_CHIPS, nt),
            in_specs=[pl.BlockSpec((1, 1, SUM_TILE, D_MODEL), lambda s, i, c_ref: (s, c_ref[0], i, 0)), spec],
            out_specs=[spec, spec]),
        out_shape=[jax.ShapeDtypeStruct((N_CHIPS, R_HALF, D_MODEL), F32),
                   jax.ShapeDtypeStruct((N_CHIPS, R_HALF, D_MODEL), BF16)],
        compiler_params=_cparams(("parallel", "parallel")),
    )(core, g5, got)


def _exchange_chips(p):
    def body(p_ref, q_ref, send_sems, recv_sems):
        x, y, c, others = _position()
        copies = [pltpu.make_async_remote_copy(src_ref=p_ref.at[2 * ox + oy], dst_ref=q_ref.at[j],
                                               send_sem=send_sems.at[j], recv_sem=recv_sems.at[j],
                                               device_id=(ox, oy, c), device_id_type=MESH)
                  for j, (ox, oy) in enumerate(others)]
        for cp in copies:
            cp.start()
        for cp in copies:
            cp.wait()

    return pl.pallas_call(
        body, name="grad_exchange_chips", in_specs=[ANY], out_specs=ANY,
        out_shape=jax.ShapeDtypeStruct((3, R_HALF, D_MODEL), p.dtype),
        scratch_shapes=[pltpu.SemaphoreType.DMA((3,)), pltpu.SemaphoreType.DMA((3,))],
    )(p)


def _add_chips(chip_core, p, q):
    def body(kc_ref, p_ref, q_ref, o_ref):
        o_ref[0] = ((p_ref[0] + q_ref[0].astype(F32)) + q_ref[1].astype(F32)) + q_ref[2].astype(F32)

    nt = R_HALF // SUM_TILE
    return pl.pallas_call(
        body, name="grad_add_chips",
        grid_spec=pltpu.PrefetchScalarGridSpec(
            num_scalar_prefetch=1, grid=(nt,),
            in_specs=[pl.BlockSpec((1, SUM_TILE, D_MODEL), lambda i, kc_ref: (kc_ref[0], i, 0)),
                      pl.BlockSpec((3, SUM_TILE, D_MODEL), lambda i, kc_ref: (0, i, 0))],
            out_specs=pl.BlockSpec((1, SUM_TILE, D_MODEL), lambda i, kc_ref: (kc_ref[1], i, 0))),
        out_shape=jax.ShapeDtypeStruct((2, R_HALF, D_MODEL), F32),
        compiler_params=_cparams(("parallel",)),
    )(chip_core, p, q)


def _share_halves(halves):
    def body(h_ref, out_ref, send_sem, recv_sem):
        x, y, c, _ = _position()
        cp = pltpu.make_async_remote_copy(src_ref=h_ref.at[c], dst_ref=out_ref.at[c], send_sem=send_sem,
                                          recv_sem=recv_sem, device_id=(x, y, 1 - c), device_id_type=MESH)
        cp.start()
        pltpu.make_async_remote_copy(src_ref=h_ref.at[c], dst_ref=out_ref.at[1 - c], send_sem=send_sem,
                                     recv_sem=recv_sem, device_id=(x, y, c), device_id_type=MESH).wait_recv()
        cp.wait_send()

    return pl.pallas_call(
        body, name="grad_share_halves", in_specs=[ANY], out_specs=ANY, input_output_aliases={0: 0},
        out_shape=jax.ShapeDtypeStruct((2, R_HALF, D_MODEL), halves.dtype),
        scratch_shapes=[pltpu.SemaphoreType.DMA, pltpu.SemaphoreType.DMA],
    )(halves)


def _reduce_scatter(g_packed, chip, core):
    g5 = g_packed.reshape(N_CHIPS, 2, R_HALF, D_MODEL)
    p, p_bf = _add_halves(core.reshape(1), g5, _swap_halves(g5))
    halves = _add_chips(jnp.stack([chip, core]), p, _exchange_chips(p_bf))
    return _share_halves(halves).reshape(R_PACK, D_MODEL)


def _slot(n):
    return -(-n // 16) * 16


def _pad_rows(a, axis):
    n = a.shape[axis]
    widths = [(0, 0)] * a.ndim
    widths[axis] = (0, _slot(n) - n)
    return jnp.pad(a, widths) if _slot(n) != n else a


def _pack_shard(parts):
    rows = [_pad_rows(parts[name].reshape(n, D_MODEL), 0) for name, n in PACK_ROWS]
    used = sum(_slot(n) for _, n in PACK_ROWS)
    return jnp.concatenate(rows + [jnp.zeros((R_PACK - used, D_MODEL), rows[0].dtype)], axis=0)


def _unpack_shard(flat):
    shapes = {"in_a": (1, D_MODEL, IN_A // N_CHIPS), "in_b": (1, D_MODEL, IN_B // N_CHIPS),
              "mem_kv": (DEPTH, D_MODEL // N_CHIPS, 2 * X_WIDTH), "out": (DEPTH, D_MODEL // N_CHIPS, D_MODEL),
              "up": (DEPTH, D_MODEL, D_FF // N_CHIPS), "down": (DEPTH, D_FF // N_CHIPS, D_MODEL)}
    out, off = {}, 0
    for name, n in PACK_ROWS:
        out[name] = flat[off:off + n].reshape(shapes[name])
        off += _slot(n)
    return out


def _unpack_gathered(g):
    out, off = {}, 0
    for name, n in PACK_ROWS:
        out[name] = g[:, off:off + n]
        off += _slot(n)
    s = N_CHIPS
    full = {
        "in_a": out["in_a"].reshape(s, D_MODEL, IN_A // s).transpose(1, 0, 2).reshape(D_MODEL, IN_A),
        "in_b": out["in_b"].reshape(s, D_MODEL, IN_B // s).transpose(1, 0, 2).reshape(D_MODEL, IN_B),
        "mem_kv": out["mem_kv"].reshape(s, DEPTH, D_MODEL // s, 2 * X_WIDTH).transpose(1, 0, 2, 3).reshape(DEPTH, D_MODEL, 2 * X_WIDTH),
        "out": out["out"].reshape(s, DEPTH, D_MODEL // s, D_MODEL).transpose(1, 0, 2, 3).reshape(DEPTH, D_MODEL, D_MODEL),
        "up": out["up"].reshape(s, DEPTH, D_MODEL, D_FF // s).transpose(1, 2, 0, 3).reshape(DEPTH, D_MODEL, D_FF),
        "down": out["down"].reshape(s, DEPTH, D_FF // s, D_MODEL).transpose(1, 0, 2, 3).reshape(DEPTH, D_FF, D_MODEL),
    }
    full["in_a_pad"] = _pad_in_a(full.pop("in_a"))
    return full


def _pack_full_grads(gw):
    s = N_CHIPS
    cols = lambda g: g.reshape(D_MODEL, s, -1).transpose(1, 0, 2).reshape(s, -1, D_MODEL)
    rows = lambda g: g.reshape(s, -1, D_MODEL)
    parts = [cols(gw["in_a"]), cols(gw["in_b"])]
    parts += [rows(g) for g in gw["mem_kv"]] + [rows(g) for g in gw["out"]]
    parts += [cols(g) for g in gw["up"]] + [rows(g) for g in gw["down"]]
    parts = [_pad_rows(p, 1) for p in parts]
    used = sum(_slot(n) for _, n in PACK_ROWS)
    return jnp.concatenate(parts + [jnp.zeros((s, R_PACK - used, D_MODEL), F32)], axis=1)


def _adamw_math(w, g, m, v):
    m = ADAM_B1 * m + (1.0 - ADAM_B1) * g
    v = ADAM_B2 * v + (1.0 - ADAM_B2) * (g * g)
    m_hat = m / (1.0 - ADAM_B1 ** ADAM_STEP)
    v_hat = v / (1.0 - ADAM_B2 ** ADAM_STEP)
    delta = -ADAM_LR * (m_hat / (jnp.sqrt(v_hat) + ADAM_EPS) + ADAM_WD * w)
    return delta, m, v


ADAM_TILE = 256


def _adamw(w, g, m, v, *, name):
    shape = w.shape
    cols = shape[-1]
    rows = w.size // cols
    tile = min(rows, ADAM_TILE)
    assert rows % tile == 0, (name, shape)

    def body(w_ref, g_ref, m_ref, v_ref, d_ref, nm_ref, nv_ref):
        d_ref[...], nm_ref[...], nv_ref[...] = _adamw_math(w_ref[...], g_ref[...], m_ref[...], v_ref[...])

    spec = pl.BlockSpec((tile, cols), lambda i: (i, 0))
    outs = pl.pallas_call(
        body, name=name, grid=(rows // tile,), in_specs=[spec] * 4, out_specs=[spec] * 3,
        out_shape=[jax.ShapeDtypeStruct((rows, cols), F32)] * 3,
        compiler_params=_cparams(("parallel",)),
    )(*[a.reshape(rows, cols) for a in (w, g, m, v)])
    return [o.reshape(shape) for o in outs]


SMALL_NAMES = (("mem_norm", 8), ("norm_pre_mix", 16), ("norm_post_mix", 16), ("norm_pre_mlp", 16),
               ("norm_post_mlp", 16), ("a_log_a", 1), ("dt_bias_a", 1), ("onorm_a", 1))
SMALL_ROWS = 80
CONV_ROWS = CONV_K * 3 * MIX_W // 128
SMALL_GRAD_ROWS = SMALL_ROWS + CONV_ROWS


def _pack_small(vals):
    rows = []
    for name, n in SMALL_NAMES:
        flat = vals[name].reshape(-1)
        rows.append(jnp.pad(flat, (0, n * 128 - flat.size)).reshape(n, 128))
    used = sum(n for _, n in SMALL_NAMES)
    return jnp.concatenate(rows + [jnp.zeros((SMALL_ROWS - used, 128), F32)], axis=0)


def _unpack_small(packed, like):
    out, off = {}, 0
    for name, n in SMALL_NAMES:
        size = like[name].size
        out[name] = packed[off:off + n].reshape(-1)[:size].reshape(like[name].shape)
        off += n
    return out


def _small_update(gathered, w, m, v):
    def body(g_ref, w_ref, m_ref, v_ref, gs_ref, d_ref, nm_ref, nv_ref):
        g = g_ref[0]
        for dev in range(1, 8):
            g = g + g_ref[dev]
        gs_ref[...] = g
        d_ref[...], nm_ref[...], nv_ref[...] = _adamw_math(w_ref[...], g[:SMALL_ROWS], m_ref[...], v_ref[...])

    small = jax.ShapeDtypeStruct((SMALL_ROWS, 128), F32)
    return pl.pallas_call(
        body, name="small_update",
        out_shape=[jax.ShapeDtypeStruct((SMALL_GRAD_ROWS, 128), F32), small, small, small],
    )(gathered.reshape(8, SMALL_GRAD_ROWS, 128), w, m, v)


def kernel(x, mem, mem_norm, norm_pre_mix, norm_post_mix, norm_pre_mlp, norm_post_mlp, w_in_a, conv_w_a, a_log_a, dt_bias_a, onorm_a, w_in_b, w_mem_kv, w_out, w_up, w_down, loss_target, m_mem_norm, m_norm_pre_mix, m_norm_post_mix, m_norm_pre_mlp, m_norm_post_mlp, m_w_in_a, m_conv_w_a, m_a_log_a, m_dt_bias_a, m_onorm_a, m_w_in_b, m_w_mem_kv, m_w_out, m_w_up, m_w_down, v_mem_norm, v_norm_pre_mix, v_norm_post_mix, v_norm_pre_mlp, v_norm_post_mlp, v_w_in_a, v_conv_w_a, v_a_log_a, v_dt_bias_a, v_onorm_a, v_w_in_b, v_w_mem_kv, v_w_out, v_w_up, v_w_down):
    nb = x.shape[0]
    chip = (2 * lax.axis_index("x") + lax.axis_index("y")).astype(jnp.int32)
    core = lax.axis_index("c").astype(jnp.int32)
    shards = {"in_a": w_in_a, "in_b": w_in_b, "mem_kv": w_mem_kv, "out": w_out, "up": w_up, "down": w_down}
    moments_m = {"in_a": m_w_in_a, "in_b": m_w_in_b, "mem_kv": m_w_mem_kv, "out": m_w_out, "up": m_w_up, "down": m_w_down}
    moments_v = {"in_a": v_w_in_a, "in_b": v_w_in_b, "mem_kv": v_w_mem_kv, "out": v_w_out, "up": v_w_up, "down": v_w_down}
    small_w = {"mem_norm": mem_norm, "norm_pre_mix": norm_pre_mix, "norm_post_mix": norm_post_mix,
               "norm_pre_mlp": norm_pre_mlp, "norm_post_mlp": norm_post_mlp, "a_log_a": a_log_a,
               "dt_bias_a": dt_bias_a, "onorm_a": onorm_a}
    small_m = {"mem_norm": m_mem_norm, "norm_pre_mix": m_norm_pre_mix, "norm_post_mix": m_norm_post_mix,
               "norm_pre_mlp": m_norm_pre_mlp, "norm_post_mlp": m_norm_post_mlp, "a_log_a": m_a_log_a,
               "dt_bias_a": m_dt_bias_a, "onorm_a": m_onorm_a}
    small_v = {"mem_norm": v_mem_norm, "norm_pre_mix": v_norm_pre_mix, "norm_post_mix": v_norm_post_mix,
               "norm_pre_mlp": v_norm_pre_mlp, "norm_post_mlp": v_norm_post_mlp, "a_log_a": v_a_log_a,
               "dt_bias_a": v_dt_bias_a, "onorm_a": v_onorm_a}

    packed = _pack_shard({k: w.astype(BF16) for k, w in shards.items()})
    slabs = lax.dynamic_update_slice(jnp.zeros((N_CHIPS, R_PACK, D_MODEL), BF16), packed[None], (chip, 0, 0))
    gathered = _gather_chips(slabs.reshape(N_CHIPS, 2, R_HALF, D_MODEL)).reshape(N_CHIPS, R_PACK, D_MODEL)
    wts = _unpack_gathered(gathered)
    conv_rows = CONV_ROWS // N_CHIPS
    conv_blk = jnp.pad(conv_w_a.reshape(conv_rows, 128), ((0, 24 - conv_rows), (0, 0)))
    conv_all = _gather_all(conv_blk, name="gather_conv").reshape(N_CHIPS, 2, 24, 128)[:, 0, :conv_rows]
    conv_full = conv_all.reshape(N_CHIPS, CONV_K, 3 * MIX_W // N_CHIPS).transpose(1, 0, 2).reshape(CONV_K, 3 * MIX_W)

    loss_local, dx, gw, gsmall = _local_step(
        x.reshape(nb * SEQ, D_MODEL), mem.reshape(nb * N_MEM, D_MODEL), loss_target.reshape(nb * SEQ, D_MODEL),
        wts, dict(small_w, conv_w=conv_full))
    loss = lax.psum(loss_local, ("x", "y", "c"))
    grad_x = dx.reshape(nb, SEQ, D_MODEL)

    g_shard = _unpack_shard(_reduce_scatter(_pack_full_grads(gw), chip, core))
    upd = {k: _adamw(shards[k], g_shard[k], moments_m[k], moments_v[k], name=f"adamw_{k}") for k in shards}

    g_rows = jnp.concatenate([_pack_small(gsmall), gsmall["conv_w"].reshape(CONV_ROWS, 128)], axis=0)
    g_all = _gather_all(g_rows, name="gather_small_grads")
    g_sum, d_small, nm_small, nv_small = _small_update(g_all, _pack_small(small_w), _pack_small(small_m), _pack_small(small_v))
    gs = _unpack_small(g_sum, small_w)
    ds, nms, nvs = (_unpack_small(p, small_w) for p in (d_small, nm_small, nv_small))
    cw = 3 * MIX_W // N_CHIPS
    g_conv = lax.dynamic_slice(g_sum[SMALL_ROWS:].reshape(CONV_K, 3 * MIX_W), (0, chip * cw), (CONV_K, cw)).reshape(conv_w_a.shape)
    d_conv, nm_conv, nv_conv = _adamw(conv_w_a, g_conv, m_conv_w_a, v_conv_w_a, name="adamw_conv")

    order = ("mem_norm", "norm_pre_mix", "norm_post_mix", "norm_pre_mlp", "norm_post_mlp", "in_a", "conv", "a_log_a",
             "dt_bias_a", "onorm_a", "in_b", "mem_kv", "out", "up", "down")
    grads = dict(gs, conv=g_conv, **g_shard)
    deltas = dict(ds, conv=d_conv, **{k: u[0] for k, u in upd.items()})
    new_m = dict(nms, conv=nm_conv, **{k: u[1] for k, u in upd.items()})
    new_v = dict(nvs, conv=nv_conv, **{k: u[2] for k, u in upd.items()})
    return (loss, grad_x, *[grads[k] for k in order], *[deltas[k] for k in order],
            *[new_m[k] for k in order], *[new_v[k] for k in order])
```

```python
import functools

import jax
import jax.numpy as jnp
from jax import lax
from jax.experimental import pallas as pl
from jax.experimental.pallas import tpu as pltpu

F32 = jnp.float32
BF16 = jnp.bfloat16
HIGHEST = lax.Precision.HIGHEST
MESH = pl.DeviceIdType.MESH

D_MODEL = 1024
SEQ = 2048
DEPTH = 2
X_WIDTH = 256
N_X_HEADS = 4
X_HEAD_DIM = 64
MIX_W = 768
LIN_DH = 128
N_LIN = 6
CONV_K = 4
CHUNK = 64
SB_DH = 64
SB_PAIRS = 6
N_MEM = 256
D_FF = 4096
EPS = 1e-6
IN_A = 3340
IN_A_PAD = 3456
IN_B = 2560
SMALL_COL = 26
N_CHIPS = 4

ADAM_LR, ADAM_B1, ADAM_B2, ADAM_EPS, ADAM_WD, ADAM_STEP = 0.001, 0.9, 0.999, 1e-08, 0.01, 10

VMEM_LIMIT = 48 * 1024 * 1024

ANY = pl.BlockSpec(memory_space=pl.ANY)

NN = (((1,), (0,)), ((), ()))
NT = (((1,), (1,)), ((), ()))
TN = (((0,), (0,)), ((), ()))


def _cparams(sem):
    return pltpu.CompilerParams(dimension_semantics=sem, vmem_limit_bytes=VMEM_LIMIT)


def _dotbf(a, b, dn=NN):
    return lax.dot_general(a.astype(BF16), b.astype(BF16), dn, preferred_element_type=F32)


def _split(a):
    hi = a.astype(BF16)
    lo = (a - hi.astype(F32)).astype(BF16)
    return hi, lo


def _dot3(a, b, dn=NN):
    ah, al = _split(a)
    bh, bl = _split(b)
    d = functools.partial(lax.dot_general, dimension_numbers=dn, preferred_element_type=F32)
    return d(ah, bh) + (d(ah, bl) + d(al, bh))


def _dot_mask(parts, m01):
    d = functools.partial(lax.dot_general, dimension_numbers=NN, preferred_element_type=F32)
    return d(parts[0], m01) + d(parts[1], m01)


def _iota(shape, dim):
    return lax.broadcasted_iota(jnp.int32, shape, dim)


def _softplus(x):
    return jnp.maximum(x, 0.0) + jnp.log(1.0 + jnp.exp(-jnp.abs(x)))


def _log_sigmoid(z):
    return jnp.minimum(z, 0.0) - jnp.log(1.0 + jnp.exp(-jnp.abs(z)))


def _rms(x, g):
    r = lax.rsqrt(jnp.mean(x * x, axis=-1, keepdims=True) + EPS)
    return (x * r) * g


def _matmul(a, b, *, mode, tm, tn, tk, name, out_dtypes=(F32,), epilogue=None, extras=(), n_outer=False):
    if n_outer:
        ix = lambda f: (lambda j, i, kk: f(i, j, kk))
    else:
        ix = lambda f: f
    if mode == "nn":
        (m, k), (k2, n) = a.shape, b.shape
        a_spec = pl.BlockSpec((tm, tk), ix(lambda i, j, kk: (i, kk)))
        b_spec = pl.BlockSpec((tk, tn), ix(lambda i, j, kk: (kk, j)))
        dn = NN
    elif mode == "nt":
        (m, k), (n, k2) = a.shape, b.shape
        a_spec = pl.BlockSpec((tm, tk), ix(lambda i, j, kk: (i, kk)))
        b_spec = pl.BlockSpec((tn, tk), ix(lambda i, j, kk: (j, kk)))
        dn = NT
    else:
        (k, m), (k2, n) = a.shape, b.shape
        a_spec = pl.BlockSpec((tk, tm), ix(lambda i, j, kk: (kk, i)))
        b_spec = pl.BlockSpec((tk, tn), ix(lambda i, j, kk: (kk, j)))
        dn = TN
    assert k == k2 and m % tm == 0 and n % tn == 0 and k % tk == 0, (name, a.shape, b.shape)
    assert a.dtype == BF16 and b.dtype == BF16, name
    nk = k // tk
    n_extra, n_out = len(extras), len(out_dtypes)

    def finish(acc, extra_refs, out_refs):
        outs = (acc,) if epilogue is None else epilogue(acc, *[r[...] for r in extra_refs])
        for o_ref, o in zip(out_refs, outs):
            o_ref[...] = o.astype(o_ref.dtype)

    def body_single(a_ref, b_ref, *rest):
        acc = lax.dot_general(a_ref[...], b_ref[...], dn, preferred_element_type=F32)
        finish(acc, rest[:n_extra], rest[n_extra:n_extra + n_out])

    def body_tiled(a_ref, b_ref, *rest):
        extra_refs, out_refs, acc_ref = rest[:n_extra], rest[n_extra:n_extra + n_out], rest[-1]
        kk = pl.program_id(2)

        @pl.when(kk == 0)
        def _():
            acc_ref[...] = jnp.zeros_like(acc_ref)

        acc_ref[...] += lax.dot_general(a_ref[...], b_ref[...], dn, preferred_element_type=F32)

        @pl.when(kk == nk - 1)
        def _():
            finish(acc_ref[...], extra_refs, out_refs)

    mn_spec = pl.BlockSpec((tm, tn), ix(lambda i, j, kk: (i, j)))
    grid = (n // tn, m // tm, nk) if n_outer else (m // tm, n // tn, nk)
    outs = pl.pallas_call(
        body_single if nk == 1 else body_tiled,
        name=name,
        grid=grid,
        in_specs=[a_spec, b_spec] + [mn_spec] * n_extra,
        out_specs=[mn_spec] * n_out,
        out_shape=[jax.ShapeDtypeStruct((m, n), dt) for dt in out_dtypes],
        scratch_shapes=[] if nk == 1 else [pltpu.VMEM((tm, tn), F32)],
        compiler_params=_cparams(("parallel", "parallel", "arbitrary")),
    )(a, b, *extras)
    return outs[0] if n_out == 1 else outs


ROW_TILE = 512


def _row_spec(width=D_MODEL, tile=ROW_TILE):
    return pl.BlockSpec((tile, width), lambda i: (i, 0))


def _vec_spec(width=D_MODEL):
    return pl.BlockSpec((1, width), lambda i: (0, 0))


def _rms_fwd(x, g, *, name, tile=ROW_TILE):
    t = x.shape[0]

    def body(x_ref, g_ref, h_ref):
        h_ref[...] = _rms(x_ref[...], g_ref[...]).astype(BF16)

    return pl.pallas_call(
        body, name=name, grid=(t // tile,),
        in_specs=[_row_spec(tile=tile), _vec_spec()], out_specs=_row_spec(tile=tile),
        out_shape=jax.ShapeDtypeStruct((t, D_MODEL), BF16),
        compiler_params=_cparams(("parallel",)),
    )(x, g.reshape(1, D_MODEL))


def _post_norm_add(xres, y, g_post, g_next, *, name):
    t = xres.shape[0]

    def body(x_ref, y_ref, gp_ref, gn_ref, xo_ref, h_ref):
        xo = x_ref[...] + _rms(y_ref[...], gp_ref[...])
        xo_ref[...] = xo
        h_ref[...] = _rms(xo, gn_ref[...]).astype(BF16)

    return pl.pallas_call(
        body, name=name, grid=(t // ROW_TILE,),
        in_specs=[_row_spec(), _row_spec(), _vec_spec(), _vec_spec()],
        out_specs=[_row_spec(), _row_spec()],
        out_shape=[jax.ShapeDtypeStruct((t, D_MODEL), F32), jax.ShapeDtypeStruct((t, D_MODEL), BF16)],
        compiler_params=_cparams(("parallel",)),
    )(xres, y, g_post.reshape(1, D_MODEL), g_next.reshape(1, D_MODEL))


def _post_norm_loss(xres, y, g_post, target, *, name):
    t = xres.shape[0]

    def body(x_ref, y_ref, gp_ref, t_ref, loss_ref, dx_ref):
        @pl.when(pl.program_id(0) == 0)
        def _():
            loss_ref[...] = jnp.zeros_like(loss_ref)

        err = (x_ref[...] + _rms(y_ref[...], gp_ref[...])) - t_ref[...]
        per_tok = jnp.mean(err * err, axis=-1, keepdims=True)
        loss_ref[...] += 0.5 * jnp.sum(per_tok, axis=0, keepdims=True)
        dx_ref[...] = err * (1.0 / D_MODEL)

    return pl.pallas_call(
        body, name=name, grid=(t // ROW_TILE,),
        in_specs=[_row_spec(), _row_spec(), _vec_spec(), _row_spec()],
        out_specs=[pl.BlockSpec((1, 128), lambda i: (0, 0)), _row_spec()],
        out_shape=[jax.ShapeDtypeStruct((1, 128), F32), jax.ShapeDtypeStruct((t, D_MODEL), F32)],
        compiler_params=_cparams(("arbitrary",)),
    )(xres, y, g_post.reshape(1, D_MODEL), target)


def _rms_bwd(dy, x, g, *, name, res=None, out_dtype=F32, tile=ROW_TILE):
    t = x.shape[0]
    has_res = res is not None

    def body(dy_ref, x_ref, g_ref, *rest):
        res_ref = rest[0] if has_res else None
        dx_ref, dg_ref = rest[-2], rest[-1]

        @pl.when(pl.program_id(0) == 0)
        def _():
            dg_ref[...] = jnp.zeros_like(dg_ref)

        xf = x_ref[...]
        dyf = dy_ref[...].astype(F32)
        r = lax.rsqrt(jnp.mean(xf * xf, axis=-1, keepdims=True) + EPS)
        xhat = xf * r
        dg_ref[...] += jnp.sum(dyf * xhat, axis=0, keepdims=True)
        dxh = dyf * g_ref[...]
        dx = r * (dxh - xhat * jnp.mean(dxh * xhat, axis=-1, keepdims=True))
        if has_res:
            dx = dx + res_ref[...]
        dx_ref[...] = dx.astype(dx_ref.dtype)

    args = [dy, x, g.reshape(1, D_MODEL)] + ([res] if has_res else [])
    return pl.pallas_call(
        body, name=name, grid=(t // tile,),
        in_specs=[_row_spec(tile=tile), _row_spec(tile=tile), _vec_spec()] + ([_row_spec(tile=tile)] if has_res else []),
        out_specs=[_row_spec(tile=tile), _vec_spec()],
        out_shape=[jax.ShapeDtypeStruct((t, D_MODEL), out_dtype), jax.ShapeDtypeStruct((1, D_MODEL), F32)],
        compiler_params=_cparams(("arbitrary",)),
    )(*args)


def _rms_bwd_pair(dh, x, g_pre, res, y, g_post, *, name):
    t = x.shape[0]

    def norm_bwd(dy, xf, g):
        r = lax.rsqrt(jnp.mean(xf * xf, axis=-1, keepdims=True) + EPS)
        xhat = xf * r
        dxh = dy * g
        dx = r * (dxh - xhat * jnp.mean(dxh * xhat, axis=-1, keepdims=True))
        return dx, jnp.sum(dy * xhat, axis=0, keepdims=True)

    def body(dh_ref, x_ref, gp_ref, res_ref, y_ref, gq_ref, dx_ref, dy_ref, dgp_ref, dgq_ref):
        @pl.when(pl.program_id(0) == 0)
        def _():
            dgp_ref[...] = jnp.zeros_like(dgp_ref)
            dgq_ref[...] = jnp.zeros_like(dgq_ref)

        dx, dgp = norm_bwd(dh_ref[...], x_ref[...], gp_ref[...])
        dx = dx + res_ref[...]
        dx_ref[...] = dx
        dy, dgq = norm_bwd(dx, y_ref[...], gq_ref[...])
        dy_ref[...] = dy.astype(BF16)
        dgp_ref[...] += dgp
        dgq_ref[...] += dgq

    return pl.pallas_call(
        body, name=name, grid=(t // ROW_TILE,),
        in_specs=[_row_spec(), _row_spec(), _vec_spec(), _row_spec(), _row_spec(), _vec_spec()],
        out_specs=[_row_spec(), _row_spec(), _vec_spec(), _vec_spec()],
        out_shape=[jax.ShapeDtypeStruct((t, D_MODEL), F32), jax.ShapeDtypeStruct((t, D_MODEL), BF16),
                   jax.ShapeDtypeStruct((1, D_MODEL), F32), jax.ShapeDtypeStruct((1, D_MODEL), F32)],
        compiler_params=_cparams(("arbitrary",)),
    )(dh, x, g_pre.reshape(1, D_MODEL), res, y, g_post.reshape(1, D_MODEL))


CONV_COLS = 256
N_CONV_BLOCKS = 3 * MIX_W // CONV_COLS


def _shift_down(x, k):
    if k == 0:
        return x
    return jnp.where(_iota(x.shape, 0) >= k, pltpu.roll(x, k, 0), 0.0)


def _shift_up(x, k):
    if k == 0:
        return x
    s = x.shape[0]
    return jnp.where(_iota(x.shape, 0) < s - k, pltpu.roll(x, s - k, 0), 0.0)


def _conv_pre(x, w_ref):
    c = w_ref[CONV_K - 1:CONV_K, :] * x
    for i in range(CONV_K - 1):
        c = c + w_ref[i:i + 1, :] * _shift_down(x, CONV_K - 1 - i)
    return c


def _conv_silu_fwd(proj, conv_w, n_batch):
    def body(x_ref, w_ref, y_ref):
        c = _conv_pre(x_ref[...], w_ref)
        y_ref[...] = c * jax.nn.sigmoid(c)

    return pl.pallas_call(
        body, name="conv_silu_fwd", grid=(n_batch, N_CONV_BLOCKS),
        in_specs=[pl.BlockSpec((SEQ, CONV_COLS), lambda b, j: (b, j)),
                  pl.BlockSpec((CONV_K, CONV_COLS), lambda b, j: (0, j))],
        out_specs=pl.BlockSpec((SEQ, CONV_COLS), lambda b, j: (b, j)),
        out_shape=jax.ShapeDtypeStruct((n_batch * SEQ, 3 * MIX_W), F32),
        compiler_params=_cparams(("parallel", "parallel")),
    )(proj, conv_w)


def _conv_silu_bwd(dy, proj, conv_w, dproj, n_batch):
    def body(dy_ref, x_ref, w_ref, _, dx_ref, dw_ref):
        @pl.when(pl.program_id(1) == 0)
        def _():
            dw_ref[...] = jnp.zeros_like(dw_ref)

        x = x_ref[...]
        c = _conv_pre(x, w_ref)
        sig = jax.nn.sigmoid(c)
        dc = dy_ref[...] * (sig * (1.0 + c * (1.0 - sig)))
        dx = w_ref[CONV_K - 1:CONV_K, :] * dc
        dw_ref[CONV_K - 1:CONV_K, :] += jnp.sum(dc * x, axis=0, keepdims=True)
        for i in range(CONV_K - 1):
            k = CONV_K - 1 - i
            dx = dx + w_ref[i:i + 1, :] * _shift_up(dc, k)
            dw_ref[i:i + 1, :] += jnp.sum(dc * _shift_down(x, k), axis=0, keepdims=True)
        dx_ref[...] = dx.astype(BF16)

    return pl.pallas_call(
        body, name="conv_silu_bwd", grid=(N_CONV_BLOCKS, n_batch),
        in_specs=[pl.BlockSpec((SEQ, CONV_COLS), lambda j, b: (b, j)),
                  pl.BlockSpec((SEQ, CONV_COLS), lambda j, b: (b, j)),
                  pl.BlockSpec((CONV_K, CONV_COLS), lambda j, b: (0, j)), ANY],
        out_specs=[pl.BlockSpec((SEQ, CONV_COLS), lambda j, b: (b, j)),
                   pl.BlockSpec((CONV_K, CONV_COLS), lambda j, b: (0, j))],
        out_shape=[jax.ShapeDtypeStruct(dproj.shape, BF16),
                   jax.ShapeDtypeStruct((CONV_K, 3 * MIX_W), F32)],
        input_output_aliases={3: 0},
        compiler_params=_cparams(("parallel", "arbitrary")),
    )(dy, proj, conv_w, dproj)


@jax.custom_vjp
def _solve_apply(low, rhs, tinv):
    return _dot3(tinv, rhs)


def _solve_apply_fwd(low, rhs, tinv):
    sol = _dot3(tinv, rhs)
    return sol, (tinv, sol)


def _solve_apply_bwd(resid, g):
    tinv, sol = resid
    y = _dot3(tinv, g, TN)
    return -_dot3(y, sol, NT), y, jnp.zeros_like(tinv)


_solve_apply.defvjp(_solve_apply_fwd, _solve_apply_bwd)


def _inv_unit_lower(lows):
    c = lows[0].shape[0]
    eye = (_iota((c, c), 0) == _iota((c, c), 1)).astype(F32)
    ms = [-low for low in lows]
    ps = [eye + m for m in ms]
    for _ in range(5):
        ms = [_dot3(m, m) for m in ms]
        ps = [p + _dot3(p, m) for p, m in zip(ps, ms)]
    return ps


def _gdn_chunk(qs, ks, vs, gates, states, small, alog_row, dtb_row, gain_row, tinvs):
    c = small.shape[0]
    heads = range(N_LIN)
    lane = _iota((c, 128), 1)
    row, col = _iota((c, c), 0), _iota((c, c), 1)
    causal, strict = row >= col, row > col
    last = _iota((c, 1), 0) == c - 1

    beta_all = jax.nn.sigmoid(small)
    g_all = -jnp.exp(alog_row) * _softplus(small + dtb_row)
    ltri = (col <= row).astype(F32)
    gc_all = lax.dot_general(ltri, g_all, NN, precision=HIGHEST, preferred_element_type=F32)

    beta = [jnp.sum(jnp.where(lane == h, beta_all, 0.0), axis=1, keepdims=True) for h in heads]
    gc = [jnp.sum(jnp.where(lane == N_LIN + h, gc_all, 0.0), axis=1, keepdims=True) for h in heads]
    gc_j = [lax.dot_general((lane == N_LIN + h).astype(F32), gc_all, NT, precision=HIGHEST,
                            preferred_element_type=F32) for h in heads]
    decay = [jnp.where(causal, jnp.exp(jnp.where(causal, gc[h] - gc_j[h], 0.0)), 0.0) for h in heads]
    gc_last = [jnp.sum(jnp.where(last, gc[h], 0.0), axis=0, keepdims=True) for h in heads]
    egc = [jnp.exp(g) for g in gc]
    qn = [q * lax.rsqrt(jnp.sum(q * q, axis=-1, keepdims=True) + EPS) * (LIN_DH ** -0.5) for q in qs]
    kn = [k * lax.rsqrt(jnp.sum(k * k, axis=-1, keepdims=True) + EPS) for k in ks]
    kb = [kn[h] * beta[h] for h in heads]
    low = [jnp.where(strict, _dotbf(kb[h], kn[h], NT) * decay[h], 0.0) for h in heads]
    if tinvs is None:
        tinvs = _inv_unit_lower(low)
    u = [_solve_apply(low[h], vs[h] * beta[h], tinvs[h]) for h in heads]
    w = [_solve_apply(low[h], kb[h] * egc[h], tinvs[h]) for h in heads]
    intra = [_dotbf(qn[h], kn[h], NT) * decay[h] for h in heads]
    v_new = [u[h] - _dotbf(w[h], states[h]) for h in heads]
    o = [_dotbf(qn[h] * egc[h], states[h]) + _dotbf(intra[h], v_new[h]) for h in heads]
    new_states = [states[h] * jnp.exp(gc_last[h]) + _dotbf(kn[h] * jnp.exp(gc_last[h] - gc[h]), v_new[h], TN)
                  for h in heads]
    o = [x * lax.rsqrt(jnp.mean(x * x, axis=-1, keepdims=True) + EPS) * gain_row for x in o]
    outs = [o[h] * (gates[h] * jax.nn.sigmoid(gates[h])) for h in heads]
    return outs, new_states, tinvs


def _gdn_param_rows(a_log, dt_bias, onorm):
    alog_row = jnp.zeros((1, 128), F32).at[0, N_LIN:2 * N_LIN].set(a_log)
    dtb_row = jnp.zeros((1, 128), F32).at[0, N_LIN:2 * N_LIN].set(dt_bias)
    return alog_row, dtb_row, onorm.reshape(1, LIN_DH)


def _head(ref_or_val, h):
    return ref_or_val[:, LIN_DH * h:LIN_DH * (h + 1)]


def _gdn_fwd(qkv, proj, alog_row, dtb_row, gain_row, n_batch):
    nc = SEQ // CHUNK
    t = n_batch * SEQ

    def body(qkv_ref, small_ref, gate_ref, al_ref, dt_ref, gn_ref, mix_ref, st_ref, ti_ref, s_scr):
        @pl.when(pl.program_id(1) == 0)
        def _():
            s_scr[...] = jnp.zeros_like(s_scr)

        heads = range(N_LIN)
        states = [s_scr[h] for h in heads]
        outs, new_states, tinvs = _gdn_chunk(
            [_head(qkv_ref, h) for h in heads], [_head(qkv_ref, N_LIN + h) for h in heads],
            [_head(qkv_ref, 2 * N_LIN + h) for h in heads], [_head(gate_ref, h) for h in heads],
            states, small_ref[...], al_ref[...], dt_ref[...], gn_ref[...], None)
        mix_ref[...] = jnp.concatenate(outs, axis=1).astype(BF16)
        for h in heads:
            st_ref[0, 0, h] = states[h]
            s_scr[h] = new_states[h]
            ti_ref[0, 0, h] = tinvs[h]

    row = lambda b, n: b * nc + n
    vec = pl.BlockSpec((1, 128), lambda b, n: (0, 0))
    return pl.pallas_call(
        body, name="gdn_fwd", grid=(n_batch, nc),
        in_specs=[pl.BlockSpec((CHUNK, 3 * MIX_W), lambda b, n: (row(b, n), 0)),
                  pl.BlockSpec((CHUNK, 128), lambda b, n: (row(b, n), SMALL_COL)),
                  pl.BlockSpec((CHUNK, MIX_W), lambda b, n: (row(b, n), 3)),
                  vec, vec, vec],
        out_specs=[pl.BlockSpec((CHUNK, MIX_W), lambda b, n: (row(b, n), 0)),
                   pl.BlockSpec((1, 1, N_LIN, LIN_DH, LIN_DH), lambda b, n: (b, n, 0, 0, 0)),
                   pl.BlockSpec((1, 1, N_LIN, CHUNK, CHUNK), lambda b, n: (b, n, 0, 0, 0))],
        out_shape=[jax.ShapeDtypeStruct((t, D_MODEL), BF16),
                   jax.ShapeDtypeStruct((n_batch, nc, N_LIN, LIN_DH, LIN_DH), F32),
                   jax.ShapeDtypeStruct((n_batch, nc, N_LIN, CHUNK, CHUNK), F32)],
        scratch_shapes=[pltpu.VMEM((N_LIN, LIN_DH, LIN_DH), F32)],
        compiler_params=_cparams(("parallel", "arbitrary")),
    )(qkv, proj, proj, alog_row, dtb_row, gain_row)


def _gdn_bwd(dcat, qkv, proj, states, tinvs, alog_row, dtb_row, gain_row, n_batch):
    nc = SEQ // CHUNK
    t = n_batch * SEQ

    def body(dmix_ref, qkv_ref, small_ref, gate_ref, st_ref, ti_ref, al_ref, dt_ref, gn_ref,
             dqkv_ref, dgate_ref, dsmall_ref, dal_ref, ddt_ref, dgn_ref, ds_scr):
        @pl.when(pl.program_id(1) == 0)
        def _():
            ds_scr[...] = jnp.zeros_like(ds_scr)

        @pl.when((pl.program_id(0) == 0) & (pl.program_id(1) == 0))
        def _():
            dal_ref[...] = jnp.zeros_like(dal_ref)
            ddt_ref[...] = jnp.zeros_like(ddt_ref)
            dgn_ref[...] = jnp.zeros_like(dgn_ref)

        heads = range(N_LIN)
        tinvs = [ti_ref[0, 0, h] for h in heads]

        def chunk(qs, ks, vs, gates, states_in, small, al, dt, gn):
            outs, new_states, _ = _gdn_chunk(qs, ks, vs, gates, states_in, small, al, dt, gn, tinvs)
            return tuple(outs), tuple(new_states)

        prim = (tuple(_head(qkv_ref, h) for h in heads),
                tuple(_head(qkv_ref, N_LIN + h) for h in heads),
                tuple(_head(qkv_ref, 2 * N_LIN + h) for h in heads),
                tuple(_head(gate_ref, h) for h in heads),
                tuple(st_ref[0, 0, h] for h in heads),
                small_ref[...], al_ref[...], dt_ref[...], gn_ref[...])
        _, vjp = jax.vjp(chunk, *prim)
        cot = (tuple(_head(dmix_ref, h) for h in heads), tuple(ds_scr[h] for h in heads))
        dq, dk, dv, dgate, dstate, dsmall, dal, ddt, dgn = vjp(cot)
        dqkv_ref[...] = jnp.concatenate(list(dq) + list(dk) + list(dv), axis=1)
        dgate_ref[...] = jnp.concatenate(list(dgate), axis=1).astype(BF16)
        dsmall_ref[...] = dsmall.astype(BF16)
        for h in heads:
            ds_scr[h] = dstate[h]
        dal_ref[...] += dal
        ddt_ref[...] += ddt
        dgn_ref[...] += dgn

    row = lambda b, n: b * nc + (nc - 1 - n)
    vec = pl.BlockSpec((1, 128), lambda b, n: (0, 0))
    return pl.pallas_call(
        body, name="gdn_bwd", grid=(n_batch, nc),
        in_specs=[pl.BlockSpec((CHUNK, MIX_W), lambda b, n: (row(b, n), 0)),
                  pl.BlockSpec((CHUNK, 3 * MIX_W), lambda b, n: (row(b, n), 0)),
                  pl.BlockSpec((CHUNK, 128), lambda b, n: (row(b, n), SMALL_COL)),
                  pl.BlockSpec((CHUNK, MIX_W), lambda b, n: (row(b, n), 3)),
                  pl.BlockSpec((1, 1, N_LIN, LIN_DH, LIN_DH), lambda b, n: (b, nc - 1 - n, 0, 0, 0)),
                  pl.BlockSpec((1, 1, N_LIN, CHUNK, CHUNK), lambda b, n: (b, nc - 1 - n, 0, 0, 0)),
                  vec, vec, vec],
        out_specs=[pl.BlockSpec((CHUNK, 3 * MIX_W), lambda b, n: (row(b, n), 0)),
                   pl.BlockSpec((CHUNK, MIX_W), lambda b, n: (row(b, n), 3)),
                   pl.BlockSpec((CHUNK, 128), lambda b, n: (row(b, n), 0)),
                   vec, vec, vec],
        out_shape=[jax.ShapeDtypeStruct((t, 3 * MIX_W), F32),
                   jax.ShapeDtypeStruct((t, IN_A_PAD), BF16),
                   jax.ShapeDtypeStruct((t, 128), BF16),
                   jax.ShapeDtypeStruct((1, 128), F32),
                   jax.ShapeDtypeStruct((1, 128), F32),
                   jax.ShapeDtypeStruct((1, 128), F32)],
        scratch_shapes=[pltpu.VMEM((N_LIN, LIN_DH, LIN_DH), F32)],
        compiler_params=_cparams(("arbitrary", "arbitrary")),
    )(dcat, qkv, proj, proj, states, tinvs, alog_row, dtb_row, gain_row)


SB_T = 256


def _sb_masks():
    r, c = _iota((SB_T, SB_T), 0), _iota((SB_T, SB_T), 1)
    return r, c


def _staggered(chains):
    pending, live = list(chains), []
    while pending or live:
        if pending:
            live.append(pending.pop(0))
        for g in list(live):
            try:
                next(g)
            except StopIteration:
                live.remove(g)


def _sb_rows(kb):
    start = kb * SB_T
    return pl.ds(start if isinstance(kb, int) else pl.multiple_of(start, SB_T), SB_T)


def _sb_fwd(proj, n_batch):
    nq = SEQ // SB_T
    t = n_batch * SEQ
    scale = SB_DH ** -0.5
    both = range(2)

    def body(q_ref, k_ref, v_ref, o_ref, tot_ref, acc_scr, run_scr):
        qi = pl.program_id(2)
        lane = _iota((SB_T, 128), 1)
        r, c = _sb_masks()
        upper = (r > c).astype(BF16)
        q = q_ref[...] * scale
        qm = [jnp.where((lane < SB_DH) == (hh == 0), q, jnp.zeros_like(q)) for hh in both]

        def blocks(kbs, diagonal):
            k_blk = [k_ref[_sb_rows(kb), :] for kb in kbs]
            v_blk = [v_ref[_sb_rows(kb), :] for kb in kbs]
            run = [None if diagonal else run_scr[hh][:, 0:1] for hh in both]
            pv = {hh: [] for hh in both}
            rowsums = {hh: [] for hh in both}

            def chain(n, hh):
                z = lax.dot_general(qm[hh], k_blk[n], NT, preferred_element_type=F32)
                yield
                lb = _log_sigmoid(z)
                l1m = lb - z
                if diagonal:
                    l1m = jnp.where(r > c, l1m, 0.0)
                parts = _split(l1m)
                before = run[hh] if not rowsums[hh] else run[hh] + sum(rowsums[hh][1:], rowsums[hh][0])
                rowsums[hh].append(jnp.sum(l1m, axis=1, keepdims=True))
                yield
                tail = _dot_mask(parts, upper)
                yield
                a = jnp.exp(lb + (tail if diagonal else before + tail))
                if diagonal:
                    a = jnp.where(r > c, a, 0.0)
                a = a.astype(BF16)
                yield
                pv[hh].append(lax.dot_general(a, v_blk[n], NN, preferred_element_type=F32))

            _staggered([chain(n, hh) for n in range(len(kbs)) for hh in both])
            for hh in both:
                if diagonal:
                    acc_scr[hh] = pv[hh][0]
                    run_scr[hh] = jnp.broadcast_to(rowsums[hh][0], (SB_T, 128))
                else:
                    acc_scr[hh] += sum(pv[hh][1:], pv[hh][0])
                    run_scr[hh] += sum(rowsums[hh][1:], rowsums[hh][0])

        blocks([qi], True)

        def step(it, carry):
            kb = qi - 1 - 2 * it
            blocks([kb, kb - 1], False)
            return carry

        lax.fori_loop(0, qi >> 1, step, 0)

        @pl.when((qi & 1) == 1)
        def _():
            blocks([0], False)
        first = lane < SB_DH
        o_ref[...] = jnp.where(first, acc_scr[0], acc_scr[1]).astype(BF16)
        tot_ref[...] = jnp.where(first, run_scr[0], run_scr[1])

    nq_blocks = lambda b, p, i: (b * nq + i, p)
    seq_spec = lambda which: pl.BlockSpec((SEQ, 128), lambda b, p, i: (b, 3 * p + which))
    return pl.pallas_call(
        body, name="sb_fwd", grid=(n_batch, SB_PAIRS, nq),
        in_specs=[pl.BlockSpec((SB_T, 128), lambda b, p, i: (b * nq + i, 3 * p)), seq_spec(1), seq_spec(2)],
        out_specs=[pl.BlockSpec((SB_T, 128), nq_blocks), pl.BlockSpec((SB_T, 128), nq_blocks)],
        out_shape=[jax.ShapeDtypeStruct((t, D_MODEL), BF16),
                   jax.ShapeDtypeStruct((t, MIX_W), F32)],
        scratch_shapes=[pltpu.VMEM((2, SB_T, 128), F32), pltpu.VMEM((2, SB_T, 128), F32)],
        compiler_params=_cparams(("parallel", "parallel", "arbitrary")),
    )(proj, proj, proj)


def _sb_bwd(dcat, proj, totals, n_batch):
    nq = SEQ // SB_T
    t = n_batch * SEQ
    scale = SB_DH ** -0.5
    both = range(2)

    def body(do_ref, q_ref, k_ref, v_ref, tot_ref, dp_ref, dq_scr, run_scr, grun_scr, dk_ref, dv_ref):
        qi = pl.program_id(2)

        @pl.when(qi == 0)
        def _():
            dk_ref[...] = jnp.zeros_like(dk_ref)
            dv_ref[...] = jnp.zeros_like(dv_ref)

        lane = _iota((SB_T, 128), 1)
        r, c = _sb_masks()
        incl = (r <= c).astype(BF16)
        earlier = (r < c).astype(BF16)
        dq_scr[...] = jnp.zeros_like(dq_scr)
        run_scr[...] = jnp.zeros_like(run_scr)
        grun_scr[...] = jnp.zeros_like(grun_scr)
        q, do, tot = q_ref[...] * scale, do_ref[...], tot_ref[...]
        sel = [(lane < SB_DH) == (hh == 0) for hh in both]
        qm = [jnp.where(sel[hh], q, jnp.zeros_like(q)) for hh in both]
        dom = [jnp.where(sel[hh], do, 0.0).astype(BF16) for hh in both]
        total = [jnp.sum(jnp.where(lane == hh * SB_DH, tot, 0.0), axis=1, keepdims=True) for hh in both]

        def blocks(kbs, diagonal):
            k_blk = [k_ref[_sb_rows(kb), :] for kb in kbs]
            v_blk = [v_ref[_sb_rows(kb), :] for kb in kbs]
            run = [run_scr[hh][:, 0:1] for hh in both]
            grun = [grun_scr[hh][:, 0:1] for hh in both]
            rs_l, rs_e, dqp = ({hh: [] for hh in both} for _ in range(3))
            dk, dv = ([[] for _ in kbs] for _ in range(2))

            def plus(base, terms):
                return base if not terms else base + sum(terms[1:], terms[0])

            def chain(n, hh):
                z = lax.dot_general(qm[hh], k_blk[n], NT, preferred_element_type=F32)
                da = lax.dot_general(dom[hh], v_blk[n], NT, preferred_element_type=F32)
                yield
                lb = _log_sigmoid(z)
                sig = jnp.exp(lb)
                l1m = lb - z
                if diagonal:
                    l1m = jnp.where(r > c, l1m, 0.0)
                parts = _split(l1m)
                run_before = plus(run[hh], rs_l[hh])
                rs_l[hh].append(jnp.sum(l1m, axis=1, keepdims=True))
                yield
                prefix = run_before + _dot_mask(parts, incl)
                yield
                a = jnp.exp(lb + (total[hh] - prefix))
                if diagonal:
                    a = jnp.where(r > c, a, 0.0)
                de = a * da
                a = a.astype(BF16)
                parts = _split(de)
                grun_before = plus(grun[hh], rs_e[hh])
                rs_e[hh].append(jnp.sum(de, axis=1, keepdims=True))
                yield
                dv[n].append(lax.dot_general(a, dom[hh], TN, preferred_element_type=F32))
                dl1m = grun_before + _dot_mask(parts, earlier)
                yield
                if diagonal:
                    dl1m = jnp.where(r > c, dl1m, 0.0)
                dz = (de * (1.0 - sig) - dl1m * sig).astype(BF16)
                yield
                dqp[hh].append(lax.dot_general(dz, k_blk[n], NN, preferred_element_type=F32))
                dk[n].append(lax.dot_general(dz, qm[hh], TN, preferred_element_type=F32))

            _staggered([chain(n, hh) for n in range(len(kbs)) for hh in both])
            for hh in both:
                dq_scr[hh] += sum(dqp[hh][1:], dqp[hh][0])
                run_scr[hh] += sum(rs_l[hh][1:], rs_l[hh][0])
                grun_scr[hh] += sum(rs_e[hh][1:], rs_e[hh][0])
            for n, kb in enumerate(kbs):
                dk_ref[_sb_rows(kb), :] += dk[n][0] + dk[n][1]
                dv_ref[_sb_rows(kb), :] += dv[n][0] + dv[n][1]

        def step(it, carry):
            blocks([2 * it, 2 * it + 1], False)
            return carry

        lax.fori_loop(0, qi >> 1, step, 0)

        @pl.when((qi & 1) == 1)
        def _():
            blocks([qi - 1], False)

        blocks([qi], True)
        dq = (jnp.where(sel[0], dq_scr[0], dq_scr[1]) * scale).astype(BF16)
        dp_ref[pl.ds(pl.multiple_of(qi * SB_T, SB_T), SB_T), 0:128] = dq

        @pl.when(qi == nq - 1)
        def _():
            dp_ref[:, 128:256] = dk_ref[...].astype(BF16)
            dp_ref[:, 256:384] = dv_ref[...].astype(BF16)

    q_blocks = lambda b, p, i: (b * nq + i, p)
    seq_spec = lambda which: pl.BlockSpec((SEQ, 128), lambda b, p, i: (b, 3 * p + which))
    return pl.pallas_call(
        body, name="sb_bwd", grid=(n_batch, SB_PAIRS, nq),
        in_specs=[pl.BlockSpec((SB_T, 128), q_blocks),
                  pl.BlockSpec((SB_T, 128), lambda b, p, i: (b * nq + i, 3 * p)),
                  seq_spec(1), seq_spec(2), pl.BlockSpec((SB_T, 128), q_blocks)],
        out_specs=pl.BlockSpec((SEQ, 384), lambda b, p, i: (b, p)),
        out_shape=jax.ShapeDtypeStruct((t, IN_B), BF16),
        scratch_shapes=[pltpu.VMEM((2, SB_T, 128), F32), pltpu.VMEM((2, SB_T, 128), F32),
                        pltpu.VMEM((2, SB_T, 128), F32), pltpu.VMEM((SEQ, 128), F32), pltpu.VMEM((SEQ, 128), F32)],
        compiler_params=_cparams(("parallel", "arbitrary", "arbitrary")),
    )(dcat, proj, proj, proj, totals)


MEM_TQ = 512


def _mem_heads(lane):
    return [(lane >= X_HEAD_DIM * h) & (lane < X_HEAD_DIM * (h + 1)) for h in range(N_X_HEADS)]


def _mem_attn_fwd(proj, q_col, memkv, cat, n_batch):
    nq = SEQ // MEM_TQ
    scale = X_HEAD_DIM ** -0.5

    def body(q_ref, kv_ref, _, o_ref):
        q = q_ref[...]
        k = kv_ref[:, :X_WIDTH].astype(BF16)
        v = kv_ref[:, X_WIDTH:].astype(BF16)
        out = jnp.zeros((MEM_TQ, X_WIDTH), F32)
        for sel in _mem_heads(_iota((MEM_TQ, X_WIDTH), 1)):
            s = lax.dot_general(jnp.where(sel, q, 0.0).astype(BF16), k, NT, preferred_element_type=F32) * scale
            e = jnp.exp(s - jnp.max(s, axis=-1, keepdims=True))
            p = e / jnp.sum(e, axis=-1, keepdims=True)
            out = out + jnp.where(sel, lax.dot_general(p.astype(BF16), v, NN, preferred_element_type=F32), 0.0)
        o_ref[...] = out.astype(BF16)

    return pl.pallas_call(
        body, name="mem_attn_fwd", grid=(n_batch, nq),
        in_specs=[pl.BlockSpec((MEM_TQ, X_WIDTH), lambda b, i: (b * nq + i, q_col)),
                  pl.BlockSpec((N_MEM, 2 * X_WIDTH), lambda b, i: (b, 0)), ANY],
        out_specs=pl.BlockSpec((MEM_TQ, X_WIDTH), lambda b, i: (b * nq + i, MIX_W // X_WIDTH)),
        out_shape=jax.ShapeDtypeStruct(cat.shape, BF16),
        input_output_aliases={2: 0},
        compiler_params=_cparams(("parallel", "parallel")),
    )(proj, memkv, cat)


def _mem_attn_bwd(dcat, proj, q_col, memkv, dproj, n_batch, tail=None):
    nq = SEQ // MEM_TQ
    scale = X_HEAD_DIM ** -0.5
    width = X_WIDTH + (0 if tail is None else 128)
    assert (q_col * X_WIDTH) % width == 0

    def body(do_ref, q_ref, kv_ref, *rest):
        dq_ref, dkv_ref = rest[-2:]

        @pl.when(pl.program_id(1) == 0)
        def _():
            dkv_ref[...] = jnp.zeros_like(dkv_ref)

        q, do = q_ref[...], do_ref[...]
        k = kv_ref[:, :X_WIDTH].astype(BF16)
        v = kv_ref[:, X_WIDTH:].astype(BF16)
        dq = jnp.zeros((MEM_TQ, X_WIDTH), F32)
        dk = jnp.zeros((N_MEM, X_WIDTH), F32)
        dv = jnp.zeros((N_MEM, X_WIDTH), F32)
        for sel in _mem_heads(_iota((MEM_TQ, X_WIDTH), 1)):
            qm = jnp.where(sel, q, 0.0).astype(BF16)
            dom = jnp.where(sel, do, 0.0).astype(BF16)
            s = lax.dot_general(qm, k, NT, preferred_element_type=F32) * scale
            e = jnp.exp(s - jnp.max(s, axis=-1, keepdims=True))
            p = e / jnp.sum(e, axis=-1, keepdims=True)
            dp = lax.dot_general(dom, v, NT, preferred_element_type=F32)
            ds = ((p * (dp - jnp.sum(dp * p, axis=-1, keepdims=True))) * scale).astype(BF16)
            dv = dv + lax.dot_general(p.astype(BF16), dom, TN, preferred_element_type=F32)
            dk = dk + lax.dot_general(ds, qm, TN, preferred_element_type=F32)
            dq = dq + jnp.where(sel, lax.dot_general(ds, k, NN, preferred_element_type=F32), 0.0)
        if tail is None:
            dq_ref[...] = dq.astype(BF16)
        else:
            dq_ref[...] = jnp.concatenate([dq.astype(BF16), rest[0][...]], axis=1)
        dkv_ref[...] += jnp.concatenate([dk, dv], axis=1)

    rows = lambda b, i: b * nq + i
    extra = [] if tail is None else [tail]
    return pl.pallas_call(
        body, name="mem_attn_bwd", grid=(n_batch, nq),
        in_specs=[pl.BlockSpec((MEM_TQ, X_WIDTH), lambda b, i: (rows(b, i), MIX_W // X_WIDTH)),
                  pl.BlockSpec((MEM_TQ, X_WIDTH), lambda b, i: (rows(b, i), q_col)),
                  pl.BlockSpec((N_MEM, 2 * X_WIDTH), lambda b, i: (b, 0))]
                 + [pl.BlockSpec((MEM_TQ, 128), lambda b, i: (rows(b, i), 0))] * len(extra) + [ANY],
        out_specs=[pl.BlockSpec((MEM_TQ, width), lambda b, i: (rows(b, i), q_col * X_WIDTH // width)),
                   pl.BlockSpec((N_MEM, 2 * X_WIDTH), lambda b, i: (b, 0))],
        out_shape=[jax.ShapeDtypeStruct(dproj.shape, BF16),
                   jax.ShapeDtypeStruct((n_batch * N_MEM, 2 * X_WIDTH), F32)],
        input_output_aliases={3 + len(extra): 0},
        compiler_params=_cparams(("parallel", "arbitrary")),
    )(dcat, proj, memkv, *extra, dproj)


def _relu2_epilogue(acc):
    r = jnp.maximum(acc, 0.0)
    return (r * r,)


def _relu2_bwd_epilogue(acc, a):
    return (acc * (2.0 * jnp.sqrt(a.astype(F32))),)


def _pad_in_a(w_in_a):
    w = 3 * MIX_W
    parts = [w_in_a[:, :w], w_in_a[:, w:w + MIX_W], w_in_a[:, IN_A - X_WIDTH:],
             w_in_a[:, w + MIX_W:w + MIX_W + 2 * N_LIN]]
    pad = jnp.zeros((D_MODEL, IN_A_PAD - IN_A), w_in_a.dtype)
    return jnp.concatenate(parts + [pad], axis=1)


def _unpad_in_a(g):
    w = 3 * MIX_W
    return jnp.concatenate([g[:, :w + MIX_W], g[:, w + MIX_W + X_WIDTH:w + MIX_W + X_WIDTH + 2 * N_LIN],
                            g[:, w + MIX_W:w + MIX_W + X_WIDTH]], axis=1)


def _qkv_to_pairs(w):
    w3 = 3 * MIX_W
    qkv = w[:, :w3].reshape(-1, 3, SB_PAIRS, 128).transpose(0, 2, 1, 3).reshape(-1, w3)
    return jnp.concatenate([qkv, w[:, w3:]], axis=1)


def _pairs_to_qkv(w):
    w3 = 3 * MIX_W
    qkv = w[:, :w3].reshape(-1, SB_PAIRS, 3, 128).transpose(0, 2, 1, 3).reshape(-1, w3)
    return jnp.concatenate([qkv, w[:, w3:]], axis=1)


def _local_step(x, mem, target, wts, small):
    t = x.shape[0]
    nb = t // SEQ
    npre, npost, mpre, mpost = small["norm_pre_mix"], small["norm_post_mix"], small["norm_pre_mlp"], small["norm_post_mlp"]
    alog_row, dtb_row, gain_row = _gdn_param_rows(small["a_log_a"][0], small["dt_bias_a"][0], small["onorm_a"][0])
    conv_w = small["conv_w"]

    mem_n = _rms_fwd(mem, small["mem_norm"], name="mem_norm_fwd", tile=256)
    saved = []
    h = _rms_fwd(x, npre[0], name="pre_mix_norm0")
    big = min(1024, t)
    for i in range(DEPTH):
        s = {"x_in": x, "h1": h}
        if i == 0:
            proj = _matmul(h, wts["in_a_pad"], mode="nn", tm=big, tn=1152, tk=1024, name="proj_a")
            qkv = _conv_silu_fwd(proj, conv_w, nb)
            mix, states, tinvs = _gdn_fwd(qkv, proj, alog_row, dtb_row, gain_row, nb)
            s.update(qkv=qkv, states=states, tinvs=tinvs)
            q_col = (3 * MIX_W + MIX_W) // X_WIDTH
        else:
            proj = _matmul(h, wts["in_b"], mode="nn", tm=big, tn=1280, tk=1024, name="proj_b", out_dtypes=(BF16,))
            mix, totals = _sb_fwd(proj, nb)
            s.update(totals=totals)
            q_col = 3 * MIX_W // X_WIDTH
        memkv = _matmul(mem_n, wts["mem_kv"][i], mode="nn", tm=256, tn=512, tk=1024, name=f"memkv{i}")
        cat = _mem_attn_fwd(proj, q_col, memkv, mix, nb)
        y = _matmul(cat, wts["out"][i], mode="nn", tm=big, tn=1024, tk=1024, name=f"out_proj{i}")
        x2, h2 = _post_norm_add(x, y, npost[i], mpre[i], name=f"post_mix{i}")
        a = _matmul(h2, wts["up"][i], mode="nn", tm=big, tn=2048, tk=1024, name=f"up{i}",
                    out_dtypes=(BF16,), epilogue=_relu2_epilogue, n_outer=True)
        y2 = _matmul(a, wts["down"][i], mode="nn", tm=big, tn=1024, tk=2048, name=f"down{i}")
        s.update(proj=proj, q_col=q_col, memkv=memkv, cat=cat, y=y, x2=x2, h2=h2, a=a, y2=y2)
        saved.append(s)
        if i + 1 < DEPTH:
            x, h = _post_norm_add(x2, y2, mpost[i], npre[i + 1], name=f"post_mlp{i}")
        else:
            loss_row, dx = _post_norm_loss(x2, y2, mpost[i], target, name="loss_head")

    gw = {"mem_kv": [None] * DEPTH, "out": [None] * DEPTH, "up": [None] * DEPTH, "down": [None] * DEPTH}
    gs = {k: [None] * DEPTH for k in ("norm_pre_mix", "norm_post_mix", "norm_pre_mlp", "norm_post_mlp")}
    dmem_n = None
    for i in reversed(range(DEPTH)):
        s = saved[i]
        if i == DEPTH - 1:
            dy2, gs["norm_post_mlp"][i] = _rms_bwd(dx, s["y2"], mpost[i], name=f"post_mlp_bwd{i}", out_dtype=BF16)
        du = _matmul(dy2, wts["down"][i], mode="nt", tm=big, tn=2048, tk=1024, name=f"down_dx{i}",
                     out_dtypes=(BF16,), epilogue=_relu2_bwd_epilogue, extras=(s["a"],), n_outer=True)
        gw["down"][i] = _matmul(s["a"], dy2, mode="tn", tm=1024, tn=1024, tk=big, name=f"down_dw{i}")
        dh2 = _matmul(du, wts["up"][i], mode="nt", tm=big, tn=1024, tk=2048, name=f"up_dx{i}")
        gw["up"][i] = _matmul(s["h2"], du, mode="tn", tm=1024, tn=2048, tk=512, name=f"up_dw{i}")
        dx2, dy, gs["norm_pre_mlp"][i], gs["norm_post_mix"][i] = _rms_bwd_pair(
            dh2, s["x2"], mpre[i], dx, s["y"], npost[i], name=f"mlp_norms_bwd{i}")
        dcat = _matmul(dy, wts["out"][i], mode="nt", tm=big, tn=1024, tk=1024, name=f"out_dx{i}")
        gw["out"][i] = _matmul(s["cat"], dy, mode="tn", tm=1024, tn=1024, tk=big, name=f"out_dw{i}")
        if i == 0:
            dqkv, dproj, dsmall, dalog, ddtb, dgain = _gdn_bwd(
                dcat, s["qkv"], s["proj"], s["states"], s["tinvs"], alog_row, dtb_row, gain_row, nb)
            dproj, dconv = _conv_silu_bwd(dqkv, s["proj"], conv_w, dproj, nb)
            dproj, dmemkv = _mem_attn_bwd(dcat, s["proj"], s["q_col"], s["memkv"], dproj, nb, tail=dsmall)
            w_in, tile = wts["in_a_pad"], 1152
        else:
            dproj = _sb_bwd(dcat, s["proj"], s["totals"], nb)
            dproj, dmemkv = _mem_attn_bwd(dcat, s["proj"], s["q_col"], s["memkv"], dproj, nb)
            w_in, tile = wts["in_b"], 1280
        dmemkv = dmemkv.astype(BF16)
        gw["mem_kv"][i] = _matmul(mem_n, dmemkv, mode="tn", tm=1024, tn=512, tk=256, name=f"memkv_dw{i}")
        dmn = _matmul(dmemkv, wts["mem_kv"][i], mode="nt", tm=256, tn=1024, tk=512, name=f"memkv_dx{i}")
        dmem_n = dmn if dmem_n is None else dmem_n + dmn
        dh1 = _matmul(dproj, w_in, mode="nt", tm=big, tn=1024, tk=tile, name=f"proj_dx{i}")
        g_in = _matmul(s["h1"], dproj, mode="tn", tm=1024, tn=tile, tk=big, name=f"proj_dw{i}")
        if i == 0:
            gw["in_a"] = _unpad_in_a(g_in)
        else:
            gw["in_b"] = _pairs_to_qkv(g_in)
        if i > 0:
            dx, dy2, gs["norm_pre_mix"][i], gs["norm_post_mlp"][i - 1] = _rms_bwd_pair(
                dh1, s["x_in"], npre[i], dx2, saved[i - 1]["y2"], mpost[i - 1], name=f"mix_norms_bwd{i}")
        else:
            dx, gs["norm_pre_mix"][i] = _rms_bwd(dh1, s["x_in"], npre[i], name=f"pre_mix_bwd{i}", res=dx2)

    _, g_mem_norm = _rms_bwd(dmem_n, mem, small["mem_norm"], name="mem_norm_bwd", tile=256)
    gsmall = {k: jnp.concatenate(v, axis=0) for k, v in gs.items()}
    gsmall.update(mem_norm=g_mem_norm[0], a_log_a=dalog[:, N_LIN:2 * N_LIN], dt_bias_a=ddtb[:, N_LIN:2 * N_LIN],
                  onorm_a=dgain, conv_w=dconv)
    return loss_row[0, 0], dx, gw, gsmall


PACK_ROWS = (("in_a", 835), ("in_b", 640), ("mem_kv", 256), ("out", 512), ("up", 2048), ("down", 2048))
R_PACK = 6400
R_HALF = R_PACK // 2
SUM_TILE = 640


def _position():
    x, y, c = lax.axis_index("x"), lax.axis_index("y"), lax.axis_index("c")
    others = [(1 - x, y), (x, 1 - y), (1 - x, 1 - y)]
    return x, y, c, others


def _gather_chips(wflat):
    def body(w_ref, out_ref, send_sems, recv_sems):
        x, y, c, others = _position()
        me = 2 * x + y

        def copy(k, src, dst, to):
            return pltpu.make_async_remote_copy(src_ref=src, dst_ref=dst, send_sem=send_sems.at[k],
                                                recv_sem=recv_sems.at[k], device_id=to, device_id_type=MESH)

        first = [copy(j, w_ref.at[me, c], out_ref.at[me, c], (ox, oy, c)) for j, (ox, oy) in enumerate(others)]
        for cp in first:
            cp.start()
        passed = [copy(3 + j, out_ref.at[2 * ox + oy, c], out_ref.at[2 * ox + oy, c], (x, y, 1 - c))
                  for j, (ox, oy) in enumerate(others)]
        for j, (ox, oy) in enumerate(others):
            copy(j, w_ref.at[me, c], out_ref.at[2 * ox + oy, c], (x, y, c)).wait_recv()
            passed[j].start()
        for j, (ox, oy) in enumerate(others):
            copy(3 + j, w_ref.at[me, c], out_ref.at[2 * ox + oy, 1 - c], (x, y, c)).wait_recv()
        for cp in first + passed:
            cp.wait_send()

    return pl.pallas_call(
        body, name="gather_weights",
        in_specs=[ANY], out_specs=ANY, input_output_aliases={0: 0},
        out_shape=jax.ShapeDtypeStruct((N_CHIPS, 2, R_HALF, D_MODEL), wflat.dtype),
        scratch_shapes=[pltpu.SemaphoreType.DMA((6,)), pltpu.SemaphoreType.DMA((6,))],
    )(wflat)


def _gather_all(v, *, name):
    rows, n = v.shape

    def body(x_ref, out_ref, send_sems, recv_sems, local_sem):
        x, y, c, others = _position()
        me, sibling = (x, y, c), (x, y, 1 - c)

        def blk(px, py, pc):
            return out_ref.at[pl.ds((4 * px + 2 * py + pc) * rows, rows), :]

        def copy(k, block, to, src=None):
            return pltpu.make_async_remote_copy(src_ref=blk(*block) if src is None else src, dst_ref=blk(*block),
                                                send_sem=send_sems.at[k], recv_sem=recv_sems.at[k],
                                                device_id=to, device_id_type=MESH)

        mine = pltpu.make_async_copy(x_ref, blk(*me), local_sem)
        mine.start()
        first = [copy(0, me, sibling, src=x_ref)]
        first += [copy(1 + j, me, (*chip, c), src=x_ref) for j, chip in enumerate(others)]
        for cp in first:
            cp.start()
        passed = [copy(4 + j, (*chip, c), sibling) for j, chip in enumerate(others)]
        for j, chip in enumerate(others):
            copy(1 + j, (*chip, c), me).wait_recv()
            passed[j].start()
        copy(0, sibling, me).wait_recv()
        for j, chip in enumerate(others):
            copy(4 + j, (*chip, 1 - c), me).wait_recv()
        for cp in first + passed:
            cp.wait_send()
        mine.wait()

    vmem = pl.BlockSpec(memory_space=pltpu.VMEM)
    return pl.pallas_call(
        body, name=name, in_specs=[vmem], out_specs=vmem,
        out_shape=jax.ShapeDtypeStruct((8 * rows, n), v.dtype),
        scratch_shapes=[pltpu.SemaphoreType.DMA((7,)), pltpu.SemaphoreType.DMA((7,)), pltpu.SemaphoreType.DMA],
    )(v)


def _swap_halves(g5):
    def body(g_ref, out_ref, send_sem, recv_sem):
        x, y, c, _ = _position()
        cp = pltpu.make_async_remote_copy(src_ref=g_ref.at[:, 1 - c], dst_ref=out_ref, send_sem=send_sem,
                                          recv_sem=recv_sem, device_id=(x, y, 1 - c), device_id_type=MESH)
        cp.start()
        cp.wait()

    return pl.pallas_call(
        body, name="grad_swap_halves", in_specs=[ANY], out_specs=ANY,
        out_shape=jax.ShapeDtypeStruct((N_CHIPS, R_HALF, D_MODEL), g5.dtype),
        scratch_shapes=[pltpu.SemaphoreType.DMA, pltpu.SemaphoreType.DMA],
    )(g5)


def _add_halves(core, g5, got):
    def body(c_ref, a_ref, b_ref, o_ref, ob_ref):
        s = a_ref[0] + b_ref[...]
        o_ref[...] = s
        ob_ref[...] = s.astype(BF16)

    nt = R_HALF // SUM_TILE
    spec = pl.BlockSpec((1, SUM_TILE, D_MODEL), lambda s, i, c_ref: (s, i, 0))
    return pl.pallas_call(
        body, name="grad_add_halves",
        grid_spec=pltpu.PrefetchScalarGridSpec(
            num_scalar_prefetch=1, grid=(N_CHIPS, nt),
            in_specs=[pl.BlockSpec((1, 1, SUM_TILE, D_MODEL), lambda s, i, c_ref: (s, c_ref[0], i, 0)), spec],
            out_specs=[spec, spec]),
        out_shape=[jax.ShapeDtypeStruct((N_CHIPS, R_HALF, D_MODEL), F32),
                   jax.ShapeDtypeStruct((N_CHIPS, R_HALF, D_MODEL), BF16)],
        compiler_params=_cparams(("parallel", "parallel")),
    )(core, g5, got)


def _exchange_chips(p):
    def body(p_ref, q_ref, send_sems, recv_sems):
        x, y, c, others = _position()
        copies = [pltpu.make_async_remote_copy(src_ref=p_ref.at[2 * ox + oy], dst_ref=q_ref.at[j],
                                               send_sem=send_sems.at[j], recv_sem=recv_sems.at[j],
                                               device_id=(ox, oy, c), device_id_type=MESH)
                  for j, (ox, oy) in enumerate(others)]
        for cp in copies:
            cp.start()
        for cp in copies:
            cp.wait()

    return pl.pallas_call(
        body, name="grad_exchange_chips", in_specs=[ANY], out_specs=ANY,
        out_shape=jax.ShapeDtypeStruct((3, R_HALF, D_MODEL), p.dtype),
        scratch_shapes=[pltpu.SemaphoreType.DMA((3,)), pltpu.SemaphoreType.DMA((3,))],
    )(p)


def _add_chips(chip_core, p, q):
    def body(kc_ref, p_ref, q_ref, o_ref):
        o_ref[0] = ((p_ref[0] + q_ref[0].astype(F32)) + q_ref[1].astype(F32)) + q_ref[2].astype(F32)

    nt = R_HALF // SUM_TILE
    return pl.pallas_call(
        body, name="grad_add_chips",
        grid_spec=pltpu.PrefetchScalarGridSpec(
            num_scalar_prefetch=1, grid=(nt,),
            in_specs=[pl.BlockSpec((1, SUM_TILE, D_MODEL), lambda i, kc_ref: (kc_ref[0], i, 0)),
                      pl.BlockSpec((3, SUM_TILE, D_MODEL), lambda i, kc_ref: (0, i, 0))],
            out_specs=pl.BlockSpec((1, SUM_TILE, D_MODEL), lambda i, kc_ref: (kc_ref[1], i, 0))),
        out_shape=jax.ShapeDtypeStruct((2, R_HALF, D_MODEL), F32),
        compiler_params=_cparams(("parallel",)),
    )(chip_core, p, q)


def _share_halves(halves):
    def body(h_ref, out_ref, send_sem, recv_sem):
        x, y, c, _ = _position()
        cp = pltpu.make_async_remote_copy(src_ref=h_ref.at[c], dst_ref=out_ref.at[c], send_sem=send_sem,
                                          recv_sem=recv_sem, device_id=(x, y, 1 - c), device_id_type=MESH)
        cp.start()
        pltpu.make_async_remote_copy(src_ref=h_ref.at[c], dst_ref=out_ref.at[1 - c], send_sem=send_sem,
                                     recv_sem=recv_sem, device_id=(x, y, c), device_id_type=MESH).wait_recv()
        cp.wait_send()

    return pl.pallas_call(
        body, name="grad_share_halves", in_specs=[ANY], out_specs=ANY, input_output_aliases={0: 0},
        out_shape=jax.ShapeDtypeStruct((2, R_HALF, D_MODEL), halves.dtype),
        scratch_shapes=[pltpu.SemaphoreType.DMA, pltpu.SemaphoreType.DMA],
    )(halves)


def _reduce_scatter(g_packed, chip, core):
    g5 = g_packed.reshape(N_CHIPS, 2, R_HALF, D_MODEL)
    p, p_bf = _add_halves(core.reshape(1), g5, _swap_halves(g5))
    halves = _add_chips(jnp.stack([chip, core]), p, _exchange_chips(p_bf))
    return _share_halves(halves).reshape(R_PACK, D_MODEL)


def _slot(n):
    return -(-n // 16) * 16


def _pad_rows(a, axis):
    n = a.shape[axis]
    widths = [(0, 0)] * a.ndim
    widths[axis] = (0, _slot(n) - n)
    return jnp.pad(a, widths) if _slot(n) != n else a


def _pack_shard(parts):
    rows = [_pad_rows(parts[name].reshape(n, D_MODEL), 0) for name, n in PACK_ROWS]
    used = sum(_slot(n) for _, n in PACK_ROWS)
    return jnp.concatenate(rows + [jnp.zeros((R_PACK - used, D_MODEL), rows[0].dtype)], axis=0)


def _unpack_shard(flat):
    shapes = {"in_a": (1, D_MODEL, IN_A // N_CHIPS), "in_b": (1, D_MODEL, IN_B // N_CHIPS),
              "mem_kv": (DEPTH, D_MODEL // N_CHIPS, 2 * X_WIDTH), "out": (DEPTH, D_MODEL // N_CHIPS, D_MODEL),
              "up": (DEPTH, D_MODEL, D_FF // N_CHIPS), "down": (DEPTH, D_FF // N_CHIPS, D_MODEL)}
    out, off = {}, 0
    for name, n in PACK_ROWS:
        out[name] = flat[off:off + n].reshape(shapes[name])
        off += _slot(n)
    return out


def _unpack_gathered(g):
    out, off = {}, 0
    for name, n in PACK_ROWS:
        out[name] = g[:, off:off + n]
        off += _slot(n)
    s = N_CHIPS
    full = {
        "in_a": out["in_a"].reshape(s, D_MODEL, IN_A // s).transpose(1, 0, 2).reshape(D_MODEL, IN_A),
        "in_b": out["in_b"].reshape(s, D_MODEL, IN_B // s).transpose(1, 0, 2).reshape(D_MODEL, IN_B),
        "mem_kv": out["mem_kv"].reshape(s, DEPTH, D_MODEL // s, 2 * X_WIDTH).transpose(1, 0, 2, 3).reshape(DEPTH, D_MODEL, 2 * X_WIDTH),
        "out": out["out"].reshape(s, DEPTH, D_MODEL // s, D_MODEL).transpose(1, 0, 2, 3).reshape(DEPTH, D_MODEL, D_MODEL),
        "up": out["up"].reshape(s, DEPTH, D_MODEL, D_FF // s).transpose(1, 2, 0, 3).reshape(DEPTH, D_MODEL, D_FF),
        "down": out["down"].reshape(s, DEPTH, D_FF // s, D_MODEL).transpose(1, 0, 2, 3).reshape(DEPTH, D_FF, D_MODEL),
    }
    full["in_a_pad"] = _pad_in_a(full.pop("in_a"))
    full["in_b"] = _qkv_to_pairs(full["in_b"])
    return full


def _pack_full_grads(gw):
    s = N_CHIPS
    cols = lambda g: g.reshape(D_MODEL, s, -1).transpose(1, 0, 2).reshape(s, -1, D_MODEL)
    rows = lambda g: g.reshape(s, -1, D_MODEL)
    parts = [cols(gw["in_a"]), cols(gw["in_b"])]
    parts += [rows(g) for g in gw["mem_kv"]] + [rows(g) for g in gw["out"]]
    parts += [cols(g) for g in gw["up"]] + [rows(g) for g in gw["down"]]
    parts = [_pad_rows(p, 1) for p in parts]
    used = sum(_slot(n) for _, n in PACK_ROWS)
    return jnp.concatenate(parts + [jnp.zeros((s, R_PACK - used, D_MODEL), F32)], axis=1)


def _adamw_math(w, g, m, v):
    m = ADAM_B1 * m + (1.0 - ADAM_B1) * g
    v = ADAM_B2 * v + (1.0 - ADAM_B2) * (g * g)
    m_hat = m / (1.0 - ADAM_B1 ** ADAM_STEP)
    v_hat = v / (1.0 - ADAM_B2 ** ADAM_STEP)
    delta = -ADAM_LR * (m_hat / (jnp.sqrt(v_hat) + ADAM_EPS) + ADAM_WD * w)
    return delta, m, v


ADAM_TILE = 256


def _adamw(w, g, m, v, *, name):
    shape = w.shape
    cols = shape[-1]
    rows = w.size // cols
    tile = min(rows, ADAM_TILE)
    assert rows % tile == 0, (name, shape)

    def body(w_ref, g_ref, m_ref, v_ref, d_ref, nm_ref, nv_ref):
        d_ref[...], nm_ref[...], nv_ref[...] = _adamw_math(w_ref[...], g_ref[...], m_ref[...], v_ref[...])

    spec = pl.BlockSpec((tile, cols), lambda i: (i, 0))
    outs = pl.pallas_call(
        body, name=name, grid=(rows // tile,), in_specs=[spec] * 4, out_specs=[spec] * 3,
        out_shape=[jax.ShapeDtypeStruct((rows, cols), F32)] * 3,
        compiler_params=_cparams(("parallel",)),
    )(*[a.reshape(rows, cols) for a in (w, g, m, v)])
    return [o.reshape(shape) for o in outs]


SMALL_NAMES = (("mem_norm", 8), ("norm_pre_mix", 16), ("norm_post_mix", 16), ("norm_pre_mlp", 16),
               ("norm_post_mlp", 16), ("a_log_a", 1), ("dt_bias_a", 1), ("onorm_a", 1))
SMALL_ROWS = 80
CONV_ROWS = CONV_K * 3 * MIX_W // 128
SMALL_GRAD_ROWS = SMALL_ROWS + CONV_ROWS


def _pack_small(vals):
    rows = []
    for name, n in SMALL_NAMES:
        flat = vals[name].reshape(-1)
        rows.append(jnp.pad(flat, (0, n * 128 - flat.size)).reshape(n, 128))
    used = sum(n for _, n in SMALL_NAMES)
    return jnp.concatenate(rows + [jnp.zeros((SMALL_ROWS - used, 128), F32)], axis=0)


def _unpack_small(packed, like):
    out, off = {}, 0
    for name, n in SMALL_NAMES:
        size = like[name].size
        out[name] = packed[off:off + n].reshape(-1)[:size].reshape(like[name].shape)
        off += n
    return out


def _small_update(gathered, w, m, v):
    def body(g_ref, w_ref, m_ref, v_ref, gs_ref, d_ref, nm_ref, nv_ref):
        g = g_ref[0]
        for dev in range(1, 8):
            g = g + g_ref[dev]
        gs_ref[...] = g
        d_ref[...], nm_ref[...], nv_ref[...] = _adamw_math(w_ref[...], g[:SMALL_ROWS], m_ref[...], v_ref[...])

    small = jax.ShapeDtypeStruct((SMALL_ROWS, 128), F32)
    return pl.pallas_call(
        body, name="small_update",
        out_shape=[jax.ShapeDtypeStruct((SMALL_GRAD_ROWS, 128), F32), small, small, small],
    )(gathered.reshape(8, SMALL_GRAD_ROWS, 128), w, m, v)


def kernel(x, mem, mem_norm, norm_pre_mix, norm_post_mix, norm_pre_mlp, norm_post_mlp, w_in_a, conv_w_a, a_log_a, dt_bias_a, onorm_a, w_in_b, w_mem_kv, w_out, w_up, w_down, loss_target, m_mem_norm, m_norm_pre_mix, m_norm_post_mix, m_norm_pre_mlp, m_norm_post_mlp, m_w_in_a, m_conv_w_a, m_a_log_a, m_dt_bias_a, m_onorm_a, m_w_in_b, m_w_mem_kv, m_w_out, m_w_up, m_w_down, v_mem_norm, v_norm_pre_mix, v_norm_post_mix, v_norm_pre_mlp, v_norm_post_mlp, v_w_in_a, v_conv_w_a, v_a_log_a, v_dt_bias_a, v_onorm_a, v_w_in_b, v_w_mem_kv, v_w_out, v_w_up, v_w_down):
    nb = x.shape[0]
    chip = (2 * lax.axis_index("x") + lax.axis_index("y")).astype(jnp.int32)
    core = lax.axis_index("c").astype(jnp.int32)
    shards = {"in_a": w_in_a, "in_b": w_in_b, "mem_kv": w_mem_kv, "out": w_out, "up": w_up, "down": w_down}
    moments_m = {"in_a": m_w_in_a, "in_b": m_w_in_b, "mem_kv": m_w_mem_kv, "out": m_w_out, "up": m_w_up, "down": m_w_down}
    moments_v = {"in_a": v_w_in_a, "in_b": v_w_in_b, "mem_kv": v_w_mem_kv, "out": v_w_out, "up": v_w_up, "down": v_w_down}
    small_w = {"mem_norm": mem_norm, "norm_pre_mix": norm_pre_mix, "norm_post_mix": norm_post_mix,
               "norm_pre_mlp": norm_pre_mlp, "norm_post_mlp": norm_post_mlp, "a_log_a": a_log_a,
               "dt_bias_a": dt_bias_a, "onorm_a": onorm_a}
    small_m = {"mem_norm": m_mem_norm, "norm_pre_mix": m_norm_pre_mix, "norm_post_mix": m_norm_post_mix,
               "norm_pre_mlp": m_norm_pre_mlp, "norm_post_mlp": m_norm_post_mlp, "a_log_a": m_a_log_a,
               "dt_bias_a": m_dt_bias_a, "onorm_a": m_onorm_a}
    small_v = {"mem_norm": v_mem_norm, "norm_pre_mix": v_norm_pre_mix, "norm_post_mix": v_norm_post_mix,
               "norm_pre_mlp": v_norm_pre_mlp, "norm_post_mlp": v_norm_post_mlp, "a_log_a": v_a_log_a,
               "dt_bias_a": v_dt_bias_a, "onorm_a": v_onorm_a}

    packed = _pack_shard({k: w.astype(BF16) for k, w in shards.items()})
    slabs = lax.dynamic_update_slice(jnp.zeros((N_CHIPS, R_PACK, D_MODEL), BF16), packed[None], (chip, 0, 0))
    gathered = _gather_chips(slabs.reshape(N_CHIPS, 2, R_HALF, D_MODEL)).reshape(N_CHIPS, R_PACK, D_MODEL)
    wts = _unpack_gathered(gathered)
    conv_rows = CONV_ROWS // N_CHIPS
    conv_blk = jnp.pad(conv_w_a.reshape(conv_rows, 128), ((0, 24 - conv_rows), (0, 0)))
    conv_all = _gather_all(conv_blk, name="gather_conv").reshape(N_CHIPS, 2, 24, 128)[:, 0, :conv_rows]
    conv_full = conv_all.reshape(N_CHIPS, CONV_K, 3 * MIX_W // N_CHIPS).transpose(1, 0, 2).reshape(CONV_K, 3 * MIX_W)

    loss_local, dx, gw, gsmall = _local_step(
        x.reshape(nb * SEQ, D_MODEL), mem.reshape(nb * N_MEM, D_MODEL), loss_target.reshape(nb * SEQ, D_MODEL),
        wts, dict(small_w, conv_w=conv_full))
    loss = lax.psum(loss_local, ("x", "y", "c"))
    grad_x = dx.reshape(nb, SEQ, D_MODEL)

    g_shard = _unpack_shard(_reduce_scatter(_pack_full_grads(gw), chip, core))
    upd = {k: _adamw(shards[k], g_shard[k], moments_m[k], moments_v[k], name=f"adamw_{k}") for k in shards}

    g_rows = jnp.concatenate([_pack_small(gsmall), gsmall["conv_w"].reshape(CONV_ROWS, 128)], axis=0)
    g_all = _gather_all(g_rows, name="gather_small_grads")
    g_sum, d_small, nm_small, nv_small = _small_update(g_all, _pack_small(small_w), _pack_small(small_m), _pack_small(small_v))
    gs = _unpack_small(g_sum, small_w)
    ds, nms, nvs = (_unpack_small(p, small_w) for p in (d_small, nm_small, nv_small))
    cw = 3 * MIX_W // N_CHIPS
    g_conv = lax.dynamic_slice(g_sum[SMALL_ROWS:].reshape(CONV_K, 3 * MIX_W), (0, chip * cw), (CONV_K, cw)).reshape(conv_w_a.shape)
    d_conv, nm_conv, nv_conv = _adamw(conv_w_a, g_conv, m_conv_w_a, v_conv_w_a, name="adamw_conv")

    order = ("mem_norm", "norm_pre_mix", "norm_post_mix", "norm_pre_mlp", "norm_post_mlp", "in_a", "conv", "a_log_a",
             "dt_bias_a", "onorm_a", "in_b", "mem_kv", "out", "up", "down")
    grads = dict(gs, conv=g_conv, **g_shard)
    deltas = dict(ds, conv=d_conv, **{k: u[0] for k, u in upd.items()})
    new_m = dict(nms, conv=nm_conv, **{k: u[1] for k, u in upd.items()})
    new_v = dict(nvs, conv=nv_conv, **{k: u[2] for k, u in upd.items()})
    return (loss, grad_x, *[grads[k] for k in order], *[deltas[k] for k in order],
            *[new_m[k] for k in order], *[new_v[k] for k in order])
```

```python
import functools

import jax
import jax.numpy as jnp
from jax import lax
from jax.experimental import pallas as pl
from jax.experimental.pallas import tpu as pltpu

F32 = jnp.float32
BF16 = jnp.bfloat16
HIGHEST = lax.Precision.HIGHEST
MESH = pl.DeviceIdType.MESH

D_MODEL = 1024
SEQ = 2048
DEPTH = 2
X_WIDTH = 256
N_X_HEADS = 4
X_HEAD_DIM = 64
MIX_W = 768
LIN_DH = 128
N_LIN = 6
CONV_K = 4
CHUNK = 64
SB_DH = 64
SB_PAIRS = 6
N_MEM = 256
D_FF = 4096
EPS = 1e-6
IN_A = 3340
IN_A_PAD = 3456
IN_B = 2560
SMALL_COL = 26
N_CHIPS = 4

ADAM_LR, ADAM_B1, ADAM_B2, ADAM_EPS, ADAM_WD, ADAM_STEP = 0.001, 0.9, 0.999, 1e-08, 0.01, 10

VMEM_LIMIT = 48 * 1024 * 1024

ANY = pl.BlockSpec(memory_space=pl.ANY)

NN = (((1,), (0,)), ((), ()))
NT = (((1,), (1,)), ((), ()))
TN = (((0,), (0,)), ((), ()))


def _cparams(sem):
    return pltpu.CompilerParams(dimension_semantics=sem, vmem_limit_bytes=VMEM_LIMIT)


def _dotbf(a, b, dn=NN):
    return lax.dot_general(a.astype(BF16), b.astype(BF16), dn, preferred_element_type=F32)


def _split(a):
    hi = a.astype(BF16)
    lo = (a - hi.astype(F32)).astype(BF16)
    return hi, lo


def _dot3(a, b, dn=NN):
    ah, al = _split(a)
    bh, bl = _split(b)
    d = functools.partial(lax.dot_general, dimension_numbers=dn, preferred_element_type=F32)
    return d(ah, bh) + (d(ah, bl) + d(al, bh))


def _dot_mask(parts, m01):
    d = functools.partial(lax.dot_general, dimension_numbers=NN, preferred_element_type=F32)
    return d(parts[0], m01) + d(parts[1], m01)


def _iota(shape, dim):
    return lax.broadcasted_iota(jnp.int32, shape, dim)


def _softplus(x):
    return jnp.maximum(x, 0.0) + jnp.log(1.0 + jnp.exp(-jnp.abs(x)))


def _log_sigmoid(z):
    return jnp.minimum(z, 0.0) - jnp.log(1.0 + jnp.exp(-jnp.abs(z)))


def _rms(x, g):
    r = lax.rsqrt(jnp.mean(x * x, axis=-1, keepdims=True) + EPS)
    return (x * r) * g


def _matmul(a, b, *, mode, tm, tn, tk, name, out_dtypes=(F32,), epilogue=None, extras=(), n_outer=False):
    if n_outer:
        ix = lambda f: (lambda j, i, kk: f(i, j, kk))
    else:
        ix = lambda f: f
    if mode == "nn":
        (m, k), (k2, n) = a.shape, b.shape
        a_spec = pl.BlockSpec((tm, tk), ix(lambda i, j, kk: (i, kk)))
        b_spec = pl.BlockSpec((tk, tn), ix(lambda i, j, kk: (kk, j)))
        dn = NN
    elif mode == "nt":
        (m, k), (n, k2) = a.shape, b.shape
        a_spec = pl.BlockSpec((tm, tk), ix(lambda i, j, kk: (i, kk)))
        b_spec = pl.BlockSpec((tn, tk), ix(lambda i, j, kk: (j, kk)))
        dn = NT
    else:
        (k, m), (k2, n) = a.shape, b.shape
        a_spec = pl.BlockSpec((tk, tm), ix(lambda i, j, kk: (kk, i)))
        b_spec = pl.BlockSpec((tk, tn), ix(lambda i, j, kk: (kk, j)))
        dn = TN
    assert k == k2 and m % tm == 0 and n % tn == 0 and k % tk == 0, (name, a.shape, b.shape)
    assert a.dtype == BF16 and b.dtype == BF16, name
    nk = k // tk
    n_extra, n_out = len(extras), len(out_dtypes)

    def finish(acc, extra_refs, out_refs):
        outs = (acc,) if epilogue is None else epilogue(acc, *[r[...] for r in extra_refs])
        for o_ref, o in zip(out_refs, outs):
            o_ref[...] = o.astype(o_ref.dtype)

    def body_single(a_ref, b_ref, *rest):
        acc = lax.dot_general(a_ref[...], b_ref[...], dn, preferred_element_type=F32)
        finish(acc, rest[:n_extra], rest[n_extra:n_extra + n_out])

    def body_tiled(a_ref, b_ref, *rest):
        extra_refs, out_refs, acc_ref = rest[:n_extra], rest[n_extra:n_extra + n_out], rest[-1]
        kk = pl.program_id(2)

        @pl.when(kk == 0)
        def _():
            acc_ref[...] = jnp.zeros_like(acc_ref)

        acc_ref[...] += lax.dot_general(a_ref[...], b_ref[...], dn, preferred_element_type=F32)

        @pl.when(kk == nk - 1)
        def _():
            finish(acc_ref[...], extra_refs, out_refs)

    mn_spec = pl.BlockSpec((tm, tn), ix(lambda i, j, kk: (i, j)))
    grid = (n // tn, m // tm, nk) if n_outer else (m // tm, n // tn, nk)
    outs = pl.pallas_call(
        body_single if nk == 1 else body_tiled,
        name=name,
        grid=grid,
        in_specs=[a_spec, b_spec] + [mn_spec] * n_extra,
        out_specs=[mn_spec] * n_out,
        out_shape=[jax.ShapeDtypeStruct((m, n), dt) for dt in out_dtypes],
        scratch_shapes=[] if nk == 1 else [pltpu.VMEM((tm, tn), F32)],
        compiler_params=_cparams(("parallel", "parallel", "arbitrary")),
    )(a, b, *extras)
    return outs[0] if n_out == 1 else outs


ROW_TILE = 512


def _row_spec(width=D_MODEL, tile=ROW_TILE):
    return pl.BlockSpec((tile, width), lambda i: (i, 0))


def _vec_spec(width=D_MODEL):
    return pl.BlockSpec((1, width), lambda i: (0, 0))


def _rms_fwd(x, g, *, name, tile=ROW_TILE):
    t = x.shape[0]

    def body(x_ref, g_ref, h_ref):
        h_ref[...] = _rms(x_ref[...], g_ref[...]).astype(BF16)

    return pl.pallas_call(
        body, name=name, grid=(t // tile,),
        in_specs=[_row_spec(tile=tile), _vec_spec()], out_specs=_row_spec(tile=tile),
        out_shape=jax.ShapeDtypeStruct((t, D_MODEL), BF16),
        compiler_params=_cparams(("parallel",)),
    )(x, g.reshape(1, D_MODEL))


def _post_norm_add(xres, y, g_post, g_next, *, name):
    t = xres.shape[0]

    def body(x_ref, y_ref, gp_ref, gn_ref, xo_ref, h_ref):
        xo = x_ref[...] + _rms(y_ref[...], gp_ref[...])
        xo_ref[...] = xo
        h_ref[...] = _rms(xo, gn_ref[...]).astype(BF16)

    return pl.pallas_call(
        body, name=name, grid=(t // ROW_TILE,),
        in_specs=[_row_spec(), _row_spec(), _vec_spec(), _vec_spec()],
        out_specs=[_row_spec(), _row_spec()],
        out_shape=[jax.ShapeDtypeStruct((t, D_MODEL), F32), jax.ShapeDtypeStruct((t, D_MODEL), BF16)],
        compiler_params=_cparams(("parallel",)),
    )(xres, y, g_post.reshape(1, D_MODEL), g_next.reshape(1, D_MODEL))


def _post_norm_loss(xres, y, g_post, target, *, name):
    t = xres.shape[0]

    def body(x_ref, y_ref, gp_ref, t_ref, loss_ref, dx_ref):
        @pl.when(pl.program_id(0) == 0)
        def _():
            loss_ref[...] = jnp.zeros_like(loss_ref)

        err = (x_ref[...] + _rms(y_ref[...], gp_ref[...])) - t_ref[...]
        per_tok = jnp.mean(err * err, axis=-1, keepdims=True)
        loss_ref[...] += 0.5 * jnp.sum(per_tok, axis=0, keepdims=True)
        dx_ref[...] = err * (1.0 / D_MODEL)

    return pl.pallas_call(
        body, name=name, grid=(t // ROW_TILE,),
        in_specs=[_row_spec(), _row_spec(), _vec_spec(), _row_spec()],
        out_specs=[pl.BlockSpec((1, 128), lambda i: (0, 0)), _row_spec()],
        out_shape=[jax.ShapeDtypeStruct((1, 128), F32), jax.ShapeDtypeStruct((t, D_MODEL), F32)],
        compiler_params=_cparams(("arbitrary",)),
    )(xres, y, g_post.reshape(1, D_MODEL), target)


def _rms_bwd(dy, x, g, *, name, res=None, out_dtype=F32, tile=ROW_TILE):
    t = x.shape[0]
    has_res = res is not None

    def body(dy_ref, x_ref, g_ref, *rest):
        res_ref = rest[0] if has_res else None
        dx_ref, dg_ref = rest[-2], rest[-1]

        @pl.when(pl.program_id(0) == 0)
        def _():
            dg_ref[...] = jnp.zeros_like(dg_ref)

        xf = x_ref[...]
        dyf = dy_ref[...].astype(F32)
        r = lax.rsqrt(jnp.mean(xf * xf, axis=-1, keepdims=True) + EPS)
        xhat = xf * r
        dg_ref[...] += jnp.sum(dyf * xhat, axis=0, keepdims=True)
        dxh = dyf * g_ref[...]
        dx = r * (dxh - xhat * jnp.mean(dxh * xhat, axis=-1, keepdims=True))
        if has_res:
            dx = dx + res_ref[...]
        dx_ref[...] = dx.astype(dx_ref.dtype)

    args = [dy, x, g.reshape(1, D_MODEL)] + ([res] if has_res else [])
    return pl.pallas_call(
        body, name=name, grid=(t // tile,),
        in_specs=[_row_spec(tile=tile), _row_spec(tile=tile), _vec_spec()] + ([_row_spec(tile=tile)] if has_res else []),
        out_specs=[_row_spec(tile=tile), _vec_spec()],
        out_shape=[jax.ShapeDtypeStruct((t, D_MODEL), out_dtype), jax.ShapeDtypeStruct((1, D_MODEL), F32)],
        compiler_params=_cparams(("arbitrary",)),
    )(*args)


def _rms_bwd_pair(dh, x, g_pre, res, y, g_post, *, name):
    t = x.shape[0]

    def norm_bwd(dy, xf, g):
        r = lax.rsqrt(jnp.mean(xf * xf, axis=-1, keepdims=True) + EPS)
        xhat = xf * r
        dxh = dy * g
        dx = r * (dxh - xhat * jnp.mean(dxh * xhat, axis=-1, keepdims=True))
        return dx, jnp.sum(dy * xhat, axis=0, keepdims=True)

    def body(dh_ref, x_ref, gp_ref, res_ref, y_ref, gq_ref, dx_ref, dy_ref, dgp_ref, dgq_ref):
        @pl.when(pl.program_id(0) == 0)
        def _():
            dgp_ref[...] = jnp.zeros_like(dgp_ref)
            dgq_ref[...] = jnp.zeros_like(dgq_ref)

        dx, dgp = norm_bwd(dh_ref[...], x_ref[...], gp_ref[...])
        dx = dx + res_ref[...]
        dx_ref[...] = dx
        dy, dgq = norm_bwd(dx, y_ref[...], gq_ref[...])
        dy_ref[...] = dy.astype(BF16)
        dgp_ref[...] += dgp
        dgq_ref[...] += dgq

    return pl.pallas_call(
        body, name=name, grid=(t // ROW_TILE,),
        in_specs=[_row_spec(), _row_spec(), _vec_spec(), _row_spec(), _row_spec(), _vec_spec()],
        out_specs=[_row_spec(), _row_spec(), _vec_spec(), _vec_spec()],
        out_shape=[jax.ShapeDtypeStruct((t, D_MODEL), F32), jax.ShapeDtypeStruct((t, D_MODEL), BF16),
                   jax.ShapeDtypeStruct((1, D_MODEL), F32), jax.ShapeDtypeStruct((1, D_MODEL), F32)],
        compiler_params=_cparams(("arbitrary",)),
    )(dh, x, g_pre.reshape(1, D_MODEL), res, y, g_post.reshape(1, D_MODEL))


CONV_COLS = 256
N_CONV_BLOCKS = 3 * MIX_W // CONV_COLS
CONV_STRIP = 128


def _shift_down(x, k):
    if k == 0:
        return x
    return jnp.where(_iota(x.shape, 0) >= k, pltpu.roll(x, k, 0), 0.0)


def _shift_up(x, k):
    if k == 0:
        return x
    s = x.shape[0]
    return jnp.where(_iota(x.shape, 0) < s - k, pltpu.roll(x, s - k, 0), 0.0)


def _conv_pre(x, w_ref):
    c = w_ref[CONV_K - 1:CONV_K, :] * x
    for i in range(CONV_K - 1):
        c = c + w_ref[i:i + 1, :] * _shift_down(x, CONV_K - 1 - i)
    return c


def _conv_silu_fwd(proj, conv_w, n_batch):
    def body(x_ref, w_ref, y_ref):
        c = _conv_pre(x_ref[...], w_ref)
        y_ref[...] = c * jax.nn.sigmoid(c)

    return pl.pallas_call(
        body, name="conv_silu_fwd", grid=(n_batch, N_CONV_BLOCKS),
        in_specs=[pl.BlockSpec((SEQ, CONV_COLS), lambda b, j: (b, j)),
                  pl.BlockSpec((CONV_K, CONV_COLS), lambda b, j: (0, j))],
        out_specs=pl.BlockSpec((SEQ, CONV_COLS), lambda b, j: (b, j)),
        out_shape=jax.ShapeDtypeStruct((n_batch * SEQ, 3 * MIX_W), F32),
        compiler_params=_cparams(("parallel", "parallel")),
    )(proj, conv_w)


def _conv_silu_bwd(dy, proj, conv_w, dproj, n_batch):
    strip, halo = CONV_STRIP, 8
    n_strips = SEQ // strip

    def body(dy_ref, x_ref, w_ref, _, dx_ref, dw_ref, xpad, dcpad):
        @pl.when(pl.program_id(1) == 0)
        def _():
            dw_ref[...] = jnp.zeros_like(dw_ref)

        xpad[0:halo, :] = jnp.zeros((halo, CONV_COLS), F32)
        xpad[halo:, :] = x_ref[...]
        dcpad[SEQ:, :] = jnp.zeros((halo, CONV_COLS), F32)
        taps = [w_ref[i:i + 1, :] for i in range(CONV_K)]

        def first(s, dw):
            a = pl.multiple_of(s * strip, strip)
            win = xpad[pl.ds(a, strip + halo), :]
            xs = [(win if i == CONV_K - 1 else pltpu.roll(win, CONV_K - 1 - i, 0))[halo:] for i in range(CONV_K)]
            c = taps[0] * xs[0]
            for i in range(1, CONV_K):
                c = c + taps[i] * xs[i]
            sig = jax.nn.sigmoid(c)
            dc = dy_ref[pl.ds(a, strip), :] * (sig * (1.0 + c * (1.0 - sig)))
            dcpad[pl.ds(a, strip), :] = dc
            return tuple(dw[i] + jnp.sum(dc * xs[i], axis=0, keepdims=True) for i in range(CONV_K))

        dw = lax.fori_loop(0, n_strips, first, tuple(jnp.zeros((1, CONV_COLS), F32) for _ in range(CONV_K)))
        for i in range(CONV_K):
            dw_ref[i:i + 1, :] += dw[i]

        def second(s, carry):
            a = pl.multiple_of(s * strip, strip)
            win = dcpad[pl.ds(a, strip + halo), :]
            dx = taps[CONV_K - 1] * win[:strip]
            for i in range(CONV_K - 1):
                dx = dx + taps[i] * pltpu.roll(win, strip + halo - (CONV_K - 1 - i), 0)[:strip]
            dx_ref[pl.ds(a, strip), :] = dx.astype(BF16)
            return carry

        lax.fori_loop(0, n_strips, second, 0)

    return pl.pallas_call(
        body, name="conv_silu_bwd", grid=(N_CONV_BLOCKS, n_batch),
        in_specs=[pl.BlockSpec((SEQ, CONV_COLS), lambda j, b: (b, j)),
                  pl.BlockSpec((SEQ, CONV_COLS), lambda j, b: (b, j)),
                  pl.BlockSpec((CONV_K, CONV_COLS), lambda j, b: (0, j)), ANY],
        out_specs=[pl.BlockSpec((SEQ, CONV_COLS), lambda j, b: (b, j)),
                   pl.BlockSpec((CONV_K, CONV_COLS), lambda j, b: (0, j))],
        out_shape=[jax.ShapeDtypeStruct(dproj.shape, BF16),
                   jax.ShapeDtypeStruct((CONV_K, 3 * MIX_W), F32)],
        input_output_aliases={3: 0},
        scratch_shapes=[pltpu.VMEM((SEQ + 8, CONV_COLS), F32), pltpu.VMEM((SEQ + 8, CONV_COLS), F32)],
        compiler_params=_cparams(("parallel", "arbitrary")),
    )(dy, proj, conv_w, dproj)


@jax.custom_vjp
def _solve_apply(low, rhs, tinv):
    return _dot3(tinv, rhs)


def _solve_apply_fwd(low, rhs, tinv):
    sol = _dot3(tinv, rhs)
    return sol, (tinv, sol)


def _solve_apply_bwd(resid, g):
    tinv, sol = resid
    y = _dot3(tinv, g, TN)
    return -_dot3(y, sol, NT), y, jnp.zeros_like(tinv)


_solve_apply.defvjp(_solve_apply_fwd, _solve_apply_bwd)


def _inv_unit_lower(lows):
    c = lows[0].shape[0]
    eye = (_iota((c, c), 0) == _iota((c, c), 1)).astype(F32)
    ms = [-low for low in lows]
    ps = [eye + m for m in ms]
    for _ in range(5):
        ms = [_dot3(m, m) for m in ms]
        ps = [p + _dot3(p, m) for p, m in zip(ps, ms)]
    return ps


def _gdn_chunk(qs, ks, vs, gates, states, small, alog_row, dtb_row, gain_row, tinvs):
    c = small.shape[0]
    heads = range(N_LIN)
    lane = _iota((c, 128), 1)
    row, col = _iota((c, c), 0), _iota((c, c), 1)
    causal, strict = row >= col, row > col
    last = _iota((c, 1), 0) == c - 1

    beta_all = jax.nn.sigmoid(small)
    g_all = -jnp.exp(alog_row) * _softplus(small + dtb_row)
    ltri = (col <= row).astype(F32)
    gc_all = lax.dot_general(ltri, g_all, NN, precision=HIGHEST, preferred_element_type=F32)

    beta = [jnp.sum(jnp.where(lane == h, beta_all, 0.0), axis=1, keepdims=True) for h in heads]
    gc = [jnp.sum(jnp.where(lane == N_LIN + h, gc_all, 0.0), axis=1, keepdims=True) for h in heads]
    gc_j = [lax.dot_general((lane == N_LIN + h).astype(F32), gc_all, NT, precision=HIGHEST,
                            preferred_element_type=F32) for h in heads]
    decay = [jnp.where(causal, jnp.exp(jnp.where(causal, gc[h] - gc_j[h], 0.0)), 0.0) for h in heads]
    gc_last = [jnp.sum(jnp.where(last, gc[h], 0.0), axis=0, keepdims=True) for h in heads]
    egc = [jnp.exp(g) for g in gc]
    qn = [q * lax.rsqrt(jnp.sum(q * q, axis=-1, keepdims=True) + EPS) * (LIN_DH ** -0.5) for q in qs]
    kn = [k * lax.rsqrt(jnp.sum(k * k, axis=-1, keepdims=True) + EPS) for k in ks]
    kb = [kn[h] * beta[h] for h in heads]
    low = [jnp.where(strict, _dotbf(kb[h], kn[h], NT) * decay[h], 0.0) for h in heads]
    if tinvs is None:
        tinvs = _inv_unit_lower(low)
    u = [_solve_apply(low[h], vs[h] * beta[h], tinvs[h]) for h in heads]
    w = [_solve_apply(low[h], kb[h] * egc[h], tinvs[h]) for h in heads]
    intra = [_dotbf(qn[h], kn[h], NT) * decay[h] for h in heads]
    v_new = [u[h] - _dotbf(w[h], states[h]) for h in heads]
    o = [_dotbf(qn[h] * egc[h], states[h]) + _dotbf(intra[h], v_new[h]) for h in heads]
    new_states = [states[h] * jnp.exp(gc_last[h]) + _dotbf(kn[h] * jnp.exp(gc_last[h] - gc[h]), v_new[h], TN)
                  for h in heads]
    o = [x * lax.rsqrt(jnp.mean(x * x, axis=-1, keepdims=True) + EPS) * gain_row for x in o]
    outs = [o[h] * (gates[h] * jax.nn.sigmoid(gates[h])) for h in heads]
    return outs, new_states, tinvs


def _gdn_param_rows(a_log, dt_bias, onorm):
    alog_row = jnp.zeros((1, 128), F32).at[0, N_LIN:2 * N_LIN].set(a_log)
    dtb_row = jnp.zeros((1, 128), F32).at[0, N_LIN:2 * N_LIN].set(dt_bias)
    return alog_row, dtb_row, onorm.reshape(1, LIN_DH)


def _head(ref_or_val, h):
    return ref_or_val[:, LIN_DH * h:LIN_DH * (h + 1)]


def _gdn_fwd(qkv, proj, alog_row, dtb_row, gain_row, n_batch, gather=None):
    nc = SEQ // CHUNK
    t = n_batch * SEQ
    steps = n_batch * nc

    def body(qkv_ref, small_ref, gate_ref, al_ref, dt_ref, gn_ref, *rest):
        if gather is None:
            mix_ref, st_ref, ti_ref, s_scr = rest
        else:
            w_ref, mix_ref, st_ref, ti_ref, out_ref, s_scr, send_sems, recv_sems = rest
            step = pl.program_id(0) * nc + pl.program_id(1)
            for at, phase in ((0, "start"), (steps // 2, "forward"), (steps - 1, "finish")):
                @pl.when(step == at)
                def _(phase=phase):
                    getattr(_Gather(w_ref, out_ref, send_sems, recv_sems), phase)()

        @pl.when(pl.program_id(1) == 0)
        def _():
            s_scr[...] = jnp.zeros_like(s_scr)

        heads = range(N_LIN)
        states = [s_scr[h] for h in heads]
        outs, new_states, tinvs = _gdn_chunk(
            [_head(qkv_ref, h) for h in heads], [_head(qkv_ref, N_LIN + h) for h in heads],
            [_head(qkv_ref, 2 * N_LIN + h) for h in heads], [_head(gate_ref, h) for h in heads],
            states, small_ref[...], al_ref[...], dt_ref[...], gn_ref[...], None)
        mix_ref[...] = jnp.concatenate(outs, axis=1).astype(BF16)
        for h in heads:
            st_ref[0, 0, h] = states[h]
            s_scr[h] = new_states[h]
            ti_ref[0, 0, h] = tinvs[h]

    row = lambda b, n: b * nc + n
    vec = pl.BlockSpec((1, 128), lambda b, n: (0, 0))
    extra = [] if gather is None else [gather]
    return pl.pallas_call(
        body, name="gdn_fwd", grid=(n_batch, nc),
        in_specs=[pl.BlockSpec((CHUNK, 3 * MIX_W), lambda b, n: (row(b, n), 0)),
                  pl.BlockSpec((CHUNK, 128), lambda b, n: (row(b, n), SMALL_COL)),
                  pl.BlockSpec((CHUNK, MIX_W), lambda b, n: (row(b, n), 3)),
                  vec, vec, vec] + [ANY] * len(extra),
        out_specs=[pl.BlockSpec((CHUNK, MIX_W), lambda b, n: (row(b, n), 0)),
                   pl.BlockSpec((1, 1, N_LIN, LIN_DH, LIN_DH), lambda b, n: (b, n, 0, 0, 0)),
                   pl.BlockSpec((1, 1, N_LIN, CHUNK, CHUNK), lambda b, n: (b, n, 0, 0, 0))] + [ANY] * len(extra),
        out_shape=[jax.ShapeDtypeStruct((t, D_MODEL), BF16),
                   jax.ShapeDtypeStruct((n_batch, nc, N_LIN, LIN_DH, LIN_DH), F32),
                   jax.ShapeDtypeStruct((n_batch, nc, N_LIN, CHUNK, CHUNK), F32)]
                  + [jax.ShapeDtypeStruct(g.shape, g.dtype) for g in extra],
        input_output_aliases={6: 3} if extra else {},
        scratch_shapes=[pltpu.VMEM((N_LIN, LIN_DH, LIN_DH), F32)] + (GATHER_SEMS if extra else []),
        compiler_params=_cparams(("arbitrary", "arbitrary")),
    )(qkv, proj, proj, alog_row, dtb_row, gain_row, *extra)


def _gdn_bwd(dcat, qkv, proj, states, tinvs, alog_row, dtb_row, gain_row, n_batch, exchange=None):
    nc = SEQ // CHUNK
    t = n_batch * SEQ
    steps = n_batch * nc

    def body(dmix_ref, qkv_ref, small_ref, gate_ref, st_ref, ti_ref, al_ref, dt_ref, gn_ref, *rest):
        if exchange is None:
            dqkv_ref, dgate_ref, dsmall_ref, dal_ref, ddt_ref, dgn_ref, ds_scr = rest
        else:
            p_ref, dqkv_ref, dgate_ref, dsmall_ref, dal_ref, ddt_ref, dgn_ref, q_ref, ds_scr, send_sems, recv_sems = rest
            step = pl.program_id(0) * nc + pl.program_id(1)

            @pl.when(step == 0)
            def _():
                for cp in _exchange_copies(p_ref, q_ref, send_sems, recv_sems):
                    cp.start()

            @pl.when(step == steps - 1)
            def _():
                for cp in _exchange_copies(p_ref, q_ref, send_sems, recv_sems):
                    cp.wait()

        @pl.when(pl.program_id(1) == 0)
        def _():
            ds_scr[...] = jnp.zeros_like(ds_scr)

        @pl.when((pl.program_id(0) == 0) & (pl.program_id(1) == 0))
        def _():
            dal_ref[...] = jnp.zeros_like(dal_ref)
            ddt_ref[...] = jnp.zeros_like(ddt_ref)
            dgn_ref[...] = jnp.zeros_like(dgn_ref)

        heads = range(N_LIN)
        tinvs = [ti_ref[0, 0, h] for h in heads]

        def chunk(qs, ks, vs, gates, states_in, small, al, dt, gn):
            outs, new_states, _ = _gdn_chunk(qs, ks, vs, gates, states_in, small, al, dt, gn, tinvs)
            return tuple(outs), tuple(new_states)

        prim = (tuple(_head(qkv_ref, h) for h in heads),
                tuple(_head(qkv_ref, N_LIN + h) for h in heads),
                tuple(_head(qkv_ref, 2 * N_LIN + h) for h in heads),
                tuple(_head(gate_ref, h) for h in heads),
                tuple(st_ref[0, 0, h] for h in heads),
                small_ref[...], al_ref[...], dt_ref[...], gn_ref[...])
        _, vjp = jax.vjp(chunk, *prim)
        cot = (tuple(_head(dmix_ref, h) for h in heads), tuple(ds_scr[h] for h in heads))
        dq, dk, dv, dgate, dstate, dsmall, dal, ddt, dgn = vjp(cot)
        dqkv_ref[...] = jnp.concatenate(list(dq) + list(dk) + list(dv), axis=1)
        dgate_ref[...] = jnp.concatenate(list(dgate), axis=1).astype(BF16)
        dsmall_ref[...] = dsmall.astype(BF16)
        for h in heads:
            ds_scr[h] = dstate[h]
        dal_ref[...] += dal
        ddt_ref[...] += ddt
        dgn_ref[...] += dgn

    row = lambda b, n: b * nc + (nc - 1 - n)
    vec = pl.BlockSpec((1, 128), lambda b, n: (0, 0))
    extra = [] if exchange is None else [exchange]
    return pl.pallas_call(
        body, name="gdn_bwd", grid=(n_batch, nc),
        in_specs=[pl.BlockSpec((CHUNK, MIX_W), lambda b, n: (row(b, n), 0)),
                  pl.BlockSpec((CHUNK, 3 * MIX_W), lambda b, n: (row(b, n), 0)),
                  pl.BlockSpec((CHUNK, 128), lambda b, n: (row(b, n), SMALL_COL)),
                  pl.BlockSpec((CHUNK, MIX_W), lambda b, n: (row(b, n), 3)),
                  pl.BlockSpec((1, 1, N_LIN, LIN_DH, LIN_DH), lambda b, n: (b, nc - 1 - n, 0, 0, 0)),
                  pl.BlockSpec((1, 1, N_LIN, CHUNK, CHUNK), lambda b, n: (b, nc - 1 - n, 0, 0, 0)),
                  vec, vec, vec] + [ANY] * len(extra),
        out_specs=[pl.BlockSpec((CHUNK, 3 * MIX_W), lambda b, n: (row(b, n), 0)),
                   pl.BlockSpec((CHUNK, MIX_W), lambda b, n: (row(b, n), 3)),
                   pl.BlockSpec((CHUNK, 128), lambda b, n: (row(b, n), 0)),
                   vec, vec, vec] + [ANY] * len(extra),
        out_shape=[jax.ShapeDtypeStruct((t, 3 * MIX_W), F32),
                   jax.ShapeDtypeStruct((t, IN_A_PAD), BF16),
                   jax.ShapeDtypeStruct((t, 128), BF16),
                   jax.ShapeDtypeStruct((1, 128), F32),
                   jax.ShapeDtypeStruct((1, 128), F32),
                   jax.ShapeDtypeStruct((1, 128), F32)]
                  + [jax.ShapeDtypeStruct((3,) + p.shape[1:], p.dtype) for p in extra],
        scratch_shapes=[pltpu.VMEM((N_LIN, LIN_DH, LIN_DH), F32)] + (EXCHANGE_SEMS if extra else []),
        compiler_params=_cparams(("arbitrary", "arbitrary")),
    )(dcat, qkv, proj, proj, states, tinvs, alog_row, dtb_row, gain_row, *extra)


SB_T = 256


def _sb_masks():
    r, c = _iota((SB_T, SB_T), 0), _iota((SB_T, SB_T), 1)
    return r, c


def _staggered(chains):
    pending, live = list(chains), []
    while pending or live:
        if pending:
            live.append(pending.pop(0))
        for g in list(live):
            try:
                next(g)
            except StopIteration:
                live.remove(g)


def _sb_rows(kb):
    start = kb * SB_T
    return pl.ds(start if isinstance(kb, int) else pl.multiple_of(start, SB_T), SB_T)


def _sb_fwd(proj, n_batch):
    nq = SEQ // SB_T
    t = n_batch * SEQ
    scale = SB_DH ** -0.5
    both = range(2)

    def body(q_ref, k_ref, v_ref, o_ref, tot_ref, acc_scr, run_scr):
        qi = pl.program_id(2)
        lane = _iota((SB_T, 128), 1)
        r, c = _sb_masks()
        upper = (r > c).astype(BF16)
        q = q_ref[...] * scale
        qm = [jnp.where((lane < SB_DH) == (hh == 0), q, jnp.zeros_like(q)) for hh in both]

        def blocks(kbs, diagonal):
            k_blk = [k_ref[_sb_rows(kb), :] for kb in kbs]
            v_blk = [v_ref[_sb_rows(kb), :] for kb in kbs]
            run = [None if diagonal else run_scr[hh][:, 0:1] for hh in both]
            pv = {hh: [] for hh in both}
            rowsums = {hh: [] for hh in both}

            def chain(n, hh):
                z = lax.dot_general(qm[hh], k_blk[n], NT, preferred_element_type=F32)
                yield
                lb = _log_sigmoid(z)
                l1m = lb - z
                if diagonal:
                    l1m = jnp.where(r > c, l1m, 0.0)
                parts = _split(l1m)
                before = run[hh] if not rowsums[hh] else run[hh] + sum(rowsums[hh][1:], rowsums[hh][0])
                rowsums[hh].append(jnp.sum(l1m, axis=1, keepdims=True))
                yield
                tail = _dot_mask(parts, upper)
                yield
                a = jnp.exp(lb + (tail if diagonal else before + tail))
                if diagonal:
                    a = jnp.where(r > c, a, 0.0)
                a = a.astype(BF16)
                yield
                pv[hh].append(lax.dot_general(a, v_blk[n], NN, preferred_element_type=F32))

            _staggered([chain(n, hh) for n in range(len(kbs)) for hh in both])
            for hh in both:
                if diagonal:
                    acc_scr[hh] = pv[hh][0]
                    run_scr[hh] = jnp.broadcast_to(rowsums[hh][0], (SB_T, 128))
                else:
                    acc_scr[hh] += sum(pv[hh][1:], pv[hh][0])
                    run_scr[hh] += sum(rowsums[hh][1:], rowsums[hh][0])

        blocks([qi], True)

        def step(it, carry):
            kb = qi - 1 - 2 * it
            blocks([kb, kb - 1], False)
            return carry

        lax.fori_loop(0, qi >> 1, step, 0)

        @pl.when((qi & 1) == 1)
        def _():
            blocks([0], False)
        first = lane < SB_DH
        o_ref[...] = jnp.where(first, acc_scr[0], acc_scr[1]).astype(BF16)
        tot_ref[...] = jnp.where(first, run_scr[0], run_scr[1])

    nq_blocks = lambda b, p, i: (b * nq + i, p)
    seq_spec = lambda which: pl.BlockSpec((SEQ, 128), lambda b, p, i: (b, 3 * p + which))
    return pl.pallas_call(
        body, name="sb_fwd", grid=(n_batch, SB_PAIRS, nq),
        in_specs=[pl.BlockSpec((SB_T, 128), lambda b, p, i: (b * nq + i, 3 * p)), seq_spec(1), seq_spec(2)],
        out_specs=[pl.BlockSpec((SB_T, 128), nq_blocks), pl.BlockSpec((SB_T, 128), nq_blocks)],
        out_shape=[jax.ShapeDtypeStruct((t, D_MODEL), BF16),
                   jax.ShapeDtypeStruct((t, MIX_W), F32)],
        scratch_shapes=[pltpu.VMEM((2, SB_T, 128), F32), pltpu.VMEM((2, SB_T, 128), F32)],
        compiler_params=_cparams(("parallel", "parallel", "arbitrary")),
    )(proj, proj, proj)


def _sb_bwd(dcat, proj, totals, n_batch):
    nq = SEQ // SB_T
    t = n_batch * SEQ
    scale = SB_DH ** -0.5
    both = range(2)

    def body(do_ref, q_ref, k_ref, v_ref, tot_ref, dp_ref, dq_scr, run_scr, grun_scr, dk_ref, dv_ref):
        qi = pl.program_id(2)

        @pl.when(qi == 0)
        def _():
            dk_ref[...] = jnp.zeros_like(dk_ref)
            dv_ref[...] = jnp.zeros_like(dv_ref)

        lane = _iota((SB_T, 128), 1)
        r, c = _sb_masks()
        incl = (r <= c).astype(BF16)
        earlier = (r < c).astype(BF16)
        dq_scr[...] = jnp.zeros_like(dq_scr)
        run_scr[...] = jnp.zeros_like(run_scr)
        grun_scr[...] = jnp.zeros_like(grun_scr)
        q, do, tot = q_ref[...] * scale, do_ref[...], tot_ref[...]
        sel = [(lane < SB_DH) == (hh == 0) for hh in both]
        qm = [jnp.where(sel[hh], q, jnp.zeros_like(q)) for hh in both]
        dom = [jnp.where(sel[hh], do, 0.0).astype(BF16) for hh in both]
        total = [jnp.sum(jnp.where(lane == hh * SB_DH, tot, 0.0), axis=1, keepdims=True) for hh in both]

        def blocks(kbs, diagonal):
            k_blk = [k_ref[_sb_rows(kb), :] for kb in kbs]
            v_blk = [v_ref[_sb_rows(kb), :] for kb in kbs]
            run = [run_scr[hh][:, 0:1] for hh in both]
            grun = [grun_scr[hh][:, 0:1] for hh in both]
            rs_l, rs_e, dqp = ({hh: [] for hh in both} for _ in range(3))
            dk, dv = ([[] for _ in kbs] for _ in range(2))

            def plus(base, terms):
                return base if not terms else base + sum(terms[1:], terms[0])

            def chain(n, hh):
                z = lax.dot_general(qm[hh], k_blk[n], NT, preferred_element_type=F32)
                da = lax.dot_general(dom[hh], v_blk[n], NT, preferred_element_type=F32)
                yield
                lb = _log_sigmoid(z)
                sig = jnp.exp(lb)
                l1m = lb - z
                if diagonal:
                    l1m = jnp.where(r > c, l1m, 0.0)
                parts = _split(l1m)
                run_before = plus(run[hh], rs_l[hh])
                rs_l[hh].append(jnp.sum(l1m, axis=1, keepdims=True))
                yield
                prefix = run_before + _dot_mask(parts, incl)
                yield
                a = jnp.exp(lb + (total[hh] - prefix))
                if diagonal:
                    a = jnp.where(r > c, a, 0.0)
                de = a * da
                a = a.astype(BF16)
                parts = _split(de)
                grun_before = plus(grun[hh], rs_e[hh])
                rs_e[hh].append(jnp.sum(de, axis=1, keepdims=True))
                yield
                dv[n].append(lax.dot_general(a, dom[hh], TN, preferred_element_type=F32))
                dl1m = grun_before + _dot_mask(parts, earlier)
                yield
                if diagonal:
                    dl1m = jnp.where(r > c, dl1m, 0.0)
                dz = (de * (1.0 - sig) - dl1m * sig).astype(BF16)
                yield
                dqp[hh].append(lax.dot_general(dz, k_blk[n], NN, preferred_element_type=F32))
                dk[n].append(lax.dot_general(dz, qm[hh], TN, preferred_element_type=F32))

            _staggered([chain(n, hh) for n in range(len(kbs)) for hh in both])
            for hh in both:
                dq_scr[hh] += sum(dqp[hh][1:], dqp[hh][0])
                run_scr[hh] += sum(rs_l[hh][1:], rs_l[hh][0])
                grun_scr[hh] += sum(rs_e[hh][1:], rs_e[hh][0])
            for n, kb in enumerate(kbs):
                dk_ref[_sb_rows(kb), :] += dk[n][0] + dk[n][1]
                dv_ref[_sb_rows(kb), :] += dv[n][0] + dv[n][1]

        def step(it, carry):
            blocks([2 * it, 2 * it + 1], False)
            return carry

        lax.fori_loop(0, qi >> 1, step, 0)

        @pl.when((qi & 1) == 1)
        def _():
            blocks([qi - 1], False)

        blocks([qi], True)
        dq = (jnp.where(sel[0], dq_scr[0], dq_scr[1]) * scale).astype(BF16)
        dp_ref[pl.ds(pl.multiple_of(qi * SB_T, SB_T), SB_T), 0:128] = dq

        @pl.when(qi == nq - 1)
        def _():
            dp_ref[:, 128:256] = dk_ref[...].astype(BF16)
            dp_ref[:, 256:384] = dv_ref[...].astype(BF16)

    q_blocks = lambda b, p, i: (b * nq + i, p)
    seq_spec = lambda which: pl.BlockSpec((SEQ, 128), lambda b, p, i: (b, 3 * p + which))
    return pl.pallas_call(
        body, name="sb_bwd", grid=(n_batch, SB_PAIRS, nq),
        in_specs=[pl.BlockSpec((SB_T, 128), q_blocks),
                  pl.BlockSpec((SB_T, 128), lambda b, p, i: (b * nq + i, 3 * p)),
                  seq_spec(1), seq_spec(2), pl.BlockSpec((SB_T, 128), q_blocks)],
        out_specs=pl.BlockSpec((SEQ, 384), lambda b, p, i: (b, p)),
        out_shape=jax.ShapeDtypeStruct((t, IN_B), BF16),
        scratch_shapes=[pltpu.VMEM((2, SB_T, 128), F32), pltpu.VMEM((2, SB_T, 128), F32),
                        pltpu.VMEM((2, SB_T, 128), F32), pltpu.VMEM((SEQ, 128), F32), pltpu.VMEM((SEQ, 128), F32)],
        compiler_params=_cparams(("parallel", "arbitrary", "arbitrary")),
    )(dcat, proj, proj, proj, totals)


MEM_TQ = 512


def _mem_heads(lane):
    return [(lane >= X_HEAD_DIM * h) & (lane < X_HEAD_DIM * (h + 1)) for h in range(N_X_HEADS)]


def _mem_attn_fwd(proj, q_col, memkv, cat, n_batch):
    nq = SEQ // MEM_TQ
    scale = X_HEAD_DIM ** -0.5

    def body(q_ref, kv_ref, _, o_ref):
        q = q_ref[...]
        k = kv_ref[:, :X_WIDTH].astype(BF16)
        v = kv_ref[:, X_WIDTH:].astype(BF16)
        out = jnp.zeros((MEM_TQ, X_WIDTH), F32)
        for sel in _mem_heads(_iota((MEM_TQ, X_WIDTH), 1)):
            s = lax.dot_general(jnp.where(sel, q, 0.0).astype(BF16), k, NT, preferred_element_type=F32) * scale
            e = jnp.exp(s - jnp.max(s, axis=-1, keepdims=True))
            p = e / jnp.sum(e, axis=-1, keepdims=True)
            out = out + jnp.where(sel, lax.dot_general(p.astype(BF16), v, NN, preferred_element_type=F32), 0.0)
        o_ref[...] = out.astype(BF16)

    return pl.pallas_call(
        body, name="mem_attn_fwd", grid=(n_batch, nq),
        in_specs=[pl.BlockSpec((MEM_TQ, X_WIDTH), lambda b, i: (b * nq + i, q_col)),
                  pl.BlockSpec((N_MEM, 2 * X_WIDTH), lambda b, i: (b, 0)), ANY],
        out_specs=pl.BlockSpec((MEM_TQ, X_WIDTH), lambda b, i: (b * nq + i, MIX_W // X_WIDTH)),
        out_shape=jax.ShapeDtypeStruct(cat.shape, BF16),
        input_output_aliases={2: 0},
        compiler_params=_cparams(("parallel", "parallel")),
    )(proj, memkv, cat)


def _mem_attn_bwd(dcat, proj, q_col, memkv, dproj, n_batch, tail=None):
    nq = SEQ // MEM_TQ
    scale = X_HEAD_DIM ** -0.5
    width = X_WIDTH + (0 if tail is None else 128)
    assert (q_col * X_WIDTH) % width == 0

    def body(do_ref, q_ref, kv_ref, *rest):
        dq_ref, dkv_ref = rest[-2:]

        @pl.when(pl.program_id(1) == 0)
        def _():
            dkv_ref[...] = jnp.zeros_like(dkv_ref)

        q, do = q_ref[...], do_ref[...]
        k = kv_ref[:, :X_WIDTH].astype(BF16)
        v = kv_ref[:, X_WIDTH:].astype(BF16)
        dq = jnp.zeros((MEM_TQ, X_WIDTH), F32)
        dk = jnp.zeros((N_MEM, X_WIDTH), F32)
        dv = jnp.zeros((N_MEM, X_WIDTH), F32)
        for sel in _mem_heads(_iota((MEM_TQ, X_WIDTH), 1)):
            qm = jnp.where(sel, q, 0.0).astype(BF16)
            dom = jnp.where(sel, do, 0.0).astype(BF16)
            s = lax.dot_general(qm, k, NT, preferred_element_type=F32) * scale
            e = jnp.exp(s - jnp.max(s, axis=-1, keepdims=True))
            p = e / jnp.sum(e, axis=-1, keepdims=True)
            dp = lax.dot_general(dom, v, NT, preferred_element_type=F32)
            ds = ((p * (dp - jnp.sum(dp * p, axis=-1, keepdims=True))) * scale).astype(BF16)
            dv = dv + lax.dot_general(p.astype(BF16), dom, TN, preferred_element_type=F32)
            dk = dk + lax.dot_general(ds, qm, TN, preferred_element_type=F32)
            dq = dq + jnp.where(sel, lax.dot_general(ds, k, NN, preferred_element_type=F32), 0.0)
        if tail is None:
            dq_ref[...] = dq.astype(BF16)
        else:
            dq_ref[...] = jnp.concatenate([dq.astype(BF16), rest[0][...]], axis=1)
        dkv_ref[...] += jnp.concatenate([dk, dv], axis=1)

    rows = lambda b, i: b * nq + i
    extra = [] if tail is None else [tail]
    return pl.pallas_call(
        body, name="mem_attn_bwd", grid=(n_batch, nq),
        in_specs=[pl.BlockSpec((MEM_TQ, X_WIDTH), lambda b, i: (rows(b, i), MIX_W // X_WIDTH)),
                  pl.BlockSpec((MEM_TQ, X_WIDTH), lambda b, i: (rows(b, i), q_col)),
                  pl.BlockSpec((N_MEM, 2 * X_WIDTH), lambda b, i: (b, 0))]
                 + [pl.BlockSpec((MEM_TQ, 128), lambda b, i: (rows(b, i), 0))] * len(extra) + [ANY],
        out_specs=[pl.BlockSpec((MEM_TQ, width), lambda b, i: (rows(b, i), q_col * X_WIDTH // width)),
                   pl.BlockSpec((N_MEM, 2 * X_WIDTH), lambda b, i: (b, 0))],
        out_shape=[jax.ShapeDtypeStruct(dproj.shape, BF16),
                   jax.ShapeDtypeStruct((n_batch * N_MEM, 2 * X_WIDTH), F32)],
        input_output_aliases={3 + len(extra): 0},
        compiler_params=_cparams(("parallel", "arbitrary")),
    )(dcat, proj, memkv, *extra, dproj)


def _relu2_epilogue(acc):
    r = jnp.maximum(acc, 0.0)
    return (r * r,)


def _relu2_bwd_epilogue(acc, a):
    return (acc * (2.0 * jnp.sqrt(a.astype(F32))),)


def _pad_in_a(w_in_a):
    w = 3 * MIX_W
    parts = [w_in_a[:, :w], w_in_a[:, w:w + MIX_W], w_in_a[:, IN_A - X_WIDTH:],
             w_in_a[:, w + MIX_W:w + MIX_W + 2 * N_LIN]]
    pad = jnp.zeros((D_MODEL, IN_A_PAD - IN_A), w_in_a.dtype)
    return jnp.concatenate(parts + [pad], axis=1)


def _unpad_in_a(g):
    w = 3 * MIX_W
    return jnp.concatenate([g[:, :w + MIX_W], g[:, w + MIX_W + X_WIDTH:w + MIX_W + X_WIDTH + 2 * N_LIN],
                            g[:, w + MIX_W:w + MIX_W + X_WIDTH]], axis=1)


def _qkv_to_pairs(w):
    w3 = 3 * MIX_W
    qkv = w[:, :w3].reshape(-1, 3, SB_PAIRS, 128).transpose(0, 2, 1, 3).reshape(-1, w3)
    return jnp.concatenate([qkv, w[:, w3:]], axis=1)


def _pairs_to_qkv(w):
    w3 = 3 * MIX_W
    qkv = w[:, :w3].reshape(-1, SB_PAIRS, 3, 128).transpose(0, 2, 1, 3).reshape(-1, w3)
    return jnp.concatenate([qkv, w[:, w3:]], axis=1)


def _local_step(x, mem, target, wts, small, comm=None):
    wts = dict(wts)
    t = x.shape[0]
    nb = t // SEQ
    npre, npost, mpre, mpost = small["norm_pre_mix"], small["norm_post_mix"], small["norm_pre_mlp"], small["norm_post_mlp"]
    alog_row, dtb_row, gain_row = _gdn_param_rows(small["a_log_a"][0], small["dt_bias_a"][0], small["onorm_a"][0])
    conv_w = small["conv_w"]

    mem_n = _rms_fwd(mem, small["mem_norm"], name="mem_norm_fwd", tile=256)
    saved = []
    h = _rms_fwd(x, npre[0], name="pre_mix_norm0")
    big = min(1024, t)
    for i in range(DEPTH):
        s = {"x_in": x, "h1": h}
        if i == 0:
            proj = _matmul(h, wts["in_a", None], mode="nn", tm=big, tn=1152, tk=1024, name="proj_a")
            qkv = _conv_silu_fwd(proj, conv_w, nb)
            if comm is None:
                mix, states, tinvs = _gdn_fwd(qkv, proj, alog_row, dtb_row, gain_row, nb)
            else:
                mix, states, tinvs, second = _gdn_fwd(qkv, proj, alog_row, dtb_row, gain_row, nb, gather=comm["slabs"])
                second = second.reshape(N_CHIPS, REGION_ROWS[W_SECOND], D_MODEL)
                wts.update(_as_operands(_unpack_region_full(W_SECOND, second)))
            s.update(qkv=qkv, states=states, tinvs=tinvs)
            q_col = (3 * MIX_W + MIX_W) // X_WIDTH
        else:
            proj = _matmul(h, wts["in_b", None], mode="nn", tm=big, tn=1280, tk=1024, name="proj_b", out_dtypes=(BF16,))
            mix, totals = _sb_fwd(proj, nb)
            s.update(totals=totals)
            q_col = 3 * MIX_W // X_WIDTH
        memkv = _matmul(mem_n, wts["mem_kv", i], mode="nn", tm=256, tn=512, tk=1024, name=f"memkv{i}")
        cat = _mem_attn_fwd(proj, q_col, memkv, mix, nb)
        y = _matmul(cat, wts["out", i], mode="nn", tm=big, tn=1024, tk=1024, name=f"out_proj{i}")
        x2, h2 = _post_norm_add(x, y, npost[i], mpre[i], name=f"post_mix{i}")
        a = _matmul(h2, wts["up", i], mode="nn", tm=big, tn=2048, tk=1024, name=f"up{i}",
                    out_dtypes=(BF16,), epilogue=_relu2_epilogue, n_outer=True)
        y2 = _matmul(a, wts["down", i], mode="nn", tm=big, tn=1024, tk=2048, name=f"down{i}")
        s.update(proj=proj, q_col=q_col, memkv=memkv, cat=cat, y=y, x2=x2, h2=h2, a=a, y2=y2)
        saved.append(s)
        if i + 1 < DEPTH:
            x, h = _post_norm_add(x2, y2, mpost[i], npre[i + 1], name=f"post_mlp{i}")
        else:
            loss_row, dx = _post_norm_loss(x2, y2, mpost[i], target, name="loss_head")

    gw = {}
    gs = {k: [None] * DEPTH for k in ("norm_pre_mix", "norm_post_mix", "norm_pre_mlp", "norm_post_mlp")}
    dmem_n, early = None, None
    for i in reversed(range(DEPTH)):
        s = saved[i]
        if i == DEPTH - 1:
            dy2, gs["norm_post_mlp"][i] = _rms_bwd(dx, s["y2"], mpost[i], name=f"post_mlp_bwd{i}", out_dtype=BF16)
        du = _matmul(dy2, wts["down", i], mode="nt", tm=big, tn=2048, tk=1024, name=f"down_dx{i}",
                     out_dtypes=(BF16,), epilogue=_relu2_bwd_epilogue, extras=(s["a"],), n_outer=True)
        gw["down", i] = _matmul(s["a"], dy2, mode="tn", tm=1024, tn=1024, tk=big, name=f"down_dw{i}")
        dh2 = _matmul(du, wts["up", i], mode="nt", tm=big, tn=1024, tk=2048, name=f"up_dx{i}")
        gw["up", i] = _matmul(s["h2"], du, mode="tn", tm=1024, tn=2048, tk=512, name=f"up_dw{i}")
        dx2, dy, gs["norm_pre_mlp"][i], gs["norm_post_mix"][i] = _rms_bwd_pair(
            dh2, s["x2"], mpre[i], dx, s["y"], npost[i], name=f"mlp_norms_bwd{i}")
        dcat = _matmul(dy, wts["out", i], mode="nt", tm=big, tn=1024, tk=1024, name=f"out_dx{i}")
        gw["out", i] = _matmul(s["cat"], dy, mode="tn", tm=1024, tn=1024, tk=big, name=f"out_dw{i}")
        if i == 0:
            exchange = None
            if comm is not None:
                own, exchange = _reduce_in_chip(_pack_region_full(G_EARLY, gw), comm["core"])
            res = _gdn_bwd(dcat, s["qkv"], s["proj"], s["states"], s["tinvs"], alog_row, dtb_row, gain_row, nb,
                           exchange=exchange)
            dqkv, dproj, dsmall, dalog, ddtb, dgain = res[:6]
            if comm is not None:
                early = (own, res[6])
            dproj, dconv = _conv_silu_bwd(dqkv, s["proj"], conv_w, dproj, nb)
            dproj, dmemkv = _mem_attn_bwd(dcat, s["proj"], s["q_col"], s["memkv"], dproj, nb, tail=dsmall)
            w_in, tile = wts["in_a", None], 1152
        else:
            dproj = _sb_bwd(dcat, s["proj"], s["totals"], nb)
            dproj, dmemkv = _mem_attn_bwd(dcat, s["proj"], s["q_col"], s["memkv"], dproj, nb)
            w_in, tile = wts["in_b", None], 1280
        dmemkv = dmemkv.astype(BF16)
        gw["mem_kv", i] = _matmul(mem_n, dmemkv, mode="tn", tm=1024, tn=512, tk=256, name=f"memkv_dw{i}")
        dmn = _matmul(dmemkv, wts["mem_kv", i], mode="nt", tm=256, tn=1024, tk=512, name=f"memkv_dx{i}")
        dmem_n = dmn if dmem_n is None else dmem_n + dmn
        dh1 = _matmul(dproj, w_in, mode="nt", tm=big, tn=1024, tk=tile, name=f"proj_dx{i}")
        g_in = _matmul(s["h1"], dproj, mode="tn", tm=1024, tn=tile, tk=big, name=f"proj_dw{i}")
        if i == 0:
            gw["in_a", None] = _unpad_in_a(g_in)
        else:
            gw["in_b", None] = _pairs_to_qkv(g_in)
        if i > 0:
            dx, dy2, gs["norm_pre_mix"][i], gs["norm_post_mlp"][i - 1] = _rms_bwd_pair(
                dh1, s["x_in"], npre[i], dx2, saved[i - 1]["y2"], mpost[i - 1], name=f"mix_norms_bwd{i}")
        else:
            dx, gs["norm_pre_mix"][i] = _rms_bwd(dh1, s["x_in"], npre[i], name=f"pre_mix_bwd{i}", res=dx2)

    _, g_mem_norm = _rms_bwd(dmem_n, mem, small["mem_norm"], name="mem_norm_bwd", tile=256)
    gsmall = {k: jnp.concatenate(v, axis=0) for k, v in gs.items()}
    gsmall.update(mem_norm=g_mem_norm[0], a_log_a=dalog[:, N_LIN:2 * N_LIN], dt_bias_a=ddtb[:, N_LIN:2 * N_LIN],
                  onorm_a=dgain, conv_w=dconv)
    return loss_row[0, 0], dx, gw, gsmall, early


SUM_TILE = 640


def _position():
    x, y, c = lax.axis_index("x"), lax.axis_index("y"), lax.axis_index("c")
    others = [(1 - x, y), (x, 1 - y), (1 - x, 1 - y)]
    return x, y, c, others


class _Gather:
    def __init__(self, w_ref, out_ref, send_sems, recv_sems):
        self.w, self.out, self.send, self.recv = w_ref, out_ref, send_sems, recv_sems
        self.x, self.y, self.c, self.others = _position()
        self.me = 2 * self.x + self.y

    def _copy(self, k, src, dst, to):
        return pltpu.make_async_remote_copy(src_ref=src, dst_ref=dst, send_sem=self.send.at[k],
                                            recv_sem=self.recv.at[k], device_id=to, device_id_type=MESH)

    def _first(self):
        return [self._copy(j, self.w.at[self.me, self.c], self.out.at[self.me, self.c], (ox, oy, self.c))
                for j, (ox, oy) in enumerate(self.others)]

    def _passed(self):
        sibling = (self.x, self.y, 1 - self.c)
        return [self._copy(3 + j, self.out.at[2 * ox + oy, self.c], self.out.at[2 * ox + oy, self.c], sibling)
                for j, (ox, oy) in enumerate(self.others)]

    def start(self):
        for cp in self._first():
            cp.start()

    def forward(self):
        passed = self._passed()
        for j, (ox, oy) in enumerate(self.others):
            self._copy(j, self.w.at[self.me, self.c], self.out.at[2 * ox + oy, self.c], (self.x, self.y, self.c)).wait_recv()
            passed[j].start()

    def finish(self):
        for j, (ox, oy) in enumerate(self.others):
            self._copy(3 + j, self.w.at[self.me, self.c], self.out.at[2 * ox + oy, 1 - self.c],
                       (self.x, self.y, self.c)).wait_recv()
        for cp in self._first() + self._passed():
            cp.wait_send()


GATHER_SEMS = [pltpu.SemaphoreType.DMA((6,)), pltpu.SemaphoreType.DMA((6,))]


def _gather_chips(wflat):
    def body(w_ref, out_ref, send_sems, recv_sems):
        g = _Gather(w_ref, out_ref, send_sems, recv_sems)
        g.start()
        g.forward()
        g.finish()

    return pl.pallas_call(
        body, name="gather_weights",
        in_specs=[ANY], out_specs=ANY, input_output_aliases={0: 0},
        out_shape=jax.ShapeDtypeStruct(wflat.shape, wflat.dtype),
        scratch_shapes=GATHER_SEMS,
    )(wflat)


def _gather_all(v, *, name):
    rows, n = v.shape

    def body(x_ref, out_ref, send_sems, recv_sems, local_sem):
        x, y, c, others = _position()
        me, sibling = (x, y, c), (x, y, 1 - c)

        def blk(px, py, pc):
            return out_ref.at[pl.ds((4 * px + 2 * py + pc) * rows, rows), :]

        def copy(k, block, to, src=None):
            return pltpu.make_async_remote_copy(src_ref=blk(*block) if src is None else src, dst_ref=blk(*block),
                                                send_sem=send_sems.at[k], recv_sem=recv_sems.at[k],
                                                device_id=to, device_id_type=MESH)

        mine = pltpu.make_async_copy(x_ref, blk(*me), local_sem)
        mine.start()
        first = [copy(0, me, sibling, src=x_ref)]
        first += [copy(1 + j, me, (*chip, c), src=x_ref) for j, chip in enumerate(others)]
        for cp in first:
            cp.start()
        passed = [copy(4 + j, (*chip, c), sibling) for j, chip in enumerate(others)]
        for j, chip in enumerate(others):
            copy(1 + j, (*chip, c), me).wait_recv()
            passed[j].start()
        copy(0, sibling, me).wait_recv()
        for j, chip in enumerate(others):
            copy(4 + j, (*chip, 1 - c), me).wait_recv()
        for cp in first + passed:
            cp.wait_send()
        mine.wait()

    vmem = pl.BlockSpec(memory_space=pltpu.VMEM)
    return pl.pallas_call(
        body, name=name, in_specs=[vmem], out_specs=vmem,
        out_shape=jax.ShapeDtypeStruct((8 * rows, n), v.dtype),
        scratch_shapes=[pltpu.SemaphoreType.DMA((7,)), pltpu.SemaphoreType.DMA((7,)), pltpu.SemaphoreType.DMA],
    )(v)


def _swap_halves(g5):
    def body(g_ref, out_ref, send_sem, recv_sem):
        x, y, c, _ = _position()
        cp = pltpu.make_async_remote_copy(src_ref=g_ref.at[:, 1 - c], dst_ref=out_ref, send_sem=send_sem,
                                          recv_sem=recv_sem, device_id=(x, y, 1 - c), device_id_type=MESH)
        cp.start()
        cp.wait()

    return pl.pallas_call(
        body, name="grad_swap_halves", in_specs=[ANY], out_specs=ANY,
        out_shape=jax.ShapeDtypeStruct((N_CHIPS, g5.shape[2], D_MODEL), g5.dtype),
        scratch_shapes=[pltpu.SemaphoreType.DMA, pltpu.SemaphoreType.DMA],
    )(g5)


def _add_halves(core, g5, got):
    def body(c_ref, a_ref, b_ref, o_ref, ob_ref):
        s = a_ref[0] + b_ref[...]
        o_ref[...] = s
        ob_ref[...] = s.astype(BF16)

    half = g5.shape[2]
    nt = half // SUM_TILE
    spec = pl.BlockSpec((1, SUM_TILE, D_MODEL), lambda s, i, c_ref: (s, i, 0))
    return pl.pallas_call(
        body, name="grad_add_halves",
        grid_spec=pltpu.PrefetchScalarGridSpec(
            num_scalar_prefetch=1, grid=(N_CHIPS, nt),
            in_specs=[pl.BlockSpec((1, 1, SUM_TILE, D_MODEL), lambda s, i, c_ref: (s, c_ref[0], i, 0)), spec],
            out_specs=[spec, spec]),
        out_shape=[jax.ShapeDtypeStruct((N_CHIPS, half, D_MODEL), F32),
                   jax.ShapeDtypeStruct((N_CHIPS, half, D_MODEL), BF16)],
        compiler_params=_cparams(("parallel", "parallel")),
    )(core, g5, got)


def _exchange_copies(p_ref, q_ref, send_sems, recv_sems):
    x, y, c, others = _position()
    return [pltpu.make_async_remote_copy(src_ref=p_ref.at[2 * ox + oy], dst_ref=q_ref.at[j],
                                         send_sem=send_sems.at[j], recv_sem=recv_sems.at[j],
                                         device_id=(ox, oy, c), device_id_type=MESH)
            for j, (ox, oy) in enumerate(others)]


EXCHANGE_SEMS = [pltpu.SemaphoreType.DMA((3,)), pltpu.SemaphoreType.DMA((3,))]


def _exchange_chips(p):
    def body(p_ref, q_ref, send_sems, recv_sems):
        copies = _exchange_copies(p_ref, q_ref, send_sems, recv_sems)
        for cp in copies:
            cp.start()
        for cp in copies:
            cp.wait()

    return pl.pallas_call(
        body, name="grad_exchange_chips", in_specs=[ANY], out_specs=ANY,
        out_shape=jax.ShapeDtypeStruct((3,) + p.shape[1:], p.dtype),
        scratch_shapes=EXCHANGE_SEMS,
    )(p)


def _add_chips(chip_core, p, q):
    def body(kc_ref, p_ref, q_ref, o_ref):
        o_ref[0] = ((p_ref[0] + q_ref[0].astype(F32)) + q_ref[1].astype(F32)) + q_ref[2].astype(F32)

    half = p.shape[1]
    nt = half // SUM_TILE
    return pl.pallas_call(
        body, name="grad_add_chips",
        grid_spec=pltpu.PrefetchScalarGridSpec(
            num_scalar_prefetch=1, grid=(nt,),
            in_specs=[pl.BlockSpec((1, SUM_TILE, D_MODEL), lambda i, kc_ref: (kc_ref[0], i, 0)),
                      pl.BlockSpec((3, SUM_TILE, D_MODEL), lambda i, kc_ref: (0, i, 0))],
            out_specs=pl.BlockSpec((1, SUM_TILE, D_MODEL), lambda i, kc_ref: (kc_ref[1], i, 0))),
        out_shape=jax.ShapeDtypeStruct((2, half, D_MODEL), F32),
        compiler_params=_cparams(("parallel",)),
    )(chip_core, p, q)


def _share_halves(halves):
    def body(h_ref, out_ref, send_sem, recv_sem):
        x, y, c, _ = _position()
        cp = pltpu.make_async_remote_copy(src_ref=h_ref.at[c], dst_ref=out_ref.at[c], send_sem=send_sem,
                                          recv_sem=recv_sem, device_id=(x, y, 1 - c), device_id_type=MESH)
        cp.start()
        pltpu.make_async_remote_copy(src_ref=h_ref.at[c], dst_ref=out_ref.at[1 - c], send_sem=send_sem,
                                     recv_sem=recv_sem, device_id=(x, y, c), device_id_type=MESH).wait_recv()
        cp.wait_send()

    return pl.pallas_call(
        body, name="grad_share_halves", in_specs=[ANY], out_specs=ANY, input_output_aliases={0: 0},
        out_shape=jax.ShapeDtypeStruct(halves.shape, halves.dtype),
        scratch_shapes=[pltpu.SemaphoreType.DMA, pltpu.SemaphoreType.DMA],
    )(halves)


def _reduce_in_chip(g_packed, core):
    rows = g_packed.shape[1]
    g5 = g_packed.reshape(N_CHIPS, 2, rows // 2, D_MODEL)
    return _add_halves(core.reshape(1), g5, _swap_halves(g5))


def _reduce_across_chips(p, q, chip, core):
    halves = _share_halves(_add_chips(jnp.stack([chip, core]), p, q))
    return halves.reshape(2 * halves.shape[1], D_MODEL)


def _reduce_scatter(g_packed, chip, core):
    p, p_bf = _reduce_in_chip(g_packed, core)
    return _reduce_across_chips(p, _exchange_chips(p_bf), chip, core)


def _slot(n):
    return -(-n // 16) * 16


def _pad_rows(a, axis):
    n = a.shape[axis]
    widths = [(0, 0)] * a.ndim
    widths[axis] = (0, _slot(n) - n)
    return jnp.pad(a, widths) if _slot(n) != n else a


W_FIRST = (("in_a", None), ("mem_kv", 0), ("out", 0), ("up", 0), ("down", 0))
W_SECOND = (("in_b", None), ("mem_kv", 1), ("out", 1), ("up", 1), ("down", 1))
G_LATE = (("in_a", None), ("mem_kv", 0), ("out", 0))
G_EARLY = (("up", 0), ("down", 0), ("in_b", None), ("mem_kv", 1), ("out", 1), ("up", 1), ("down", 1))
REGION_ROWS = {W_FIRST: 3328, W_SECOND: 3072, G_LATE: 1280, G_EARLY: 5120}
FULL_SHAPE = {"in_a": (D_MODEL, IN_A), "in_b": (D_MODEL, IN_B), "mem_kv": (D_MODEL, 2 * X_WIDTH),
              "out": (D_MODEL, D_MODEL), "up": (D_MODEL, D_FF), "down": (D_FF, D_MODEL)}
COLUMN_SHARDED = ("in_a", "in_b", "up")


def _shard_shape(name):
    r, c = FULL_SHAPE[name]
    return (r, c // N_CHIPS) if name in COLUMN_SHARDED else (r // N_CHIPS, c)


def _part_rows(name):
    r, c = _shard_shape(name)
    return r * c // D_MODEL


def _pack_region(region, part, dtype):
    rows = [_pad_rows(part(name, layer).reshape(-1, D_MODEL).astype(dtype), 0) for name, layer in region]
    used = sum(r.shape[0] for r in rows)
    return jnp.concatenate(rows + [jnp.zeros((REGION_ROWS[region] - used, D_MODEL), dtype)], axis=0)


def _unpack_region(region, flat):
    out, off = {}, 0
    for name, layer in region:
        n = _part_rows(name)
        out[name, layer] = flat[off:off + n].reshape(_shard_shape(name))
        off += _slot(n)
    return out


def _unpack_region_full(region, g):
    out, off = {}, 0
    for name, layer in region:
        n = _part_rows(name)
        piece = g[:, off:off + n].reshape((N_CHIPS,) + _shard_shape(name))
        if name in COLUMN_SHARDED:
            piece = piece.transpose(1, 0, 2)
        out[name, layer] = piece.reshape(FULL_SHAPE[name])
        off += _slot(n)
    return out


def _pack_region_full(region, full):
    s = N_CHIPS
    parts = []
    for name, layer in region:
        g = full[name, layer]
        if name in COLUMN_SHARDED:
            g = g.reshape(g.shape[0], s, -1).transpose(1, 0, 2)
        parts.append(_pad_rows(g.reshape(s, -1, D_MODEL), 1))
    used = sum(p.shape[1] for p in parts)
    return jnp.concatenate(parts + [jnp.zeros((s, REGION_ROWS[region] - used, D_MODEL), F32)], axis=1)


def _as_operands(full):
    out = dict(full)
    if ("in_a", None) in out:
        out["in_a", None] = _pad_in_a(out["in_a", None])
    if ("in_b", None) in out:
        out["in_b", None] = _qkv_to_pairs(out["in_b", None])
    return out


def _place(packed, chip):
    rows = packed.shape[0]
    slabs = lax.dynamic_update_slice(jnp.zeros((N_CHIPS, rows, D_MODEL), packed.dtype), packed[None], (chip, 0, 0))
    return slabs.reshape(N_CHIPS, 2, rows // 2, D_MODEL)


def _adamw_math(w, g, m, v):
    m = ADAM_B1 * m + (1.0 - ADAM_B1) * g
    v = ADAM_B2 * v + (1.0 - ADAM_B2) * (g * g)
    m_hat = m / (1.0 - ADAM_B1 ** ADAM_STEP)
    v_hat = v / (1.0 - ADAM_B2 ** ADAM_STEP)
    delta = -ADAM_LR * (m_hat / (jnp.sqrt(v_hat) + ADAM_EPS) + ADAM_WD * w)
    return delta, m, v


ADAM_TILE = 256


def _adamw(w, g, m, v, *, name):
    shape = w.shape
    cols = shape[-1]
    rows = w.size // cols
    tile = min(rows, ADAM_TILE)
    assert rows % tile == 0, (name, shape)

    def body(w_ref, g_ref, m_ref, v_ref, d_ref, nm_ref, nv_ref):
        d_ref[...], nm_ref[...], nv_ref[...] = _adamw_math(w_ref[...], g_ref[...], m_ref[...], v_ref[...])

    spec = pl.BlockSpec((tile, cols), lambda i: (i, 0))
    outs = pl.pallas_call(
        body, name=name, grid=(rows // tile,), in_specs=[spec] * 4, out_specs=[spec] * 3,
        out_shape=[jax.ShapeDtypeStruct((rows, cols), F32)] * 3,
        compiler_params=_cparams(("parallel",)),
    )(*[a.reshape(rows, cols) for a in (w, g, m, v)])
    return [o.reshape(shape) for o in outs]


SMALL_NAMES = (("mem_norm", 8), ("norm_pre_mix", 16), ("norm_post_mix", 16), ("norm_pre_mlp", 16),
               ("norm_post_mlp", 16), ("a_log_a", 1), ("dt_bias_a", 1), ("onorm_a", 1))
SMALL_ROWS = 80
CONV_ROWS = CONV_K * 3 * MIX_W // 128
SMALL_GRAD_ROWS = SMALL_ROWS + CONV_ROWS


def _pack_small(vals):
    rows = []
    for name, n in SMALL_NAMES:
        flat = vals[name].reshape(-1)
        rows.append(jnp.pad(flat, (0, n * 128 - flat.size)).reshape(n, 128))
    used = sum(n for _, n in SMALL_NAMES)
    return jnp.concatenate(rows + [jnp.zeros((SMALL_ROWS - used, 128), F32)], axis=0)


def _unpack_small(packed, like):
    out, off = {}, 0
    for name, n in SMALL_NAMES:
        size = like[name].size
        out[name] = packed[off:off + n].reshape(-1)[:size].reshape(like[name].shape)
        off += n
    return out


def _small_update(gathered, w, m, v):
    def body(g_ref, w_ref, m_ref, v_ref, gs_ref, d_ref, nm_ref, nv_ref):
        g = g_ref[0]
        for dev in range(1, 8):
            g = g + g_ref[dev]
        gs_ref[...] = g
        d_ref[...], nm_ref[...], nv_ref[...] = _adamw_math(w_ref[...], g[:SMALL_ROWS], m_ref[...], v_ref[...])

    small = jax.ShapeDtypeStruct((SMALL_ROWS, 128), F32)
    return pl.pallas_call(
        body, name="small_update",
        out_shape=[jax.ShapeDtypeStruct((SMALL_GRAD_ROWS, 128), F32), small, small, small],
    )(gathered.reshape(8, SMALL_GRAD_ROWS, 128), w, m, v)


def kernel(x, mem, mem_norm, norm_pre_mix, norm_post_mix, norm_pre_mlp, norm_post_mlp, w_in_a, conv_w_a, a_log_a, dt_bias_a, onorm_a, w_in_b, w_mem_kv, w_out, w_up, w_down, loss_target, m_mem_norm, m_norm_pre_mix, m_norm_post_mix, m_norm_pre_mlp, m_norm_post_mlp, m_w_in_a, m_conv_w_a, m_a_log_a, m_dt_bias_a, m_onorm_a, m_w_in_b, m_w_mem_kv, m_w_out, m_w_up, m_w_down, v_mem_norm, v_norm_pre_mix, v_norm_post_mix, v_norm_pre_mlp, v_norm_post_mlp, v_w_in_a, v_conv_w_a, v_a_log_a, v_dt_bias_a, v_onorm_a, v_w_in_b, v_w_mem_kv, v_w_out, v_w_up, v_w_down):
    nb = x.shape[0]
    chip = (2 * lax.axis_index("x") + lax.axis_index("y")).astype(jnp.int32)
    core = lax.axis_index("c").astype(jnp.int32)
    shards = {"in_a": w_in_a, "in_b": w_in_b, "mem_kv": w_mem_kv, "out": w_out, "up": w_up, "down": w_down}
    moments_m = {"in_a": m_w_in_a, "in_b": m_w_in_b, "mem_kv": m_w_mem_kv, "out": m_w_out, "up": m_w_up, "down": m_w_down}
    moments_v = {"in_a": v_w_in_a, "in_b": v_w_in_b, "mem_kv": v_w_mem_kv, "out": v_w_out, "up": v_w_up, "down": v_w_down}
    small_w = {"mem_norm": mem_norm, "norm_pre_mix": norm_pre_mix, "norm_post_mix": norm_post_mix,
               "norm_pre_mlp": norm_pre_mlp, "norm_post_mlp": norm_post_mlp, "a_log_a": a_log_a,
               "dt_bias_a": dt_bias_a, "onorm_a": onorm_a}
    small_m = {"mem_norm": m_mem_norm, "norm_pre_mix": m_norm_pre_mix, "norm_post_mix": m_norm_post_mix,
               "norm_pre_mlp": m_norm_pre_mlp, "norm_post_mlp": m_norm_post_mlp, "a_log_a": m_a_log_a,
               "dt_bias_a": m_dt_bias_a, "onorm_a": m_onorm_a}
    small_v = {"mem_norm": v_mem_norm, "norm_pre_mix": v_norm_pre_mix, "norm_post_mix": v_norm_post_mix,
               "norm_pre_mlp": v_norm_pre_mlp, "norm_post_mlp": v_norm_post_mlp, "a_log_a": v_a_log_a,
               "dt_bias_a": v_dt_bias_a, "onorm_a": v_onorm_a}

    def shard_part(name, layer):
        return shards[name][0 if layer is None else layer]

    first = _gather_chips(_place(_pack_region(W_FIRST, shard_part, BF16), chip))
    wts = _as_operands(_unpack_region_full(W_FIRST, first.reshape(N_CHIPS, REGION_ROWS[W_FIRST], D_MODEL)))
    comm = {"slabs": _place(_pack_region(W_SECOND, shard_part, BF16), chip), "core": core}
    conv_rows = CONV_ROWS // N_CHIPS
    conv_blk = jnp.pad(conv_w_a.reshape(conv_rows, 128), ((0, 24 - conv_rows), (0, 0)))
    conv_all = _gather_all(conv_blk, name="gather_conv").reshape(N_CHIPS, 2, 24, 128)[:, 0, :conv_rows]
    conv_full = conv_all.reshape(N_CHIPS, CONV_K, 3 * MIX_W // N_CHIPS).transpose(1, 0, 2).reshape(CONV_K, 3 * MIX_W)

    loss_local, dx, gw, gsmall, (own_early, others_early) = _local_step(
        x.reshape(nb * SEQ, D_MODEL), mem.reshape(nb * N_MEM, D_MODEL), loss_target.reshape(nb * SEQ, D_MODEL),
        wts, dict(small_w, conv_w=conv_full), comm)
    loss = lax.psum(loss_local, ("x", "y", "c"))
    grad_x = dx.reshape(nb, SEQ, D_MODEL)

    g_part = _unpack_region(G_EARLY, _reduce_across_chips(own_early, others_early, chip, core))
    g_part.update(_unpack_region(G_LATE, _reduce_scatter(_pack_region_full(G_LATE, gw), chip, core)))
    g_shard = {k: (g_part[k, None][None] if (k, None) in g_part else jnp.stack([g_part[k, i] for i in range(DEPTH)]))
               for k in shards}
    upd = {k: _adamw(shards[k], g_shard[k], moments_m[k], moments_v[k], name=f"adamw_{k}") for k in shards}

    g_rows = jnp.concatenate([_pack_small(gsmall), gsmall["conv_w"].reshape(CONV_ROWS, 128)], axis=0)
    g_all = _gather_all(g_rows, name="gather_small_grads")
    g_sum, d_small, nm_small, nv_small = _small_update(g_all, _pack_small(small_w), _pack_small(small_m), _pack_small(small_v))
    gs = _unpack_small(g_sum, small_w)
    ds, nms, nvs = (_unpack_small(p, small_w) for p in (d_small, nm_small, nv_small))
    cw = 3 * MIX_W // N_CHIPS
    g_conv = lax.dynamic_slice(g_sum[SMALL_ROWS:].reshape(CONV_K, 3 * MIX_W), (0, chip * cw), (CONV_K, cw)).reshape(conv_w_a.shape)
    d_conv, nm_conv, nv_conv = _adamw(conv_w_a, g_conv, m_conv_w_a, v_conv_w_a, name="adamw_conv")

    order = ("mem_norm", "norm_pre_mix", "norm_post_mix", "norm_pre_mlp", "norm_post_mlp", "in_a", "conv", "a_log_a",
             "dt_bias_a", "onorm_a", "in_b", "mem_kv", "out", "up", "down")
    grads = dict(gs, conv=g_conv, **g_shard)
    deltas = dict(ds, conv=d_conv, **{k: u[0] for k, u in upd.items()})
    new_m = dict(nms, conv=nm_conv, **{k: u[1] for k, u in upd.items()})
    new_v = dict(nvs, conv=nv_conv, **{k: u[2] for k, u in upd.items()})
    return (loss, grad_x, *[grads[k] for k in order], *[deltas[k] for k in order],
            *[new_m[k] for k in order], *[new_v[k] for k in order])
```

```python
import functools

import jax
import jax.numpy as jnp
from jax import lax
from jax.experimental import pallas as pl
from jax.experimental.pallas import tpu as pltpu

F32 = jnp.float32
BF16 = jnp.bfloat16
HIGHEST = lax.Precision.HIGHEST
MESH = pl.DeviceIdType.MESH

D_MODEL = 1024
SEQ = 2048
DEPTH = 2
X_WIDTH = 256
N_X_HEADS = 4
X_HEAD_DIM = 64
MIX_W = 768
LIN_DH = 128
N_LIN = 6
CONV_K = 4
CHUNK = 64
SB_DH = 64
SB_PAIRS = 6
N_MEM = 256
D_FF = 4096
EPS = 1e-6
IN_A = 3340
IN_A_PAD = 3456
IN_B = 2560
SMALL_COL = 26
N_CHIPS = 4

ADAM_LR, ADAM_B1, ADAM_B2, ADAM_EPS, ADAM_WD, ADAM_STEP = 0.001, 0.9, 0.999, 1e-08, 0.01, 10

VMEM_LIMIT = 48 * 1024 * 1024

ANY = pl.BlockSpec(memory_space=pl.ANY)

NN = (((1,), (0,)), ((), ()))
NT = (((1,), (1,)), ((), ()))
TN = (((0,), (0,)), ((), ()))


def _cparams(sem):
    return pltpu.CompilerParams(dimension_semantics=sem, vmem_limit_bytes=VMEM_LIMIT)


def _dotbf(a, b, dn=NN):
    return lax.dot_general(a.astype(BF16), b.astype(BF16), dn, preferred_element_type=F32)


def _split(a):
    hi = a.astype(BF16)
    lo = (a - hi.astype(F32)).astype(BF16)
    return hi, lo


def _dot3(a, b, dn=NN):
    ah, al = _split(a)
    bh, bl = _split(b)
    d = functools.partial(lax.dot_general, dimension_numbers=dn, preferred_element_type=F32)
    return d(ah, bh) + (d(ah, bl) + d(al, bh))


def _dot_mask(parts, m01):
    d = functools.partial(lax.dot_general, dimension_numbers=NN, preferred_element_type=F32)
    return d(parts[0], m01) + d(parts[1], m01)


def _iota(shape, dim):
    return lax.broadcasted_iota(jnp.int32, shape, dim)


def _softplus(x):
    return jnp.maximum(x, 0.0) + jnp.log(1.0 + jnp.exp(-jnp.abs(x)))


def _log_sigmoid(z):
    return jnp.minimum(z, 0.0) - jnp.log(1.0 + jnp.exp(-jnp.abs(z)))


def _rms(x, g):
    r = lax.rsqrt(jnp.mean(x * x, axis=-1, keepdims=True) + EPS)
    return (x * r) * g


def _matmul(a, b, *, mode, tm, tn, tk, name, out_dtypes=(F32,), epilogue=None, extras=(), n_outer=False):
    if n_outer:
        ix = lambda f: (lambda j, i, kk: f(i, j, kk))
    else:
        ix = lambda f: f
    if mode == "nn":
        (m, k), (k2, n) = a.shape, b.shape
        a_spec = pl.BlockSpec((tm, tk), ix(lambda i, j, kk: (i, kk)))
        b_spec = pl.BlockSpec((tk, tn), ix(lambda i, j, kk: (kk, j)))
        dn = NN
    elif mode == "nt":
        (m, k), (n, k2) = a.shape, b.shape
        a_spec = pl.BlockSpec((tm, tk), ix(lambda i, j, kk: (i, kk)))
        b_spec = pl.BlockSpec((tn, tk), ix(lambda i, j, kk: (j, kk)))
        dn = NT
    else:
        (k, m), (k2, n) = a.shape, b.shape
        a_spec = pl.BlockSpec((tk, tm), ix(lambda i, j, kk: (kk, i)))
        b_spec = pl.BlockSpec((tk, tn), ix(lambda i, j, kk: (kk, j)))
        dn = TN
    assert k == k2 and m % tm == 0 and n % tn == 0 and k % tk == 0, (name, a.shape, b.shape)
    assert a.dtype == BF16 and b.dtype == BF16, name
    nk = k // tk
    n_extra, n_out = len(extras), len(out_dtypes)

    def finish(acc, extra_refs, out_refs):
        outs = (acc,) if epilogue is None else epilogue(acc, *[r[...] for r in extra_refs])
        for o_ref, o in zip(out_refs, outs):
            o_ref[...] = o.astype(o_ref.dtype)

    def body_single(a_ref, b_ref, *rest):
        acc = lax.dot_general(a_ref[...], b_ref[...], dn, preferred_element_type=F32)
        finish(acc, rest[:n_extra], rest[n_extra:n_extra + n_out])

    def body_tiled(a_ref, b_ref, *rest):
        extra_refs, out_refs, acc_ref = rest[:n_extra], rest[n_extra:n_extra + n_out], rest[-1]
        kk = pl.program_id(2)

        @pl.when(kk == 0)
        def _():
            acc_ref[...] = jnp.zeros_like(acc_ref)

        acc_ref[...] += lax.dot_general(a_ref[...], b_ref[...], dn, preferred_element_type=F32)

        @pl.when(kk == nk - 1)
        def _():
            finish(acc_ref[...], extra_refs, out_refs)

    mn_spec = pl.BlockSpec((tm, tn), ix(lambda i, j, kk: (i, j)))
    grid = (n // tn, m // tm, nk) if n_outer else (m // tm, n // tn, nk)
    outs = pl.pallas_call(
        body_single if nk == 1 else body_tiled,
        name=name,
        grid=grid,
        in_specs=[a_spec, b_spec] + [mn_spec] * n_extra,
        out_specs=[mn_spec] * n_out,
        out_shape=[jax.ShapeDtypeStruct((m, n), dt) for dt in out_dtypes],
        scratch_shapes=[] if nk == 1 else [pltpu.VMEM((tm, tn), F32)],
        compiler_params=_cparams(("parallel", "parallel", "arbitrary")),
    )(a, b, *extras)
    return outs[0] if n_out == 1 else outs


ROW_TILE = 512


def _row_spec(width=D_MODEL, tile=ROW_TILE):
    return pl.BlockSpec((tile, width), lambda i: (i, 0))


def _vec_spec(width=D_MODEL):
    return pl.BlockSpec((1, width), lambda i: (0, 0))


def _rms_fwd(x, g, *, name, tile=ROW_TILE):
    t = x.shape[0]

    def body(x_ref, g_ref, h_ref):
        h_ref[...] = _rms(x_ref[...], g_ref[...]).astype(BF16)

    return pl.pallas_call(
        body, name=name, grid=(t // tile,),
        in_specs=[_row_spec(tile=tile), _vec_spec()], out_specs=_row_spec(tile=tile),
        out_shape=jax.ShapeDtypeStruct((t, D_MODEL), BF16),
        compiler_params=_cparams(("parallel",)),
    )(x, g.reshape(1, D_MODEL))


def _post_norm_add(xres, y, g_post, g_next, *, name):
    t = xres.shape[0]

    def body(x_ref, y_ref, gp_ref, gn_ref, xo_ref, h_ref):
        xo = x_ref[...] + _rms(y_ref[...], gp_ref[...])
        xo_ref[...] = xo
        h_ref[...] = _rms(xo, gn_ref[...]).astype(BF16)

    return pl.pallas_call(
        body, name=name, grid=(t // ROW_TILE,),
        in_specs=[_row_spec(), _row_spec(), _vec_spec(), _vec_spec()],
        out_specs=[_row_spec(), _row_spec()],
        out_shape=[jax.ShapeDtypeStruct((t, D_MODEL), F32), jax.ShapeDtypeStruct((t, D_MODEL), BF16)],
        compiler_params=_cparams(("parallel",)),
    )(xres, y, g_post.reshape(1, D_MODEL), g_next.reshape(1, D_MODEL))


def _post_norm_loss(xres, y, g_post, target, *, name):
    t = xres.shape[0]

    def body(x_ref, y_ref, gp_ref, t_ref, loss_ref, dx_ref):
        @pl.when(pl.program_id(0) == 0)
        def _():
            loss_ref[...] = jnp.zeros_like(loss_ref)

        err = (x_ref[...] + _rms(y_ref[...], gp_ref[...])) - t_ref[...]
        per_tok = jnp.mean(err * err, axis=-1, keepdims=True)
        loss_ref[...] += 0.5 * jnp.sum(per_tok, axis=0, keepdims=True)
        dx_ref[...] = err * (1.0 / D_MODEL)

    return pl.pallas_call(
        body, name=name, grid=(t // ROW_TILE,),
        in_specs=[_row_spec(), _row_spec(), _vec_spec(), _row_spec()],
        out_specs=[pl.BlockSpec((1, 128), lambda i: (0, 0)), _row_spec()],
        out_shape=[jax.ShapeDtypeStruct((1, 128), F32), jax.ShapeDtypeStruct((t, D_MODEL), F32)],
        compiler_params=_cparams(("arbitrary",)),
    )(xres, y, g_post.reshape(1, D_MODEL), target)


def _rms_bwd(dy, x, g, *, name, res=None, out_dtype=F32, tile=ROW_TILE):
    t = x.shape[0]
    has_res = res is not None

    def body(dy_ref, x_ref, g_ref, *rest):
        res_ref = rest[0] if has_res else None
        dx_ref, dg_ref = rest[-2], rest[-1]

        @pl.when(pl.program_id(0) == 0)
        def _():
            dg_ref[...] = jnp.zeros_like(dg_ref)

        xf = x_ref[...]
        dyf = dy_ref[...].astype(F32)
        r = lax.rsqrt(jnp.mean(xf * xf, axis=-1, keepdims=True) + EPS)
        xhat = xf * r
        dg_ref[...] += jnp.sum(dyf * xhat, axis=0, keepdims=True)
        dxh = dyf * g_ref[...]
        dx = r * (dxh - xhat * jnp.mean(dxh * xhat, axis=-1, keepdims=True))
        if has_res:
            dx = dx + res_ref[...]
        dx_ref[...] = dx.astype(dx_ref.dtype)

    args = [dy, x, g.reshape(1, D_MODEL)] + ([res] if has_res else [])
    return pl.pallas_call(
        body, name=name, grid=(t // tile,),
        in_specs=[_row_spec(tile=tile), _row_spec(tile=tile), _vec_spec()] + ([_row_spec(tile=tile)] if has_res else []),
        out_specs=[_row_spec(tile=tile), _vec_spec()],
        out_shape=[jax.ShapeDtypeStruct((t, D_MODEL), out_dtype), jax.ShapeDtypeStruct((1, D_MODEL), F32)],
        compiler_params=_cparams(("arbitrary",)),
    )(*args)


def _rms_bwd_pair(dh, x, g_pre, res, y, g_post, *, name):
    t = x.shape[0]

    def norm_bwd(dy, xf, g):
        r = lax.rsqrt(jnp.mean(xf * xf, axis=-1, keepdims=True) + EPS)
        xhat = xf * r
        dxh = dy * g
        dx = r * (dxh - xhat * jnp.mean(dxh * xhat, axis=-1, keepdims=True))
        return dx, jnp.sum(dy * xhat, axis=0, keepdims=True)

    def body(dh_ref, x_ref, gp_ref, res_ref, y_ref, gq_ref, dx_ref, dy_ref, dgp_ref, dgq_ref):
        @pl.when(pl.program_id(0) == 0)
        def _():
            dgp_ref[...] = jnp.zeros_like(dgp_ref)
            dgq_ref[...] = jnp.zeros_like(dgq_ref)

        dx, dgp = norm_bwd(dh_ref[...], x_ref[...], gp_ref[...])
        dx = dx + res_ref[...]
        dx_ref[...] = dx
        dy, dgq = norm_bwd(dx, y_ref[...], gq_ref[...])
        dy_ref[...] = dy.astype(BF16)
        dgp_ref[...] += dgp
        dgq_ref[...] += dgq

    return pl.pallas_call(
        body, name=name, grid=(t // ROW_TILE,),
        in_specs=[_row_spec(), _row_spec(), _vec_spec(), _row_spec(), _row_spec(), _vec_spec()],
        out_specs=[_row_spec(), _row_spec(), _vec_spec(), _vec_spec()],
        out_shape=[jax.ShapeDtypeStruct((t, D_MODEL), F32), jax.ShapeDtypeStruct((t, D_MODEL), BF16),
                   jax.ShapeDtypeStruct((1, D_MODEL), F32), jax.ShapeDtypeStruct((1, D_MODEL), F32)],
        compiler_params=_cparams(("arbitrary",)),
    )(dh, x, g_pre.reshape(1, D_MODEL), res, y, g_post.reshape(1, D_MODEL))


CONV_COLS = 256
N_CONV_BLOCKS = 3 * MIX_W // CONV_COLS
CONV_STRIP = 128


def _shift_down(x, k):
    if k == 0:
        return x
    return jnp.where(_iota(x.shape, 0) >= k, pltpu.roll(x, k, 0), 0.0)


def _shift_up(x, k):
    if k == 0:
        return x
    s = x.shape[0]
    return jnp.where(_iota(x.shape, 0) < s - k, pltpu.roll(x, s - k, 0), 0.0)


def _conv_pre(x, w_ref):
    c = w_ref[CONV_K - 1:CONV_K, :] * x
    for i in range(CONV_K - 1):
        c = c + w_ref[i:i + 1, :] * _shift_down(x, CONV_K - 1 - i)
    return c


def _conv_silu_fwd(proj, conv_w, n_batch):
    def body(x_ref, w_ref, y_ref):
        c = _conv_pre(x_ref[...], w_ref)
        y_ref[...] = c * jax.nn.sigmoid(c)

    return pl.pallas_call(
        body, name="conv_silu_fwd", grid=(n_batch, N_CONV_BLOCKS),
        in_specs=[pl.BlockSpec((SEQ, CONV_COLS), lambda b, j: (b, j)),
                  pl.BlockSpec((CONV_K, CONV_COLS), lambda b, j: (0, j))],
        out_specs=pl.BlockSpec((SEQ, CONV_COLS), lambda b, j: (b, j)),
        out_shape=jax.ShapeDtypeStruct((n_batch * SEQ, 3 * MIX_W), F32),
        compiler_params=_cparams(("parallel", "parallel")),
    )(proj, conv_w)


def _conv_silu_bwd(dy, proj, conv_w, dproj, n_batch):
    strip, halo = CONV_STRIP, 8
    n_strips = SEQ // strip

    def body(dy_ref, x_ref, w_ref, _, dx_ref, dw_ref, xpad, dcpad):
        @pl.when(pl.program_id(1) == 0)
        def _():
            dw_ref[...] = jnp.zeros_like(dw_ref)

        xpad[0:halo, :] = jnp.zeros((halo, CONV_COLS), F32)
        xpad[halo:, :] = x_ref[...]
        dcpad[SEQ:, :] = jnp.zeros((halo, CONV_COLS), F32)
        taps = [w_ref[i:i + 1, :] for i in range(CONV_K)]

        def first(s, dw):
            a = pl.multiple_of(s * strip, strip)
            win = xpad[pl.ds(a, strip + halo), :]
            xs = [(win if i == CONV_K - 1 else pltpu.roll(win, CONV_K - 1 - i, 0))[halo:] for i in range(CONV_K)]
            c = taps[0] * xs[0]
            for i in range(1, CONV_K):
                c = c + taps[i] * xs[i]
            sig = jax.nn.sigmoid(c)
            dc = dy_ref[pl.ds(a, strip), :] * (sig * (1.0 + c * (1.0 - sig)))
            dcpad[pl.ds(a, strip), :] = dc
            return tuple(dw[i] + jnp.sum(dc * xs[i], axis=0, keepdims=True) for i in range(CONV_K))

        dw = lax.fori_loop(0, n_strips, first, tuple(jnp.zeros((1, CONV_COLS), F32) for _ in range(CONV_K)))
        for i in range(CONV_K):
            dw_ref[i:i + 1, :] += dw[i]

        def second(s, carry):
            a = pl.multiple_of(s * strip, strip)
            win = dcpad[pl.ds(a, strip + halo), :]
            dx = taps[CONV_K - 1] * win[:strip]
            for i in range(CONV_K - 1):
                dx = dx + taps[i] * pltpu.roll(win, strip + halo - (CONV_K - 1 - i), 0)[:strip]
            dx_ref[pl.ds(a, strip), :] = dx.astype(BF16)
            return carry

        lax.fori_loop(0, n_strips, second, 0)

    return pl.pallas_call(
        body, name="conv_silu_bwd", grid=(N_CONV_BLOCKS, n_batch),
        in_specs=[pl.BlockSpec((SEQ, CONV_COLS), lambda j, b: (b, j)),
                  pl.BlockSpec((SEQ, CONV_COLS), lambda j, b: (b, j)),
                  pl.BlockSpec((CONV_K, CONV_COLS), lambda j, b: (0, j)), ANY],
        out_specs=[pl.BlockSpec((SEQ, CONV_COLS), lambda j, b: (b, j)),
                   pl.BlockSpec((CONV_K, CONV_COLS), lambda j, b: (0, j))],
        out_shape=[jax.ShapeDtypeStruct(dproj.shape, BF16),
                   jax.ShapeDtypeStruct((CONV_K, 3 * MIX_W), F32)],
        input_output_aliases={3: 0},
        scratch_shapes=[pltpu.VMEM((SEQ + 8, CONV_COLS), F32), pltpu.VMEM((SEQ + 8, CONV_COLS), F32)],
        compiler_params=_cparams(("parallel", "arbitrary")),
    )(dy, proj, conv_w, dproj)


@jax.custom_vjp
def _solve_apply(low, rhs, tinv):
    return _dot3(tinv, rhs)


def _solve_apply_fwd(low, rhs, tinv):
    sol = _dot3(tinv, rhs)
    return sol, (tinv, sol)


def _solve_apply_bwd(resid, g):
    tinv, sol = resid
    y = _dot3(tinv, g, TN)
    return -_dot3(y, sol, NT), y, jnp.zeros_like(tinv)


_solve_apply.defvjp(_solve_apply_fwd, _solve_apply_bwd)


def _inv_unit_lower(lows):
    c = lows[0].shape[0]
    eye = (_iota((c, c), 0) == _iota((c, c), 1)).astype(F32)
    ms = [-low for low in lows]
    ps = [eye + m for m in ms]
    for _ in range(5):
        ms = [_dot3(m, m) for m in ms]
        ps = [p + _dot3(p, m) for p, m in zip(ps, ms)]
    return ps


def _gdn_chunk(qs, ks, vs, gates, states, small, alog_row, dtb_row, gain_row, tinvs):
    c = small.shape[0]
    heads = range(N_LIN)
    lane = _iota((c, 128), 1)
    row, col = _iota((c, c), 0), _iota((c, c), 1)
    causal, strict = row >= col, row > col
    last = _iota((c, 1), 0) == c - 1

    beta_all = jax.nn.sigmoid(small)
    g_all = -jnp.exp(alog_row) * _softplus(small + dtb_row)
    ltri = (col <= row).astype(F32)
    gc_all = lax.dot_general(ltri, g_all, NN, precision=HIGHEST, preferred_element_type=F32)

    beta = [jnp.sum(jnp.where(lane == h, beta_all, 0.0), axis=1, keepdims=True) for h in heads]
    gc = [jnp.sum(jnp.where(lane == N_LIN + h, gc_all, 0.0), axis=1, keepdims=True) for h in heads]
    gc_j = [lax.dot_general((lane == N_LIN + h).astype(F32), gc_all, NT, precision=HIGHEST,
                            preferred_element_type=F32) for h in heads]
    decay = [jnp.where(causal, jnp.exp(jnp.where(causal, gc[h] - gc_j[h], 0.0)), 0.0) for h in heads]
    gc_last = [jnp.sum(jnp.where(last, gc[h], 0.0), axis=0, keepdims=True) for h in heads]
    egc = [jnp.exp(g) for g in gc]
    qn = [q * lax.rsqrt(jnp.sum(q * q, axis=-1, keepdims=True) + EPS) * (LIN_DH ** -0.5) for q in qs]
    kn = [k * lax.rsqrt(jnp.sum(k * k, axis=-1, keepdims=True) + EPS) for k in ks]
    kb = [kn[h] * beta[h] for h in heads]
    low = [jnp.where(strict, _dotbf(kb[h], kn[h], NT) * decay[h], 0.0) for h in heads]
    if tinvs is None:
        tinvs = _inv_unit_lower(low)
    u = [_solve_apply(low[h], vs[h] * beta[h], tinvs[h]) for h in heads]
    w = [_solve_apply(low[h], kb[h] * egc[h], tinvs[h]) for h in heads]
    intra = [_dotbf(qn[h], kn[h], NT) * decay[h] for h in heads]
    v_new = [u[h] - _dotbf(w[h], states[h]) for h in heads]
    o = [_dotbf(qn[h] * egc[h], states[h]) + _dotbf(intra[h], v_new[h]) for h in heads]
    new_states = [states[h] * jnp.exp(gc_last[h]) + _dotbf(kn[h] * jnp.exp(gc_last[h] - gc[h]), v_new[h], TN)
                  for h in heads]
    o = [x * lax.rsqrt(jnp.mean(x * x, axis=-1, keepdims=True) + EPS) * gain_row for x in o]
    outs = [o[h] * (gates[h] * jax.nn.sigmoid(gates[h])) for h in heads]
    return outs, new_states, tinvs


def _gdn_param_rows(a_log, dt_bias, onorm):
    row = lambda v: jnp.pad(v.reshape(1, N_LIN), ((0, 0), (N_LIN, 128 - 2 * N_LIN)))
    return row(a_log), row(dt_bias), onorm.reshape(1, LIN_DH)


def _head(ref_or_val, h):
    return ref_or_val[:, LIN_DH * h:LIN_DH * (h + 1)]


def _gdn_fwd(qkv, proj, alog_row, dtb_row, gain_row, n_batch, gather=None):
    nc = SEQ // CHUNK
    t = n_batch * SEQ
    steps = n_batch * nc

    def body(qkv_ref, small_ref, gate_ref, al_ref, dt_ref, gn_ref, *rest):
        if gather is None:
            mix_ref, st_ref, ti_ref, s_scr = rest
        else:
            w_ref, mix_ref, st_ref, ti_ref, out_ref, s_scr, send_sems, recv_sems = rest
            step = pl.program_id(0) * nc + pl.program_id(1)
            for at, phase in ((0, "start"), (3 * steps // 4, "forward"), (steps - 1, "finish")):
                @pl.when(step == at)
                def _(phase=phase):
                    getattr(_Gather(w_ref, out_ref, send_sems, recv_sems), phase)()

        @pl.when(pl.program_id(1) == 0)
        def _():
            s_scr[...] = jnp.zeros_like(s_scr)

        heads = range(N_LIN)
        states = [s_scr[h] for h in heads]
        outs, new_states, tinvs = _gdn_chunk(
            [_head(qkv_ref, h) for h in heads], [_head(qkv_ref, N_LIN + h) for h in heads],
            [_head(qkv_ref, 2 * N_LIN + h) for h in heads], [_head(gate_ref, h) for h in heads],
            states, small_ref[...], al_ref[...], dt_ref[...], gn_ref[...], None)
        mix_ref[...] = jnp.concatenate(outs, axis=1).astype(BF16)
        for h in heads:
            st_ref[0, 0, h] = states[h]
            s_scr[h] = new_states[h]
            ti_ref[0, 0, h] = tinvs[h]

    row = lambda b, n: b * nc + n
    vec = pl.BlockSpec((1, 128), lambda b, n: (0, 0))
    extra = [] if gather is None else [gather]
    return pl.pallas_call(
        body, name="gdn_fwd", grid=(n_batch, nc),
        in_specs=[pl.BlockSpec((CHUNK, 3 * MIX_W), lambda b, n: (row(b, n), 0)),
                  pl.BlockSpec((CHUNK, 128), lambda b, n: (row(b, n), SMALL_COL)),
                  pl.BlockSpec((CHUNK, MIX_W), lambda b, n: (row(b, n), 3)),
                  vec, vec, vec] + [ANY] * len(extra),
        out_specs=[pl.BlockSpec((CHUNK, MIX_W), lambda b, n: (row(b, n), 0)),
                   pl.BlockSpec((1, 1, N_LIN, LIN_DH, LIN_DH), lambda b, n: (b, n, 0, 0, 0)),
                   pl.BlockSpec((1, 1, N_LIN, CHUNK, CHUNK), lambda b, n: (b, n, 0, 0, 0))] + [ANY] * len(extra),
        out_shape=[jax.ShapeDtypeStruct((t, D_MODEL), BF16),
                   jax.ShapeDtypeStruct((n_batch, nc, N_LIN, LIN_DH, LIN_DH), F32),
                   jax.ShapeDtypeStruct((n_batch, nc, N_LIN, CHUNK, CHUNK), F32)]
                  + [jax.ShapeDtypeStruct(g.shape, g.dtype) for g in extra],
        input_output_aliases={6: 3} if extra else {},
        scratch_shapes=[pltpu.VMEM((N_LIN, LIN_DH, LIN_DH), F32)] + (GATHER_SEMS if extra else []),
        compiler_params=_cparams(("arbitrary", "arbitrary")),
    )(qkv, proj, proj, alog_row, dtb_row, gain_row, *extra)


def _gdn_bwd(dcat, qkv, proj, states, tinvs, alog_row, dtb_row, gain_row, n_batch, exchange=None):
    nc = SEQ // CHUNK
    t = n_batch * SEQ
    steps = n_batch * nc

    def body(dmix_ref, qkv_ref, small_ref, gate_ref, st_ref, ti_ref, al_ref, dt_ref, gn_ref, *rest):
        if exchange is None:
            dqkv_ref, dgate_ref, dsmall_ref, dal_ref, ddt_ref, dgn_ref, ds_scr = rest
        else:
            p_ref, dqkv_ref, dgate_ref, dsmall_ref, dal_ref, ddt_ref, dgn_ref, q_ref, ds_scr, send_sems, recv_sems = rest
            step = pl.program_id(0) * nc + pl.program_id(1)

            @pl.when(step == 0)
            def _():
                for cp in _exchange_copies(p_ref, q_ref, send_sems, recv_sems):
                    cp.start()

            @pl.when(step == steps - 1)
            def _():
                for cp in _exchange_copies(p_ref, q_ref, send_sems, recv_sems):
                    cp.wait()

        @pl.when(pl.program_id(1) == 0)
        def _():
            ds_scr[...] = jnp.zeros_like(ds_scr)

        @pl.when((pl.program_id(0) == 0) & (pl.program_id(1) == 0))
        def _():
            dal_ref[...] = jnp.zeros_like(dal_ref)
            ddt_ref[...] = jnp.zeros_like(ddt_ref)
            dgn_ref[...] = jnp.zeros_like(dgn_ref)

        heads = range(N_LIN)
        tinvs = [ti_ref[0, 0, h] for h in heads]

        def chunk(qs, ks, vs, gates, states_in, small, al, dt, gn):
            outs, new_states, _ = _gdn_chunk(qs, ks, vs, gates, states_in, small, al, dt, gn, tinvs)
            return tuple(outs), tuple(new_states)

        prim = (tuple(_head(qkv_ref, h) for h in heads),
                tuple(_head(qkv_ref, N_LIN + h) for h in heads),
                tuple(_head(qkv_ref, 2 * N_LIN + h) for h in heads),
                tuple(_head(gate_ref, h) for h in heads),
                tuple(st_ref[0, 0, h] for h in heads),
                small_ref[...], al_ref[...], dt_ref[...], gn_ref[...])
        _, vjp = jax.vjp(chunk, *prim)
        cot = (tuple(_head(dmix_ref, h) for h in heads), tuple(ds_scr[h] for h in heads))
        dq, dk, dv, dgate, dstate, dsmall, dal, ddt, dgn = vjp(cot)
        dqkv_ref[...] = jnp.concatenate(list(dq) + list(dk) + list(dv), axis=1)
        dgate_ref[...] = jnp.concatenate(list(dgate), axis=1).astype(BF16)
        dsmall_ref[...] = dsmall.astype(BF16)
        for h in heads:
            ds_scr[h] = dstate[h]
        dal_ref[...] += dal
        ddt_ref[...] += ddt
        dgn_ref[...] += dgn

    row = lambda b, n: b * nc + (nc - 1 - n)
    vec = pl.BlockSpec((1, 128), lambda b, n: (0, 0))
    extra = [] if exchange is None else [exchange]
    return pl.pallas_call(
        body, name="gdn_bwd", grid=(n_batch, nc),
        in_specs=[pl.BlockSpec((CHUNK, MIX_W), lambda b, n: (row(b, n), 0)),
                  pl.BlockSpec((CHUNK, 3 * MIX_W), lambda b, n: (row(b, n), 0)),
                  pl.BlockSpec((CHUNK, 128), lambda b, n: (row(b, n), SMALL_COL)),
                  pl.BlockSpec((CHUNK, MIX_W), lambda b, n: (row(b, n), 3)),
                  pl.BlockSpec((1, 1, N_LIN, LIN_DH, LIN_DH), lambda b, n: (b, nc - 1 - n, 0, 0, 0)),
                  pl.BlockSpec((1, 1, N_LIN, CHUNK, CHUNK), lambda b, n: (b, nc - 1 - n, 0, 0, 0)),
                  vec, vec, vec] + [ANY] * len(extra),
        out_specs=[pl.BlockSpec((CHUNK, 3 * MIX_W), lambda b, n: (row(b, n), 0)),
                   pl.BlockSpec((CHUNK, MIX_W), lambda b, n: (row(b, n), 3)),
                   pl.BlockSpec((CHUNK, 128), lambda b, n: (row(b, n), 0)),
                   vec, vec, vec] + [ANY] * len(extra),
        out_shape=[jax.ShapeDtypeStruct((t, 3 * MIX_W), F32),
                   jax.ShapeDtypeStruct((t, IN_A_PAD), BF16),
                   jax.ShapeDtypeStruct((t, 128), BF16),
                   jax.ShapeDtypeStruct((1, 128), F32),
                   jax.ShapeDtypeStruct((1, 128), F32),
                   jax.ShapeDtypeStruct((1, 128), F32)]
                  + [jax.ShapeDtypeStruct((3,) + p.shape[1:], p.dtype) for p in extra],
        scratch_shapes=[pltpu.VMEM((N_LIN, LIN_DH, LIN_DH), F32)] + (EXCHANGE_SEMS if extra else []),
        compiler_params=_cparams(("arbitrary", "arbitrary")),
    )(dcat, qkv, proj, proj, states, tinvs, alog_row, dtb_row, gain_row, *extra)


SB_T = 256


def _sb_masks():
    r, c = _iota((SB_T, SB_T), 0), _iota((SB_T, SB_T), 1)
    return r, c


def _staggered(chains):
    pending, live = list(chains), []
    while pending or live:
        if pending:
            live.append(pending.pop(0))
        for g in list(live):
            try:
                next(g)
            except StopIteration:
                live.remove(g)


def _sb_rows(kb):
    start = kb * SB_T
    return pl.ds(start if isinstance(kb, int) else pl.multiple_of(start, SB_T), SB_T)


def _sb_fwd(proj, n_batch):
    nq = SEQ // SB_T
    t = n_batch * SEQ
    scale = SB_DH ** -0.5
    both = range(2)

    def body(q_ref, k_ref, v_ref, o_ref, tot_ref, acc_scr, run_scr):
        qi = pl.program_id(2)
        lane = _iota((SB_T, 128), 1)
        r, c = _sb_masks()
        upper = (r > c).astype(BF16)
        q = q_ref[...] * scale
        qm = [jnp.where((lane < SB_DH) == (hh == 0), q, jnp.zeros_like(q)) for hh in both]

        def blocks(kbs, diagonal):
            k_blk = [k_ref[_sb_rows(kb), :] for kb in kbs]
            v_blk = [v_ref[_sb_rows(kb), :] for kb in kbs]
            run = [None if diagonal else run_scr[hh][:, 0:1] for hh in both]
            pv = {hh: [] for hh in both}
            rowsums = {hh: [] for hh in both}

            def chain(n, hh):
                z = lax.dot_general(qm[hh], k_blk[n], NT, preferred_element_type=F32)
                yield
                lb = _log_sigmoid(z)
                l1m = lb - z
                if diagonal:
                    l1m = jnp.where(r > c, l1m, 0.0)
                parts = _split(l1m)
                before = run[hh] if not rowsums[hh] else run[hh] + sum(rowsums[hh][1:], rowsums[hh][0])
                rowsums[hh].append(jnp.sum(l1m, axis=1, keepdims=True))
                yield
                tail = _dot_mask(parts, upper)
                yield
                a = jnp.exp(lb + (tail if diagonal else before + tail))
                if diagonal:
                    a = jnp.where(r > c, a, 0.0)
                a = a.astype(BF16)
                yield
                pv[hh].append(lax.dot_general(a, v_blk[n], NN, preferred_element_type=F32))

            _staggered([chain(n, hh) for n in range(len(kbs)) for hh in both])
            for hh in both:
                if diagonal:
                    acc_scr[hh] = pv[hh][0]
                    run_scr[hh] = jnp.broadcast_to(rowsums[hh][0], (SB_T, 128))
                else:
                    acc_scr[hh] += sum(pv[hh][1:], pv[hh][0])
                    run_scr[hh] += sum(rowsums[hh][1:], rowsums[hh][0])

        blocks([qi], True)

        def step(it, carry):
            kb = qi - 1 - 2 * it
            blocks([kb, kb - 1], False)
            return carry

        lax.fori_loop(0, qi >> 1, step, 0)

        @pl.when((qi & 1) == 1)
        def _():
            blocks([0], False)
        first = lane < SB_DH
        o_ref[...] = jnp.where(first, acc_scr[0], acc_scr[1]).astype(BF16)
        tot_ref[...] = jnp.where(first, run_scr[0], run_scr[1])

    nq_blocks = lambda b, p, i: (b * nq + i, p)
    seq_spec = lambda which: pl.BlockSpec((SEQ, 128), lambda b, p, i: (b, 3 * p + which))
    return pl.pallas_call(
        body, name="sb_fwd", grid=(n_batch, SB_PAIRS, nq),
        in_specs=[pl.BlockSpec((SB_T, 128), lambda b, p, i: (b * nq + i, 3 * p)), seq_spec(1), seq_spec(2)],
        out_specs=[pl.BlockSpec((SB_T, 128), nq_blocks), pl.BlockSpec((SB_T, 128), nq_blocks)],
        out_shape=[jax.ShapeDtypeStruct((t, D_MODEL), BF16),
                   jax.ShapeDtypeStruct((t, MIX_W), F32)],
        scratch_shapes=[pltpu.VMEM((2, SB_T, 128), F32), pltpu.VMEM((2, SB_T, 128), F32)],
        compiler_params=_cparams(("parallel", "parallel", "arbitrary")),
    )(proj, proj, proj)


def _sb_bwd(dcat, proj, totals, n_batch):
    nq = SEQ // SB_T
    t = n_batch * SEQ
    scale = SB_DH ** -0.5
    both = range(2)

    def body(do_ref, q_ref, k_ref, v_ref, tot_ref, dp_ref, dq_scr, run_scr, grun_scr, dk_ref, dv_ref):
        qi = pl.program_id(2)

        @pl.when(qi == 0)
        def _():
            dk_ref[...] = jnp.zeros_like(dk_ref)
            dv_ref[...] = jnp.zeros_like(dv_ref)

        lane = _iota((SB_T, 128), 1)
        r, c = _sb_masks()
        incl = (r <= c).astype(BF16)
        earlier = (r < c).astype(BF16)
        dq_scr[...] = jnp.zeros_like(dq_scr)
        run_scr[...] = jnp.zeros_like(run_scr)
        grun_scr[...] = jnp.zeros_like(grun_scr)
        q, do, tot = q_ref[...] * scale, do_ref[...], tot_ref[...]
        sel = [(lane < SB_DH) == (hh == 0) for hh in both]
        qm = [jnp.where(sel[hh], q, jnp.zeros_like(q)) for hh in both]
        dom = [jnp.where(sel[hh], do, 0.0).astype(BF16) for hh in both]
        total = [jnp.sum(jnp.where(lane == hh * SB_DH, tot, 0.0), axis=1, keepdims=True) for hh in both]

        def blocks(kbs, diagonal):
            k_blk = [k_ref[_sb_rows(kb), :] for kb in kbs]
            v_blk = [v_ref[_sb_rows(kb), :] for kb in kbs]
            run = [run_scr[hh][:, 0:1] for hh in both]
            grun = [grun_scr[hh][:, 0:1] for hh in both]
            rs_l, rs_e, dqp = ({hh: [] for hh in both} for _ in range(3))
            dk, dv = ([[] for _ in kbs] for _ in range(2))

            def plus(base, terms):
                return base if not terms else base + sum(terms[1:], terms[0])

            def chain(n, hh):
                z = lax.dot_general(qm[hh], k_blk[n], NT, preferred_element_type=F32)
                da = lax.dot_general(dom[hh], v_blk[n], NT, preferred_element_type=F32)
                yield
                lb = _log_sigmoid(z)
                sig = jnp.exp(lb)
                l1m = lb - z
                if diagonal:
                    l1m = jnp.where(r > c, l1m, 0.0)
                parts = _split(l1m)
                run_before = plus(run[hh], rs_l[hh])
                rs_l[hh].append(jnp.sum(l1m, axis=1, keepdims=True))
                yield
                prefix = run_before + _dot_mask(parts, incl)
                yield
                a = jnp.exp(lb + (total[hh] - prefix))
                if diagonal:
                    a = jnp.where(r > c, a, 0.0)
                de = a * da
                a = a.astype(BF16)
                parts = _split(de)
                grun_before = plus(grun[hh], rs_e[hh])
                rs_e[hh].append(jnp.sum(de, axis=1, keepdims=True))
                yield
                dv[n].append(lax.dot_general(a, dom[hh], TN, preferred_element_type=F32))
                dl1m = grun_before + _dot_mask(parts, earlier)
                yield
                if diagonal:
                    dl1m = jnp.where(r > c, dl1m, 0.0)
                dz = (de * (1.0 - sig) - dl1m * sig).astype(BF16)
                yield
                dqp[hh].append(lax.dot_general(dz, k_blk[n], NN, preferred_element_type=F32))
                dk[n].append(lax.dot_general(dz, qm[hh], TN, preferred_element_type=F32))

            _staggered([chain(n, hh) for n in range(len(kbs)) for hh in both])
            for hh in both:
                dq_scr[hh] += sum(dqp[hh][1:], dqp[hh][0])
                run_scr[hh] += sum(rs_l[hh][1:], rs_l[hh][0])
                grun_scr[hh] += sum(rs_e[hh][1:], rs_e[hh][0])
            for n, kb in enumerate(kbs):
                dk_ref[_sb_rows(kb), :] += dk[n][0] + dk[n][1]
                dv_ref[_sb_rows(kb), :] += dv[n][0] + dv[n][1]

        def step(it, carry):
            blocks([2 * it, 2 * it + 1], False)
            return carry

        lax.fori_loop(0, qi >> 1, step, 0)

        @pl.when((qi & 1) == 1)
        def _():
            blocks([qi - 1], False)

        blocks([qi], True)
        dq = (jnp.where(sel[0], dq_scr[0], dq_scr[1]) * scale).astype(BF16)
        dp_ref[pl.ds(pl.multiple_of(qi * SB_T, SB_T), SB_T), 0:128] = dq

        @pl.when(qi == nq - 1)
        def _():
            dp_ref[:, 128:256] = dk_ref[...].astype(BF16)
            dp_ref[:, 256:384] = dv_ref[...].astype(BF16)

    q_blocks = lambda b, p, i: (b * nq + i, p)
    seq_spec = lambda which: pl.BlockSpec((SEQ, 128), lambda b, p, i: (b, 3 * p + which))
    return pl.pallas_call(
        body, name="sb_bwd", grid=(n_batch, SB_PAIRS, nq),
        in_specs=[pl.BlockSpec((SB_T, 128), q_blocks),
                  pl.BlockSpec((SB_T, 128), lambda b, p, i: (b * nq + i, 3 * p)),
                  seq_spec(1), seq_spec(2), pl.BlockSpec((SB_T, 128), q_blocks)],
        out_specs=pl.BlockSpec((SEQ, 384), lambda b, p, i: (b, p)),
        out_shape=jax.ShapeDtypeStruct((t, IN_B), BF16),
        scratch_shapes=[pltpu.VMEM((2, SB_T, 128), F32), pltpu.VMEM((2, SB_T, 128), F32),
                        pltpu.VMEM((2, SB_T, 128), F32), pltpu.VMEM((SEQ, 128), F32), pltpu.VMEM((SEQ, 128), F32)],
        compiler_params=_cparams(("parallel", "arbitrary", "arbitrary")),
    )(dcat, proj, proj, proj, totals)


MEM_TQ = 512


def _mem_heads(lane):
    return [(lane >= X_HEAD_DIM * h) & (lane < X_HEAD_DIM * (h + 1)) for h in range(N_X_HEADS)]


def _mem_attn_fwd(proj, q_col, memkv, cat, n_batch):
    nq = SEQ // MEM_TQ
    scale = X_HEAD_DIM ** -0.5

    def body(q_ref, kv_ref, _, o_ref):
        q = q_ref[...]
        k = kv_ref[:, :X_WIDTH].astype(BF16)
        v = kv_ref[:, X_WIDTH:].astype(BF16)
        out = jnp.zeros((MEM_TQ, X_WIDTH), F32)
        for sel in _mem_heads(_iota((MEM_TQ, X_WIDTH), 1)):
            s = lax.dot_general(jnp.where(sel, q, 0.0).astype(BF16), k, NT, preferred_element_type=F32) * scale
            e = jnp.exp(s - jnp.max(s, axis=-1, keepdims=True))
            p = e / jnp.sum(e, axis=-1, keepdims=True)
            out = out + jnp.where(sel, lax.dot_general(p.astype(BF16), v, NN, preferred_element_type=F32), 0.0)
        o_ref[...] = out.astype(BF16)

    return pl.pallas_call(
        body, name="mem_attn_fwd", grid=(n_batch, nq),
        in_specs=[pl.BlockSpec((MEM_TQ, X_WIDTH), lambda b, i: (b * nq + i, q_col)),
                  pl.BlockSpec((N_MEM, 2 * X_WIDTH), lambda b, i: (b, 0)), ANY],
        out_specs=pl.BlockSpec((MEM_TQ, X_WIDTH), lambda b, i: (b * nq + i, MIX_W // X_WIDTH)),
        out_shape=jax.ShapeDtypeStruct(cat.shape, BF16),
        input_output_aliases={2: 0},
        compiler_params=_cparams(("parallel", "parallel")),
    )(proj, memkv, cat)


def _mem_attn_bwd(dcat, proj, q_col, memkv, dproj, n_batch, tail=None):
    nq = SEQ // MEM_TQ
    scale = X_HEAD_DIM ** -0.5
    width = X_WIDTH + (0 if tail is None else 128)
    assert (q_col * X_WIDTH) % width == 0

    def body(do_ref, q_ref, kv_ref, *rest):
        dq_ref, dkv_ref = rest[-2:]

        @pl.when(pl.program_id(1) == 0)
        def _():
            dkv_ref[...] = jnp.zeros_like(dkv_ref)

        q, do = q_ref[...], do_ref[...]
        k = kv_ref[:, :X_WIDTH].astype(BF16)
        v = kv_ref[:, X_WIDTH:].astype(BF16)
        dq = jnp.zeros((MEM_TQ, X_WIDTH), F32)
        dk = jnp.zeros((N_MEM, X_WIDTH), F32)
        dv = jnp.zeros((N_MEM, X_WIDTH), F32)
        for sel in _mem_heads(_iota((MEM_TQ, X_WIDTH), 1)):
            qm = jnp.where(sel, q, 0.0).astype(BF16)
            dom = jnp.where(sel, do, 0.0).astype(BF16)
            s = lax.dot_general(qm, k, NT, preferred_element_type=F32) * scale
            e = jnp.exp(s - jnp.max(s, axis=-1, keepdims=True))
            p = e / jnp.sum(e, axis=-1, keepdims=True)
            dp = lax.dot_general(dom, v, NT, preferred_element_type=F32)
            ds = ((p * (dp - jnp.sum(dp * p, axis=-1, keepdims=True))) * scale).astype(BF16)
            dv = dv + lax.dot_general(p.astype(BF16), dom, TN, preferred_element_type=F32)
            dk = dk + lax.dot_general(ds, qm, TN, preferred_element_type=F32)
            dq = dq + jnp.where(sel, lax.dot_general(ds, k, NN, preferred_element_type=F32), 0.0)
        if tail is None:
            dq_ref[...] = dq.astype(BF16)
        else:
            dq_ref[...] = jnp.concatenate([dq.astype(BF16), rest[0][...]], axis=1)
        dkv_ref[...] += jnp.concatenate([dk, dv], axis=1)

    rows = lambda b, i: b * nq + i
    extra = [] if tail is None else [tail]
    return pl.pallas_call(
        body, name="mem_attn_bwd", grid=(n_batch, nq),
        in_specs=[pl.BlockSpec((MEM_TQ, X_WIDTH), lambda b, i: (rows(b, i), MIX_W // X_WIDTH)),
                  pl.BlockSpec((MEM_TQ, X_WIDTH), lambda b, i: (rows(b, i), q_col)),
                  pl.BlockSpec((N_MEM, 2 * X_WIDTH), lambda b, i: (b, 0))]
                 + [pl.BlockSpec((MEM_TQ, 128), lambda b, i: (rows(b, i), 0))] * len(extra) + [ANY],
        out_specs=[pl.BlockSpec((MEM_TQ, width), lambda b, i: (rows(b, i), q_col * X_WIDTH // width)),
                   pl.BlockSpec((N_MEM, 2 * X_WIDTH), lambda b, i: (b, 0))],
        out_shape=[jax.ShapeDtypeStruct(dproj.shape, BF16),
                   jax.ShapeDtypeStruct((n_batch * N_MEM, 2 * X_WIDTH), F32)],
        input_output_aliases={3 + len(extra): 0},
        compiler_params=_cparams(("parallel", "arbitrary")),
    )(dcat, proj, memkv, *extra, dproj)


def _relu2_epilogue(acc):
    r = jnp.maximum(acc, 0.0)
    return (r * r,)


def _relu2_bwd_epilogue(acc, a):
    return (acc * (2.0 * jnp.sqrt(a.astype(F32))),)


def _pad_in_a(w_in_a):
    w = 3 * MIX_W
    parts = [w_in_a[:, :w], w_in_a[:, w:w + MIX_W], w_in_a[:, IN_A - X_WIDTH:],
             w_in_a[:, w + MIX_W:w + MIX_W + 2 * N_LIN]]
    pad = jnp.zeros((D_MODEL, IN_A_PAD - IN_A), w_in_a.dtype)
    return jnp.concatenate(parts + [pad], axis=1)


def _unpad_in_a(g):
    w = 3 * MIX_W
    return jnp.concatenate([g[:, :w + MIX_W], g[:, w + MIX_W + X_WIDTH:w + MIX_W + X_WIDTH + 2 * N_LIN],
                            g[:, w + MIX_W:w + MIX_W + X_WIDTH]], axis=1)


def _qkv_to_pairs(w):
    w3 = 3 * MIX_W
    qkv = w[:, :w3].reshape(-1, 3, SB_PAIRS, 128).transpose(0, 2, 1, 3).reshape(-1, w3)
    return jnp.concatenate([qkv, w[:, w3:]], axis=1)


def _pairs_to_qkv(w):
    w3 = 3 * MIX_W
    qkv = w[:, :w3].reshape(-1, SB_PAIRS, 3, 128).transpose(0, 2, 1, 3).reshape(-1, w3)
    return jnp.concatenate([qkv, w[:, w3:]], axis=1)


def _local_step(x, mem, target, wts, small, comm=None):
    wts = dict(wts)
    t = x.shape[0]
    nb = t // SEQ
    npre, npost, mpre, mpost = small["norm_pre_mix"], small["norm_post_mix"], small["norm_pre_mlp"], small["norm_post_mlp"]
    alog_row, dtb_row, gain_row = _gdn_param_rows(small["a_log_a"][0], small["dt_bias_a"][0], small["onorm_a"][0])
    conv_w = small["conv_w"]

    mem_n = _rms_fwd(mem, small["mem_norm"], name="mem_norm_fwd", tile=256)
    saved = []
    h = _rms_fwd(x, npre[0], name="pre_mix_norm0")
    big = min(1024, t)
    for i in range(DEPTH):
        s = {"x_in": x, "h1": h}
        if i == 0:
            proj = _matmul(h, wts["in_a", None], mode="nn", tm=big, tn=1152, tk=1024, name="proj_a")
            qkv = _conv_silu_fwd(proj, conv_w, nb)
            if comm is None:
                mix, states, tinvs = _gdn_fwd(qkv, proj, alog_row, dtb_row, gain_row, nb)
            else:
                mix, states, tinvs, second = _gdn_fwd(qkv, proj, alog_row, dtb_row, gain_row, nb, gather=comm["slabs"])
                second = second.reshape(N_CHIPS, REGION_ROWS[W_SECOND], D_MODEL)
                wts.update(_as_operands(_unpack_region_full(W_SECOND, second)))
            s.update(qkv=qkv, states=states, tinvs=tinvs)
            q_col = (3 * MIX_W + MIX_W) // X_WIDTH
        else:
            proj = _matmul(h, wts["in_b", None], mode="nn", tm=big, tn=1280, tk=1024, name="proj_b", out_dtypes=(BF16,))
            mix, totals = _sb_fwd(proj, nb)
            s.update(totals=totals)
            q_col = 3 * MIX_W // X_WIDTH
        memkv = _matmul(mem_n, wts["mem_kv", i], mode="nn", tm=256, tn=512, tk=1024, name=f"memkv{i}")
        cat = _mem_attn_fwd(proj, q_col, memkv, mix, nb)
        y = _matmul(cat, wts["out", i], mode="nn", tm=big, tn=1024, tk=1024, name=f"out_proj{i}")
        x2, h2 = _post_norm_add(x, y, npost[i], mpre[i], name=f"post_mix{i}")
        a = _matmul(h2, wts["up", i], mode="nn", tm=big, tn=2048, tk=1024, name=f"up{i}",
                    out_dtypes=(BF16,), epilogue=_relu2_epilogue, n_outer=True)
        y2 = _matmul(a, wts["down", i], mode="nn", tm=big, tn=1024, tk=2048, name=f"down{i}")
        s.update(proj=proj, q_col=q_col, memkv=memkv, cat=cat, y=y, x2=x2, h2=h2, a=a, y2=y2)
        saved.append(s)
        if i + 1 < DEPTH:
            x, h = _post_norm_add(x2, y2, mpost[i], npre[i + 1], name=f"post_mlp{i}")
        else:
            loss_row, dx = _post_norm_loss(x2, y2, mpost[i], target, name="loss_head")

    gw = {}
    gs = {k: [None] * DEPTH for k in ("norm_pre_mix", "norm_post_mix", "norm_pre_mlp", "norm_post_mlp")}
    dmem_n, early = None, None
    for i in reversed(range(DEPTH)):
        s = saved[i]
        if i == DEPTH - 1:
            dy2, gs["norm_post_mlp"][i] = _rms_bwd(dx, s["y2"], mpost[i], name=f"post_mlp_bwd{i}", out_dtype=BF16)
        du = _matmul(dy2, wts["down", i], mode="nt", tm=big, tn=2048, tk=1024, name=f"down_dx{i}",
                     out_dtypes=(BF16,), epilogue=_relu2_bwd_epilogue, extras=(s["a"],), n_outer=True)
        gw["down", i] = _matmul(s["a"], dy2, mode="tn", tm=1024, tn=1024, tk=big, name=f"down_dw{i}")
        dh2 = _matmul(du, wts["up", i], mode="nt", tm=big, tn=1024, tk=2048, name=f"up_dx{i}")
        gw["up", i] = _matmul(s["h2"], du, mode="tn", tm=1024, tn=2048, tk=512, name=f"up_dw{i}")
        dx2, dy, gs["norm_pre_mlp"][i], gs["norm_post_mix"][i] = _rms_bwd_pair(
            dh2, s["x2"], mpre[i], dx, s["y"], npost[i], name=f"mlp_norms_bwd{i}")
        dcat = _matmul(dy, wts["out", i], mode="nt", tm=big, tn=1024, tk=1024, name=f"out_dx{i}")
        gw["out", i] = _matmul(s["cat"], dy, mode="tn", tm=1024, tn=1024, tk=big, name=f"out_dw{i}")
        if i == 0:
            exchange = None
            if comm is not None:
                own, exchange = _reduce_in_chip(_pack_region_full(G_EARLY, gw), comm["core"])
            res = _gdn_bwd(dcat, s["qkv"], s["proj"], s["states"], s["tinvs"], alog_row, dtb_row, gain_row, nb,
                           exchange=exchange)
            dqkv, dproj, dsmall, dalog, ddtb, dgain = res[:6]
            if comm is not None:
                early = (own, res[6])
            dproj, dconv = _conv_silu_bwd(dqkv, s["proj"], conv_w, dproj, nb)
            dproj, dmemkv = _mem_attn_bwd(dcat, s["proj"], s["q_col"], s["memkv"], dproj, nb, tail=dsmall)
            w_in, tile = wts["in_a", None], 1152
        else:
            dproj = _sb_bwd(dcat, s["proj"], s["totals"], nb)
            dproj, dmemkv = _mem_attn_bwd(dcat, s["proj"], s["q_col"], s["memkv"], dproj, nb)
            w_in, tile = wts["in_b", None], 1280
        dmemkv = dmemkv.astype(BF16)
        gw["mem_kv", i] = _matmul(mem_n, dmemkv, mode="tn", tm=1024, tn=512, tk=256, name=f"memkv_dw{i}")
        dmn = _matmul(dmemkv, wts["mem_kv", i], mode="nt", tm=256, tn=1024, tk=512, name=f"memkv_dx{i}")
        dmem_n = dmn if dmem_n is None else dmem_n + dmn
        dh1 = _matmul(dproj, w_in, mode="nt", tm=big, tn=1024, tk=tile, name=f"proj_dx{i}")
        g_in = _matmul(s["h1"], dproj, mode="tn", tm=1024, tn=tile, tk=big, name=f"proj_dw{i}")
        if i == 0:
            gw["in_a", None] = _unpad_in_a(g_in)
        else:
            gw["in_b", None] = _pairs_to_qkv(g_in)
        if i > 0:
            dx, dy2, gs["norm_pre_mix"][i], gs["norm_post_mlp"][i - 1] = _rms_bwd_pair(
                dh1, s["x_in"], npre[i], dx2, saved[i - 1]["y2"], mpost[i - 1], name=f"mix_norms_bwd{i}")
        else:
            dx, gs["norm_pre_mix"][i] = _rms_bwd(dh1, s["x_in"], npre[i], name=f"pre_mix_bwd{i}", res=dx2)

    _, g_mem_norm = _rms_bwd(dmem_n, mem, small["mem_norm"], name="mem_norm_bwd", tile=256)
    gsmall = {k: jnp.concatenate(v, axis=0) for k, v in gs.items()}
    gsmall.update(mem_norm=g_mem_norm[0], a_log_a=dalog[:, N_LIN:2 * N_LIN], dt_bias_a=ddtb[:, N_LIN:2 * N_LIN],
                  onorm_a=dgain, conv_w=dconv)
    return loss_row[0, 0], dx, gw, gsmall, early


SUM_TILE = 640


def _position():
    x, y, c = lax.axis_index("x"), lax.axis_index("y"), lax.axis_index("c")
    others = [(1 - x, y), (x, 1 - y), (1 - x, 1 - y)]
    return x, y, c, others


class _Gather:
    def __init__(self, w_ref, out_ref, send_sems, recv_sems):
        self.w, self.out, self.send, self.recv = w_ref, out_ref, send_sems, recv_sems
        self.x, self.y, self.c, self.others = _position()
        self.me = 2 * self.x + self.y

    def _copy(self, k, src, dst, to):
        return pltpu.make_async_remote_copy(src_ref=src, dst_ref=dst, send_sem=self.send.at[k],
                                            recv_sem=self.recv.at[k], device_id=to, device_id_type=MESH)

    def _first(self):
        return [self._copy(j, self.w.at[self.me, self.c], self.out.at[self.me, self.c], (ox, oy, self.c))
                for j, (ox, oy) in enumerate(self.others)]

    def _passed(self):
        sibling = (self.x, self.y, 1 - self.c)
        return [self._copy(3 + j, self.out.at[2 * ox + oy, self.c], self.out.at[2 * ox + oy, self.c], sibling)
                for j, (ox, oy) in enumerate(self.others)]

    def start(self):
        for cp in self._first():
            cp.start()

    def forward(self):
        passed = self._passed()
        for j, (ox, oy) in enumerate(self.others):
            self._copy(j, self.w.at[self.me, self.c], self.out.at[2 * ox + oy, self.c], (self.x, self.y, self.c)).wait_recv()
            passed[j].start()

    def finish(self):
        for j, (ox, oy) in enumerate(self.others):
            self._copy(3 + j, self.w.at[self.me, self.c], self.out.at[2 * ox + oy, 1 - self.c],
                       (self.x, self.y, self.c)).wait_recv()
        for cp in self._first() + self._passed():
            cp.wait_send()


GATHER_SEMS = [pltpu.SemaphoreType.DMA((6,)), pltpu.SemaphoreType.DMA((6,))]


def _gather_chips(wflat):
    def body(w_ref, out_ref, send_sems, recv_sems):
        g = _Gather(w_ref, out_ref, send_sems, recv_sems)
        g.start()
        g.forward()
        g.finish()

    return pl.pallas_call(
        body, name="gather_weights",
        in_specs=[ANY], out_specs=ANY, input_output_aliases={0: 0},
        out_shape=jax.ShapeDtypeStruct(wflat.shape, wflat.dtype),
        scratch_shapes=GATHER_SEMS,
    )(wflat)


def _gather_all(v, *, name):
    rows, n = v.shape

    def body(x_ref, out_ref, send_sems, recv_sems, local_sem):
        x, y, c, others = _position()
        me, sibling = (x, y, c), (x, y, 1 - c)

        def blk(px, py, pc):
            return out_ref.at[pl.ds((4 * px + 2 * py + pc) * rows, rows), :]

        def copy(k, block, to, src=None):
            return pltpu.make_async_remote_copy(src_ref=blk(*block) if src is None else src, dst_ref=blk(*block),
                                                send_sem=send_sems.at[k], recv_sem=recv_sems.at[k],
                                                device_id=to, device_id_type=MESH)

        mine = pltpu.make_async_copy(x_ref, blk(*me), local_sem)
        mine.start()
        first = [copy(0, me, sibling, src=x_ref)]
        first += [copy(1 + j, me, (*chip, c), src=x_ref) for j, chip in enumerate(others)]
        for cp in first:
            cp.start()
        passed = [copy(4 + j, (*chip, c), sibling) for j, chip in enumerate(others)]
        for j, chip in enumerate(others):
            copy(1 + j, (*chip, c), me).wait_recv()
            passed[j].start()
        copy(0, sibling, me).wait_recv()
        for j, chip in enumerate(others):
            copy(4 + j, (*chip, 1 - c), me).wait_recv()
        for cp in first + passed:
            cp.wait_send()
        mine.wait()

    vmem = pl.BlockSpec(memory_space=pltpu.VMEM)
    return pl.pallas_call(
        body, name=name, in_specs=[vmem], out_specs=vmem,
        out_shape=jax.ShapeDtypeStruct((8 * rows, n), v.dtype),
        scratch_shapes=[pltpu.SemaphoreType.DMA((7,)), pltpu.SemaphoreType.DMA((7,)), pltpu.SemaphoreType.DMA],
    )(v)


def _swap_halves(g5):
    def body(g_ref, out_ref, send_sem, recv_sem):
        x, y, c, _ = _position()
        cp = pltpu.make_async_remote_copy(src_ref=g_ref.at[:, 1 - c], dst_ref=out_ref, send_sem=send_sem,
                                          recv_sem=recv_sem, device_id=(x, y, 1 - c), device_id_type=MESH)
        cp.start()
        cp.wait()

    return pl.pallas_call(
        body, name="grad_swap_halves", in_specs=[ANY], out_specs=ANY,
        out_shape=jax.ShapeDtypeStruct((N_CHIPS, g5.shape[2], D_MODEL), g5.dtype),
        scratch_shapes=[pltpu.SemaphoreType.DMA, pltpu.SemaphoreType.DMA],
    )(g5)


def _add_halves(core, g5, got):
    def body(c_ref, a_ref, b_ref, o_ref, ob_ref):
        s = a_ref[0] + b_ref[...]
        o_ref[...] = s
        ob_ref[...] = s.astype(BF16)

    half = g5.shape[2]
    nt = half // SUM_TILE
    spec = pl.BlockSpec((1, SUM_TILE, D_MODEL), lambda s, i, c_ref: (s, i, 0))
    return pl.pallas_call(
        body, name="grad_add_halves",
        grid_spec=pltpu.PrefetchScalarGridSpec(
            num_scalar_prefetch=1, grid=(N_CHIPS, nt),
            in_specs=[pl.BlockSpec((1, 1, SUM_TILE, D_MODEL), lambda s, i, c_ref: (s, c_ref[0], i, 0)), spec],
            out_specs=[spec, spec]),
        out_shape=[jax.ShapeDtypeStruct((N_CHIPS, half, D_MODEL), F32),
                   jax.ShapeDtypeStruct((N_CHIPS, half, D_MODEL), BF16)],
        compiler_params=_cparams(("parallel", "parallel")),
    )(core, g5, got)


def _exchange_copies(p_ref, q_ref, send_sems, recv_sems):
    x, y, c, others = _position()
    return [pltpu.make_async_remote_copy(src_ref=p_ref.at[2 * ox + oy], dst_ref=q_ref.at[j],
                                         send_sem=send_sems.at[j], recv_sem=recv_sems.at[j],
                                         device_id=(ox, oy, c), device_id_type=MESH)
            for j, (ox, oy) in enumerate(others)]


EXCHANGE_SEMS = [pltpu.SemaphoreType.DMA((3,)), pltpu.SemaphoreType.DMA((3,))]


def _exchange_chips(p):
    def body(p_ref, q_ref, send_sems, recv_sems):
        copies = _exchange_copies(p_ref, q_ref, send_sems, recv_sems)
        for cp in copies:
            cp.start()
        for cp in copies:
            cp.wait()

    return pl.pallas_call(
        body, name="grad_exchange_chips", in_specs=[ANY], out_specs=ANY,
        out_shape=jax.ShapeDtypeStruct((3,) + p.shape[1:], p.dtype),
        scratch_shapes=EXCHANGE_SEMS,
    )(p)


def _add_chips(chip_core, p, q):
    def body(kc_ref, p_ref, q_ref, o_ref):
        o_ref[0] = ((p_ref[0] + q_ref[0].astype(F32)) + q_ref[1].astype(F32)) + q_ref[2].astype(F32)

    half = p.shape[1]
    nt = half // SUM_TILE
    return pl.pallas_call(
        body, name="grad_add_chips",
        grid_spec=pltpu.PrefetchScalarGridSpec(
            num_scalar_prefetch=1, grid=(nt,),
            in_specs=[pl.BlockSpec((1, SUM_TILE, D_MODEL), lambda i, kc_ref: (kc_ref[0], i, 0)),
                      pl.BlockSpec((3, SUM_TILE, D_MODEL), lambda i, kc_ref: (0, i, 0))],
            out_specs=pl.BlockSpec((1, SUM_TILE, D_MODEL), lambda i, kc_ref: (kc_ref[1], i, 0))),
        out_shape=jax.ShapeDtypeStruct((2, half, D_MODEL), F32),
        compiler_params=_cparams(("parallel",)),
    )(chip_core, p, q)


def _share_halves(halves):
    def body(h_ref, out_ref, send_sem, recv_sem):
        x, y, c, _ = _position()
        cp = pltpu.make_async_remote_copy(src_ref=h_ref.at[c], dst_ref=out_ref.at[c], send_sem=send_sem,
                                          recv_sem=recv_sem, device_id=(x, y, 1 - c), device_id_type=MESH)
        cp.start()
        pltpu.make_async_remote_copy(src_ref=h_ref.at[c], dst_ref=out_ref.at[1 - c], send_sem=send_sem,
                                     recv_sem=recv_sem, device_id=(x, y, c), device_id_type=MESH).wait_recv()
        cp.wait_send()

    return pl.pallas_call(
        body, name="grad_share_halves", in_specs=[ANY], out_specs=ANY, input_output_aliases={0: 0},
        out_shape=jax.ShapeDtypeStruct(halves.shape, halves.dtype),
        scratch_shapes=[pltpu.SemaphoreType.DMA, pltpu.SemaphoreType.DMA],
    )(halves)


def _reduce_in_chip(g_packed, core):
    rows = g_packed.shape[1]
    g5 = g_packed.reshape(N_CHIPS, 2, rows // 2, D_MODEL)
    return _add_halves(core.reshape(1), g5, _swap_halves(g5))


def _reduce_across_chips(p, q, chip, core):
    halves = _share_halves(_add_chips(jnp.stack([chip, core]), p, q))
    return halves.reshape(2 * halves.shape[1], D_MODEL)


def _reduce_scatter(g_packed, chip, core):
    p, p_bf = _reduce_in_chip(g_packed, core)
    return _reduce_across_chips(p, _exchange_chips(p_bf), chip, core)


def _slot(n):
    return -(-n // 16) * 16


def _pad_rows(a, axis):
    n = a.shape[axis]
    widths = [(0, 0)] * a.ndim
    widths[axis] = (0, _slot(n) - n)
    return jnp.pad(a, widths) if _slot(n) != n else a


W_FIRST = (("in_a", None),)
W_SECOND = (("mem_kv", 0), ("out", 0), ("up", 0), ("down", 0),
            ("in_b", None), ("mem_kv", 1), ("out", 1), ("up", 1), ("down", 1))
G_LATE = (("in_a", None), ("mem_kv", 0), ("out", 0))
G_EARLY = (("up", 0), ("down", 0), ("in_b", None), ("mem_kv", 1), ("out", 1), ("up", 1), ("down", 1))
REGION_ROWS = {W_FIRST: 896, W_SECOND: 5504, G_LATE: 1280, G_EARLY: 5120}
FULL_SHAPE = {"in_a": (D_MODEL, IN_A), "in_b": (D_MODEL, IN_B), "mem_kv": (D_MODEL, 2 * X_WIDTH),
              "out": (D_MODEL, D_MODEL), "up": (D_MODEL, D_FF), "down": (D_FF, D_MODEL)}
COLUMN_SHARDED = ("in_a", "in_b", "up")


def _shard_shape(name):
    r, c = FULL_SHAPE[name]
    return (r, c // N_CHIPS) if name in COLUMN_SHARDED else (r // N_CHIPS, c)


def _part_rows(name):
    r, c = _shard_shape(name)
    return r * c // D_MODEL


def _pack_region(region, part, dtype):
    rows = [_pad_rows(part(name, layer).reshape(-1, D_MODEL).astype(dtype), 0) for name, layer in region]
    used = sum(r.shape[0] for r in rows)
    return jnp.concatenate(rows + [jnp.zeros((REGION_ROWS[region] - used, D_MODEL), dtype)], axis=0)


def _unpack_region(region, flat):
    out, off = {}, 0
    for name, layer in region:
        n = _part_rows(name)
        out[name, layer] = flat[off:off + n].reshape(_shard_shape(name))
        off += _slot(n)
    return out


def _unpack_region_full(region, g):
    out, off = {}, 0
    for name, layer in region:
        n = _part_rows(name)
        piece = g[:, off:off + n].reshape((N_CHIPS,) + _shard_shape(name))
        if name in COLUMN_SHARDED:
            piece = piece.transpose(1, 0, 2)
        out[name, layer] = piece.reshape(FULL_SHAPE[name])
        off += _slot(n)
    return out


def _pack_region_full(region, full):
    s = N_CHIPS
    parts = []
    for name, layer in region:
        g = full[name, layer]
        if name in COLUMN_SHARDED:
            g = g.reshape(g.shape[0], s, -1).transpose(1, 0, 2)
        parts.append(_pad_rows(g.reshape(s, -1, D_MODEL), 1))
    used = sum(p.shape[1] for p in parts)
    return jnp.concatenate(parts + [jnp.zeros((s, REGION_ROWS[region] - used, D_MODEL), F32)], axis=1)


def _as_operands(full):
    out = dict(full)
    if ("in_a", None) in out:
        out["in_a", None] = _pad_in_a(out["in_a", None])
    if ("in_b", None) in out:
        out["in_b", None] = _qkv_to_pairs(out["in_b", None])
    return out


def _place(packed, chip):
    rows = packed.shape[0]
    slabs = lax.dynamic_update_slice(jnp.zeros((N_CHIPS, rows, D_MODEL), packed.dtype), packed[None], (chip, 0, 0))
    return slabs.reshape(N_CHIPS, 2, rows // 2, D_MODEL)


def _adamw_math(w, g, m, v):
    m = ADAM_B1 * m + (1.0 - ADAM_B1) * g
    v = ADAM_B2 * v + (1.0 - ADAM_B2) * (g * g)
    m_hat = m / (1.0 - ADAM_B1 ** ADAM_STEP)
    v_hat = v / (1.0 - ADAM_B2 ** ADAM_STEP)
    delta = -ADAM_LR * (m_hat / (jnp.sqrt(v_hat) + ADAM_EPS) + ADAM_WD * w)
    return delta, m, v


ADAM_TILE = 256


def _adamw(w, g, m, v, *, name):
    shape = w.shape
    cols = shape[-1]
    rows = w.size // cols
    tile = min(rows, ADAM_TILE)
    assert rows % tile == 0, (name, shape)

    def body(w_ref, g_ref, m_ref, v_ref, d_ref, nm_ref, nv_ref):
        d_ref[...], nm_ref[...], nv_ref[...] = _adamw_math(w_ref[...], g_ref[...], m_ref[...], v_ref[...])

    spec = pl.BlockSpec((tile, cols), lambda i: (i, 0))
    outs = pl.pallas_call(
        body, name=name, grid=(rows // tile,), in_specs=[spec] * 4, out_specs=[spec] * 3,
        out_shape=[jax.ShapeDtypeStruct((rows, cols), F32)] * 3,
        compiler_params=_cparams(("parallel",)),
    )(*[a.reshape(rows, cols) for a in (w, g, m, v)])
    return [o.reshape(shape) for o in outs]


SMALL_NAMES = (("mem_norm", 8), ("norm_pre_mix", 16), ("norm_post_mix", 16), ("norm_pre_mlp", 16),
               ("norm_post_mlp", 16), ("a_log_a", 1), ("dt_bias_a", 1), ("onorm_a", 1))
SMALL_ROWS = 80
CONV_ROWS = CONV_K * 3 * MIX_W // 128
SMALL_GRAD_ROWS = SMALL_ROWS + CONV_ROWS


def _pack_small(vals):
    rows = []
    for name, n in SMALL_NAMES:
        flat = vals[name].reshape(-1)
        rows.append(jnp.pad(flat, (0, n * 128 - flat.size)).reshape(n, 128))
    used = sum(n for _, n in SMALL_NAMES)
    return jnp.concatenate(rows + [jnp.zeros((SMALL_ROWS - used, 128), F32)], axis=0)


def _unpack_small(packed, like):
    out, off = {}, 0
    for name, n in SMALL_NAMES:
        size = like[name].size
        out[name] = packed[off:off + n].reshape(-1)[:size].reshape(like[name].shape)
        off += n
    return out


def _small_update(gathered, w, m, v):
    def body(g_ref, w_ref, m_ref, v_ref, gs_ref, d_ref, nm_ref, nv_ref):
        g = g_ref[0]
        for dev in range(1, 8):
            g = g + g_ref[dev]
        gs_ref[...] = g
        d_ref[...], nm_ref[...], nv_ref[...] = _adamw_math(w_ref[...], g[:SMALL_ROWS], m_ref[...], v_ref[...])

    small = jax.ShapeDtypeStruct((SMALL_ROWS, 128), F32)
    return pl.pallas_call(
        body, name="small_update",
        out_shape=[jax.ShapeDtypeStruct((SMALL_GRAD_ROWS, 128), F32), small, small, small],
    )(gathered.reshape(8, SMALL_GRAD_ROWS, 128), w, m, v)


def kernel(x, mem, mem_norm, norm_pre_mix, norm_post_mix, norm_pre_mlp, norm_post_mlp, w_in_a, conv_w_a, a_log_a, dt_bias_a, onorm_a, w_in_b, w_mem_kv, w_out, w_up, w_down, loss_target, m_mem_norm, m_norm_pre_mix, m_norm_post_mix, m_norm_pre_mlp, m_norm_post_mlp, m_w_in_a, m_conv_w_a, m_a_log_a, m_dt_bias_a, m_onorm_a, m_w_in_b, m_w_mem_kv, m_w_out, m_w_up, m_w_down, v_mem_norm, v_norm_pre_mix, v_norm_post_mix, v_norm_pre_mlp, v_norm_post_mlp, v_w_in_a, v_conv_w_a, v_a_log_a, v_dt_bias_a, v_onorm_a, v_w_in_b, v_w_mem_kv, v_w_out, v_w_up, v_w_down):
    nb = x.shape[0]
    chip = (2 * lax.axis_index("x") + lax.axis_index("y")).astype(jnp.int32)
    core = lax.axis_index("c").astype(jnp.int32)
    shards = {"in_a": w_in_a, "in_b": w_in_b, "mem_kv": w_mem_kv, "out": w_out, "up": w_up, "down": w_down}
    moments_m = {"in_a": m_w_in_a, "in_b": m_w_in_b, "mem_kv": m_w_mem_kv, "out": m_w_out, "up": m_w_up, "down": m_w_down}
    moments_v = {"in_a": v_w_in_a, "in_b": v_w_in_b, "mem_kv": v_w_mem_kv, "out": v_w_out, "up": v_w_up, "down": v_w_down}
    small_w = {"mem_norm": mem_norm, "norm_pre_mix": norm_pre_mix, "norm_post_mix": norm_post_mix,
               "norm_pre_mlp": norm_pre_mlp, "norm_post_mlp": norm_post_mlp, "a_log_a": a_log_a,
               "dt_bias_a": dt_bias_a, "onorm_a": onorm_a}
    small_m = {"mem_norm": m_mem_norm, "norm_pre_mix": m_norm_pre_mix, "norm_post_mix": m_norm_post_mix,
               "norm_pre_mlp": m_norm_pre_mlp, "norm_post_mlp": m_norm_post_mlp, "a_log_a": m_a_log_a,
               "dt_bias_a": m_dt_bias_a, "onorm_a": m_onorm_a}
    small_v = {"mem_norm": v_mem_norm, "norm_pre_mix": v_norm_pre_mix, "norm_post_mix": v_norm_post_mix,
               "norm_pre_mlp": v_norm_pre_mlp, "norm_post_mlp": v_norm_post_mlp, "a_log_a": v_a_log_a,
               "dt_bias_a": v_dt_bias_a, "onorm_a": v_onorm_a}

    def shard_part(name, layer):
        return shards[name][0 if layer is None else layer]

    first = _gather_chips(_place(_pack_region(W_FIRST, shard_part, BF16), chip))
    wts = _as_operands(_unpack_region_full(W_FIRST, first.reshape(N_CHIPS, REGION_ROWS[W_FIRST], D_MODEL)))
    comm = {"slabs": _place(_pack_region(W_SECOND, shard_part, BF16), chip), "core": core}
    conv_rows = CONV_ROWS // N_CHIPS
    conv_blk = jnp.pad(conv_w_a.reshape(conv_rows, 128), ((0, 24 - conv_rows), (0, 0)))
    conv_all = _gather_all(conv_blk, name="gather_conv").reshape(N_CHIPS, 2, 24, 128)[:, 0, :conv_rows]
    conv_full = conv_all.reshape(N_CHIPS, CONV_K, 3 * MIX_W // N_CHIPS).transpose(1, 0, 2).reshape(CONV_K, 3 * MIX_W)

    loss_local, dx, gw, gsmall, (own_early, others_early) = _local_step(
        x.reshape(nb * SEQ, D_MODEL), mem.reshape(nb * N_MEM, D_MODEL), loss_target.reshape(nb * SEQ, D_MODEL),
        wts, dict(small_w, conv_w=conv_full), comm)
    loss = lax.psum(loss_local, ("x", "y", "c"))
    grad_x = dx.reshape(nb, SEQ, D_MODEL)

    g_part = _unpack_region(G_EARLY, _reduce_across_chips(own_early, others_early, chip, core))
    g_part.update(_unpack_region(G_LATE, _reduce_scatter(_pack_region_full(G_LATE, gw), chip, core)))
    g_shard = {k: (g_part[k, None][None] if (k, None) in g_part else jnp.stack([g_part[k, i] for i in range(DEPTH)]))
               for k in shards}
    upd = {k: _adamw(shards[k], g_shard[k], moments_m[k], moments_v[k], name=f"adamw_{k}") for k in shards}

    g_rows = jnp.concatenate([_pack_small(gsmall), gsmall["conv_w"].reshape(CONV_ROWS, 128)], axis=0)
    g_all = _gather_all(g_rows, name="gather_small_grads")
    g_sum, d_small, nm_small, nv_small = _small_update(g_all, _pack_small(small_w), _pack_small(small_m), _pack_small(small_v))
    gs = _unpack_small(g_sum, small_w)
    ds, nms, nvs = (_unpack_small(p, small_w) for p in (d_small, nm_small, nv_small))
    cw = 3 * MIX_W // N_CHIPS
    g_conv = lax.dynamic_slice(g_sum[SMALL_ROWS:].reshape(CONV_K, 3 * MIX_W), (0, chip * cw), (CONV_K, cw)).reshape(conv_w_a.shape)
    d_conv, nm_conv, nv_conv = _adamw(conv_w_a, g_conv, m_conv_w_a, v_conv_w_a, name="adamw_conv")

    order = ("mem_norm", "norm_pre_mix", "norm_post_mix", "norm_pre_mlp", "norm_post_mlp", "in_a", "conv", "a_log_a",
             "dt_bias_a", "onorm_a", "in_b", "mem_kv", "out", "up", "down")
    grads = dict(gs, conv=g_conv, **g_shard)
    deltas = dict(ds, conv=d_conv, **{k: u[0] for k, u in upd.items()})
    new_m = dict(nms, conv=nm_conv, **{k: u[1] for k, u in upd.items()})
    new_v = dict(nvs, conv=nv_conv, **{k: u[2] for k, u in upd.items()})
    return (loss, grad_x, *[grads[k] for k in order], *[deltas[k] for k in order],
            *[new_m[k] for k in order], *[new_v[k] for k in order])
```

```python
import functools

import jax
import jax.numpy as jnp
from jax import lax
from jax.experimental import pallas as pl
from jax.experimental.pallas import tpu as pltpu

F32 = jnp.float32
BF16 = jnp.bfloat16
HIGHEST = lax.Precision.HIGHEST
MESH = pl.DeviceIdType.MESH

D_MODEL = 1024
SEQ = 2048
DEPTH = 2
X_WIDTH = 256
N_X_HEADS = 4
X_HEAD_DIM = 64
MIX_W = 768
LIN_DH = 128
N_LIN = 6
CONV_K = 4
CHUNK = 64
SB_DH = 64
SB_PAIRS = 6
N_MEM = 256
D_FF = 4096
EPS = 1e-6
IN_A = 3340
IN_A_PAD = 3456
IN_B = 2560
SMALL_COL = 26
N_CHIPS = 4

ADAM_LR, ADAM_B1, ADAM_B2, ADAM_EPS, ADAM_WD, ADAM_STEP = 0.001, 0.9, 0.999, 1e-08, 0.01, 10

VMEM_LIMIT = 48 * 1024 * 1024

ANY = pl.BlockSpec(memory_space=pl.ANY)

NN = (((1,), (0,)), ((), ()))
NT = (((1,), (1,)), ((), ()))
TN = (((0,), (0,)), ((), ()))


def _cparams(sem):
    return pltpu.CompilerParams(dimension_semantics=sem, vmem_limit_bytes=VMEM_LIMIT)


def _dotbf(a, b, dn=NN):
    return lax.dot_general(a.astype(BF16), b.astype(BF16), dn, preferred_element_type=F32)


def _split(a):
    hi = a.astype(BF16)
    lo = (a - hi.astype(F32)).astype(BF16)
    return hi, lo


def _dot3(a, b, dn=NN):
    ah, al = _split(a)
    bh, bl = _split(b)
    d = functools.partial(lax.dot_general, dimension_numbers=dn, preferred_element_type=F32)
    return d(ah, bh) + (d(ah, bl) + d(al, bh))


def _dot_mask(parts, m01):
    d = functools.partial(lax.dot_general, dimension_numbers=NN, preferred_element_type=F32)
    return d(parts[0], m01) + d(parts[1], m01)


def _iota(shape, dim):
    return lax.broadcasted_iota(jnp.int32, shape, dim)


def _softplus(x):
    return jnp.maximum(x, 0.0) + jnp.log(1.0 + jnp.exp(-jnp.abs(x)))


def _log_sigmoid(z):
    return jnp.minimum(z, 0.0) - jnp.log(1.0 + jnp.exp(-jnp.abs(z)))


def _rms(x, g):
    r = lax.rsqrt(jnp.mean(x * x, axis=-1, keepdims=True) + EPS)
    return (x * r) * g


def _matmul(a, b, *, mode, tm, tn, tk, name, out_dtypes=(F32,), epilogue=None, extras=(), n_outer=False):
    if n_outer:
        ix = lambda f: (lambda j, i, kk: f(i, j, kk))
    else:
        ix = lambda f: f
    if mode == "nn":
        (m, k), (k2, n) = a.shape, b.shape
        a_spec = pl.BlockSpec((tm, tk), ix(lambda i, j, kk: (i, kk)))
        b_spec = pl.BlockSpec((tk, tn), ix(lambda i, j, kk: (kk, j)))
        dn = NN
    elif mode == "nt":
        (m, k), (n, k2) = a.shape, b.shape
        a_spec = pl.BlockSpec((tm, tk), ix(lambda i, j, kk: (i, kk)))
        b_spec = pl.BlockSpec((tn, tk), ix(lambda i, j, kk: (j, kk)))
        dn = NT
    else:
        (k, m), (k2, n) = a.shape, b.shape
        a_spec = pl.BlockSpec((tk, tm), ix(lambda i, j, kk: (kk, i)))
        b_spec = pl.BlockSpec((tk, tn), ix(lambda i, j, kk: (kk, j)))
        dn = TN
    assert k == k2 and m % tm == 0 and n % tn == 0 and k % tk == 0, (name, a.shape, b.shape)
    assert a.dtype == BF16 and b.dtype == BF16, name
    nk = k // tk
    n_extra, n_out = len(extras), len(out_dtypes)

    def finish(acc, extra_refs, out_refs):
        outs = (acc,) if epilogue is None else epilogue(acc, *[r[...] for r in extra_refs])
        for o_ref, o in zip(out_refs, outs):
            o_ref[...] = o.astype(o_ref.dtype)

    def body_single(a_ref, b_ref, *rest):
        acc = lax.dot_general(a_ref[...], b_ref[...], dn, preferred_element_type=F32)
        finish(acc, rest[:n_extra], rest[n_extra:n_extra + n_out])

    def body_tiled(a_ref, b_ref, *rest):
        extra_refs, out_refs, acc_ref = rest[:n_extra], rest[n_extra:n_extra + n_out], rest[-1]
        kk = pl.program_id(2)

        @pl.when(kk == 0)
        def _():
            acc_ref[...] = jnp.zeros_like(acc_ref)

        acc_ref[...] += lax.dot_general(a_ref[...], b_ref[...], dn, preferred_element_type=F32)

        @pl.when(kk == nk - 1)
        def _():
            finish(acc_ref[...], extra_refs, out_refs)

    mn_spec = pl.BlockSpec((tm, tn), ix(lambda i, j, kk: (i, j)))
    grid = (n // tn, m // tm, nk) if n_outer else (m // tm, n // tn, nk)
    outs = pl.pallas_call(
        body_single if nk == 1 else body_tiled,
        name=name,
        grid=grid,
        in_specs=[a_spec, b_spec] + [mn_spec] * n_extra,
        out_specs=[mn_spec] * n_out,
        out_shape=[jax.ShapeDtypeStruct((m, n), dt) for dt in out_dtypes],
        scratch_shapes=[] if nk == 1 else [pltpu.VMEM((tm, tn), F32)],
        compiler_params=_cparams(("parallel", "parallel", "arbitrary")),
    )(a, b, *extras)
    return outs[0] if n_out == 1 else outs


ROW_TILE = 512


def _row_spec(width=D_MODEL, tile=ROW_TILE):
    return pl.BlockSpec((tile, width), lambda i: (i, 0))


def _vec_spec(width=D_MODEL):
    return pl.BlockSpec((1, width), lambda i: (0, 0))


def _rms_fwd(x, g, *, name, tile=ROW_TILE):
    t = x.shape[0]

    def body(x_ref, g_ref, h_ref):
        h_ref[...] = _rms(x_ref[...], g_ref[...]).astype(BF16)

    return pl.pallas_call(
        body, name=name, grid=(t // tile,),
        in_specs=[_row_spec(tile=tile), _vec_spec()], out_specs=_row_spec(tile=tile),
        out_shape=jax.ShapeDtypeStruct((t, D_MODEL), BF16),
        compiler_params=_cparams(("parallel",)),
    )(x, g.reshape(1, D_MODEL))


def _post_norm_add(xres, y, g_post, g_next, *, name):
    t = xres.shape[0]

    def body(x_ref, y_ref, gp_ref, gn_ref, xo_ref, h_ref):
        xo = x_ref[...] + _rms(y_ref[...], gp_ref[...])
        xo_ref[...] = xo
        h_ref[...] = _rms(xo, gn_ref[...]).astype(BF16)

    return pl.pallas_call(
        body, name=name, grid=(t // ROW_TILE,),
        in_specs=[_row_spec(), _row_spec(), _vec_spec(), _vec_spec()],
        out_specs=[_row_spec(), _row_spec()],
        out_shape=[jax.ShapeDtypeStruct((t, D_MODEL), F32), jax.ShapeDtypeStruct((t, D_MODEL), BF16)],
        compiler_params=_cparams(("parallel",)),
    )(xres, y, g_post.reshape(1, D_MODEL), g_next.reshape(1, D_MODEL))


def _post_norm_loss(xres, y, g_post, target, *, name):
    t = xres.shape[0]

    def body(x_ref, y_ref, gp_ref, t_ref, loss_ref, dx_ref):
        @pl.when(pl.program_id(0) == 0)
        def _():
            loss_ref[...] = jnp.zeros_like(loss_ref)

        err = (x_ref[...] + _rms(y_ref[...], gp_ref[...])) - t_ref[...]
        per_tok = jnp.mean(err * err, axis=-1, keepdims=True)
        loss_ref[...] += 0.5 * jnp.sum(per_tok, axis=0, keepdims=True)
        dx_ref[...] = err * (1.0 / D_MODEL)

    return pl.pallas_call(
        body, name=name, grid=(t // ROW_TILE,),
        in_specs=[_row_spec(), _row_spec(), _vec_spec(), _row_spec()],
        out_specs=[pl.BlockSpec((1, 128), lambda i: (0, 0)), _row_spec()],
        out_shape=[jax.ShapeDtypeStruct((1, 128), F32), jax.ShapeDtypeStruct((t, D_MODEL), F32)],
        compiler_params=_cparams(("arbitrary",)),
    )(xres, y, g_post.reshape(1, D_MODEL), target)


def _rms_bwd(dy, x, g, *, name, res=None, out_dtype=F32, tile=ROW_TILE):
    t = x.shape[0]
    has_res = res is not None

    def body(dy_ref, x_ref, g_ref, *rest):
        res_ref = rest[0] if has_res else None
        dx_ref, dg_ref = rest[-2], rest[-1]

        @pl.when(pl.program_id(0) == 0)
        def _():
            dg_ref[...] = jnp.zeros_like(dg_ref)

        xf = x_ref[...]
        dyf = dy_ref[...].astype(F32)
        r = lax.rsqrt(jnp.mean(xf * xf, axis=-1, keepdims=True) + EPS)
        xhat = xf * r
        dg_ref[...] += jnp.sum(dyf * xhat, axis=0, keepdims=True)
        dxh = dyf * g_ref[...]
        dx = r * (dxh - xhat * jnp.mean(dxh * xhat, axis=-1, keepdims=True))
        if has_res:
            dx = dx + res_ref[...]
        dx_ref[...] = dx.astype(dx_ref.dtype)

    args = [dy, x, g.reshape(1, D_MODEL)] + ([res] if has_res else [])
    return pl.pallas_call(
        body, name=name, grid=(t // tile,),
        in_specs=[_row_spec(tile=tile), _row_spec(tile=tile), _vec_spec()] + ([_row_spec(tile=tile)] if has_res else []),
        out_specs=[_row_spec(tile=tile), _vec_spec()],
        out_shape=[jax.ShapeDtypeStruct((t, D_MODEL), out_dtype), jax.ShapeDtypeStruct((1, D_MODEL), F32)],
        compiler_params=_cparams(("arbitrary",)),
    )(*args)


def _rms_bwd_pair(dh, x, g_pre, res, y, g_post, *, name):
    t = x.shape[0]

    def norm_bwd(dy, xf, g):
        r = lax.rsqrt(jnp.mean(xf * xf, axis=-1, keepdims=True) + EPS)
        xhat = xf * r
        dxh = dy * g
        dx = r * (dxh - xhat * jnp.mean(dxh * xhat, axis=-1, keepdims=True))
        return dx, jnp.sum(dy * xhat, axis=0, keepdims=True)

    def body(dh_ref, x_ref, gp_ref, res_ref, y_ref, gq_ref, dx_ref, dy_ref, dgp_ref, dgq_ref):
        @pl.when(pl.program_id(0) == 0)
        def _():
            dgp_ref[...] = jnp.zeros_like(dgp_ref)
            dgq_ref[...] = jnp.zeros_like(dgq_ref)

        dx, dgp = norm_bwd(dh_ref[...], x_ref[...], gp_ref[...])
        dx = dx + res_ref[...]
        dx_ref[...] = dx
        dy, dgq = norm_bwd(dx, y_ref[...], gq_ref[...])
        dy_ref[...] = dy.astype(BF16)
        dgp_ref[...] += dgp
        dgq_ref[...] += dgq

    return pl.pallas_call(
        body, name=name, grid=(t // ROW_TILE,),
        in_specs=[_row_spec(), _row_spec(), _vec_spec(), _row_spec(), _row_spec(), _vec_spec()],
        out_specs=[_row_spec(), _row_spec(), _vec_spec(), _vec_spec()],
        out_shape=[jax.ShapeDtypeStruct((t, D_MODEL), F32), jax.ShapeDtypeStruct((t, D_MODEL), BF16),
                   jax.ShapeDtypeStruct((1, D_MODEL), F32), jax.ShapeDtypeStruct((1, D_MODEL), F32)],
        compiler_params=_cparams(("arbitrary",)),
    )(dh, x, g_pre.reshape(1, D_MODEL), res, y, g_post.reshape(1, D_MODEL))


CONV_COLS = 256
N_CONV_BLOCKS = 3 * MIX_W // CONV_COLS
CONV_STRIP = 128


def _shift_down(x, k):
    if k == 0:
        return x
    return jnp.where(_iota(x.shape, 0) >= k, pltpu.roll(x, k, 0), 0.0)


def _shift_up(x, k):
    if k == 0:
        return x
    s = x.shape[0]
    return jnp.where(_iota(x.shape, 0) < s - k, pltpu.roll(x, s - k, 0), 0.0)


def _conv_pre(x, w_ref):
    c = w_ref[CONV_K - 1:CONV_K, :] * x
    for i in range(CONV_K - 1):
        c = c + w_ref[i:i + 1, :] * _shift_down(x, CONV_K - 1 - i)
    return c


def _conv_silu_fwd(proj, conv_w, n_batch):
    def body(x_ref, w_ref, y_ref):
        c = _conv_pre(x_ref[...], w_ref)
        y_ref[...] = c * jax.nn.sigmoid(c)

    return pl.pallas_call(
        body, name="conv_silu_fwd", grid=(n_batch, N_CONV_BLOCKS),
        in_specs=[pl.BlockSpec((SEQ, CONV_COLS), lambda b, j: (b, j)),
                  pl.BlockSpec((CONV_K, CONV_COLS), lambda b, j: (0, j))],
        out_specs=pl.BlockSpec((SEQ, CONV_COLS), lambda b, j: (b, j)),
        out_shape=jax.ShapeDtypeStruct((n_batch * SEQ, 3 * MIX_W), F32),
        compiler_params=_cparams(("parallel", "parallel")),
    )(proj, conv_w)


def _conv_silu_bwd(dy, proj, conv_w, dproj, n_batch):
    strip, halo = CONV_STRIP, 8
    n_strips = SEQ // strip

    def body(dy_ref, x_ref, w_ref, _, dx_ref, dw_ref, xpad, dcpad):
        @pl.when(pl.program_id(1) == 0)
        def _():
            dw_ref[...] = jnp.zeros_like(dw_ref)

        xpad[0:halo, :] = jnp.zeros((halo, CONV_COLS), F32)
        xpad[halo:, :] = x_ref[...]
        dcpad[SEQ:, :] = jnp.zeros((halo, CONV_COLS), F32)
        taps = [w_ref[i:i + 1, :] for i in range(CONV_K)]

        def first(s, dw):
            a = pl.multiple_of(s * strip, strip)
            win = xpad[pl.ds(a, strip + halo), :]
            xs = [(win if i == CONV_K - 1 else pltpu.roll(win, CONV_K - 1 - i, 0))[halo:] for i in range(CONV_K)]
            c = taps[0] * xs[0]
            for i in range(1, CONV_K):
                c = c + taps[i] * xs[i]
            sig = jax.nn.sigmoid(c)
            dc = dy_ref[pl.ds(a, strip), :] * (sig * (1.0 + c * (1.0 - sig)))
            dcpad[pl.ds(a, strip), :] = dc
            return tuple(dw[i] + jnp.sum(dc * xs[i], axis=0, keepdims=True) for i in range(CONV_K))

        dw = lax.fori_loop(0, n_strips, first, tuple(jnp.zeros((1, CONV_COLS), F32) for _ in range(CONV_K)))
        for i in range(CONV_K):
            dw_ref[i:i + 1, :] += dw[i]

        def second(s, carry):
            a = pl.multiple_of(s * strip, strip)
            win = dcpad[pl.ds(a, strip + halo), :]
            dx = taps[CONV_K - 1] * win[:strip]
            for i in range(CONV_K - 1):
                dx = dx + taps[i] * pltpu.roll(win, strip + halo - (CONV_K - 1 - i), 0)[:strip]
            dx_ref[pl.ds(a, strip), :] = dx.astype(BF16)
            return carry

        lax.fori_loop(0, n_strips, second, 0)

    return pl.pallas_call(
        body, name="conv_silu_bwd", grid=(N_CONV_BLOCKS, n_batch),
        in_specs=[pl.BlockSpec((SEQ, CONV_COLS), lambda j, b: (b, j)),
                  pl.BlockSpec((SEQ, CONV_COLS), lambda j, b: (b, j)),
                  pl.BlockSpec((CONV_K, CONV_COLS), lambda j, b: (0, j)), ANY],
        out_specs=[pl.BlockSpec((SEQ, CONV_COLS), lambda j, b: (b, j)),
                   pl.BlockSpec((CONV_K, CONV_COLS), lambda j, b: (0, j))],
        out_shape=[jax.ShapeDtypeStruct(dproj.shape, BF16),
                   jax.ShapeDtypeStruct((CONV_K, 3 * MIX_W), F32)],
        input_output_aliases={3: 0},
        scratch_shapes=[pltpu.VMEM((SEQ + 8, CONV_COLS), F32), pltpu.VMEM((SEQ + 8, CONV_COLS), F32)],
        compiler_params=_cparams(("parallel", "arbitrary")),
    )(dy, proj, conv_w, dproj)


@jax.custom_vjp
def _solve_apply(low, rhs, tinv):
    return _dot3(tinv, rhs)


def _solve_apply_fwd(low, rhs, tinv):
    sol = _dot3(tinv, rhs)
    return sol, (tinv, sol)


def _solve_apply_bwd(resid, g):
    tinv, sol = resid
    y = _dotbf(tinv, g, TN)
    return -_dotbf(y, sol, NT), y, jnp.zeros_like(tinv)


_solve_apply.defvjp(_solve_apply_fwd, _solve_apply_bwd)


def _inv_unit_lower(lows):
    c = lows[0].shape[0]
    eye = (_iota((c, c), 0) == _iota((c, c), 1)).astype(F32)
    ms = [-low for low in lows]
    ps = [eye + m for m in ms]
    for _ in range(5):
        ms = [_dot3(m, m) for m in ms]
        ps = [p + _dot3(p, m) for p, m in zip(ps, ms)]
    return ps


def _gdn_chunk(qs, ks, vs, gates, states, small, alog_row, dtb_row, gain_row, tinvs):
    c = small.shape[0]
    heads = range(N_LIN)
    lane = _iota((c, 128), 1)
    row, col = _iota((c, c), 0), _iota((c, c), 1)
    causal, strict = row >= col, row > col
    last = _iota((c, 1), 0) == c - 1

    beta_all = jax.nn.sigmoid(small)
    g_all = -jnp.exp(alog_row) * _softplus(small + dtb_row)
    ltri = (col <= row).astype(F32)
    gc_all = lax.dot_general(ltri, g_all, NN, precision=HIGHEST, preferred_element_type=F32)

    beta = [jnp.sum(jnp.where(lane == h, beta_all, 0.0), axis=1, keepdims=True) for h in heads]
    gc = [jnp.sum(jnp.where(lane == N_LIN + h, gc_all, 0.0), axis=1, keepdims=True) for h in heads]
    gc_j = [lax.dot_general((lane == N_LIN + h).astype(F32), gc_all, NT, precision=HIGHEST,
                            preferred_element_type=F32) for h in heads]
    decay = [jnp.where(causal, jnp.exp(jnp.where(causal, gc[h] - gc_j[h], 0.0)), 0.0) for h in heads]
    gc_last = [jnp.sum(jnp.where(last, gc[h], 0.0), axis=0, keepdims=True) for h in heads]
    egc = [jnp.exp(g) for g in gc]
    qn = [q * lax.rsqrt(jnp.sum(q * q, axis=-1, keepdims=True) + EPS) * (LIN_DH ** -0.5) for q in qs]
    kn = [k * lax.rsqrt(jnp.sum(k * k, axis=-1, keepdims=True) + EPS) for k in ks]
    kb = [kn[h] * beta[h] for h in heads]
    low = [jnp.where(strict, _dotbf(kb[h], kn[h], NT) * decay[h], 0.0) for h in heads]
    if tinvs is None:
        tinvs = _inv_unit_lower(low)
    u = [_solve_apply(low[h], vs[h] * beta[h], tinvs[h]) for h in heads]
    w = [_solve_apply(low[h], kb[h] * egc[h], tinvs[h]) for h in heads]
    intra = [_dotbf(qn[h], kn[h], NT) * decay[h] for h in heads]
    v_new = [u[h] - _dotbf(w[h], states[h]) for h in heads]
    o = [_dotbf(qn[h] * egc[h], states[h]) + _dotbf(intra[h], v_new[h]) for h in heads]
    new_states = [states[h] * jnp.exp(gc_last[h]) + _dotbf(kn[h] * jnp.exp(gc_last[h] - gc[h]), v_new[h], TN)
                  for h in heads]
    o = [x * lax.rsqrt(jnp.mean(x * x, axis=-1, keepdims=True) + EPS) * gain_row for x in o]
    outs = [o[h] * (gates[h] * jax.nn.sigmoid(gates[h])) for h in heads]
    return outs, new_states, tinvs


def _gdn_param_rows(a_log, dt_bias, onorm):
    row = lambda v: jnp.pad(v.reshape(1, N_LIN), ((0, 0), (N_LIN, 128 - 2 * N_LIN)))
    return row(a_log), row(dt_bias), onorm.reshape(1, LIN_DH)


def _head(ref_or_val, h):
    return ref_or_val[:, LIN_DH * h:LIN_DH * (h + 1)]


def _gdn_fwd(qkv, proj, alog_row, dtb_row, gain_row, n_batch, gather=None):
    nc = SEQ // CHUNK
    t = n_batch * SEQ
    steps = n_batch * nc

    def body(qkv_ref, small_ref, gate_ref, al_ref, dt_ref, gn_ref, *rest):
        if gather is None:
            mix_ref, st_ref, ti_ref, s_scr = rest
        else:
            w_ref, mix_ref, st_ref, ti_ref, out_ref, s_scr, send_sems, recv_sems = rest
            step = pl.program_id(0) * nc + pl.program_id(1)
            for at, phase in ((0, "start"), (3 * steps // 4, "forward"), (steps - 1, "finish")):
                @pl.when(step == at)
                def _(phase=phase):
                    getattr(_Gather(w_ref, out_ref, send_sems, recv_sems), phase)()

        @pl.when(pl.program_id(1) == 0)
        def _():
            s_scr[...] = jnp.zeros_like(s_scr)

        heads = range(N_LIN)
        states = [s_scr[h] for h in heads]
        outs, new_states, tinvs = _gdn_chunk(
            [_head(qkv_ref, h) for h in heads], [_head(qkv_ref, N_LIN + h) for h in heads],
            [_head(qkv_ref, 2 * N_LIN + h) for h in heads], [_head(gate_ref, h) for h in heads],
            states, small_ref[...], al_ref[...], dt_ref[...], gn_ref[...], None)
        mix_ref[...] = jnp.concatenate(outs, axis=1).astype(BF16)
        for h in heads:
            st_ref[0, 0, h] = states[h]
            s_scr[h] = new_states[h]
            ti_ref[0, 0, h] = tinvs[h]

    row = lambda b, n: b * nc + n
    vec = pl.BlockSpec((1, 128), lambda b, n: (0, 0))
    extra = [] if gather is None else [gather]
    return pl.pallas_call(
        body, name="gdn_fwd", grid=(n_batch, nc),
        in_specs=[pl.BlockSpec((CHUNK, 3 * MIX_W), lambda b, n: (row(b, n), 0)),
                  pl.BlockSpec((CHUNK, 128), lambda b, n: (row(b, n), SMALL_COL)),
                  pl.BlockSpec((CHUNK, MIX_W), lambda b, n: (row(b, n), 3)),
                  vec, vec, vec] + [ANY] * len(extra),
        out_specs=[pl.BlockSpec((CHUNK, MIX_W), lambda b, n: (row(b, n), 0)),
                   pl.BlockSpec((1, 1, N_LIN, LIN_DH, LIN_DH), lambda b, n: (b, n, 0, 0, 0)),
                   pl.BlockSpec((1, 1, N_LIN, CHUNK, CHUNK), lambda b, n: (b, n, 0, 0, 0))] + [ANY] * len(extra),
        out_shape=[jax.ShapeDtypeStruct((t, D_MODEL), BF16),
                   jax.ShapeDtypeStruct((n_batch, nc, N_LIN, LIN_DH, LIN_DH), F32),
                   jax.ShapeDtypeStruct((n_batch, nc, N_LIN, CHUNK, CHUNK), F32)]
                  + [jax.ShapeDtypeStruct(g.shape, g.dtype) for g in extra],
        input_output_aliases={6: 3} if extra else {},
        scratch_shapes=[pltpu.VMEM((N_LIN, LIN_DH, LIN_DH), F32)] + (GATHER_SEMS if extra else []),
        compiler_params=_cparams(("arbitrary", "arbitrary")),
    )(qkv, proj, proj, alog_row, dtb_row, gain_row, *extra)


def _gdn_bwd(dcat, qkv, proj, states, tinvs, alog_row, dtb_row, gain_row, n_batch, exchange=None):
    nc = SEQ // CHUNK
    t = n_batch * SEQ
    steps = n_batch * nc

    def body(dmix_ref, qkv_ref, small_ref, gate_ref, st_ref, ti_ref, al_ref, dt_ref, gn_ref, *rest):
        if exchange is None:
            dqkv_ref, dgate_ref, dsmall_ref, dal_ref, ddt_ref, dgn_ref, ds_scr = rest
        else:
            p_ref, dqkv_ref, dgate_ref, dsmall_ref, dal_ref, ddt_ref, dgn_ref, q_ref, ds_scr, send_sems, recv_sems = rest
            step = pl.program_id(0) * nc + pl.program_id(1)

            @pl.when(step == 0)
            def _():
                for cp in _exchange_copies(p_ref, q_ref, send_sems, recv_sems):
                    cp.start()

            @pl.when(step == steps - 1)
            def _():
                for cp in _exchange_copies(p_ref, q_ref, send_sems, recv_sems):
                    cp.wait()

        @pl.when(pl.program_id(1) == 0)
        def _():
            ds_scr[...] = jnp.zeros_like(ds_scr)

        @pl.when((pl.program_id(0) == 0) & (pl.program_id(1) == 0))
        def _():
            dal_ref[...] = jnp.zeros_like(dal_ref)
            ddt_ref[...] = jnp.zeros_like(ddt_ref)
            dgn_ref[...] = jnp.zeros_like(dgn_ref)

        heads = range(N_LIN)
        tinvs = [ti_ref[0, 0, h] for h in heads]

        def chunk(qs, ks, vs, gates, states_in, small, al, dt, gn):
            outs, new_states, _ = _gdn_chunk(qs, ks, vs, gates, states_in, small, al, dt, gn, tinvs)
            return tuple(outs), tuple(new_states)

        prim = (tuple(_head(qkv_ref, h) for h in heads),
                tuple(_head(qkv_ref, N_LIN + h) for h in heads),
                tuple(_head(qkv_ref, 2 * N_LIN + h) for h in heads),
                tuple(_head(gate_ref, h) for h in heads),
                tuple(st_ref[0, 0, h] for h in heads),
                small_ref[...], al_ref[...], dt_ref[...], gn_ref[...])
        _, vjp = jax.vjp(chunk, *prim)
        cot = (tuple(_head(dmix_ref, h) for h in heads), tuple(ds_scr[h] for h in heads))
        dq, dk, dv, dgate, dstate, dsmall, dal, ddt, dgn = vjp(cot)
        dqkv_ref[...] = jnp.concatenate(list(dq) + list(dk) + list(dv), axis=1)
        dgate_ref[...] = jnp.concatenate(list(dgate), axis=1).astype(BF16)
        dsmall_ref[...] = dsmall.astype(BF16)
        for h in heads:
            ds_scr[h] = dstate[h]
        dal_ref[...] += dal
        ddt_ref[...] += ddt
        dgn_ref[...] += dgn

    row = lambda b, n: b * nc + (nc - 1 - n)
    vec = pl.BlockSpec((1, 128), lambda b, n: (0, 0))
    extra = [] if exchange is None else [exchange]
    return pl.pallas_call(
        body, name="gdn_bwd", grid=(n_batch, nc),
        in_specs=[pl.BlockSpec((CHUNK, MIX_W), lambda b, n: (row(b, n), 0)),
                  pl.BlockSpec((CHUNK, 3 * MIX_W), lambda b, n: (row(b, n), 0)),
                  pl.BlockSpec((CHUNK, 128), lambda b, n: (row(b, n), SMALL_COL)),
                  pl.BlockSpec((CHUNK, MIX_W), lambda b, n: (row(b, n), 3)),
                  pl.BlockSpec((1, 1, N_LIN, LIN_DH, LIN_DH), lambda b, n: (b, nc - 1 - n, 0, 0, 0)),
                  pl.BlockSpec((1, 1, N_LIN, CHUNK, CHUNK), lambda b, n: (b, nc - 1 - n, 0, 0, 0)),
                  vec, vec, vec] + [ANY] * len(extra),
        out_specs=[pl.BlockSpec((CHUNK, 3 * MIX_W), lambda b, n: (row(b, n), 0)),
                   pl.BlockSpec((CHUNK, MIX_W), lambda b, n: (row(b, n), 3)),
                   pl.BlockSpec((CHUNK, 128), lambda b, n: (row(b, n), 0)),
                   vec, vec, vec] + [ANY] * len(extra),
        out_shape=[jax.ShapeDtypeStruct((t, 3 * MIX_W), F32),
                   jax.ShapeDtypeStruct((t, IN_A_PAD), BF16),
                   jax.ShapeDtypeStruct((t, 128), BF16),
                   jax.ShapeDtypeStruct((1, 128), F32),
                   jax.ShapeDtypeStruct((1, 128), F32),
                   jax.ShapeDtypeStruct((1, 128), F32)]
                  + [jax.ShapeDtypeStruct((3,) + p.shape[1:], p.dtype) for p in extra],
        scratch_shapes=[pltpu.VMEM((N_LIN, LIN_DH, LIN_DH), F32)] + (EXCHANGE_SEMS if extra else []),
        compiler_params=_cparams(("arbitrary", "arbitrary")),
    )(dcat, qkv, proj, proj, states, tinvs, alog_row, dtb_row, gain_row, *extra)


SB_T = 256


def _sb_masks():
    r, c = _iota((SB_T, SB_T), 0), _iota((SB_T, SB_T), 1)
    return r, c


def _staggered(chains):
    pending, live = list(chains), []
    while pending or live:
        if pending:
            live.append(pending.pop(0))
        for g in list(live):
            try:
                next(g)
            except StopIteration:
                live.remove(g)


def _sb_rows(kb):
    start = kb * SB_T
    return pl.ds(start if isinstance(kb, int) else pl.multiple_of(start, SB_T), SB_T)


def _sb_fwd(proj, n_batch):
    nq = SEQ // SB_T
    t = n_batch * SEQ
    scale = SB_DH ** -0.5
    both = range(2)

    def body(q_ref, k_ref, v_ref, o_ref, tot_ref, acc_scr, run_scr):
        qi = pl.program_id(2)
        lane = _iota((SB_T, 128), 1)
        r, c = _sb_masks()
        upper = (r > c).astype(BF16)
        q = q_ref[...] * scale
        qm = [jnp.where((lane < SB_DH) == (hh == 0), q, jnp.zeros_like(q)) for hh in both]

        def blocks(kbs, diag=None):
            first = diag is not None
            k_blk = [k_ref[_sb_rows(kb), :] for kb in kbs]
            v_blk = [v_ref[_sb_rows(kb), :] for kb in kbs]
            run = [None if first else run_scr[hh][:, 0:1] for hh in both]
            pv = {hh: [] for hh in both}
            rowsums = {hh: [] for hh in both}

            def chain(n, hh):
                z = lax.dot_general(qm[hh], k_blk[n], NT, preferred_element_type=F32)
                yield
                lb = _log_sigmoid(z)
                l1m = lb - z
                if n == diag:
                    l1m = jnp.where(r > c, l1m, 0.0)
                parts = _split(l1m)
                terms = ([] if first else [run[hh]]) + rowsums[hh]
                before = sum(terms[1:], terms[0]) if terms else None
                rowsums[hh].append(jnp.sum(l1m, axis=1, keepdims=True))
                yield
                tail = _dot_mask(parts, upper)
                yield
                a = jnp.exp(lb + (tail if before is None else before + tail))
                if n == diag:
                    a = jnp.where(r > c, a, 0.0)
                a = a.astype(BF16)
                yield
                pv[hh].append(lax.dot_general(a, v_blk[n], NN, preferred_element_type=F32))

            _staggered([chain(n, hh) for n in range(len(kbs)) for hh in both])
            for hh in both:
                if first:
                    acc_scr[hh] = sum(pv[hh][1:], pv[hh][0])
                    run_scr[hh] = jnp.broadcast_to(sum(rowsums[hh][1:], rowsums[hh][0]), (SB_T, 128))
                else:
                    acc_scr[hh] += sum(pv[hh][1:], pv[hh][0])
                    run_scr[hh] += sum(rowsums[hh][1:], rowsums[hh][0])

        @pl.when(qi == 0)
        def _():
            blocks([0], diag=0)

        @pl.when(qi >= 1)
        def _():
            blocks([qi, qi - 1], diag=0)

        rest = jnp.maximum(qi - 1, 0)

        def step(it, carry):
            kb = rest - 1 - 2 * it
            blocks([kb, kb - 1])
            return carry

        lax.fori_loop(0, rest >> 1, step, 0)

        @pl.when((rest & 1) == 1)
        def _():
            blocks([0])
        first = lane < SB_DH
        o_ref[...] = jnp.where(first, acc_scr[0], acc_scr[1]).astype(BF16)
        tot_ref[...] = jnp.where(first, run_scr[0], run_scr[1])

    nq_blocks = lambda b, p, i: (b * nq + i, p)
    seq_spec = lambda which: pl.BlockSpec((SEQ, 128), lambda b, p, i: (b, 3 * p + which))
    return pl.pallas_call(
        body, name="sb_fwd", grid=(n_batch, SB_PAIRS, nq),
        in_specs=[pl.BlockSpec((SB_T, 128), lambda b, p, i: (b * nq + i, 3 * p)), seq_spec(1), seq_spec(2)],
        out_specs=[pl.BlockSpec((SB_T, 128), nq_blocks), pl.BlockSpec((SB_T, 128), nq_blocks)],
        out_shape=[jax.ShapeDtypeStruct((t, D_MODEL), BF16),
                   jax.ShapeDtypeStruct((t, MIX_W), F32)],
        scratch_shapes=[pltpu.VMEM((2, SB_T, 128), F32), pltpu.VMEM((2, SB_T, 128), F32)],
        compiler_params=_cparams(("parallel", "parallel", "arbitrary")),
    )(proj, proj, proj)


def _sb_bwd(dcat, proj, totals, n_batch):
    nq = SEQ // SB_T
    t = n_batch * SEQ
    scale = SB_DH ** -0.5
    both = range(2)

    def body(do_ref, q_ref, k_ref, v_ref, tot_ref, dp_ref, dq_scr, run_scr, grun_scr, dk_ref, dv_ref):
        qi = pl.program_id(2)

        @pl.when(qi == 0)
        def _():
            dk_ref[...] = jnp.zeros_like(dk_ref)
            dv_ref[...] = jnp.zeros_like(dv_ref)

        lane = _iota((SB_T, 128), 1)
        r, c = _sb_masks()
        incl = (r <= c).astype(BF16)
        earlier = (r < c).astype(BF16)
        dq_scr[...] = jnp.zeros_like(dq_scr)
        run_scr[...] = jnp.zeros_like(run_scr)
        grun_scr[...] = jnp.zeros_like(grun_scr)
        q, do, tot = q_ref[...] * scale, do_ref[...], tot_ref[...]
        sel = [(lane < SB_DH) == (hh == 0) for hh in both]
        qm = [jnp.where(sel[hh], q, jnp.zeros_like(q)) for hh in both]
        dom = [jnp.where(sel[hh], do, 0.0).astype(BF16) for hh in both]
        total = [jnp.sum(jnp.where(lane == hh * SB_DH, tot, 0.0), axis=1, keepdims=True) for hh in both]

        def blocks(kbs, diag=None):
            k_blk = [k_ref[_sb_rows(kb), :] for kb in kbs]
            v_blk = [v_ref[_sb_rows(kb), :] for kb in kbs]
            run = [run_scr[hh][:, 0:1] for hh in both]
            grun = [grun_scr[hh][:, 0:1] for hh in both]
            rs_l, rs_e, dqp = ({hh: [] for hh in both} for _ in range(3))
            dk, dv = ([[] for _ in kbs] for _ in range(2))

            def plus(base, terms):
                return base if not terms else base + sum(terms[1:], terms[0])

            def chain(n, hh):
                z = lax.dot_general(qm[hh], k_blk[n], NT, preferred_element_type=F32)
                da = lax.dot_general(dom[hh], v_blk[n], NT, preferred_element_type=F32)
                yield
                lb = _log_sigmoid(z)
                sig = jnp.exp(lb)
                l1m = lb - z
                if n == diag:
                    l1m = jnp.where(r > c, l1m, 0.0)
                parts = _split(l1m)
                run_before = plus(run[hh], rs_l[hh])
                rs_l[hh].append(jnp.sum(l1m, axis=1, keepdims=True))
                yield
                prefix = run_before + _dot_mask(parts, incl)
                yield
                a = jnp.exp(lb + (total[hh] - prefix))
                if n == diag:
                    a = jnp.where(r > c, a, 0.0)
                de = a * da
                a = a.astype(BF16)
                parts = _split(de)
                grun_before = plus(grun[hh], rs_e[hh])
                rs_e[hh].append(jnp.sum(de, axis=1, keepdims=True))
                yield
                dv[n].append(lax.dot_general(a, dom[hh], TN, preferred_element_type=F32))
                dl1m = grun_before + _dot_mask(parts, earlier)
                yield
                if n == diag:
                    dl1m = jnp.where(r > c, dl1m, 0.0)
                dz = (de * (1.0 - sig) - dl1m * sig).astype(BF16)
                yield
                dqp[hh].append(lax.dot_general(dz, k_blk[n], NN, preferred_element_type=F32))
                dk[n].append(lax.dot_general(dz, qm[hh], TN, preferred_element_type=F32))

            _staggered([chain(n, hh) for n in range(len(kbs)) for hh in both])
            for hh in both:
                dq_scr[hh] += sum(dqp[hh][1:], dqp[hh][0])
                run_scr[hh] += sum(rs_l[hh][1:], rs_l[hh][0])
                grun_scr[hh] += sum(rs_e[hh][1:], rs_e[hh][0])
            for n, kb in enumerate(kbs):
                dk_ref[_sb_rows(kb), :] += dk[n][0] + dk[n][1]
                dv_ref[_sb_rows(kb), :] += dv[n][0] + dv[n][1]

        rest = jnp.maximum(qi - 1, 0)

        def step(it, carry):
            blocks([2 * it, 2 * it + 1])
            return carry

        lax.fori_loop(0, rest >> 1, step, 0)

        @pl.when((rest & 1) == 1)
        def _():
            blocks([rest - 1])

        @pl.when(qi == 0)
        def _():
            blocks([0], diag=0)

        @pl.when(qi >= 1)
        def _():
            blocks([qi - 1, qi], diag=1)

        dq = (jnp.where(sel[0], dq_scr[0], dq_scr[1]) * scale).astype(BF16)
        dp_ref[pl.ds(pl.multiple_of(qi * SB_T, SB_T), SB_T), 0:128] = dq

        @pl.when(qi == nq - 1)
        def _():
            dp_ref[:, 128:256] = dk_ref[...].astype(BF16)
            dp_ref[:, 256:384] = dv_ref[...].astype(BF16)

    q_blocks = lambda b, p, i: (b * nq + i, p)
    seq_spec = lambda which: pl.BlockSpec((SEQ, 128), lambda b, p, i: (b, 3 * p + which))
    return pl.pallas_call(
        body, name="sb_bwd", grid=(n_batch, SB_PAIRS, nq),
        in_specs=[pl.BlockSpec((SB_T, 128), q_blocks),
                  pl.BlockSpec((SB_T, 128), lambda b, p, i: (b * nq + i, 3 * p)),
                  seq_spec(1), seq_spec(2), pl.BlockSpec((SB_T, 128), q_blocks)],
        out_specs=pl.BlockSpec((SEQ, 384), lambda b, p, i: (b, p)),
        out_shape=jax.ShapeDtypeStruct((t, IN_B), BF16),
        scratch_shapes=[pltpu.VMEM((2, SB_T, 128), F32), pltpu.VMEM((2, SB_T, 128), F32),
                        pltpu.VMEM((2, SB_T, 128), F32), pltpu.VMEM((SEQ, 128), F32), pltpu.VMEM((SEQ, 128), F32)],
        compiler_params=_cparams(("parallel", "arbitrary", "arbitrary")),
    )(dcat, proj, proj, proj, totals)


MEM_TQ = 512


def _mem_heads(lane):
    return [(lane >= X_HEAD_DIM * h) & (lane < X_HEAD_DIM * (h + 1)) for h in range(N_X_HEADS)]


def _mem_attn_fwd(proj, q_col, memkv, cat, n_batch):
    nq = SEQ // MEM_TQ
    scale = X_HEAD_DIM ** -0.5

    def body(q_ref, kv_ref, _, o_ref):
        q = q_ref[...]
        k = kv_ref[:, :X_WIDTH].astype(BF16)
        v = kv_ref[:, X_WIDTH:].astype(BF16)
        out = jnp.zeros((MEM_TQ, X_WIDTH), F32)
        for sel in _mem_heads(_iota((MEM_TQ, X_WIDTH), 1)):
            s = lax.dot_general(jnp.where(sel, q, 0.0).astype(BF16), k, NT, preferred_element_type=F32) * scale
            e = jnp.exp(s - jnp.max(s, axis=-1, keepdims=True))
            p = e / jnp.sum(e, axis=-1, keepdims=True)
            out = out + jnp.where(sel, lax.dot_general(p.astype(BF16), v, NN, preferred_element_type=F32), 0.0)
        o_ref[...] = out.astype(BF16)

    return pl.pallas_call(
        body, name="mem_attn_fwd", grid=(n_batch, nq),
        in_specs=[pl.BlockSpec((MEM_TQ, X_WIDTH), lambda b, i: (b * nq + i, q_col)),
                  pl.BlockSpec((N_MEM, 2 * X_WIDTH), lambda b, i: (b, 0)), ANY],
        out_specs=pl.BlockSpec((MEM_TQ, X_WIDTH), lambda b, i: (b * nq + i, MIX_W // X_WIDTH)),
        out_shape=jax.ShapeDtypeStruct(cat.shape, BF16),
        input_output_aliases={2: 0},
        compiler_params=_cparams(("parallel", "parallel")),
    )(proj, memkv, cat)


def _mem_attn_bwd(dcat, proj, q_col, memkv, dproj, n_batch, tail=None):
    nq = SEQ // MEM_TQ
    scale = X_HEAD_DIM ** -0.5
    width = X_WIDTH + (0 if tail is None else 128)
    assert (q_col * X_WIDTH) % width == 0

    def body(do_ref, q_ref, kv_ref, *rest):
        dq_ref, dkv_ref = rest[-2:]

        @pl.when(pl.program_id(1) == 0)
        def _():
            dkv_ref[...] = jnp.zeros_like(dkv_ref)

        q, do = q_ref[...], do_ref[...]
        k = kv_ref[:, :X_WIDTH].astype(BF16)
        v = kv_ref[:, X_WIDTH:].astype(BF16)
        dq = jnp.zeros((MEM_TQ, X_WIDTH), F32)
        dk = jnp.zeros((N_MEM, X_WIDTH), F32)
        dv = jnp.zeros((N_MEM, X_WIDTH), F32)
        for sel in _mem_heads(_iota((MEM_TQ, X_WIDTH), 1)):
            qm = jnp.where(sel, q, 0.0).astype(BF16)
            dom = jnp.where(sel, do, 0.0).astype(BF16)
            s = lax.dot_general(qm, k, NT, preferred_element_type=F32) * scale
            e = jnp.exp(s - jnp.max(s, axis=-1, keepdims=True))
            p = e / jnp.sum(e, axis=-1, keepdims=True)
            dp = lax.dot_general(dom, v, NT, preferred_element_type=F32)
            ds = ((p * (dp - jnp.sum(dp * p, axis=-1, keepdims=True))) * scale).astype(BF16)
            dv = dv + lax.dot_general(p.astype(BF16), dom, TN, preferred_element_type=F32)
            dk = dk + lax.dot_general(ds, qm, TN, preferred_element_type=F32)
            dq = dq + jnp.where(sel, lax.dot_general(ds, k, NN, preferred_element_type=F32), 0.0)
        if tail is None:
            dq_ref[...] = dq.astype(BF16)
        else:
            dq_ref[...] = jnp.concatenate([dq.astype(BF16), rest[0][...]], axis=1)
        dkv_ref[...] += jnp.concatenate([dk, dv], axis=1)

    rows = lambda b, i: b * nq + i
    extra = [] if tail is None else [tail]
    return pl.pallas_call(
        body, name="mem_attn_bwd", grid=(n_batch, nq),
        in_specs=[pl.BlockSpec((MEM_TQ, X_WIDTH), lambda b, i: (rows(b, i), MIX_W // X_WIDTH)),
                  pl.BlockSpec((MEM_TQ, X_WIDTH), lambda b, i: (rows(b, i), q_col)),
                  pl.BlockSpec((N_MEM, 2 * X_WIDTH), lambda b, i: (b, 0))]
                 + [pl.BlockSpec((MEM_TQ, 128), lambda b, i: (rows(b, i), 0))] * len(extra) + [ANY],
        out_specs=[pl.BlockSpec((MEM_TQ, width), lambda b, i: (rows(b, i), q_col * X_WIDTH // width)),
                   pl.BlockSpec((N_MEM, 2 * X_WIDTH), lambda b, i: (b, 0))],
        out_shape=[jax.ShapeDtypeStruct(dproj.shape, BF16),
                   jax.ShapeDtypeStruct((n_batch * N_MEM, 2 * X_WIDTH), F32)],
        input_output_aliases={3 + len(extra): 0},
        compiler_params=_cparams(("parallel", "arbitrary")),
    )(dcat, proj, memkv, *extra, dproj)


def _relu2_epilogue(acc):
    r = jnp.maximum(acc, 0.0)
    return (r * r,)


def _relu2_bwd_epilogue(acc, a):
    return (acc * (2.0 * jnp.sqrt(a.astype(F32))),)


def _pad_in_a(w_in_a):
    w = 3 * MIX_W
    parts = [w_in_a[:, :w], w_in_a[:, w:w + MIX_W], w_in_a[:, IN_A - X_WIDTH:],
             w_in_a[:, w + MIX_W:w + MIX_W + 2 * N_LIN]]
    pad = jnp.zeros((D_MODEL, IN_A_PAD - IN_A), w_in_a.dtype)
    return jnp.concatenate(parts + [pad], axis=1)


def _unpad_in_a(g):
    w = 3 * MIX_W
    return jnp.concatenate([g[:, :w + MIX_W], g[:, w + MIX_W + X_WIDTH:w + MIX_W + X_WIDTH + 2 * N_LIN],
                            g[:, w + MIX_W:w + MIX_W + X_WIDTH]], axis=1)


def _qkv_to_pairs(w):
    w3 = 3 * MIX_W
    qkv = w[:, :w3].reshape(-1, 3, SB_PAIRS, 128).transpose(0, 2, 1, 3).reshape(-1, w3)
    return jnp.concatenate([qkv, w[:, w3:]], axis=1)


def _pairs_to_qkv(w):
    w3 = 3 * MIX_W
    qkv = w[:, :w3].reshape(-1, SB_PAIRS, 3, 128).transpose(0, 2, 1, 3).reshape(-1, w3)
    return jnp.concatenate([qkv, w[:, w3:]], axis=1)


def _local_step(x, mem, target, wts, small, comm=None):
    wts = dict(wts)
    t = x.shape[0]
    nb = t // SEQ
    npre, npost, mpre, mpost = small["norm_pre_mix"], small["norm_post_mix"], small["norm_pre_mlp"], small["norm_post_mlp"]
    alog_row, dtb_row, gain_row = _gdn_param_rows(small["a_log_a"][0], small["dt_bias_a"][0], small["onorm_a"][0])
    conv_w = small["conv_w"]

    mem_n = _rms_fwd(mem, small["mem_norm"], name="mem_norm_fwd", tile=256)
    saved = []
    h = _rms_fwd(x, npre[0], name="pre_mix_norm0")
    big = min(1024, t)
    for i in range(DEPTH):
        s = {"x_in": x, "h1": h}
        if i == 0:
            proj = _matmul(h, wts["in_a", None], mode="nn", tm=big, tn=1152, tk=1024, name="proj_a")
            qkv = _conv_silu_fwd(proj, conv_w, nb)
            if comm is None:
                mix, states, tinvs = _gdn_fwd(qkv, proj, alog_row, dtb_row, gain_row, nb)
            else:
                mix, states, tinvs, second = _gdn_fwd(qkv, proj, alog_row, dtb_row, gain_row, nb, gather=comm["slabs"])
                second = second.reshape(N_CHIPS, REGION_ROWS[W_SECOND], D_MODEL)
                wts.update(_as_operands(_unpack_region_full(W_SECOND, second)))
            s.update(qkv=qkv, states=states, tinvs=tinvs)
            q_col = (3 * MIX_W + MIX_W) // X_WIDTH
        else:
            proj = _matmul(h, wts["in_b", None], mode="nn", tm=big, tn=1280, tk=1024, name="proj_b", out_dtypes=(BF16,))
            mix, totals = _sb_fwd(proj, nb)
            s.update(totals=totals)
            q_col = 3 * MIX_W // X_WIDTH
        memkv = _matmul(mem_n, wts["mem_kv", i], mode="nn", tm=256, tn=512, tk=1024, name=f"memkv{i}")
        cat = _mem_attn_fwd(proj, q_col, memkv, mix, nb)
        y = _matmul(cat, wts["out", i], mode="nn", tm=big, tn=1024, tk=1024, name=f"out_proj{i}")
        x2, h2 = _post_norm_add(x, y, npost[i], mpre[i], name=f"post_mix{i}")
        a = _matmul(h2, wts["up", i], mode="nn", tm=big, tn=2048, tk=1024, name=f"up{i}",
                    out_dtypes=(BF16,), epilogue=_relu2_epilogue, n_outer=True)
        y2 = _matmul(a, wts["down", i], mode="nn", tm=big, tn=1024, tk=2048, name=f"down{i}")
        s.update(proj=proj, q_col=q_col, memkv=memkv, cat=cat, y=y, x2=x2, h2=h2, a=a, y2=y2)
        saved.append(s)
        if i + 1 < DEPTH:
            x, h = _post_norm_add(x2, y2, mpost[i], npre[i + 1], name=f"post_mlp{i}")
        else:
            loss_row, dx = _post_norm_loss(x2, y2, mpost[i], target, name="loss_head")

    gw = {}
    gs = {k: [None] * DEPTH for k in ("norm_pre_mix", "norm_post_mix", "norm_pre_mlp", "norm_post_mlp")}
    dmem_n, early = None, None
    for i in reversed(range(DEPTH)):
        s = saved[i]
        if i == DEPTH - 1:
            dy2, gs["norm_post_mlp"][i] = _rms_bwd(dx, s["y2"], mpost[i], name=f"post_mlp_bwd{i}", out_dtype=BF16)
        du = _matmul(dy2, wts["down", i], mode="nt", tm=big, tn=2048, tk=1024, name=f"down_dx{i}",
                     out_dtypes=(BF16,), epilogue=_relu2_bwd_epilogue, extras=(s["a"],), n_outer=True)
        gw["down", i] = _matmul(s["a"], dy2, mode="tn", tm=1024, tn=1024, tk=big, name=f"down_dw{i}")
        dh2 = _matmul(du, wts["up", i], mode="nt", tm=big, tn=1024, tk=2048, name=f"up_dx{i}")
        gw["up", i] = _matmul(s["h2"], du, mode="tn", tm=1024, tn=2048, tk=512, name=f"up_dw{i}")
        dx2, dy, gs["norm_pre_mlp"][i], gs["norm_post_mix"][i] = _rms_bwd_pair(
            dh2, s["x2"], mpre[i], dx, s["y"], npost[i], name=f"mlp_norms_bwd{i}")
        dcat = _matmul(dy, wts["out", i], mode="nt", tm=big, tn=1024, tk=1024, name=f"out_dx{i}")
        gw["out", i] = _matmul(s["cat"], dy, mode="tn", tm=1024, tn=1024, tk=big, name=f"out_dw{i}")
        if i == 0:
            exchange = None
            if comm is not None:
                own, exchange = _reduce_in_chip(_pack_region_full(G_EARLY, gw), comm["core"])
            res = _gdn_bwd(dcat, s["qkv"], s["proj"], s["states"], s["tinvs"], alog_row, dtb_row, gain_row, nb,
                           exchange=exchange)
            dqkv, dproj, dsmall, dalog, ddtb, dgain = res[:6]
            if comm is not None:
                early = (own, res[6])
            dproj, dconv = _conv_silu_bwd(dqkv, s["proj"], conv_w, dproj, nb)
            dproj, dmemkv = _mem_attn_bwd(dcat, s["proj"], s["q_col"], s["memkv"], dproj, nb, tail=dsmall)
            w_in, tile = wts["in_a", None], 1152
        else:
            dproj = _sb_bwd(dcat, s["proj"], s["totals"], nb)
            dproj, dmemkv = _mem_attn_bwd(dcat, s["proj"], s["q_col"], s["memkv"], dproj, nb)
            w_in, tile = wts["in_b", None], 1280
        dmemkv = dmemkv.astype(BF16)
        gw["mem_kv", i] = _matmul(mem_n, dmemkv, mode="tn", tm=1024, tn=512, tk=256, name=f"memkv_dw{i}")
        dmn = _matmul(dmemkv, wts["mem_kv", i], mode="nt", tm=256, tn=1024, tk=512, name=f"memkv_dx{i}")
        dmem_n = dmn if dmem_n is None else dmem_n + dmn
        dh1 = _matmul(dproj, w_in, mode="nt", tm=big, tn=1024, tk=tile, name=f"proj_dx{i}")
        g_in = _matmul(s["h1"], dproj, mode="tn", tm=1024, tn=tile, tk=big, name=f"proj_dw{i}")
        if i == 0:
            gw["in_a", None] = _unpad_in_a(g_in)
        else:
            gw["in_b", None] = _pairs_to_qkv(g_in)
        if i > 0:
            dx, dy2, gs["norm_pre_mix"][i], gs["norm_post_mlp"][i - 1] = _rms_bwd_pair(
                dh1, s["x_in"], npre[i], dx2, saved[i - 1]["y2"], mpost[i - 1], name=f"mix_norms_bwd{i}")
        else:
            dx, gs["norm_pre_mix"][i] = _rms_bwd(dh1, s["x_in"], npre[i], name=f"pre_mix_bwd{i}", res=dx2)

    _, g_mem_norm = _rms_bwd(dmem_n, mem, small["mem_norm"], name="mem_norm_bwd", tile=256)
    gsmall = {k: jnp.concatenate(v, axis=0) for k, v in gs.items()}
    gsmall.update(mem_norm=g_mem_norm[0], a_log_a=dalog[:, N_LIN:2 * N_LIN], dt_bias_a=ddtb[:, N_LIN:2 * N_LIN],
                  onorm_a=dgain, conv_w=dconv)
    return loss_row[0, 0], dx, gw, gsmall, early


SUM_TILE = 640


def _position():
    x, y, c = lax.axis_index("x"), lax.axis_index("y"), lax.axis_index("c")
    others = [(1 - x, y), (x, 1 - y), (1 - x, 1 - y)]
    return x, y, c, others


class _Gather:
    def __init__(self, w_ref, out_ref, send_sems, recv_sems):
        self.w, self.out, self.send, self.recv = w_ref, out_ref, send_sems, recv_sems
        self.x, self.y, self.c, self.others = _position()
        self.me = 2 * self.x + self.y

    def _copy(self, k, src, dst, to):
        return pltpu.make_async_remote_copy(src_ref=src, dst_ref=dst, send_sem=self.send.at[k],
                                            recv_sem=self.recv.at[k], device_id=to, device_id_type=MESH)

    def _first(self):
        return [self._copy(j, self.w.at[self.me, self.c], self.out.at[self.me, self.c], (ox, oy, self.c))
                for j, (ox, oy) in enumerate(self.others)]

    def _passed(self):
        sibling = (self.x, self.y, 1 - self.c)
        return [self._copy(3 + j, self.out.at[2 * ox + oy, self.c], self.out.at[2 * ox + oy, self.c], sibling)
                for j, (ox, oy) in enumerate(self.others)]

    def start(self):
        for cp in self._first():
            cp.start()

    def forward(self):
        passed = self._passed()
        for j, (ox, oy) in enumerate(self.others):
            self._copy(j, self.w.at[self.me, self.c], self.out.at[2 * ox + oy, self.c], (self.x, self.y, self.c)).wait_recv()
            passed[j].start()

    def finish(self):
        for j, (ox, oy) in enumerate(self.others):
            self._copy(3 + j, self.w.at[self.me, self.c], self.out.at[2 * ox + oy, 1 - self.c],
                       (self.x, self.y, self.c)).wait_recv()
        for cp in self._first() + self._passed():
            cp.wait_send()


GATHER_SEMS = [pltpu.SemaphoreType.DMA((6,)), pltpu.SemaphoreType.DMA((6,))]


def _gather_chips(wflat):
    def body(w_ref, out_ref, send_sems, recv_sems):
        g = _Gather(w_ref, out_ref, send_sems, recv_sems)
        g.start()
        g.forward()
        g.finish()

    return pl.pallas_call(
        body, name="gather_weights",
        in_specs=[ANY], out_specs=ANY, input_output_aliases={0: 0},
        out_shape=jax.ShapeDtypeStruct(wflat.shape, wflat.dtype),
        scratch_shapes=GATHER_SEMS,
    )(wflat)


def _gather_all(v, *, name):
    rows, n = v.shape

    def body(x_ref, out_ref, send_sems, recv_sems, local_sem):
        x, y, c, others = _position()
        me, sibling = (x, y, c), (x, y, 1 - c)

        def blk(px, py, pc):
            return out_ref.at[pl.ds((4 * px + 2 * py + pc) * rows, rows), :]

        def copy(k, block, to, src=None):
            return pltpu.make_async_remote_copy(src_ref=blk(*block) if src is None else src, dst_ref=blk(*block),
                                                send_sem=send_sems.at[k], recv_sem=recv_sems.at[k],
                                                device_id=to, device_id_type=MESH)

        mine = pltpu.make_async_copy(x_ref, blk(*me), local_sem)
        mine.start()
        first = [copy(0, me, sibling, src=x_ref)]
        first += [copy(1 + j, me, (*chip, c), src=x_ref) for j, chip in enumerate(others)]
        for cp in first:
            cp.start()
        passed = [copy(4 + j, (*chip, c), sibling) for j, chip in enumerate(others)]
        for j, chip in enumerate(others):
            copy(1 + j, (*chip, c), me).wait_recv()
            passed[j].start()
        copy(0, sibling, me).wait_recv()
        for j, chip in enumerate(others):
            copy(4 + j, (*chip, 1 - c), me).wait_recv()
        for cp in first + passed:
            cp.wait_send()
        mine.wait()

    vmem = pl.BlockSpec(memory_space=pltpu.VMEM)
    return pl.pallas_call(
        body, name=name, in_specs=[vmem], out_specs=vmem,
        out_shape=jax.ShapeDtypeStruct((8 * rows, n), v.dtype),
        scratch_shapes=[pltpu.SemaphoreType.DMA((7,)), pltpu.SemaphoreType.DMA((7,)), pltpu.SemaphoreType.DMA],
    )(v)


def _swap_halves(g5):
    def body(g_ref, out_ref, send_sem, recv_sem):
        x, y, c, _ = _position()
        cp = pltpu.make_async_remote_copy(src_ref=g_ref.at[:, 1 - c], dst_ref=out_ref, send_sem=send_sem,
                                          recv_sem=recv_sem, device_id=(x, y, 1 - c), device_id_type=MESH)
        cp.start()
        cp.wait()

    return pl.pallas_call(
        body, name="grad_swap_halves", in_specs=[ANY], out_specs=ANY,
        out_shape=jax.ShapeDtypeStruct((N_CHIPS, g5.shape[2], D_MODEL), g5.dtype),
        scratch_shapes=[pltpu.SemaphoreType.DMA, pltpu.SemaphoreType.DMA],
    )(g5)


def _add_halves(core, g5, got):
    def body(c_ref, a_ref, b_ref, o_ref, ob_ref):
        s = a_ref[0] + b_ref[...]
        o_ref[...] = s
        ob_ref[...] = s.astype(BF16)

    half = g5.shape[2]
    nt = half // SUM_TILE
    spec = pl.BlockSpec((1, SUM_TILE, D_MODEL), lambda s, i, c_ref: (s, i, 0))
    return pl.pallas_call(
        body, name="grad_add_halves",
        grid_spec=pltpu.PrefetchScalarGridSpec(
            num_scalar_prefetch=1, grid=(N_CHIPS, nt),
            in_specs=[pl.BlockSpec((1, 1, SUM_TILE, D_MODEL), lambda s, i, c_ref: (s, c_ref[0], i, 0)), spec],
            out_specs=[spec, spec]),
        out_shape=[jax.ShapeDtypeStruct((N_CHIPS, half, D_MODEL), F32),
                   jax.ShapeDtypeStruct((N_CHIPS, half, D_MODEL), BF16)],
        compiler_params=_cparams(("parallel", "parallel")),
    )(core, g5, got)


def _exchange_copies(p_ref, q_ref, send_sems, recv_sems):
    x, y, c, others = _position()
    return [pltpu.make_async_remote_copy(src_ref=p_ref.at[2 * ox + oy], dst_ref=q_ref.at[j],
                                         send_sem=send_sems.at[j], recv_sem=recv_sems.at[j],
                                         device_id=(ox, oy, c), device_id_type=MESH)
            for j, (ox, oy) in enumerate(others)]


EXCHANGE_SEMS = [pltpu.SemaphoreType.DMA((3,)), pltpu.SemaphoreType.DMA((3,))]


def _exchange_chips(p):
    def body(p_ref, q_ref, send_sems, recv_sems):
        copies = _exchange_copies(p_ref, q_ref, send_sems, recv_sems)
        for cp in copies:
            cp.start()
        for cp in copies:
            cp.wait()

    return pl.pallas_call(
        body, name="grad_exchange_chips", in_specs=[ANY], out_specs=ANY,
        out_shape=jax.ShapeDtypeStruct((3,) + p.shape[1:], p.dtype),
        scratch_shapes=EXCHANGE_SEMS,
    )(p)


def _add_chips(chip_core, p, q):
    def body(kc_ref, p_ref, q_ref, o_ref):
        o_ref[0] = ((p_ref[0] + q_ref[0].astype(F32)) + q_ref[1].astype(F32)) + q_ref[2].astype(F32)

    half = p.shape[1]
    nt = half // SUM_TILE
    return pl.pallas_call(
        body, name="grad_add_chips",
        grid_spec=pltpu.PrefetchScalarGridSpec(
            num_scalar_prefetch=1, grid=(nt,),
            in_specs=[pl.BlockSpec((1, SUM_TILE, D_MODEL), lambda i, kc_ref: (kc_ref[0], i, 0)),
                      pl.BlockSpec((3, SUM_TILE, D_MODEL), lambda i, kc_ref: (0, i, 0))],
            out_specs=pl.BlockSpec((1, SUM_TILE, D_MODEL), lambda i, kc_ref: (kc_ref[1], i, 0))),
        out_shape=jax.ShapeDtypeStruct((2, half, D_MODEL), F32),
        compiler_params=_cparams(("parallel",)),
    )(chip_core, p, q)


def _share_halves(halves):
    def body(h_ref, out_ref, send_sem, recv_sem):
        x, y, c, _ = _position()
        cp = pltpu.make_async_remote_copy(src_ref=h_ref.at[c], dst_ref=out_ref.at[c], send_sem=send_sem,
                                          recv_sem=recv_sem, device_id=(x, y, 1 - c), device_id_type=MESH)
        cp.start()
        pltpu.make_async_remote_copy(src_ref=h_ref.at[c], dst_ref=out_ref.at[1 - c], send_sem=send_sem,
                                     recv_sem=recv_sem, device_id=(x, y, c), device_id_type=MESH).wait_recv()
        cp.wait_send()

    return pl.pallas_call(
        body, name="grad_share_halves", in_specs=[ANY], out_specs=ANY, input_output_aliases={0: 0},
        out_shape=jax.ShapeDtypeStruct(halves.shape, halves.dtype),
        scratch_shapes=[pltpu.SemaphoreType.DMA, pltpu.SemaphoreType.DMA],
    )(halves)


def _reduce_in_chip(g_packed, core):
    rows = g_packed.shape[1]
    g5 = g_packed.reshape(N_CHIPS, 2, rows // 2, D_MODEL)
    return _add_halves(core.reshape(1), g5, _swap_halves(g5))


def _reduce_across_chips(p, q, chip, core):
    halves = _share_halves(_add_chips(jnp.stack([chip, core]), p, q))
    return halves.reshape(2 * halves.shape[1], D_MODEL)


def _reduce_scatter(g_packed, chip, core):
    p, p_bf = _reduce_in_chip(g_packed, core)
    return _reduce_across_chips(p, _exchange_chips(p_bf), chip, core)


def _slot(n):
    return -(-n // 16) * 16


def _pad_rows(a, axis):
    n = a.shape[axis]
    widths = [(0, 0)] * a.ndim
    widths[axis] = (0, _slot(n) - n)
    return jnp.pad(a, widths) if _slot(n) != n else a


W_FIRST = (("in_a", None),)
W_SECOND = (("mem_kv", 0), ("out", 0), ("up", 0), ("down", 0),
            ("in_b", None), ("mem_kv", 1), ("out", 1), ("up", 1), ("down", 1))
G_LATE = (("in_a", None), ("mem_kv", 0), ("out", 0))
G_EARLY = (("up", 0), ("down", 0), ("in_b", None), ("mem_kv", 1), ("out", 1), ("up", 1), ("down", 1))
REGION_ROWS = {W_FIRST: 896, W_SECOND: 5504, G_LATE: 1280, G_EARLY: 5120}
FULL_SHAPE = {"in_a": (D_MODEL, IN_A), "in_b": (D_MODEL, IN_B), "mem_kv": (D_MODEL, 2 * X_WIDTH),
              "out": (D_MODEL, D_MODEL), "up": (D_MODEL, D_FF), "down": (D_FF, D_MODEL)}
COLUMN_SHARDED = ("in_a", "in_b", "up")


def _shard_shape(name):
    r, c = FULL_SHAPE[name]
    return (r, c // N_CHIPS) if name in COLUMN_SHARDED else (r // N_CHIPS, c)


def _part_rows(name):
    r, c = _shard_shape(name)
    return r * c // D_MODEL


def _pack_region(region, part, dtype):
    rows = [_pad_rows(part(name, layer).reshape(-1, D_MODEL).astype(dtype), 0) for name, layer in region]
    used = sum(r.shape[0] for r in rows)
    return jnp.concatenate(rows + [jnp.zeros((REGION_ROWS[region] - used, D_MODEL), dtype)], axis=0)


def _unpack_region(region, flat):
    out, off = {}, 0
    for name, layer in region:
        n = _part_rows(name)
        out[name, layer] = flat[off:off + n].reshape(_shard_shape(name))
        off += _slot(n)
    return out


def _unpack_region_full(region, g):
    out, off = {}, 0
    for name, layer in region:
        n = _part_rows(name)
        piece = g[:, off:off + n].reshape((N_CHIPS,) + _shard_shape(name))
        if name in COLUMN_SHARDED:
            piece = piece.transpose(1, 0, 2)
        out[name, layer] = piece.reshape(FULL_SHAPE[name])
        off += _slot(n)
    return out


def _pack_region_full(region, full):
    s = N_CHIPS
    parts = []
    for name, layer in region:
        g = full[name, layer]
        if name in COLUMN_SHARDED:
            g = g.reshape(g.shape[0], s, -1).transpose(1, 0, 2)
        parts.append(_pad_rows(g.reshape(s, -1, D_MODEL), 1))
    used = sum(p.shape[1] for p in parts)
    return jnp.concatenate(parts + [jnp.zeros((s, REGION_ROWS[region] - used, D_MODEL), F32)], axis=1)


def _as_operands(full):
    out = dict(full)
    if ("in_a", None) in out:
        out["in_a", None] = _pad_in_a(out["in_a", None])
    if ("in_b", None) in out:
        out["in_b", None] = _qkv_to_pairs(out["in_b", None])
    return out


def _place(packed, chip):
    rows = packed.shape[0]
    slabs = lax.dynamic_update_slice(jnp.zeros((N_CHIPS, rows, D_MODEL), packed.dtype), packed[None], (chip, 0, 0))
    return slabs.reshape(N_CHIPS, 2, rows // 2, D_MODEL)


def _adamw_math(w, g, m, v):
    m = ADAM_B1 * m + (1.0 - ADAM_B1) * g
    v = ADAM_B2 * v + (1.0 - ADAM_B2) * (g * g)
    m_hat = m / (1.0 - ADAM_B1 ** ADAM_STEP)
    v_hat = v / (1.0 - ADAM_B2 ** ADAM_STEP)
    delta = -ADAM_LR * (m_hat / (jnp.sqrt(v_hat) + ADAM_EPS) + ADAM_WD * w)
    return delta, m, v


ADAM_TILE = 256


def _adamw(w, g, m, v, *, name):
    shape = w.shape
    cols = shape[-1]
    rows = w.size // cols
    tile = min(rows, ADAM_TILE)
    assert rows % tile == 0, (name, shape)

    def body(w_ref, g_ref, m_ref, v_ref, d_ref, nm_ref, nv_ref):
        d_ref[...], nm_ref[...], nv_ref[...] = _adamw_math(w_ref[...], g_ref[...], m_ref[...], v_ref[...])

    spec = pl.BlockSpec((tile, cols), lambda i: (i, 0))
    outs = pl.pallas_call(
        body, name=name, grid=(rows // tile,), in_specs=[spec] * 4, out_specs=[spec] * 3,
        out_shape=[jax.ShapeDtypeStruct((rows, cols), F32)] * 3,
        compiler_params=_cparams(("parallel",)),
    )(*[a.reshape(rows, cols) for a in (w, g, m, v)])
    return [o.reshape(shape) for o in outs]


SMALL_NAMES = (("mem_norm", 8), ("norm_pre_mix", 16), ("norm_post_mix", 16), ("norm_pre_mlp", 16),
               ("norm_post_mlp", 16), ("a_log_a", 1), ("dt_bias_a", 1), ("onorm_a", 1))
SMALL_ROWS = 80
CONV_ROWS = CONV_K * 3 * MIX_W // 128
SMALL_GRAD_ROWS = SMALL_ROWS + CONV_ROWS


def _pack_small(vals):
    rows = []
    for name, n in SMALL_NAMES:
        flat = vals[name].reshape(-1)
        rows.append(jnp.pad(flat, (0, n * 128 - flat.size)).reshape(n, 128))
    used = sum(n for _, n in SMALL_NAMES)
    return jnp.concatenate(rows + [jnp.zeros((SMALL_ROWS - used, 128), F32)], axis=0)


def _unpack_small(packed, like):
    out, off = {}, 0
    for name, n in SMALL_NAMES:
        size = like[name].size
        out[name] = packed[off:off + n].reshape(-1)[:size].reshape(like[name].shape)
        off += n
    return out


def _small_update(gathered, w, m, v):
    def body(g_ref, w_ref, m_ref, v_ref, gs_ref, d_ref, nm_ref, nv_ref):
        g = g_ref[0]
        for dev in range(1, 8):
            g = g + g_ref[dev]
        gs_ref[...] = g
        d_ref[...], nm_ref[...], nv_ref[...] = _adamw_math(w_ref[...], g[:SMALL_ROWS], m_ref[...], v_ref[...])

    small = jax.ShapeDtypeStruct((SMALL_ROWS, 128), F32)
    return pl.pallas_call(
        body, name="small_update",
        out_shape=[jax.ShapeDtypeStruct((SMALL_GRAD_ROWS, 128), F32), small, small, small],
    )(gathered.reshape(8, SMALL_GRAD_ROWS, 128), w, m, v)


def kernel(x, mem, mem_norm, norm_pre_mix, norm_post_mix, norm_pre_mlp, norm_post_mlp, w_in_a, conv_w_a, a_log_a, dt_bias_a, onorm_a, w_in_b, w_mem_kv, w_out, w_up, w_down, loss_target, m_mem_norm, m_norm_pre_mix, m_norm_post_mix, m_norm_pre_mlp, m_norm_post_mlp, m_w_in_a, m_conv_w_a, m_a_log_a, m_dt_bias_a, m_onorm_a, m_w_in_b, m_w_mem_kv, m_w_out, m_w_up, m_w_down, v_mem_norm, v_norm_pre_mix, v_norm_post_mix, v_norm_pre_mlp, v_norm_post_mlp, v_w_in_a, v_conv_w_a, v_a_log_a, v_dt_bias_a, v_onorm_a, v_w_in_b, v_w_mem_kv, v_w_out, v_w_up, v_w_down):
    nb = x.shape[0]
    chip = (2 * lax.axis_index("x") + lax.axis_index("y")).astype(jnp.int32)
    core = lax.axis_index("c").astype(jnp.int32)
    shards = {"in_a": w_in_a, "in_b": w_in_b, "mem_kv": w_mem_kv, "out": w_out, "up": w_up, "down": w_down}
    moments_m = {"in_a": m_w_in_a, "in_b": m_w_in_b, "mem_kv": m_w_mem_kv, "out": m_w_out, "up": m_w_up, "down": m_w_down}
    moments_v = {"in_a": v_w_in_a, "in_b": v_w_in_b, "mem_kv": v_w_mem_kv, "out": v_w_out, "up": v_w_up, "down": v_w_down}
    small_w = {"mem_norm": mem_norm, "norm_pre_mix": norm_pre_mix, "norm_post_mix": norm_post_mix,
               "norm_pre_mlp": norm_pre_mlp, "norm_post_mlp": norm_post_mlp, "a_log_a": a_log_a,
               "dt_bias_a": dt_bias_a, "onorm_a": onorm_a}
    small_m = {"mem_norm": m_mem_norm, "norm_pre_mix": m_norm_pre_mix, "norm_post_mix": m_norm_post_mix,
               "norm_pre_mlp": m_norm_pre_mlp, "norm_post_mlp": m_norm_post_mlp, "a_log_a": m_a_log_a,
               "dt_bias_a": m_dt_bias_a, "onorm_a": m_onorm_a}
    small_v = {"mem_norm": v_mem_norm, "norm_pre_mix": v_norm_pre_mix, "norm_post_mix": v_norm_post_mix,
               "norm_pre_mlp": v_norm_pre_mlp, "norm_post_mlp": v_norm_post_mlp, "a_log_a": v_a_log_a,
               "dt_bias_a": v_dt_bias_a, "onorm_a": v_onorm_a}

    def shard_part(name, layer):
        return shards[name][0 if layer is None else layer]

    first = _gather_chips(_place(_pack_region(W_FIRST, shard_part, BF16), chip))
    wts = _as_operands(_unpack_region_full(W_FIRST, first.reshape(N_CHIPS, REGION_ROWS[W_FIRST], D_MODEL)))
    comm = {"slabs": _place(_pack_region(W_SECOND, shard_part, BF16), chip), "core": core}
    conv_rows = CONV_ROWS // N_CHIPS
    conv_blk = jnp.pad(conv_w_a.reshape(conv_rows, 128), ((0, 24 - conv_rows), (0, 0)))
    conv_all = _gather_all(conv_blk, name="gather_conv").reshape(N_CHIPS, 2, 24, 128)[:, 0, :conv_rows]
    conv_full = conv_all.reshape(N_CHIPS, CONV_K, 3 * MIX_W // N_CHIPS).transpose(1, 0, 2).reshape(CONV_K, 3 * MIX_W)

    loss_local, dx, gw, gsmall, (own_early, others_early) = _local_step(
        x.reshape(nb * SEQ, D_MODEL), mem.reshape(nb * N_MEM, D_MODEL), loss_target.reshape(nb * SEQ, D_MODEL),
        wts, dict(small_w, conv_w=conv_full), comm)
    loss = lax.psum(loss_local, ("x", "y", "c"))
    grad_x = dx.reshape(nb, SEQ, D_MODEL)

    g_part = _unpack_region(G_EARLY, _reduce_across_chips(own_early, others_early, chip, core))
    g_part.update(_unpack_region(G_LATE, _reduce_scatter(_pack_region_full(G_LATE, gw), chip, core)))
    g_shard = {k: (g_part[k, None][None] if (k, None) in g_part else jnp.stack([g_part[k, i] for i in range(DEPTH)]))
               for k in shards}
    upd = {k: _adamw(shards[k], g_shard[k], moments_m[k], moments_v[k], name=f"adamw_{k}") for k in shards}

    g_rows = jnp.concatenate([_pack_small(gsmall), gsmall["conv_w"].reshape(CONV_ROWS, 128)], axis=0)
    g_all = _gather_all(g_rows, name="gather_small_grads")
    g_sum, d_small, nm_small, nv_small = _small_update(g_all, _pack_small(small_w), _pack_small(small_m), _pack_small(small_v))
    gs = _unpack_small(g_sum, small_w)
    ds, nms, nvs = (_unpack_small(p, small_w) for p in (d_small, nm_small, nv_small))
    cw = 3 * MIX_W // N_CHIPS
    g_conv = lax.dynamic_slice(g_sum[SMALL_ROWS:].reshape(CONV_K, 3 * MIX_W), (0, chip * cw), (CONV_K, cw)).reshape(conv_w_a.shape)
    d_conv, nm_conv, nv_conv = _adamw(conv_w_a, g_conv, m_conv_w_a, v_conv_w_a, name="adamw_conv")

    order = ("mem_norm", "norm_pre_mix", "norm_post_mix", "norm_pre_mlp", "norm_post_mlp", "in_a", "conv", "a_log_a",
             "dt_bias_a", "onorm_a", "in_b", "mem_kv", "out", "up", "down")
    grads = dict(gs, conv=g_conv, **g_shard)
    deltas = dict(ds, conv=d_conv, **{k: u[0] for k, u in upd.items()})
    new_m = dict(nms, conv=nm_conv, **{k: u[1] for k, u in upd.items()})
    new_v = dict(nvs, conv=nv_conv, **{k: u[2] for k, u in upd.items()})
    return (loss, grad_x, *[grads[k] for k in order], *[deltas[k] for k in order],
            *[new_m[k] for k in order], *[new_v[k] for k in order])
```

```python
import functools

import jax
import jax.numpy as jnp
from jax import lax
from jax.experimental import pallas as pl
from jax.experimental.pallas import tpu as pltpu

F32 = jnp.float32
BF16 = jnp.bfloat16
HIGHEST = lax.Precision.HIGHEST
MESH = pl.DeviceIdType.MESH

D_MODEL = 1024
SEQ = 2048
DEPTH = 2
X_WIDTH = 256
N_X_HEADS = 4
X_HEAD_DIM = 64
MIX_W = 768
LIN_DH = 128
N_LIN = 6
CONV_K = 4
CHUNK = 64
SB_DH = 64
SB_PAIRS = 6
N_MEM = 256
D_FF = 4096
EPS = 1e-6
IN_A = 3340
IN_A_PAD = 3456
IN_B = 2560
SMALL_COL = 26
N_CHIPS = 4

ADAM_LR, ADAM_B1, ADAM_B2, ADAM_EPS, ADAM_WD, ADAM_STEP = 0.001, 0.9, 0.999, 1e-08, 0.01, 10

VMEM_LIMIT = 48 * 1024 * 1024

ANY = pl.BlockSpec(memory_space=pl.ANY)

NN = (((1,), (0,)), ((), ()))
NT = (((1,), (1,)), ((), ()))
TN = (((0,), (0,)), ((), ()))


def _cparams(sem):
    return pltpu.CompilerParams(dimension_semantics=sem, vmem_limit_bytes=VMEM_LIMIT)


def _dotbf(a, b, dn=NN):
    return lax.dot_general(a.astype(BF16), b.astype(BF16), dn, preferred_element_type=F32)


def _split(a):
    hi = a.astype(BF16)
    lo = (a - hi.astype(F32)).astype(BF16)
    return hi, lo


def _dot3(a, b, dn=NN):
    ah, al = _split(a)
    bh, bl = _split(b)
    d = functools.partial(lax.dot_general, dimension_numbers=dn, preferred_element_type=F32)
    return d(ah, bh) + (d(ah, bl) + d(al, bh))


def _exact01(a, b, dn, mask_left):
    m, x = (a, b) if mask_left else (b, a)
    m = m.astype(BF16)
    hi = x.astype(BF16)
    r1 = x - hi.astype(F32)
    mid = r1.astype(BF16)
    lo = (r1 - mid.astype(F32)).astype(BF16)
    d = functools.partial(lax.dot_general, dimension_numbers=dn, preferred_element_type=F32)
    pair = (lambda p: d(m, p)) if mask_left else (lambda p: d(p, m))
    return pair(hi) + (pair(mid) + pair(lo))


@functools.partial(jax.custom_vjp, nondiff_argnums=(2,))
def _dot01(m01, x, dn):
    return _exact01(m01, x, dn, True)


def _dot01_fwd(m01, x, dn):
    return _exact01(m01, x, dn, True), m01


def _dot01_bwd(dn, m01, ct):
    dx = _exact01(m01, ct, TN, True) if dn == NN else _exact01(ct, m01, TN, False)
    return jnp.zeros_like(m01), dx


_dot01.defvjp(_dot01_fwd, _dot01_bwd)


def _dot_mask(parts, m01):
    d = functools.partial(lax.dot_general, dimension_numbers=NN, preferred_element_type=F32)
    return d(parts[0], m01) + d(parts[1], m01)


def _iota(shape, dim):
    return lax.broadcasted_iota(jnp.int32, shape, dim)


def _softplus(x):
    return jnp.maximum(x, 0.0) + jnp.log(1.0 + jnp.exp(-jnp.abs(x)))


def _log_sigmoid(z):
    return jnp.minimum(z, 0.0) - jnp.log(1.0 + jnp.exp(-jnp.abs(z)))


def _rms(x, g):
    r = lax.rsqrt(jnp.mean(x * x, axis=-1, keepdims=True) + EPS)
    return (x * r) * g


def _matmul(a, b, *, mode, tm, tn, tk, name, out_dtypes=(F32,), epilogue=None, extras=(), n_outer=False):
    if n_outer:
        ix = lambda f: (lambda j, i, kk: f(i, j, kk))
    else:
        ix = lambda f: f
    if mode == "nn":
        (m, k), (k2, n) = a.shape, b.shape
        a_spec = pl.BlockSpec((tm, tk), ix(lambda i, j, kk: (i, kk)))
        b_spec = pl.BlockSpec((tk, tn), ix(lambda i, j, kk: (kk, j)))
        dn = NN
    elif mode == "nt":
        (m, k), (n, k2) = a.shape, b.shape
        a_spec = pl.BlockSpec((tm, tk), ix(lambda i, j, kk: (i, kk)))
        b_spec = pl.BlockSpec((tn, tk), ix(lambda i, j, kk: (j, kk)))
        dn = NT
    else:
        (k, m), (k2, n) = a.shape, b.shape
        a_spec = pl.BlockSpec((tk, tm), ix(lambda i, j, kk: (kk, i)))
        b_spec = pl.BlockSpec((tk, tn), ix(lambda i, j, kk: (kk, j)))
        dn = TN
    assert k == k2 and m % tm == 0 and n % tn == 0 and k % tk == 0, (name, a.shape, b.shape)
    assert a.dtype == BF16 and b.dtype == BF16, name
    nk = k // tk
    n_extra, n_out = len(extras), len(out_dtypes)

    def finish(acc, extra_refs, out_refs):
        outs = (acc,) if epilogue is None else epilogue(acc, *[r[...] for r in extra_refs])
        for o_ref, o in zip(out_refs, outs):
            o_ref[...] = o.astype(o_ref.dtype)

    def body_single(a_ref, b_ref, *rest):
        acc = lax.dot_general(a_ref[...], b_ref[...], dn, preferred_element_type=F32)
        finish(acc, rest[:n_extra], rest[n_extra:n_extra + n_out])

    def body_tiled(a_ref, b_ref, *rest):
        extra_refs, out_refs, acc_ref = rest[:n_extra], rest[n_extra:n_extra + n_out], rest[-1]
        kk = pl.program_id(2)

        @pl.when(kk == 0)
        def _():
            acc_ref[...] = jnp.zeros_like(acc_ref)

        acc_ref[...] += lax.dot_general(a_ref[...], b_ref[...], dn, preferred_element_type=F32)

        @pl.when(kk == nk - 1)
        def _():
            finish(acc_ref[...], extra_refs, out_refs)

    mn_spec = pl.BlockSpec((tm, tn), ix(lambda i, j, kk: (i, j)))
    grid = (n // tn, m // tm, nk) if n_outer else (m // tm, n // tn, nk)
    outs = pl.pallas_call(
        body_single if nk == 1 else body_tiled,
        name=name,
        grid=grid,
        in_specs=[a_spec, b_spec] + [mn_spec] * n_extra,
        out_specs=[mn_spec] * n_out,
        out_shape=[jax.ShapeDtypeStruct((m, n), dt) for dt in out_dtypes],
        scratch_shapes=[] if nk == 1 else [pltpu.VMEM((tm, tn), F32)],
        compiler_params=_cparams(("parallel", "parallel", "arbitrary")),
    )(a, b, *extras)
    return outs[0] if n_out == 1 else outs


ROW_TILE = 512


def _row_spec(width=D_MODEL, tile=ROW_TILE):
    return pl.BlockSpec((tile, width), lambda i: (i, 0))


def _vec_spec(width=D_MODEL):
    return pl.BlockSpec((1, width), lambda i: (0, 0))


def _rms_fwd(x, g, *, name, tile=ROW_TILE):
    t = x.shape[0]

    def body(x_ref, g_ref, h_ref):
        h_ref[...] = _rms(x_ref[...], g_ref[...]).astype(BF16)

    return pl.pallas_call(
        body, name=name, grid=(t // tile,),
        in_specs=[_row_spec(tile=tile), _vec_spec()], out_specs=_row_spec(tile=tile),
        out_shape=jax.ShapeDtypeStruct((t, D_MODEL), BF16),
        compiler_params=_cparams(("parallel",)),
    )(x, g.reshape(1, D_MODEL))


def _post_norm_add(xres, y, g_post, g_next, *, name):
    t = xres.shape[0]

    def body(x_ref, y_ref, gp_ref, gn_ref, xo_ref, h_ref):
        xo = x_ref[...] + _rms(y_ref[...], gp_ref[...])
        xo_ref[...] = xo
        h_ref[...] = _rms(xo, gn_ref[...]).astype(BF16)

    return pl.pallas_call(
        body, name=name, grid=(t // ROW_TILE,),
        in_specs=[_row_spec(), _row_spec(), _vec_spec(), _vec_spec()],
        out_specs=[_row_spec(), _row_spec()],
        out_shape=[jax.ShapeDtypeStruct((t, D_MODEL), F32), jax.ShapeDtypeStruct((t, D_MODEL), BF16)],
        compiler_params=_cparams(("parallel",)),
    )(xres, y, g_post.reshape(1, D_MODEL), g_next.reshape(1, D_MODEL))


def _post_norm_loss(xres, y, g_post, target, *, name):
    t = xres.shape[0]

    def body(x_ref, y_ref, gp_ref, t_ref, loss_ref, dx_ref):
        @pl.when(pl.program_id(0) == 0)
        def _():
            loss_ref[...] = jnp.zeros_like(loss_ref)

        err = (x_ref[...] + _rms(y_ref[...], gp_ref[...])) - t_ref[...]
        per_tok = jnp.mean(err * err, axis=-1, keepdims=True)
        loss_ref[...] += 0.5 * jnp.sum(per_tok, axis=0, keepdims=True)
        dx_ref[...] = err * (1.0 / D_MODEL)

    return pl.pallas_call(
        body, name=name, grid=(t // ROW_TILE,),
        in_specs=[_row_spec(), _row_spec(), _vec_spec(), _row_spec()],
        out_specs=[pl.BlockSpec((1, 128), lambda i: (0, 0)), _row_spec()],
        out_shape=[jax.ShapeDtypeStruct((1, 128), F32), jax.ShapeDtypeStruct((t, D_MODEL), F32)],
        compiler_params=_cparams(("arbitrary",)),
    )(xres, y, g_post.reshape(1, D_MODEL), target)


def _rms_bwd(dy, x, g, *, name, res=None, out_dtype=F32, tile=ROW_TILE):
    t = x.shape[0]
    has_res = res is not None

    def body(dy_ref, x_ref, g_ref, *rest):
        res_ref = rest[0] if has_res else None
        dx_ref, dg_ref = rest[-2], rest[-1]

        @pl.when(pl.program_id(0) == 0)
        def _():
            dg_ref[...] = jnp.zeros_like(dg_ref)

        xf = x_ref[...]
        dyf = dy_ref[...].astype(F32)
        r = lax.rsqrt(jnp.mean(xf * xf, axis=-1, keepdims=True) + EPS)
        xhat = xf * r
        dg_ref[...] += jnp.sum(dyf * xhat, axis=0, keepdims=True)
        dxh = dyf * g_ref[...]
        dx = r * (dxh - xhat * jnp.mean(dxh * xhat, axis=-1, keepdims=True))
        if has_res:
            dx = dx + res_ref[...]
        dx_ref[...] = dx.astype(dx_ref.dtype)

    args = [dy, x, g.reshape(1, D_MODEL)] + ([res] if has_res else [])
    return pl.pallas_call(
        body, name=name, grid=(t // tile,),
        in_specs=[_row_spec(tile=tile), _row_spec(tile=tile), _vec_spec()] + ([_row_spec(tile=tile)] if has_res else []),
        out_specs=[_row_spec(tile=tile), _vec_spec()],
        out_shape=[jax.ShapeDtypeStruct((t, D_MODEL), out_dtype), jax.ShapeDtypeStruct((1, D_MODEL), F32)],
        compiler_params=_cparams(("arbitrary",)),
    )(*args)


def _rms_bwd_pair(dh, x, g_pre, res, y, g_post, *, name):
    t = x.shape[0]

    def norm_bwd(dy, xf, g):
        r = lax.rsqrt(jnp.mean(xf * xf, axis=-1, keepdims=True) + EPS)
        xhat = xf * r
        dxh = dy * g
        dx = r * (dxh - xhat * jnp.mean(dxh * xhat, axis=-1, keepdims=True))
        return dx, jnp.sum(dy * xhat, axis=0, keepdims=True)

    def body(dh_ref, x_ref, gp_ref, res_ref, y_ref, gq_ref, dx_ref, dy_ref, dgp_ref, dgq_ref):
        @pl.when(pl.program_id(0) == 0)
        def _():
            dgp_ref[...] = jnp.zeros_like(dgp_ref)
            dgq_ref[...] = jnp.zeros_like(dgq_ref)

        dx, dgp = norm_bwd(dh_ref[...], x_ref[...], gp_ref[...])
        dx = dx + res_ref[...]
        dx_ref[...] = dx
        dy, dgq = norm_bwd(dx, y_ref[...], gq_ref[...])
        dy_ref[...] = dy.astype(BF16)
        dgp_ref[...] += dgp
        dgq_ref[...] += dgq

    return pl.pallas_call(
        body, name=name, grid=(t // ROW_TILE,),
        in_specs=[_row_spec(), _row_spec(), _vec_spec(), _row_spec(), _row_spec(), _vec_spec()],
        out_specs=[_row_spec(), _row_spec(), _vec_spec(), _vec_spec()],
        out_shape=[jax.ShapeDtypeStruct((t, D_MODEL), F32), jax.ShapeDtypeStruct((t, D_MODEL), BF16),
                   jax.ShapeDtypeStruct((1, D_MODEL), F32), jax.ShapeDtypeStruct((1, D_MODEL), F32)],
        compiler_params=_cparams(("arbitrary",)),
    )(dh, x, g_pre.reshape(1, D_MODEL), res, y, g_post.reshape(1, D_MODEL))


CONV_COLS = 256
N_CONV_BLOCKS = 3 * MIX_W // CONV_COLS
CONV_STRIP = 128


def _shift_down(x, k):
    if k == 0:
        return x
    return jnp.where(_iota(x.shape, 0) >= k, pltpu.roll(x, k, 0), 0.0)


def _shift_up(x, k):
    if k == 0:
        return x
    s = x.shape[0]
    return jnp.where(_iota(x.shape, 0) < s - k, pltpu.roll(x, s - k, 0), 0.0)


def _conv_pre(x, w_ref):
    c = w_ref[CONV_K - 1:CONV_K, :] * x
    for i in range(CONV_K - 1):
        c = c + w_ref[i:i + 1, :] * _shift_down(x, CONV_K - 1 - i)
    return c


def _conv_silu_fwd(proj, conv_w, n_batch):
    def body(x_ref, w_ref, y_ref):
        c = _conv_pre(x_ref[...], w_ref)
        y_ref[...] = c * jax.nn.sigmoid(c)

    return pl.pallas_call(
        body, name="conv_silu_fwd", grid=(n_batch, N_CONV_BLOCKS),
        in_specs=[pl.BlockSpec((SEQ, CONV_COLS), lambda b, j: (b, j)),
                  pl.BlockSpec((CONV_K, CONV_COLS), lambda b, j: (0, j))],
        out_specs=pl.BlockSpec((SEQ, CONV_COLS), lambda b, j: (b, j)),
        out_shape=jax.ShapeDtypeStruct((n_batch * SEQ, 3 * MIX_W), F32),
        compiler_params=_cparams(("parallel", "parallel")),
    )(proj, conv_w)


def _conv_silu_bwd(dy, proj, conv_w, dproj, n_batch):
    strip, halo = CONV_STRIP, 8
    n_strips = SEQ // strip

    def body(dy_ref, x_ref, w_ref, _, dx_ref, dw_ref, xpad, dcpad):
        @pl.when(pl.program_id(1) == 0)
        def _():
            dw_ref[...] = jnp.zeros_like(dw_ref)

        xpad[0:halo, :] = jnp.zeros((halo, CONV_COLS), F32)
        xpad[halo:, :] = x_ref[...]
        dcpad[SEQ:, :] = jnp.zeros((halo, CONV_COLS), F32)
        taps = [w_ref[i:i + 1, :] for i in range(CONV_K)]

        def first(s, dw):
            a = pl.multiple_of(s * strip, strip)
            win = xpad[pl.ds(a, strip + halo), :]
            xs = [(win if i == CONV_K - 1 else pltpu.roll(win, CONV_K - 1 - i, 0))[halo:] for i in range(CONV_K)]
            c = taps[0] * xs[0]
            for i in range(1, CONV_K):
                c = c + taps[i] * xs[i]
            sig = jax.nn.sigmoid(c)
            dc = dy_ref[pl.ds(a, strip), :] * (sig * (1.0 + c * (1.0 - sig)))
            dcpad[pl.ds(a, strip), :] = dc
            return tuple(dw[i] + jnp.sum(dc * xs[i], axis=0, keepdims=True) for i in range(CONV_K))

        dw = lax.fori_loop(0, n_strips, first, tuple(jnp.zeros((1, CONV_COLS), F32) for _ in range(CONV_K)))
        for i in range(CONV_K):
            dw_ref[i:i + 1, :] += dw[i]

        def second(s, carry):
            a = pl.multiple_of(s * strip, strip)
            win = dcpad[pl.ds(a, strip + halo), :]
            dx = taps[CONV_K - 1] * win[:strip]
            for i in range(CONV_K - 1):
                dx = dx + taps[i] * pltpu.roll(win, strip + halo - (CONV_K - 1 - i), 0)[:strip]
            dx_ref[pl.ds(a, strip), :] = dx.astype(BF16)
            return carry

        lax.fori_loop(0, n_strips, second, 0)

    return pl.pallas_call(
        body, name="conv_silu_bwd", grid=(N_CONV_BLOCKS, n_batch),
        in_specs=[pl.BlockSpec((SEQ, CONV_COLS), lambda j, b: (b, j)),
                  pl.BlockSpec((SEQ, CONV_COLS), lambda j, b: (b, j)),
                  pl.BlockSpec((CONV_K, CONV_COLS), lambda j, b: (0, j)), ANY],
        out_specs=[pl.BlockSpec((SEQ, CONV_COLS), lambda j, b: (b, j)),
                   pl.BlockSpec((CONV_K, CONV_COLS), lambda j, b: (0, j))],
        out_shape=[jax.ShapeDtypeStruct(dproj.shape, BF16),
                   jax.ShapeDtypeStruct((CONV_K, 3 * MIX_W), F32)],
        input_output_aliases={3: 0},
        scratch_shapes=[pltpu.VMEM((SEQ + 8, CONV_COLS), F32), pltpu.VMEM((SEQ + 8, CONV_COLS), F32)],
        compiler_params=_cparams(("parallel", "arbitrary")),
    )(dy, proj, conv_w, dproj)


@jax.custom_vjp
def _solve_apply(low, rhs, tinv):
    return _dot3(tinv, rhs)


def _solve_apply_fwd(low, rhs, tinv):
    sol = _dot3(tinv, rhs)
    return sol, (tinv, sol)


def _solve_apply_bwd(resid, g):
    tinv, sol = resid
    y = _dotbf(tinv, g, TN)
    return -_dotbf(y, sol, NT), y, jnp.zeros_like(tinv)


_solve_apply.defvjp(_solve_apply_fwd, _solve_apply_bwd)


def _inv_unit_lower(lows):
    c = lows[0].shape[0]
    eye = (_iota((c, c), 0) == _iota((c, c), 1)).astype(F32)
    ms = [-low for low in lows]
    ps = [eye + m for m in ms]
    for _ in range(5):
        ms = [_dot3(m, m) for m in ms]
        ps = [p + _dot3(p, m) for p, m in zip(ps, ms)]
    return ps


def _gdn_chunk(qs, ks, vs, gates, states, small, alog_row, dtb_row, gain_row, tinvs):
    c = small.shape[0]
    heads = range(N_LIN)
    lane = _iota((c, 128), 1)
    row, col = _iota((c, c), 0), _iota((c, c), 1)
    causal, strict = row >= col, row > col
    last = _iota((c, 1), 0) == c - 1

    beta_all = jax.nn.sigmoid(small)
    g_all = -jnp.exp(alog_row) * _softplus(small + dtb_row)
    gc_all = _dot01((col <= row).astype(F32), g_all, NN)

    beta = [jnp.sum(jnp.where(lane == h, beta_all, 0.0), axis=1, keepdims=True) for h in heads]
    gc = [jnp.sum(jnp.where(lane == N_LIN + h, gc_all, 0.0), axis=1, keepdims=True) for h in heads]
    gc_j = [_dot01((lane == N_LIN + h).astype(F32), gc_all, NT) for h in heads]
    decay = [jnp.where(causal, jnp.exp(jnp.where(causal, gc[h] - gc_j[h], 0.0)), 0.0) for h in heads]
    gc_last = [jnp.sum(jnp.where(last, gc[h], 0.0), axis=0, keepdims=True) for h in heads]
    egc = [jnp.exp(g) for g in gc]
    qn = [q * lax.rsqrt(jnp.sum(q * q, axis=-1, keepdims=True) + EPS) * (LIN_DH ** -0.5) for q in qs]
    kn = [k * lax.rsqrt(jnp.sum(k * k, axis=-1, keepdims=True) + EPS) for k in ks]
    kb = [kn[h] * beta[h] for h in heads]
    low = [jnp.where(strict, _dotbf(kb[h], kn[h], NT) * decay[h], 0.0) for h in heads]
    if tinvs is None:
        tinvs = _inv_unit_lower(low)
    u = [_solve_apply(low[h], vs[h] * beta[h], tinvs[h]) for h in heads]
    w = [_solve_apply(low[h], kb[h] * egc[h], tinvs[h]) for h in heads]
    intra = [_dotbf(qn[h], kn[h], NT) * decay[h] for h in heads]
    v_new = [u[h] - _dotbf(w[h], states[h]) for h in heads]
    o = [_dotbf(qn[h] * egc[h], states[h]) + _dotbf(intra[h], v_new[h]) for h in heads]
    new_states = [states[h] * jnp.exp(gc_last[h]) + _dotbf(kn[h] * jnp.exp(gc_last[h] - gc[h]), v_new[h], TN)
                  for h in heads]
    o = [x * lax.rsqrt(jnp.mean(x * x, axis=-1, keepdims=True) + EPS) * gain_row for x in o]
    outs = [o[h] * (gates[h] * jax.nn.sigmoid(gates[h])) for h in heads]
    return outs, new_states, tinvs


def _gdn_param_rows(a_log, dt_bias, onorm):
    row = lambda v: jnp.pad(v.reshape(1, N_LIN), ((0, 0), (N_LIN, 128 - 2 * N_LIN)))
    return row(a_log), row(dt_bias), onorm.reshape(1, LIN_DH)


def _head(ref_or_val, h):
    return ref_or_val[:, LIN_DH * h:LIN_DH * (h + 1)]


def _gdn_fwd(qkv, proj, alog_row, dtb_row, gain_row, n_batch, gather=None):
    nc = SEQ // CHUNK
    t = n_batch * SEQ
    steps = n_batch * nc

    def body(qkv_ref, small_ref, gate_ref, al_ref, dt_ref, gn_ref, *rest):
        if gather is None:
            mix_ref, st_ref, ti_ref, s_scr = rest
        else:
            w_ref, mix_ref, st_ref, ti_ref, out_ref, s_scr, send_sems, recv_sems = rest
            step = pl.program_id(0) * nc + pl.program_id(1)
            for at, phase in ((0, "start"), (3 * steps // 4, "forward"), (steps - 1, "finish")):
                @pl.when(step == at)
                def _(phase=phase):
                    getattr(_Gather(w_ref, out_ref, send_sems, recv_sems), phase)()

        @pl.when(pl.program_id(1) == 0)
        def _():
            s_scr[...] = jnp.zeros_like(s_scr)

        heads = range(N_LIN)
        states = [s_scr[h] for h in heads]
        outs, new_states, tinvs = _gdn_chunk(
            [_head(qkv_ref, h) for h in heads], [_head(qkv_ref, N_LIN + h) for h in heads],
            [_head(qkv_ref, 2 * N_LIN + h) for h in heads], [_head(gate_ref, h) for h in heads],
            states, small_ref[...], al_ref[...], dt_ref[...], gn_ref[...], None)
        mix_ref[...] = jnp.concatenate(outs, axis=1).astype(BF16)
        for h in heads:
            st_ref[0, 0, h] = states[h]
            s_scr[h] = new_states[h]
            ti_ref[0, 0, h] = tinvs[h]

    row = lambda b, n: b * nc + n
    vec = pl.BlockSpec((1, 128), lambda b, n: (0, 0))
    extra = [] if gather is None else [gather]
    return pl.pallas_call(
        body, name="gdn_fwd", grid=(n_batch, nc),
        in_specs=[pl.BlockSpec((CHUNK, 3 * MIX_W), lambda b, n: (row(b, n), 0)),
                  pl.BlockSpec((CHUNK, 128), lambda b, n: (row(b, n), SMALL_COL)),
                  pl.BlockSpec((CHUNK, MIX_W), lambda b, n: (row(b, n), 3)),
                  vec, vec, vec] + [ANY] * len(extra),
        out_specs=[pl.BlockSpec((CHUNK, MIX_W), lambda b, n: (row(b, n), 0)),
                   pl.BlockSpec((1, 1, N_LIN, LIN_DH, LIN_DH), lambda b, n: (b, n, 0, 0, 0)),
                   pl.BlockSpec((1, 1, N_LIN, CHUNK, CHUNK), lambda b, n: (b, n, 0, 0, 0))] + [ANY] * len(extra),
        out_shape=[jax.ShapeDtypeStruct((t, D_MODEL), BF16),
                   jax.ShapeDtypeStruct((n_batch, nc, N_LIN, LIN_DH, LIN_DH), F32),
                   jax.ShapeDtypeStruct((n_batch, nc, N_LIN, CHUNK, CHUNK), F32)]
                  + [jax.ShapeDtypeStruct(g.shape, g.dtype) for g in extra],
        input_output_aliases={6: 3} if extra else {},
        scratch_shapes=[pltpu.VMEM((N_LIN, LIN_DH, LIN_DH), F32)] + (GATHER_SEMS if extra else []),
        compiler_params=_cparams(("arbitrary", "arbitrary")),
    )(qkv, proj, proj, alog_row, dtb_row, gain_row, *extra)


def _gdn_bwd(dcat, qkv, proj, states, tinvs, alog_row, dtb_row, gain_row, n_batch, exchange=None):
    nc = SEQ // CHUNK
    t = n_batch * SEQ
    steps = n_batch * nc

    def body(dmix_ref, qkv_ref, small_ref, gate_ref, st_ref, ti_ref, al_ref, dt_ref, gn_ref, *rest):
        if exchange is None:
            dqkv_ref, dgate_ref, dsmall_ref, dal_ref, ddt_ref, dgn_ref, ds_scr = rest
        else:
            p_ref, dqkv_ref, dgate_ref, dsmall_ref, dal_ref, ddt_ref, dgn_ref, q_ref, ds_scr, send_sems, recv_sems = rest
            step = pl.program_id(0) * nc + pl.program_id(1)

            @pl.when(step == 0)
            def _():
                for cp in _exchange_copies(p_ref, q_ref, send_sems, recv_sems):
                    cp.start()

            @pl.when(step == steps - 1)
            def _():
                for cp in _exchange_copies(p_ref, q_ref, send_sems, recv_sems):
                    cp.wait()

        @pl.when(pl.program_id(1) == 0)
        def _():
            ds_scr[...] = jnp.zeros_like(ds_scr)

        @pl.when((pl.program_id(0) == 0) & (pl.program_id(1) == 0))
        def _():
            dal_ref[...] = jnp.zeros_like(dal_ref)
            ddt_ref[...] = jnp.zeros_like(ddt_ref)
            dgn_ref[...] = jnp.zeros_like(dgn_ref)

        heads = range(N_LIN)
        tinvs = [ti_ref[0, 0, h] for h in heads]

        def chunk(qs, ks, vs, gates, states_in, small, al, dt, gn):
            outs, new_states, _ = _gdn_chunk(qs, ks, vs, gates, states_in, small, al, dt, gn, tinvs)
            return tuple(outs), tuple(new_states)

        prim = (tuple(_head(qkv_ref, h) for h in heads),
                tuple(_head(qkv_ref, N_LIN + h) for h in heads),
                tuple(_head(qkv_ref, 2 * N_LIN + h) for h in heads),
                tuple(_head(gate_ref, h) for h in heads),
                tuple(st_ref[0, 0, h] for h in heads),
                small_ref[...], al_ref[...], dt_ref[...], gn_ref[...])
        _, vjp = jax.vjp(chunk, *prim)
        cot = (tuple(_head(dmix_ref, h) for h in heads), tuple(ds_scr[h] for h in heads))
        dq, dk, dv, dgate, dstate, dsmall, dal, ddt, dgn = vjp(cot)
        dqkv_ref[...] = jnp.concatenate(list(dq) + list(dk) + list(dv), axis=1)
        dgate_ref[...] = jnp.concatenate(list(dgate), axis=1).astype(BF16)
        dsmall_ref[...] = dsmall.astype(BF16)
        for h in heads:
            ds_scr[h] = dstate[h]
        dal_ref[...] += dal
        ddt_ref[...] += ddt
        dgn_ref[...] += dgn

    row = lambda b, n: b * nc + (nc - 1 - n)
    vec = pl.BlockSpec((1, 128), lambda b, n: (0, 0))
    extra = [] if exchange is None else [exchange]
    return pl.pallas_call(
        body, name="gdn_bwd", grid=(n_batch, nc),
        in_specs=[pl.BlockSpec((CHUNK, MIX_W), lambda b, n: (row(b, n), 0)),
                  pl.BlockSpec((CHUNK, 3 * MIX_W), lambda b, n: (row(b, n), 0)),
                  pl.BlockSpec((CHUNK, 128), lambda b, n: (row(b, n), SMALL_COL)),
                  pl.BlockSpec((CHUNK, MIX_W), lambda b, n: (row(b, n), 3)),
                  pl.BlockSpec((1, 1, N_LIN, LIN_DH, LIN_DH), lambda b, n: (b, nc - 1 - n, 0, 0, 0)),
                  pl.BlockSpec((1, 1, N_LIN, CHUNK, CHUNK), lambda b, n: (b, nc - 1 - n, 0, 0, 0)),
                  vec, vec, vec] + [ANY] * len(extra),
        out_specs=[pl.BlockSpec((CHUNK, 3 * MIX_W), lambda b, n: (row(b, n), 0)),
                   pl.BlockSpec((CHUNK, MIX_W), lambda b, n: (row(b, n), 3)),
                   pl.BlockSpec((CHUNK, 128), lambda b, n: (row(b, n), 0)),
                   vec, vec, vec] + [ANY] * len(extra),
        out_shape=[jax.ShapeDtypeStruct((t, 3 * MIX_W), F32),
                   jax.ShapeDtypeStruct((t, IN_A_PAD), BF16),
                   jax.ShapeDtypeStruct((t, 128), BF16),
                   jax.ShapeDtypeStruct((1, 128), F32),
                   jax.ShapeDtypeStruct((1, 128), F32),
                   jax.ShapeDtypeStruct((1, 128), F32)]
                  + [jax.ShapeDtypeStruct((3,) + p.shape[1:], p.dtype) for p in extra],
        scratch_shapes=[pltpu.VMEM((N_LIN, LIN_DH, LIN_DH), F32)] + (EXCHANGE_SEMS if extra else []),
        compiler_params=_cparams(("arbitrary", "arbitrary")),
    )(dcat, qkv, proj, proj, states, tinvs, alog_row, dtb_row, gain_row, *extra)


SB_T = 256


def _sb_masks():
    r, c = _iota((SB_T, SB_T), 0), _iota((SB_T, SB_T), 1)
    return r, c


def _staggered(chains):
    pending, live = list(chains), []
    while pending or live:
        if pending:
            live.append(pending.pop(0))
        for g in list(live):
            try:
                next(g)
            except StopIteration:
                live.remove(g)


def _sb_rows(kb):
    start = kb * SB_T
    return pl.ds(start if isinstance(kb, int) else pl.multiple_of(start, SB_T), SB_T)


def _sb_fwd(proj, n_batch):
    nq = SEQ // SB_T
    t = n_batch * SEQ
    scale = SB_DH ** -0.5
    both = range(2)

    def body(q_ref, k_ref, v_ref, o_ref, tot_ref, acc_scr, run_scr):
        qi = pl.program_id(2)
        lane = _iota((SB_T, 128), 1)
        r, c = _sb_masks()
        upper = (r > c).astype(BF16)
        q = q_ref[...] * scale
        qm = [jnp.where((lane < SB_DH) == (hh == 0), q, jnp.zeros_like(q)) for hh in both]

        def blocks(kbs, diag=None):
            first = diag is not None
            k_blk = [k_ref[_sb_rows(kb), :] for kb in kbs]
            v_blk = [v_ref[_sb_rows(kb), :] for kb in kbs]
            run = [None if first else run_scr[hh][:, 0:1] for hh in both]
            pv = {hh: [] for hh in both}
            rowsums = {hh: [] for hh in both}

            def chain(n, hh):
                z = lax.dot_general(qm[hh], k_blk[n], NT, preferred_element_type=F32)
                yield
                lb = _log_sigmoid(z)
                l1m = lb - z
                if n == diag:
                    l1m = jnp.where(r > c, l1m, 0.0)
                parts = _split(l1m)
                terms = ([] if first else [run[hh]]) + rowsums[hh]
                before = sum(terms[1:], terms[0]) if terms else None
                rowsums[hh].append(jnp.sum(l1m, axis=1, keepdims=True))
                yield
                tail = _dot_mask(parts, upper)
                yield
                a = jnp.exp(lb + (tail if before is None else before + tail))
                if n == diag:
                    a = jnp.where(r > c, a, 0.0)
                a = a.astype(BF16)
                yield
                pv[hh].append(lax.dot_general(a, v_blk[n], NN, preferred_element_type=F32))

            _staggered([chain(n, hh) for n in range(len(kbs)) for hh in both])
            for hh in both:
                if first:
                    acc_scr[hh] = sum(pv[hh][1:], pv[hh][0])
                    run_scr[hh] = jnp.broadcast_to(sum(rowsums[hh][1:], rowsums[hh][0]), (SB_T, 128))
                else:
                    acc_scr[hh] += sum(pv[hh][1:], pv[hh][0])
                    run_scr[hh] += sum(rowsums[hh][1:], rowsums[hh][0])

        @pl.when(qi == 0)
        def _():
            blocks([0], diag=0)

        @pl.when(qi >= 1)
        def _():
            blocks([qi, qi - 1], diag=0)

        rest = jnp.maximum(qi - 1, 0)

        def step(it, carry):
            kb = rest - 1 - 2 * it
            blocks([kb, kb - 1])
            return carry

        lax.fori_loop(0, rest >> 1, step, 0)

        @pl.when((rest & 1) == 1)
        def _():
            blocks([0])
        first = lane < SB_DH
        o_ref[...] = jnp.where(first, acc_scr[0], acc_scr[1]).astype(BF16)
        tot_ref[...] = jnp.where(first, run_scr[0], run_scr[1])

    nq_blocks = lambda b, p, i: (b * nq + i, p)
    seq_spec = lambda which: pl.BlockSpec((SEQ, 128), lambda b, p, i: (b, 3 * p + which))
    return pl.pallas_call(
        body, name="sb_fwd", grid=(n_batch, SB_PAIRS, nq),
        in_specs=[pl.BlockSpec((SB_T, 128), lambda b, p, i: (b * nq + i, 3 * p)), seq_spec(1), seq_spec(2)],
        out_specs=[pl.BlockSpec((SB_T, 128), nq_blocks), pl.BlockSpec((SB_T, 128), nq_blocks)],
        out_shape=[jax.ShapeDtypeStruct((t, D_MODEL), BF16),
                   jax.ShapeDtypeStruct((t, MIX_W), F32)],
        scratch_shapes=[pltpu.VMEM((2, SB_T, 128), F32), pltpu.VMEM((2, SB_T, 128), F32)],
        compiler_params=_cparams(("parallel", "parallel", "arbitrary")),
    )(proj, proj, proj)


def _sb_bwd(dcat, proj, totals, n_batch):
    nq = SEQ // SB_T
    t = n_batch * SEQ
    scale = SB_DH ** -0.5
    both = range(2)

    def body(do_ref, q_ref, k_ref, v_ref, tot_ref, dp_ref, dq_scr, run_scr, grun_scr, dk_ref, dv_ref):
        qi = pl.program_id(2)

        @pl.when(qi == 0)
        def _():
            dk_ref[...] = jnp.zeros_like(dk_ref)
            dv_ref[...] = jnp.zeros_like(dv_ref)

        lane = _iota((SB_T, 128), 1)
        r, c = _sb_masks()
        incl = (r <= c).astype(BF16)
        earlier = (r < c).astype(BF16)
        dq_scr[...] = jnp.zeros_like(dq_scr)
        run_scr[...] = jnp.zeros_like(run_scr)
        grun_scr[...] = jnp.zeros_like(grun_scr)
        q, do, tot = q_ref[...] * scale, do_ref[...], tot_ref[...]
        sel = [(lane < SB_DH) == (hh == 0) for hh in both]
        qm = [jnp.where(sel[hh], q, jnp.zeros_like(q)) for hh in both]
        dom = [jnp.where(sel[hh], do, 0.0).astype(BF16) for hh in both]
        total = [jnp.sum(jnp.where(lane == hh * SB_DH, tot, 0.0), axis=1, keepdims=True) for hh in both]

        def blocks(kbs, diag=None):
            k_blk = [k_ref[_sb_rows(kb), :] for kb in kbs]
            v_blk = [v_ref[_sb_rows(kb), :] for kb in kbs]
            run = [run_scr[hh][:, 0:1] for hh in both]
            grun = [grun_scr[hh][:, 0:1] for hh in both]
            rs_l, rs_e, dqp = ({hh: [] for hh in both} for _ in range(3))
            dk, dv = ([[] for _ in kbs] for _ in range(2))

            def plus(base, terms):
                return base if not terms else base + sum(terms[1:], terms[0])

            def chain(n, hh):
                z = lax.dot_general(qm[hh], k_blk[n], NT, preferred_element_type=F32)
                da = lax.dot_general(dom[hh], v_blk[n], NT, preferred_element_type=F32)
                yield
                lb = _log_sigmoid(z)
                sig = jnp.exp(lb)
                l1m = lb - z
                if n == diag:
                    l1m = jnp.where(r > c, l1m, 0.0)
                parts = _split(l1m)
                run_before = plus(run[hh], rs_l[hh])
                rs_l[hh].append(jnp.sum(l1m, axis=1, keepdims=True))
                yield
                prefix = run_before + _dot_mask(parts, incl)
                yield
                a = jnp.exp(lb + (total[hh] - prefix))
                if n == diag:
                    a = jnp.where(r > c, a, 0.0)
                de = a * da
                a = a.astype(BF16)
                parts = _split(de)
                grun_before = plus(grun[hh], rs_e[hh])
                rs_e[hh].append(jnp.sum(de, axis=1, keepdims=True))
                yield
                dv[n].append(lax.dot_general(a, dom[hh], TN, preferred_element_type=F32))
                dl1m = grun_before + _dot_mask(parts, earlier)
                yield
                if n == diag:
                    dl1m = jnp.where(r > c, dl1m, 0.0)
                dz = (de * (1.0 - sig) - dl1m * sig).astype(BF16)
                yield
                dqp[hh].append(lax.dot_general(dz, k_blk[n], NN, preferred_element_type=F32))
                dk[n].append(lax.dot_general(dz, qm[hh], TN, preferred_element_type=F32))

            _staggered([chain(n, hh) for n in range(len(kbs)) for hh in both])
            for hh in both:
                dq_scr[hh] += sum(dqp[hh][1:], dqp[hh][0])
                run_scr[hh] += sum(rs_l[hh][1:], rs_l[hh][0])
                grun_scr[hh] += sum(rs_e[hh][1:], rs_e[hh][0])
            for n, kb in enumerate(kbs):
                dk_ref[_sb_rows(kb), :] += dk[n][0] + dk[n][1]
                dv_ref[_sb_rows(kb), :] += dv[n][0] + dv[n][1]

        rest = jnp.maximum(qi - 1, 0)

        def step(it, carry):
            blocks([2 * it, 2 * it + 1])
            return carry

        lax.fori_loop(0, rest >> 1, step, 0)

        @pl.when((rest & 1) == 1)
        def _():
            blocks([rest - 1])

        @pl.when(qi == 0)
        def _():
            blocks([0], diag=0)

        @pl.when(qi >= 1)
        def _():
            blocks([qi - 1, qi], diag=1)

        dq = (jnp.where(sel[0], dq_scr[0], dq_scr[1]) * scale).astype(BF16)
        dp_ref[pl.ds(pl.multiple_of(qi * SB_T, SB_T), SB_T), 0:128] = dq

        @pl.when(qi == nq - 1)
        def _():
            dp_ref[:, 128:256] = dk_ref[...].astype(BF16)
            dp_ref[:, 256:384] = dv_ref[...].astype(BF16)

    q_blocks = lambda b, p, i: (b * nq + i, p)
    seq_spec = lambda which: pl.BlockSpec((SEQ, 128), lambda b, p, i: (b, 3 * p + which))
    return pl.pallas_call(
        body, name="sb_bwd", grid=(n_batch, SB_PAIRS, nq),
        in_specs=[pl.BlockSpec((SB_T, 128), q_blocks),
                  pl.BlockSpec((SB_T, 128), lambda b, p, i: (b * nq + i, 3 * p)),
                  seq_spec(1), seq_spec(2), pl.BlockSpec((SB_T, 128), q_blocks)],
        out_specs=pl.BlockSpec((SEQ, 384), lambda b, p, i: (b, p)),
        out_shape=jax.ShapeDtypeStruct((t, IN_B), BF16),
        scratch_shapes=[pltpu.VMEM((2, SB_T, 128), F32), pltpu.VMEM((2, SB_T, 128), F32),
                        pltpu.VMEM((2, SB_T, 128), F32), pltpu.VMEM((SEQ, 128), F32), pltpu.VMEM((SEQ, 128), F32)],
        compiler_params=_cparams(("parallel", "arbitrary", "arbitrary")),
    )(dcat, proj, proj, proj, totals)


MEM_TQ = 512


def _mem_heads(lane):
    return [(lane >= X_HEAD_DIM * h) & (lane < X_HEAD_DIM * (h + 1)) for h in range(N_X_HEADS)]


def _mem_attn_fwd(proj, q_col, memkv, cat, n_batch):
    nq = SEQ // MEM_TQ
    scale = X_HEAD_DIM ** -0.5

    def body(q_ref, kv_ref, _, o_ref):
        q = q_ref[...]
        k = kv_ref[:, :X_WIDTH].astype(BF16)
        v = kv_ref[:, X_WIDTH:].astype(BF16)
        out = jnp.zeros((MEM_TQ, X_WIDTH), F32)
        for sel in _mem_heads(_iota((MEM_TQ, X_WIDTH), 1)):
            s = lax.dot_general(jnp.where(sel, q, 0.0).astype(BF16), k, NT, preferred_element_type=F32) * scale
            e = jnp.exp(s - jnp.max(s, axis=-1, keepdims=True))
            p = e / jnp.sum(e, axis=-1, keepdims=True)
            out = out + jnp.where(sel, lax.dot_general(p.astype(BF16), v, NN, preferred_element_type=F32), 0.0)
        o_ref[...] = out.astype(BF16)

    return pl.pallas_call(
        body, name="mem_attn_fwd", grid=(n_batch, nq),
        in_specs=[pl.BlockSpec((MEM_TQ, X_WIDTH), lambda b, i: (b * nq + i, q_col)),
                  pl.BlockSpec((N_MEM, 2 * X_WIDTH), lambda b, i: (b, 0)), ANY],
        out_specs=pl.BlockSpec((MEM_TQ, X_WIDTH), lambda b, i: (b * nq + i, MIX_W // X_WIDTH)),
        out_shape=jax.ShapeDtypeStruct(cat.shape, BF16),
        input_output_aliases={2: 0},
        compiler_params=_cparams(("parallel", "parallel")),
    )(proj, memkv, cat)


def _mem_attn_bwd(dcat, proj, q_col, memkv, dproj, n_batch, tail=None):
    nq = SEQ // MEM_TQ
    scale = X_HEAD_DIM ** -0.5
    width = X_WIDTH + (0 if tail is None else 128)
    assert (q_col * X_WIDTH) % width == 0

    def body(do_ref, q_ref, kv_ref, *rest):
        dq_ref, dkv_ref = rest[-2:]

        @pl.when(pl.program_id(1) == 0)
        def _():
            dkv_ref[...] = jnp.zeros_like(dkv_ref)

        q, do = q_ref[...], do_ref[...]
        k = kv_ref[:, :X_WIDTH].astype(BF16)
        v = kv_ref[:, X_WIDTH:].astype(BF16)
        dq = jnp.zeros((MEM_TQ, X_WIDTH), F32)
        dk = jnp.zeros((N_MEM, X_WIDTH), F32)
        dv = jnp.zeros((N_MEM, X_WIDTH), F32)
        for sel in _mem_heads(_iota((MEM_TQ, X_WIDTH), 1)):
            qm = jnp.where(sel, q, 0.0).astype(BF16)
            dom = jnp.where(sel, do, 0.0).astype(BF16)
            s = lax.dot_general(qm, k, NT, preferred_element_type=F32) * scale
            e = jnp.exp(s - jnp.max(s, axis=-1, keepdims=True))
            p = e / jnp.sum(e, axis=-1, keepdims=True)
            dp = lax.dot_general(dom, v, NT, preferred_element_type=F32)
            ds = ((p * (dp - jnp.sum(dp * p, axis=-1, keepdims=True))) * scale).astype(BF16)
            dv = dv + lax.dot_general(p.astype(BF16), dom, TN, preferred_element_type=F32)
            dk = dk + lax.dot_general(ds, qm, TN, preferred_element_type=F32)
            dq = dq + jnp.where(sel, lax.dot_general(ds, k, NN, preferred_element_type=F32), 0.0)
        if tail is None:
            dq_ref[...] = dq.astype(BF16)
        else:
            dq_ref[...] = jnp.concatenate([dq.astype(BF16), rest[0][...]], axis=1)
        dkv_ref[...] += jnp.concatenate([dk, dv], axis=1)

    rows = lambda b, i: b * nq + i
    extra = [] if tail is None else [tail]
    return pl.pallas_call(
        body, name="mem_attn_bwd", grid=(n_batch, nq),
        in_specs=[pl.BlockSpec((MEM_TQ, X_WIDTH), lambda b, i: (rows(b, i), MIX_W // X_WIDTH)),
                  pl.BlockSpec((MEM_TQ, X_WIDTH), lambda b, i: (rows(b, i), q_col)),
                  pl.BlockSpec((N_MEM, 2 * X_WIDTH), lambda b, i: (b, 0))]
                 + [pl.BlockSpec((MEM_TQ, 128), lambda b, i: (rows(b, i), 0))] * len(extra) + [ANY],
        out_specs=[pl.BlockSpec((MEM_TQ, width), lambda b, i: (rows(b, i), q_col * X_WIDTH // width)),
                   pl.BlockSpec((N_MEM, 2 * X_WIDTH), lambda b, i: (b, 0))],
        out_shape=[jax.ShapeDtypeStruct(dproj.shape, BF16),
                   jax.ShapeDtypeStruct((n_batch * N_MEM, 2 * X_WIDTH), F32)],
        input_output_aliases={3 + len(extra): 0},
        compiler_params=_cparams(("parallel", "arbitrary")),
    )(dcat, proj, memkv, *extra, dproj)


def _relu2_epilogue(acc):
    r = jnp.maximum(acc, 0.0)
    return (r * r,)


def _relu2_bwd_epilogue(acc, a):
    return (acc * (2.0 * jnp.sqrt(a.astype(F32))),)


def _pad_in_a(w_in_a):
    w = 3 * MIX_W
    parts = [w_in_a[:, :w], w_in_a[:, w:w + MIX_W], w_in_a[:, IN_A - X_WIDTH:],
             w_in_a[:, w + MIX_W:w + MIX_W + 2 * N_LIN]]
    pad = jnp.zeros((D_MODEL, IN_A_PAD - IN_A), w_in_a.dtype)
    return jnp.concatenate(parts + [pad], axis=1)


def _unpad_in_a(g):
    w = 3 * MIX_W
    return jnp.concatenate([g[:, :w + MIX_W], g[:, w + MIX_W + X_WIDTH:w + MIX_W + X_WIDTH + 2 * N_LIN],
                            g[:, w + MIX_W:w + MIX_W + X_WIDTH]], axis=1)


def _qkv_to_pairs(w):
    w3 = 3 * MIX_W
    qkv = w[:, :w3].reshape(-1, 3, SB_PAIRS, 128).transpose(0, 2, 1, 3).reshape(-1, w3)
    return jnp.concatenate([qkv, w[:, w3:]], axis=1)


def _pairs_to_qkv(w):
    w3 = 3 * MIX_W
    qkv = w[:, :w3].reshape(-1, SB_PAIRS, 3, 128).transpose(0, 2, 1, 3).reshape(-1, w3)
    return jnp.concatenate([qkv, w[:, w3:]], axis=1)


def _local_step(x, mem, target, wts, small, comm=None):
    wts = dict(wts)
    t = x.shape[0]
    nb = t // SEQ
    npre, npost, mpre, mpost = small["norm_pre_mix"], small["norm_post_mix"], small["norm_pre_mlp"], small["norm_post_mlp"]
    alog_row, dtb_row, gain_row = _gdn_param_rows(small["a_log_a"][0], small["dt_bias_a"][0], small["onorm_a"][0])
    conv_w = small["conv_w"]

    mem_n = _rms_fwd(mem, small["mem_norm"], name="mem_norm_fwd", tile=256)
    saved = []
    h = _rms_fwd(x, npre[0], name="pre_mix_norm0")
    big = min(1024, t)
    for i in range(DEPTH):
        s = {"x_in": x, "h1": h}
        if i == 0:
            proj = _matmul(h, wts["in_a", None], mode="nn", tm=big, tn=1152, tk=1024, name="proj_a")
            qkv = _conv_silu_fwd(proj, conv_w, nb)
            if comm is None:
                mix, states, tinvs = _gdn_fwd(qkv, proj, alog_row, dtb_row, gain_row, nb)
            else:
                mix, states, tinvs, second = _gdn_fwd(qkv, proj, alog_row, dtb_row, gain_row, nb, gather=comm["slabs"])
                second = second.reshape(N_CHIPS, REGION_ROWS[W_SECOND], D_MODEL)
                wts.update(_as_operands(_unpack_region_full(W_SECOND, second)))
            s.update(qkv=qkv, states=states, tinvs=tinvs)
            q_col = (3 * MIX_W + MIX_W) // X_WIDTH
        else:
            proj = _matmul(h, wts["in_b", None], mode="nn", tm=big, tn=1280, tk=1024, name="proj_b", out_dtypes=(BF16,))
            mix, totals = _sb_fwd(proj, nb)
            s.update(totals=totals)
            q_col = 3 * MIX_W // X_WIDTH
        memkv = _matmul(mem_n, wts["mem_kv", i], mode="nn", tm=256, tn=512, tk=1024, name=f"memkv{i}")
        cat = _mem_attn_fwd(proj, q_col, memkv, mix, nb)
        y = _matmul(cat, wts["out", i], mode="nn", tm=big, tn=1024, tk=1024, name=f"out_proj{i}")
        x2, h2 = _post_norm_add(x, y, npost[i], mpre[i], name=f"post_mix{i}")
        a = _matmul(h2, wts["up", i], mode="nn", tm=big, tn=2048, tk=1024, name=f"up{i}",
                    out_dtypes=(BF16,), epilogue=_relu2_epilogue, n_outer=True)
        y2 = _matmul(a, wts["down", i], mode="nn", tm=big, tn=1024, tk=2048, name=f"down{i}")
        s.update(proj=proj, q_col=q_col, memkv=memkv, cat=cat, y=y, x2=x2, h2=h2, a=a, y2=y2)
        saved.append(s)
        if i + 1 < DEPTH:
            x, h = _post_norm_add(x2, y2, mpost[i], npre[i + 1], name=f"post_mlp{i}")
        else:
            loss_row, dx = _post_norm_loss(x2, y2, mpost[i], target, name="loss_head")

    gw = {}
    gs = {k: [None] * DEPTH for k in ("norm_pre_mix", "norm_post_mix", "norm_pre_mlp", "norm_post_mlp")}
    dmem_n, early = None, None
    for i in reversed(range(DEPTH)):
        s = saved[i]
        if i == DEPTH - 1:
            dy2, gs["norm_post_mlp"][i] = _rms_bwd(dx, s["y2"], mpost[i], name=f"post_mlp_bwd{i}", out_dtype=BF16)
        du = _matmul(dy2, wts["down", i], mode="nt", tm=big, tn=2048, tk=1024, name=f"down_dx{i}",
                     out_dtypes=(BF16,), epilogue=_relu2_bwd_epilogue, extras=(s["a"],), n_outer=True)
        gw["down", i] = _matmul(s["a"], dy2, mode="tn", tm=1024, tn=1024, tk=big, name=f"down_dw{i}")
        dh2 = _matmul(du, wts["up", i], mode="nt", tm=big, tn=1024, tk=2048, name=f"up_dx{i}")
        gw["up", i] = _matmul(s["h2"], du, mode="tn", tm=1024, tn=2048, tk=512, name=f"up_dw{i}")
        dx2, dy, gs["norm_pre_mlp"][i], gs["norm_post_mix"][i] = _rms_bwd_pair(
            dh2, s["x2"], mpre[i], dx, s["y"], npost[i], name=f"mlp_norms_bwd{i}")
        dcat = _matmul(dy, wts["out", i], mode="nt", tm=big, tn=1024, tk=1024, name=f"out_dx{i}")
        gw["out", i] = _matmul(s["cat"], dy, mode="tn", tm=1024, tn=1024, tk=big, name=f"out_dw{i}")
        if i == 0:
            exchange = None
            if comm is not None:
                own, exchange = _reduce_in_chip(_pack_region_full(G_EARLY, gw), comm["core"])
            res = _gdn_bwd(dcat, s["qkv"], s["proj"], s["states"], s["tinvs"], alog_row, dtb_row, gain_row, nb,
                           exchange=exchange)
            dqkv, dproj, dsmall, dalog, ddtb, dgain = res[:6]
            if comm is not None:
                early = (own, res[6])
            dproj, dconv = _conv_silu_bwd(dqkv, s["proj"], conv_w, dproj, nb)
            dproj, dmemkv = _mem_attn_bwd(dcat, s["proj"], s["q_col"], s["memkv"], dproj, nb, tail=dsmall)
            w_in, tile = wts["in_a", None], 1152
        else:
            dproj = _sb_bwd(dcat, s["proj"], s["totals"], nb)
            dproj, dmemkv = _mem_attn_bwd(dcat, s["proj"], s["q_col"], s["memkv"], dproj, nb)
            w_in, tile = wts["in_b", None], 1280
        dmemkv = dmemkv.astype(BF16)
        gw["mem_kv", i] = _matmul(mem_n, dmemkv, mode="tn", tm=1024, tn=512, tk=256, name=f"memkv_dw{i}")
        dmn = _matmul(dmemkv, wts["mem_kv", i], mode="nt", tm=256, tn=1024, tk=512, name=f"memkv_dx{i}")
        dmem_n = dmn if dmem_n is None else dmem_n + dmn
        dh1 = _matmul(dproj, w_in, mode="nt", tm=big, tn=1024, tk=tile, name=f"proj_dx{i}")
        g_in = _matmul(s["h1"], dproj, mode="tn", tm=1024, tn=tile, tk=big, name=f"proj_dw{i}")
        if i == 0:
            gw["in_a", None] = _unpad_in_a(g_in)
        else:
            gw["in_b", None] = _pairs_to_qkv(g_in)
        if i > 0:
            dx, dy2, gs["norm_pre_mix"][i], gs["norm_post_mlp"][i - 1] = _rms_bwd_pair(
                dh1, s["x_in"], npre[i], dx2, saved[i - 1]["y2"], mpost[i - 1], name=f"mix_norms_bwd{i}")
        else:
            dx, gs["norm_pre_mix"][i] = _rms_bwd(dh1, s["x_in"], npre[i], name=f"pre_mix_bwd{i}", res=dx2)

    _, g_mem_norm = _rms_bwd(dmem_n, mem, small["mem_norm"], name="mem_norm_bwd", tile=256)
    gsmall = {k: jnp.concatenate(v, axis=0) for k, v in gs.items()}
    gsmall.update(mem_norm=g_mem_norm[0], a_log_a=dalog[:, N_LIN:2 * N_LIN], dt_bias_a=ddtb[:, N_LIN:2 * N_LIN],
                  onorm_a=dgain, conv_w=dconv)
    return loss_row[0, 0], dx, gw, gsmall, early


SUM_TILE = 640


def _position():
    x, y, c = lax.axis_index("x"), lax.axis_index("y"), lax.axis_index("c")
    others = [(1 - x, y), (x, 1 - y), (1 - x, 1 - y)]
    return x, y, c, others


class _Gather:
    def __init__(self, w_ref, out_ref, send_sems, recv_sems):
        self.w, self.out, self.send, self.recv = w_ref, out_ref, send_sems, recv_sems
        self.x, self.y, self.c, self.others = _position()
        self.me = 2 * self.x + self.y

    def _copy(self, k, src, dst, to):
        return pltpu.make_async_remote_copy(src_ref=src, dst_ref=dst, send_sem=self.send.at[k],
                                            recv_sem=self.recv.at[k], device_id=to, device_id_type=MESH)

    def _first(self):
        return [self._copy(j, self.w.at[self.me, self.c], self.out.at[self.me, self.c], (ox, oy, self.c))
                for j, (ox, oy) in enumerate(self.others)]

    def _passed(self):
        sibling = (self.x, self.y, 1 - self.c)
        return [self._copy(3 + j, self.out.at[2 * ox + oy, self.c], self.out.at[2 * ox + oy, self.c], sibling)
                for j, (ox, oy) in enumerate(self.others)]

    def start(self):
        for cp in self._first():
            cp.start()

    def forward(self):
        passed = self._passed()
        for j, (ox, oy) in enumerate(self.others):
            self._copy(j, self.w.at[self.me, self.c], self.out.at[2 * ox + oy, self.c], (self.x, self.y, self.c)).wait_recv()
            passed[j].start()

    def finish(self):
        for j, (ox, oy) in enumerate(self.others):
            self._copy(3 + j, self.w.at[self.me, self.c], self.out.at[2 * ox + oy, 1 - self.c],
                       (self.x, self.y, self.c)).wait_recv()
        for cp in self._first() + self._passed():
            cp.wait_send()


GATHER_SEMS = [pltpu.SemaphoreType.DMA((6,)), pltpu.SemaphoreType.DMA((6,))]


def _gather_chips(wflat):
    def body(w_ref, out_ref, send_sems, recv_sems):
        g = _Gather(w_ref, out_ref, send_sems, recv_sems)
        g.start()
        g.forward()
        g.finish()

    return pl.pallas_call(
        body, name="gather_weights",
        in_specs=[ANY], out_specs=ANY, input_output_aliases={0: 0},
        out_shape=jax.ShapeDtypeStruct(wflat.shape, wflat.dtype),
        scratch_shapes=GATHER_SEMS,
    )(wflat)


def _gather_all(v, *, name):
    rows, n = v.shape

    def body(x_ref, out_ref, send_sems, recv_sems, local_sem):
        x, y, c, others = _position()
        me, sibling = (x, y, c), (x, y, 1 - c)

        def blk(px, py, pc):
            return out_ref.at[pl.ds((4 * px + 2 * py + pc) * rows, rows), :]

        def copy(k, block, to, src=None):
            return pltpu.make_async_remote_copy(src_ref=blk(*block) if src is None else src, dst_ref=blk(*block),
                                                send_sem=send_sems.at[k], recv_sem=recv_sems.at[k],
                                                device_id=to, device_id_type=MESH)

        mine = pltpu.make_async_copy(x_ref, blk(*me), local_sem)
        mine.start()
        first = [copy(0, me, sibling, src=x_ref)]
        first += [copy(1 + j, me, (*chip, c), src=x_ref) for j, chip in enumerate(others)]
        for cp in first:
            cp.start()
        passed = [copy(4 + j, (*chip, c), sibling) for j, chip in enumerate(others)]
        for j, chip in enumerate(others):
            copy(1 + j, (*chip, c), me).wait_recv()
            passed[j].start()
        copy(0, sibling, me).wait_recv()
        for j, chip in enumerate(others):
            copy(4 + j, (*chip, 1 - c), me).wait_recv()
        for cp in first + passed:
            cp.wait_send()
        mine.wait()

    vmem = pl.BlockSpec(memory_space=pltpu.VMEM)
    return pl.pallas_call(
        body, name=name, in_specs=[vmem], out_specs=vmem,
        out_shape=jax.ShapeDtypeStruct((8 * rows, n), v.dtype),
        scratch_shapes=[pltpu.SemaphoreType.DMA((7,)), pltpu.SemaphoreType.DMA((7,)), pltpu.SemaphoreType.DMA],
    )(v)


def _swap_halves(g5):
    def body(g_ref, out_ref, send_sem, recv_sem):
        x, y, c, _ = _position()
        cp = pltpu.make_async_remote_copy(src_ref=g_ref.at[:, 1 - c], dst_ref=out_ref, send_sem=send_sem,
                                          recv_sem=recv_sem, device_id=(x, y, 1 - c), device_id_type=MESH)
        cp.start()
        cp.wait()

    return pl.pallas_call(
        body, name="grad_swap_halves", in_specs=[ANY], out_specs=ANY,
        out_shape=jax.ShapeDtypeStruct((N_CHIPS, g5.shape[2], D_MODEL), g5.dtype),
        scratch_shapes=[pltpu.SemaphoreType.DMA, pltpu.SemaphoreType.DMA],
    )(g5)


def _add_halves(core, g5, got):
    def body(c_ref, a_ref, b_ref, o_ref, ob_ref):
        s = a_ref[0] + b_ref[...]
        o_ref[...] = s
        ob_ref[...] = s.astype(BF16)

    half = g5.shape[2]
    nt = half // SUM_TILE
    spec = pl.BlockSpec((1, SUM_TILE, D_MODEL), lambda s, i, c_ref: (s, i, 0))
    return pl.pallas_call(
        body, name="grad_add_halves",
        grid_spec=pltpu.PrefetchScalarGridSpec(
            num_scalar_prefetch=1, grid=(N_CHIPS, nt),
            in_specs=[pl.BlockSpec((1, 1, SUM_TILE, D_MODEL), lambda s, i, c_ref: (s, c_ref[0], i, 0)), spec],
            out_specs=[spec, spec]),
        out_shape=[jax.ShapeDtypeStruct((N_CHIPS, half, D_MODEL), F32),
                   jax.ShapeDtypeStruct((N_CHIPS, half, D_MODEL), BF16)],
        compiler_params=_cparams(("parallel", "parallel")),
    )(core, g5, got)


def _exchange_copies(p_ref, q_ref, send_sems, recv_sems):
    x, y, c, others = _position()
    return [pltpu.make_async_remote_copy(src_ref=p_ref.at[2 * ox + oy], dst_ref=q_ref.at[j],
                                         send_sem=send_sems.at[j], recv_sem=recv_sems.at[j],
                                         device_id=(ox, oy, c), device_id_type=MESH)
            for j, (ox, oy) in enumerate(others)]


EXCHANGE_SEMS = [pltpu.SemaphoreType.DMA((3,)), pltpu.SemaphoreType.DMA((3,))]


def _exchange_chips(p):
    def body(p_ref, q_ref, send_sems, recv_sems):
        copies = _exchange_copies(p_ref, q_ref, send_sems, recv_sems)
        for cp in copies:
            cp.start()
        for cp in copies:
            cp.wait()

    return pl.pallas_call(
        body, name="grad_exchange_chips", in_specs=[ANY], out_specs=ANY,
        out_shape=jax.ShapeDtypeStruct((3,) + p.shape[1:], p.dtype),
        scratch_shapes=EXCHANGE_SEMS,
    )(p)


def _add_chips(chip_core, p, q):
    def body(kc_ref, p_ref, q_ref, o_ref):
        o_ref[0] = ((p_ref[0] + q_ref[0].astype(F32)) + q_ref[1].astype(F32)) + q_ref[2].astype(F32)

    half = p.shape[1]
    nt = half // SUM_TILE
    return pl.pallas_call(
        body, name="grad_add_chips",
        grid_spec=pltpu.PrefetchScalarGridSpec(
            num_scalar_prefetch=1, grid=(nt,),
            in_specs=[pl.BlockSpec((1, SUM_TILE, D_MODEL), lambda i, kc_ref: (kc_ref[0], i, 0)),
                      pl.BlockSpec((3, SUM_TILE, D_MODEL), lambda i, kc_ref: (0, i, 0))],
            out_specs=pl.BlockSpec((1, SUM_TILE, D_MODEL), lambda i, kc_ref: (kc_ref[1], i, 0))),
        out_shape=jax.ShapeDtypeStruct((2, half, D_MODEL), F32),
        compiler_params=_cparams(("parallel",)),
    )(chip_core, p, q)


def _share_halves(halves):
    def body(h_ref, out_ref, send_sem, recv_sem):
        x, y, c, _ = _position()
        cp = pltpu.make_async_remote_copy(src_ref=h_ref.at[c], dst_ref=out_ref.at[c], send_sem=send_sem,
                                          recv_sem=recv_sem, device_id=(x, y, 1 - c), device_id_type=MESH)
        cp.start()
        pltpu.make_async_remote_copy(src_ref=h_ref.at[c], dst_ref=out_ref.at[1 - c], send_sem=send_sem,
                                     recv_sem=recv_sem, device_id=(x, y, c), device_id_type=MESH).wait_recv()
        cp.wait_send()

    return pl.pallas_call(
        body, name="grad_share_halves", in_specs=[ANY], out_specs=ANY, input_output_aliases={0: 0},
        out_shape=jax.ShapeDtypeStruct(halves.shape, halves.dtype),
        scratch_shapes=[pltpu.SemaphoreType.DMA, pltpu.SemaphoreType.DMA],
    )(halves)


def _reduce_in_chip(g_packed, core):
    rows = g_packed.shape[1]
    g5 = g_packed.reshape(N_CHIPS, 2, rows // 2, D_MODEL)
    return _add_halves(core.reshape(1), g5, _swap_halves(g5))


def _reduce_across_chips(p, q, chip, core):
    halves = _share_halves(_add_chips(jnp.stack([chip, core]), p, q))
    return halves.reshape(2 * halves.shape[1], D_MODEL)


def _reduce_scatter(g_packed, chip, core):
    p, p_bf = _reduce_in_chip(g_packed, core)
    return _reduce_across_chips(p, _exchange_chips(p_bf), chip, core)


def _slot(n):
    return -(-n // 16) * 16


def _pad_rows(a, axis):
    n = a.shape[axis]
    widths = [(0, 0)] * a.ndim
    widths[axis] = (0, _slot(n) - n)
    return jnp.pad(a, widths) if _slot(n) != n else a


W_FIRST = (("in_a", None),)
W_SECOND = (("mem_kv", 0), ("out", 0), ("up", 0), ("down", 0),
            ("in_b", None), ("mem_kv", 1), ("out", 1), ("up", 1), ("down", 1))
G_LATE = (("in_a", None), ("mem_kv", 0), ("out", 0))
G_EARLY = (("up", 0), ("down", 0), ("in_b", None), ("mem_kv", 1), ("out", 1), ("up", 1), ("down", 1))
REGION_ROWS = {W_FIRST: 896, W_SECOND: 5504, G_LATE: 1280, G_EARLY: 5120}
FULL_SHAPE = {"in_a": (D_MODEL, IN_A), "in_b": (D_MODEL, IN_B), "mem_kv": (D_MODEL, 2 * X_WIDTH),
              "out": (D_MODEL, D_MODEL), "up": (D_MODEL, D_FF), "down": (D_FF, D_MODEL)}
COLUMN_SHARDED = ("in_a", "in_b", "up")


def _shard_shape(name):
    r, c = FULL_SHAPE[name]
    return (r, c // N_CHIPS) if name in COLUMN_SHARDED else (r // N_CHIPS, c)


def _part_rows(name):
    r, c = _shard_shape(name)
    return r * c // D_MODEL


def _pack_region(region, part, dtype):
    rows = [_pad_rows(part(name, layer).reshape(-1, D_MODEL).astype(dtype), 0) for name, layer in region]
    used = sum(r.shape[0] for r in rows)
    return jnp.concatenate(rows + [jnp.zeros((REGION_ROWS[region] - used, D_MODEL), dtype)], axis=0)


def _unpack_region(region, flat):
    out, off = {}, 0
    for name, layer in region:
        n = _part_rows(name)
        out[name, layer] = flat[off:off + n].reshape(_shard_shape(name))
        off += _slot(n)
    return out


def _unpack_region_full(region, g):
    out, off = {}, 0
    for name, layer in region:
        n = _part_rows(name)
        piece = g[:, off:off + n].reshape((N_CHIPS,) + _shard_shape(name))
        if name in COLUMN_SHARDED:
            piece = piece.transpose(1, 0, 2)
        out[name, layer] = piece.reshape(FULL_SHAPE[name])
        off += _slot(n)
    return out


def _pack_region_full(region, full):
    s = N_CHIPS
    parts = []
    for name, layer in region:
        g = full[name, layer]
        if name in COLUMN_SHARDED:
            g = g.reshape(g.shape[0], s, -1).transpose(1, 0, 2)
        parts.append(_pad_rows(g.reshape(s, -1, D_MODEL), 1))
    used = sum(p.shape[1] for p in parts)
    return jnp.concatenate(parts + [jnp.zeros((s, REGION_ROWS[region] - used, D_MODEL), F32)], axis=1)


def _as_operands(full):
    out = dict(full)
    if ("in_a", None) in out:
        out["in_a", None] = _pad_in_a(out["in_a", None])
    if ("in_b", None) in out:
        out["in_b", None] = _qkv_to_pairs(out["in_b", None])
    return out


def _place(packed, chip):
    rows = packed.shape[0]
    slabs = lax.dynamic_update_slice(jnp.zeros((N_CHIPS, rows, D_MODEL), packed.dtype), packed[None], (chip, 0, 0))
    return slabs.reshape(N_CHIPS, 2, rows // 2, D_MODEL)


def _adamw_math(w, g, m, v):
    m = ADAM_B1 * m + (1.0 - ADAM_B1) * g
    v = ADAM_B2 * v + (1.0 - ADAM_B2) * (g * g)
    m_hat = m / (1.0 - ADAM_B1 ** ADAM_STEP)
    v_hat = v / (1.0 - ADAM_B2 ** ADAM_STEP)
    delta = -ADAM_LR * (m_hat / (jnp.sqrt(v_hat) + ADAM_EPS) + ADAM_WD * w)
    return delta, m, v


ADAM_TILE = 256


def _adamw(w, g, m, v, *, name):
    shape = w.shape
    cols = shape[-1]
    rows = w.size // cols
    tile = min(rows, ADAM_TILE)
    assert rows % tile == 0, (name, shape)

    def body(w_ref, g_ref, m_ref, v_ref, d_ref, nm_ref, nv_ref):
        d_ref[...], nm_ref[...], nv_ref[...] = _adamw_math(w_ref[...], g_ref[...], m_ref[...], v_ref[...])

    spec = pl.BlockSpec((tile, cols), lambda i: (i, 0))
    outs = pl.pallas_call(
        body, name=name, grid=(rows // tile,), in_specs=[spec] * 4, out_specs=[spec] * 3,
        out_shape=[jax.ShapeDtypeStruct((rows, cols), F32)] * 3,
        compiler_params=_cparams(("parallel",)),
    )(*[a.reshape(rows, cols) for a in (w, g, m, v)])
    return [o.reshape(shape) for o in outs]


SMALL_NAMES = (("mem_norm", 8), ("norm_pre_mix", 16), ("norm_post_mix", 16), ("norm_pre_mlp", 16),
               ("norm_post_mlp", 16), ("a_log_a", 1), ("dt_bias_a", 1), ("onorm_a", 1))
SMALL_ROWS = 80
CONV_ROWS = CONV_K * 3 * MIX_W // 128
SMALL_GRAD_ROWS = SMALL_ROWS + CONV_ROWS


def _pack_small(vals):
    rows = []
    for name, n in SMALL_NAMES:
        flat = vals[name].reshape(-1)
        rows.append(jnp.pad(flat, (0, n * 128 - flat.size)).reshape(n, 128))
    used = sum(n for _, n in SMALL_NAMES)
    return jnp.concatenate(rows + [jnp.zeros((SMALL_ROWS - used, 128), F32)], axis=0)


def _unpack_small(packed, like):
    out, off = {}, 0
    for name, n in SMALL_NAMES:
        size = like[name].size
        out[name] = packed[off:off + n].reshape(-1)[:size].reshape(like[name].shape)
        off += n
    return out


def _small_update(gathered, w, m, v):
    def body(g_ref, w_ref, m_ref, v_ref, gs_ref, d_ref, nm_ref, nv_ref):
        g = g_ref[0]
        for dev in range(1, 8):
            g = g + g_ref[dev]
        gs_ref[...] = g
        d_ref[...], nm_ref[...], nv_ref[...] = _adamw_math(w_ref[...], g[:SMALL_ROWS], m_ref[...], v_ref[...])

    small = jax.ShapeDtypeStruct((SMALL_ROWS, 128), F32)
    return pl.pallas_call(
        body, name="small_update",
        out_shape=[jax.ShapeDtypeStruct((SMALL_GRAD_ROWS, 128), F32), small, small, small],
    )(gathered.reshape(8, SMALL_GRAD_ROWS, 128), w, m, v)


def kernel(x, mem, mem_norm, norm_pre_mix, norm_post_mix, norm_pre_mlp, norm_post_mlp, w_in_a, conv_w_a, a_log_a, dt_bias_a, onorm_a, w_in_b, w_mem_kv, w_out, w_up, w_down, loss_target, m_mem_norm, m_norm_pre_mix, m_norm_post_mix, m_norm_pre_mlp, m_norm_post_mlp, m_w_in_a, m_conv_w_a, m_a_log_a, m_dt_bias_a, m_onorm_a, m_w_in_b, m_w_mem_kv, m_w_out, m_w_up, m_w_down, v_mem_norm, v_norm_pre_mix, v_norm_post_mix, v_norm_pre_mlp, v_norm_post_mlp, v_w_in_a, v_conv_w_a, v_a_log_a, v_dt_bias_a, v_onorm_a, v_w_in_b, v_w_mem_kv, v_w_out, v_w_up, v_w_down):
    nb = x.shape[0]
    chip = (2 * lax.axis_index("x") + lax.axis_index("y")).astype(jnp.int32)
    core = lax.axis_index("c").astype(jnp.int32)
    shards = {"in_a": w_in_a, "in_b": w_in_b, "mem_kv": w_mem_kv, "out": w_out, "up": w_up, "down": w_down}
    moments_m = {"in_a": m_w_in_a, "in_b": m_w_in_b, "mem_kv": m_w_mem_kv, "out": m_w_out, "up": m_w_up, "down": m_w_down}
    moments_v = {"in_a": v_w_in_a, "in_b": v_w_in_b, "mem_kv": v_w_mem_kv, "out": v_w_out, "up": v_w_up, "down": v_w_down}
    small_w = {"mem_norm": mem_norm, "norm_pre_mix": norm_pre_mix, "norm_post_mix": norm_post_mix,
               "norm_pre_mlp": norm_pre_mlp, "norm_post_mlp": norm_post_mlp, "a_log_a": a_log_a,
               "dt_bias_a": dt_bias_a, "onorm_a": onorm_a}
    small_m = {"mem_norm": m_mem_norm, "norm_pre_mix": m_norm_pre_mix, "norm_post_mix": m_norm_post_mix,
               "norm_pre_mlp": m_norm_pre_mlp, "norm_post_mlp": m_norm_post_mlp, "a_log_a": m_a_log_a,
               "dt_bias_a": m_dt_bias_a, "onorm_a": m_onorm_a}
    small_v = {"mem_norm": v_mem_norm, "norm_pre_mix": v_norm_pre_mix, "norm_post_mix": v_norm_post_mix,
               "norm_pre_mlp": v_norm_pre_mlp, "norm_post_mlp": v_norm_post_mlp, "a_log_a": v_a_log_a,
               "dt_bias_a": v_dt_bias_a, "onorm_a": v_onorm_a}

    def shard_part(name, layer):
        return shards[name][0 if layer is None else layer]

    first = _gather_chips(_place(_pack_region(W_FIRST, shard_part, BF16), chip))
    wts = _as_operands(_unpack_region_full(W_FIRST, first.reshape(N_CHIPS, REGION_ROWS[W_FIRST], D_MODEL)))
    comm = {"slabs": _place(_pack_region(W_SECOND, shard_part, BF16), chip), "core": core}
    conv_rows = CONV_ROWS // N_CHIPS
    conv_blk = jnp.pad(conv_w_a.reshape(conv_rows, 128), ((0, 24 - conv_rows), (0, 0)))
    conv_all = _gather_all(conv_blk, name="gather_conv").reshape(N_CHIPS, 2, 24, 128)[:, 0, :conv_rows]
    conv_full = conv_all.reshape(N_CHIPS, CONV_K, 3 * MIX_W // N_CHIPS).transpose(1, 0, 2).reshape(CONV_K, 3 * MIX_W)

    loss_local, dx, gw, gsmall, (own_early, others_early) = _local_step(
        x.reshape(nb * SEQ, D_MODEL), mem.reshape(nb * N_MEM, D_MODEL), loss_target.reshape(nb * SEQ, D_MODEL),
        wts, dict(small_w, conv_w=conv_full), comm)
    loss = lax.psum(loss_local, ("x", "y", "c"))
    grad_x = dx.reshape(nb, SEQ, D_MODEL)

    g_part = _unpack_region(G_EARLY, _reduce_across_chips(own_early, others_early, chip, core))
    g_part.update(_unpack_region(G_LATE, _reduce_scatter(_pack_region_full(G_LATE, gw), chip, core)))
    g_shard = {k: (g_part[k, None][None] if (k, None) in g_part else jnp.stack([g_part[k, i] for i in range(DEPTH)]))
               for k in shards}
    upd = {k: _adamw(shards[k], g_shard[k], moments_m[k], moments_v[k], name=f"adamw_{k}") for k in shards}

    g_rows = jnp.concatenate([_pack_small(gsmall), gsmall["conv_w"].reshape(CONV_ROWS, 128)], axis=0)
    g_all = _gather_all(g_rows, name="gather_small_grads")
    g_sum, d_small, nm_small, nv_small = _small_update(g_all, _pack_small(small_w), _pack_small(small_m), _pack_small(small_v))
    gs = _unpack_small(g_sum, small_w)
    ds, nms, nvs = (_unpack_small(p, small_w) for p in (d_small, nm_small, nv_small))
    cw = 3 * MIX_W // N_CHIPS
    g_conv = lax.dynamic_slice(g_sum[SMALL_ROWS:].reshape(CONV_K, 3 * MIX_W), (0, chip * cw), (CONV_K, cw)).reshape(conv_w_a.shape)
    d_conv, nm_conv, nv_conv = _adamw(conv_w_a, g_conv, m_conv_w_a, v_conv_w_a, name="adamw_conv")

    order = ("mem_norm", "norm_pre_mix", "norm_post_mix", "norm_pre_mlp", "norm_post_mlp", "in_a", "conv", "a_log_a",
             "dt_bias_a", "onorm_a", "in_b", "mem_kv", "out", "up", "down")
    grads = dict(gs, conv=g_conv, **g_shard)
    deltas = dict(ds, conv=d_conv, **{k: u[0] for k, u in upd.items()})
    new_m = dict(nms, conv=nm_conv, **{k: u[1] for k, u in upd.items()})
    new_v = dict(nvs, conv=nv_conv, **{k: u[2] for k, u in upd.items()})
    return (loss, grad_x, *[grads[k] for k in order], *[deltas[k] for k in order],
            *[new_m[k] for k in order], *[new_v[k] for k in order])
```

```python
import functools

import jax
import jax.numpy as jnp
from jax import lax
from jax.experimental import pallas as pl
from jax.experimental.pallas import tpu as pltpu

F32 = jnp.float32
BF16 = jnp.bfloat16
HIGHEST = lax.Precision.HIGHEST
MESH = pl.DeviceIdType.MESH

D_MODEL = 1024
SEQ = 2048
DEPTH = 2
X_WIDTH = 256
N_X_HEADS = 4
X_HEAD_DIM = 64
MIX_W = 768
LIN_DH = 128
N_LIN = 6
CONV_K = 4
CHUNK = 64
SB_DH = 64
SB_PAIRS = 6
N_MEM = 256
D_FF = 4096
EPS = 1e-6
IN_A = 3340
IN_A_PAD = 3456
IN_B = 2560
SMALL_COL = 26
N_CHIPS = 4

ADAM_LR, ADAM_B1, ADAM_B2, ADAM_EPS, ADAM_WD, ADAM_STEP = 0.001, 0.9, 0.999, 1e-08, 0.01, 10

VMEM_LIMIT = 48 * 1024 * 1024

ANY = pl.BlockSpec(memory_space=pl.ANY)

NN = (((1,), (0,)), ((), ()))
NT = (((1,), (1,)), ((), ()))
TN = (((0,), (0,)), ((), ()))


def _cparams(sem):
    return pltpu.CompilerParams(dimension_semantics=sem, vmem_limit_bytes=VMEM_LIMIT)


def _dotbf(a, b, dn=NN):
    return lax.dot_general(a.astype(BF16), b.astype(BF16), dn, preferred_element_type=F32)


def _split(a):
    hi = a.astype(BF16)
    lo = (a - hi.astype(F32)).astype(BF16)
    return hi, lo


def _dot3(a, b, dn=NN):
    ah, al = _split(a)
    bh, bl = _split(b)
    d = functools.partial(lax.dot_general, dimension_numbers=dn, preferred_element_type=F32)
    return d(ah, bh) + (d(ah, bl) + d(al, bh))


def _exact01(a, b, dn, mask_left):
    m, x = (a, b) if mask_left else (b, a)
    m = m.astype(BF16)
    hi = x.astype(BF16)
    r1 = x - hi.astype(F32)
    mid = r1.astype(BF16)
    lo = (r1 - mid.astype(F32)).astype(BF16)
    d = functools.partial(lax.dot_general, dimension_numbers=dn, preferred_element_type=F32)
    pair = (lambda p: d(m, p)) if mask_left else (lambda p: d(p, m))
    return pair(hi) + (pair(mid) + pair(lo))


@functools.partial(jax.custom_vjp, nondiff_argnums=(2,))
def _dot01(m01, x, dn):
    return _exact01(m01, x, dn, True)


def _dot01_fwd(m01, x, dn):
    return _exact01(m01, x, dn, True), m01


def _dot01_bwd(dn, m01, ct):
    dx = _exact01(m01, ct, TN, True) if dn == NN else _exact01(ct, m01, TN, False)
    return jnp.zeros_like(m01), dx


_dot01.defvjp(_dot01_fwd, _dot01_bwd)


def _dot_mask(parts, m01):
    d = functools.partial(lax.dot_general, dimension_numbers=NN, preferred_element_type=F32)
    return d(parts[0], m01) + d(parts[1], m01)


def _iota(shape, dim):
    return lax.broadcasted_iota(jnp.int32, shape, dim)


def _softplus(x):
    return jnp.maximum(x, 0.0) + jnp.log(1.0 + jnp.exp(-jnp.abs(x)))


def _log_sigmoid(z):
    return jnp.minimum(z, 0.0) - jnp.log(1.0 + jnp.exp(-jnp.abs(z)))


def _rms(x, g):
    r = lax.rsqrt(jnp.mean(x * x, axis=-1, keepdims=True) + EPS)
    return (x * r) * g


def _matmul(a, b, *, mode, tm, tn, tk, name, out_dtypes=(F32,), epilogue=None, extras=(), n_outer=False):
    if n_outer:
        ix = lambda f: (lambda j, i, kk: f(i, j, kk))
    else:
        ix = lambda f: f
    if mode == "nn":
        (m, k), (k2, n) = a.shape, b.shape
        a_spec = pl.BlockSpec((tm, tk), ix(lambda i, j, kk: (i, kk)))
        b_spec = pl.BlockSpec((tk, tn), ix(lambda i, j, kk: (kk, j)))
        dn = NN
    elif mode == "nt":
        (m, k), (n, k2) = a.shape, b.shape
        a_spec = pl.BlockSpec((tm, tk), ix(lambda i, j, kk: (i, kk)))
        b_spec = pl.BlockSpec((tn, tk), ix(lambda i, j, kk: (j, kk)))
        dn = NT
    else:
        (k, m), (k2, n) = a.shape, b.shape
        a_spec = pl.BlockSpec((tk, tm), ix(lambda i, j, kk: (kk, i)))
        b_spec = pl.BlockSpec((tk, tn), ix(lambda i, j, kk: (kk, j)))
        dn = TN
    assert k == k2 and m % tm == 0 and n % tn == 0 and k % tk == 0, (name, a.shape, b.shape)
    assert a.dtype == BF16 and b.dtype == BF16, name
    nk = k // tk
    n_extra, n_out = len(extras), len(out_dtypes)

    def finish(acc, extra_refs, out_refs):
        outs = (acc,) if epilogue is None else epilogue(acc, *[r[...] for r in extra_refs])
        for o_ref, o in zip(out_refs, outs):
            o_ref[...] = o.astype(o_ref.dtype)

    def body_single(a_ref, b_ref, *rest):
        acc = lax.dot_general(a_ref[...], b_ref[...], dn, preferred_element_type=F32)
        finish(acc, rest[:n_extra], rest[n_extra:n_extra + n_out])

    def body_tiled(a_ref, b_ref, *rest):
        extra_refs, out_refs, acc_ref = rest[:n_extra], rest[n_extra:n_extra + n_out], rest[-1]
        kk = pl.program_id(2)

        @pl.when(kk == 0)
        def _():
            acc_ref[...] = jnp.zeros_like(acc_ref)

        acc_ref[...] += lax.dot_general(a_ref[...], b_ref[...], dn, preferred_element_type=F32)

        @pl.when(kk == nk - 1)
        def _():
            finish(acc_ref[...], extra_refs, out_refs)

    mn_spec = pl.BlockSpec((tm, tn), ix(lambda i, j, kk: (i, j)))
    grid = (n // tn, m // tm, nk) if n_outer else (m // tm, n // tn, nk)
    outs = pl.pallas_call(
        body_single if nk == 1 else body_tiled,
        name=name,
        grid=grid,
        in_specs=[a_spec, b_spec] + [mn_spec] * n_extra,
        out_specs=[mn_spec] * n_out,
        out_shape=[jax.ShapeDtypeStruct((m, n), dt) for dt in out_dtypes],
        scratch_shapes=[] if nk == 1 else [pltpu.VMEM((tm, tn), F32)],
        compiler_params=_cparams(("parallel", "parallel", "arbitrary")),
    )(a, b, *extras)
    return outs[0] if n_out == 1 else outs


ROW_TILE = 512


def _row_spec(width=D_MODEL, tile=ROW_TILE):
    return pl.BlockSpec((tile, width), lambda i: (i, 0))


def _vec_spec(width=D_MODEL):
    return pl.BlockSpec((1, width), lambda i: (0, 0))


def _rms_fwd(x, g, *, name, tile=ROW_TILE):
    t = x.shape[0]

    def body(x_ref, g_ref, h_ref):
        h_ref[...] = _rms(x_ref[...], g_ref[...]).astype(BF16)

    return pl.pallas_call(
        body, name=name, grid=(t // tile,),
        in_specs=[_row_spec(tile=tile), _vec_spec()], out_specs=_row_spec(tile=tile),
        out_shape=jax.ShapeDtypeStruct((t, D_MODEL), BF16),
        compiler_params=_cparams(("parallel",)),
    )(x, g.reshape(1, D_MODEL))


def _post_norm_add(xres, y, g_post, g_next, *, name):
    t = xres.shape[0]

    def body(x_ref, y_ref, gp_ref, gn_ref, xo_ref, h_ref):
        xo = x_ref[...] + _rms(y_ref[...], gp_ref[...])
        xo_ref[...] = xo
        h_ref[...] = _rms(xo, gn_ref[...]).astype(BF16)

    return pl.pallas_call(
        body, name=name, grid=(t // ROW_TILE,),
        in_specs=[_row_spec(), _row_spec(), _vec_spec(), _vec_spec()],
        out_specs=[_row_spec(), _row_spec()],
        out_shape=[jax.ShapeDtypeStruct((t, D_MODEL), F32), jax.ShapeDtypeStruct((t, D_MODEL), BF16)],
        compiler_params=_cparams(("parallel",)),
    )(xres, y, g_post.reshape(1, D_MODEL), g_next.reshape(1, D_MODEL))


def _post_norm_loss(xres, y, g_post, target, *, name):
    t = xres.shape[0]

    def body(x_ref, y_ref, gp_ref, t_ref, loss_ref, dx_ref):
        @pl.when(pl.program_id(0) == 0)
        def _():
            loss_ref[...] = jnp.zeros_like(loss_ref)

        err = (x_ref[...] + _rms(y_ref[...], gp_ref[...])) - t_ref[...]
        per_tok = jnp.mean(err * err, axis=-1, keepdims=True)
        loss_ref[...] += 0.5 * jnp.sum(per_tok, axis=0, keepdims=True)
        dx_ref[...] = err * (1.0 / D_MODEL)

    return pl.pallas_call(
        body, name=name, grid=(t // ROW_TILE,),
        in_specs=[_row_spec(), _row_spec(), _vec_spec(), _row_spec()],
        out_specs=[pl.BlockSpec((1, 128), lambda i: (0, 0)), _row_spec()],
        out_shape=[jax.ShapeDtypeStruct((1, 128), F32), jax.ShapeDtypeStruct((t, D_MODEL), F32)],
        compiler_params=_cparams(("arbitrary",)),
    )(xres, y, g_post.reshape(1, D_MODEL), target)


def _rms_bwd(dy, x, g, *, name, res=None, out_dtype=F32, tile=ROW_TILE):
    t = x.shape[0]
    has_res = res is not None

    def body(dy_ref, x_ref, g_ref, *rest):
        res_ref = rest[0] if has_res else None
        dx_ref, dg_ref = rest[-2], rest[-1]

        @pl.when(pl.program_id(0) == 0)
        def _():
            dg_ref[...] = jnp.zeros_like(dg_ref)

        xf = x_ref[...]
        dyf = dy_ref[...].astype(F32)
        r = lax.rsqrt(jnp.mean(xf * xf, axis=-1, keepdims=True) + EPS)
        xhat = xf * r
        dg_ref[...] += jnp.sum(dyf * xhat, axis=0, keepdims=True)
        dxh = dyf * g_ref[...]
        dx = r * (dxh - xhat * jnp.mean(dxh * xhat, axis=-1, keepdims=True))
        if has_res:
            dx = dx + res_ref[...]
        dx_ref[...] = dx.astype(dx_ref.dtype)

    args = [dy, x, g.reshape(1, D_MODEL)] + ([res] if has_res else [])
    return pl.pallas_call(
        body, name=name, grid=(t // tile,),
        in_specs=[_row_spec(tile=tile), _row_spec(tile=tile), _vec_spec()] + ([_row_spec(tile=tile)] if has_res else []),
        out_specs=[_row_spec(tile=tile), _vec_spec()],
        out_shape=[jax.ShapeDtypeStruct((t, D_MODEL), out_dtype), jax.ShapeDtypeStruct((1, D_MODEL), F32)],
        compiler_params=_cparams(("arbitrary",)),
    )(*args)


def _rms_bwd_pair(dh, x, g_pre, res, y, g_post, *, name):
    t = x.shape[0]

    def norm_bwd(dy, xf, g):
        r = lax.rsqrt(jnp.mean(xf * xf, axis=-1, keepdims=True) + EPS)
        xhat = xf * r
        dxh = dy * g
        dx = r * (dxh - xhat * jnp.mean(dxh * xhat, axis=-1, keepdims=True))
        return dx, jnp.sum(dy * xhat, axis=0, keepdims=True)

    def body(dh_ref, x_ref, gp_ref, res_ref, y_ref, gq_ref, dx_ref, dy_ref, dgp_ref, dgq_ref):
        @pl.when(pl.program_id(0) == 0)
        def _():
            dgp_ref[...] = jnp.zeros_like(dgp_ref)
            dgq_ref[...] = jnp.zeros_like(dgq_ref)

        dx, dgp = norm_bwd(dh_ref[...], x_ref[...], gp_ref[...])
        dx = dx + res_ref[...]
        dx_ref[...] = dx
        dy, dgq = norm_bwd(dx, y_ref[...], gq_ref[...])
        dy_ref[...] = dy.astype(BF16)
        dgp_ref[...] += dgp
        dgq_ref[...] += dgq

    return pl.pallas_call(
        body, name=name, grid=(t // ROW_TILE,),
        in_specs=[_row_spec(), _row_spec(), _vec_spec(), _row_spec(), _row_spec(), _vec_spec()],
        out_specs=[_row_spec(), _row_spec(), _vec_spec(), _vec_spec()],
        out_shape=[jax.ShapeDtypeStruct((t, D_MODEL), F32), jax.ShapeDtypeStruct((t, D_MODEL), BF16),
                   jax.ShapeDtypeStruct((1, D_MODEL), F32), jax.ShapeDtypeStruct((1, D_MODEL), F32)],
        compiler_params=_cparams(("arbitrary",)),
    )(dh, x, g_pre.reshape(1, D_MODEL), res, y, g_post.reshape(1, D_MODEL))


CONV_COLS = 256
N_CONV_BLOCKS = 3 * MIX_W // CONV_COLS
CONV_STRIP = 128


def _shift_down(x, k):
    if k == 0:
        return x
    return jnp.where(_iota(x.shape, 0) >= k, pltpu.roll(x, k, 0), 0.0)


def _shift_up(x, k):
    if k == 0:
        return x
    s = x.shape[0]
    return jnp.where(_iota(x.shape, 0) < s - k, pltpu.roll(x, s - k, 0), 0.0)


def _conv_pre(x, w_ref):
    c = w_ref[CONV_K - 1:CONV_K, :] * x
    for i in range(CONV_K - 1):
        c = c + w_ref[i:i + 1, :] * _shift_down(x, CONV_K - 1 - i)
    return c


def _conv_silu_fwd(proj, conv_w, n_batch):
    def body(x_ref, w_ref, y_ref):
        c = _conv_pre(x_ref[...], w_ref)
        y_ref[...] = c * jax.nn.sigmoid(c)

    return pl.pallas_call(
        body, name="conv_silu_fwd", grid=(n_batch, N_CONV_BLOCKS),
        in_specs=[pl.BlockSpec((SEQ, CONV_COLS), lambda b, j: (b, j)),
                  pl.BlockSpec((CONV_K, CONV_COLS), lambda b, j: (0, j))],
        out_specs=pl.BlockSpec((SEQ, CONV_COLS), lambda b, j: (b, j)),
        out_shape=jax.ShapeDtypeStruct((n_batch * SEQ, 3 * MIX_W), F32),
        compiler_params=_cparams(("parallel", "parallel")),
    )(proj, conv_w)


def _conv_silu_bwd(dy, proj, conv_w, dproj, n_batch):
    strip, halo = CONV_STRIP, 8
    n_strips = SEQ // strip

    def body(dy_ref, x_ref, w_ref, _, dx_ref, dw_ref, xpad, dcpad):
        @pl.when(pl.program_id(1) == 0)
        def _():
            dw_ref[...] = jnp.zeros_like(dw_ref)

        xpad[0:halo, :] = jnp.zeros((halo, CONV_COLS), F32)
        xpad[halo:, :] = x_ref[...]
        dcpad[SEQ:, :] = jnp.zeros((halo, CONV_COLS), F32)
        taps = [w_ref[i:i + 1, :] for i in range(CONV_K)]

        def first(s, dw):
            a = pl.multiple_of(s * strip, strip)
            win = xpad[pl.ds(a, strip + halo), :]
            xs = [(win if i == CONV_K - 1 else pltpu.roll(win, CONV_K - 1 - i, 0))[halo:] for i in range(CONV_K)]
            c = taps[0] * xs[0]
            for i in range(1, CONV_K):
                c = c + taps[i] * xs[i]
            sig = jax.nn.sigmoid(c)
            dc = dy_ref[pl.ds(a, strip), :] * (sig * (1.0 + c * (1.0 - sig)))
            dcpad[pl.ds(a, strip), :] = dc
            return tuple(dw[i] + jnp.sum(dc * xs[i], axis=0, keepdims=True) for i in range(CONV_K))

        dw = lax.fori_loop(0, n_strips, first, tuple(jnp.zeros((1, CONV_COLS), F32) for _ in range(CONV_K)))
        for i in range(CONV_K):
            dw_ref[i:i + 1, :] += dw[i]

        def second(s, carry):
            a = pl.multiple_of(s * strip, strip)
            win = dcpad[pl.ds(a, strip + halo), :]
            dx = taps[CONV_K - 1] * win[:strip]
            for i in range(CONV_K - 1):
                dx = dx + taps[i] * pltpu.roll(win, strip + halo - (CONV_K - 1 - i), 0)[:strip]
            dx_ref[pl.ds(a, strip), :] = dx.astype(BF16)
            return carry

        lax.fori_loop(0, n_strips, second, 0)

    return pl.pallas_call(
        body, name="conv_silu_bwd", grid=(N_CONV_BLOCKS, n_batch),
        in_specs=[pl.BlockSpec((SEQ, CONV_COLS), lambda j, b: (b, j)),
                  pl.BlockSpec((SEQ, CONV_COLS), lambda j, b: (b, j)),
                  pl.BlockSpec((CONV_K, CONV_COLS), lambda j, b: (0, j)), ANY],
        out_specs=[pl.BlockSpec((SEQ, CONV_COLS), lambda j, b: (b, j)),
                   pl.BlockSpec((CONV_K, CONV_COLS), lambda j, b: (0, j))],
        out_shape=[jax.ShapeDtypeStruct(dproj.shape, BF16),
                   jax.ShapeDtypeStruct((CONV_K, 3 * MIX_W), F32)],
        input_output_aliases={3: 0},
        scratch_shapes=[pltpu.VMEM((SEQ + 8, CONV_COLS), F32), pltpu.VMEM((SEQ + 8, CONV_COLS), F32)],
        compiler_params=_cparams(("parallel", "arbitrary")),
    )(dy, proj, conv_w, dproj)


@jax.custom_vjp
def _solve_apply(low, rhs, tinv):
    return _dot3(tinv, rhs)


def _solve_apply_fwd(low, rhs, tinv):
    sol = _dot3(tinv, rhs)
    return sol, (tinv, sol)


def _solve_apply_bwd(resid, g):
    tinv, sol = resid
    y = _dotbf(tinv, g, TN)
    return -_dotbf(y, sol, NT), y, jnp.zeros_like(tinv)


_solve_apply.defvjp(_solve_apply_fwd, _solve_apply_bwd)


def _inv_unit_lower(lows):
    c = lows[0].shape[0]
    eye = (_iota((c, c), 0) == _iota((c, c), 1)).astype(F32)
    ms = [-low for low in lows]
    ps = [eye + m for m in ms]
    for _ in range(5):
        ms = [_dot3(m, m) for m in ms]
        ps = [p + _dot3(p, m) for p, m in zip(ps, ms)]
    return ps


def _gdn_chunk(qs, ks, vs, gates, states, small, alog_row, dtb_row, gain_row, tinvs):
    c = small.shape[0]
    heads = range(N_LIN)
    lane = _iota((c, 128), 1)
    row, col = _iota((c, c), 0), _iota((c, c), 1)
    causal, strict = row >= col, row > col
    last = _iota((c, 1), 0) == c - 1

    beta_all = jax.nn.sigmoid(small)
    g_all = -jnp.exp(alog_row) * _softplus(small + dtb_row)
    gc_all = _dot01((col <= row).astype(F32), g_all, NN)

    beta = [jnp.sum(jnp.where(lane == h, beta_all, 0.0), axis=1, keepdims=True) for h in heads]
    gc = [jnp.sum(jnp.where(lane == N_LIN + h, gc_all, 0.0), axis=1, keepdims=True) for h in heads]
    gc_j = [_dot01((lane == N_LIN + h).astype(F32), gc_all, NT) for h in heads]
    decay = [jnp.where(causal, jnp.exp(jnp.where(causal, gc[h] - gc_j[h], 0.0)), 0.0) for h in heads]
    gc_last = [jnp.sum(jnp.where(last, gc[h], 0.0), axis=0, keepdims=True) for h in heads]
    egc = [jnp.exp(g) for g in gc]
    qn = [q * lax.rsqrt(jnp.sum(q * q, axis=-1, keepdims=True) + EPS) * (LIN_DH ** -0.5) for q in qs]
    kn = [k * lax.rsqrt(jnp.sum(k * k, axis=-1, keepdims=True) + EPS) for k in ks]
    kb = [kn[h] * beta[h] for h in heads]
    low = [jnp.where(strict, _dotbf(kb[h], kn[h], NT) * decay[h], 0.0) for h in heads]
    if tinvs is None:
        tinvs = _inv_unit_lower(low)
    u = [_solve_apply(low[h], vs[h] * beta[h], tinvs[h]) for h in heads]
    w = [_solve_apply(low[h], kb[h] * egc[h], tinvs[h]) for h in heads]
    intra = [_dotbf(qn[h], kn[h], NT) * decay[h] for h in heads]
    v_new = [u[h] - _dotbf(w[h], states[h]) for h in heads]
    o = [_dotbf(qn[h] * egc[h], states[h]) + _dotbf(intra[h], v_new[h]) for h in heads]
    new_states = [states[h] * jnp.exp(gc_last[h]) + _dotbf(kn[h] * jnp.exp(gc_last[h] - gc[h]), v_new[h], TN)
                  for h in heads]
    o = [x * lax.rsqrt(jnp.mean(x * x, axis=-1, keepdims=True) + EPS) * gain_row for x in o]
    outs = [o[h] * (gates[h] * jax.nn.sigmoid(gates[h])) for h in heads]
    return outs, new_states, tinvs


def _gdn_param_rows(a_log, dt_bias, onorm):
    row = lambda v: jnp.pad(v.reshape(1, N_LIN), ((0, 0), (N_LIN, 128 - 2 * N_LIN)))
    return row(a_log), row(dt_bias), onorm.reshape(1, LIN_DH)


def _head(ref_or_val, h):
    return ref_or_val[:, LIN_DH * h:LIN_DH * (h + 1)]


def _gdn_fwd(qkv, proj, alog_row, dtb_row, gain_row, n_batch, gather=None):
    nc = SEQ // CHUNK
    t = n_batch * SEQ
    steps = n_batch * nc

    def body(qkv_ref, small_ref, gate_ref, al_ref, dt_ref, gn_ref, *rest):
        if gather is None:
            mix_ref, st_ref, ti_ref, s_scr = rest
        else:
            w_ref, mix_ref, st_ref, ti_ref, out_ref, s_scr, send_sems, recv_sems = rest
            step = pl.program_id(0) * nc + pl.program_id(1)
            for at, phase in ((0, "start"), (3 * steps // 4, "forward"), (steps - 1, "finish")):
                @pl.when(step == at)
                def _(phase=phase):
                    getattr(_Gather(w_ref, out_ref, send_sems, recv_sems), phase)()

        @pl.when(pl.program_id(1) == 0)
        def _():
            s_scr[...] = jnp.zeros_like(s_scr)

        heads = range(N_LIN)
        states = [s_scr[h] for h in heads]
        outs, new_states, tinvs = _gdn_chunk(
            [_head(qkv_ref, h) for h in heads], [_head(qkv_ref, N_LIN + h) for h in heads],
            [_head(qkv_ref, 2 * N_LIN + h) for h in heads], [_head(gate_ref, h) for h in heads],
            states, small_ref[...], al_ref[...], dt_ref[...], gn_ref[...], None)
        mix_ref[...] = jnp.concatenate(outs, axis=1).astype(BF16)
        for h in heads:
            st_ref[0, 0, h] = states[h]
            s_scr[h] = new_states[h]
            ti_ref[0, 0, h] = tinvs[h]

    row = lambda b, n: b * nc + n
    vec = pl.BlockSpec((1, 128), lambda b, n: (0, 0))
    extra = [] if gather is None else [gather]
    return pl.pallas_call(
        body, name="gdn_fwd", grid=(n_batch, nc),
        in_specs=[pl.BlockSpec((CHUNK, 3 * MIX_W), lambda b, n: (row(b, n), 0)),
                  pl.BlockSpec((CHUNK, 128), lambda b, n: (row(b, n), SMALL_COL)),
                  pl.BlockSpec((CHUNK, MIX_W), lambda b, n: (row(b, n), 3)),
                  vec, vec, vec] + [ANY] * len(extra),
        out_specs=[pl.BlockSpec((CHUNK, MIX_W), lambda b, n: (row(b, n), 0)),
                   pl.BlockSpec((1, 1, N_LIN, LIN_DH, LIN_DH), lambda b, n: (b, n, 0, 0, 0)),
                   pl.BlockSpec((1, 1, N_LIN, CHUNK, CHUNK), lambda b, n: (b, n, 0, 0, 0))] + [ANY] * len(extra),
        out_shape=[jax.ShapeDtypeStruct((t, D_MODEL), BF16),
                   jax.ShapeDtypeStruct((n_batch, nc, N_LIN, LIN_DH, LIN_DH), F32),
                   jax.ShapeDtypeStruct((n_batch, nc, N_LIN, CHUNK, CHUNK), F32)]
                  + [jax.ShapeDtypeStruct(g.shape, g.dtype) for g in extra],
        input_output_aliases={6: 3} if extra else {},
        scratch_shapes=[pltpu.VMEM((N_LIN, LIN_DH, LIN_DH), F32)] + (GATHER_SEMS if extra else []),
        compiler_params=_cparams(("arbitrary", "arbitrary")),
    )(qkv, proj, proj, alog_row, dtb_row, gain_row, *extra)


def _gdn_bwd(dcat, qkv, proj, states, tinvs, alog_row, dtb_row, gain_row, n_batch, exchange=None):
    nc = SEQ // CHUNK
    t = n_batch * SEQ
    steps = n_batch * nc

    def body(dmix_ref, qkv_ref, small_ref, gate_ref, st_ref, ti_ref, al_ref, dt_ref, gn_ref, *rest):
        if exchange is None:
            dqkv_ref, dgate_ref, dsmall_ref, dal_ref, ddt_ref, dgn_ref, ds_scr = rest
        else:
            p_ref, dqkv_ref, dgate_ref, dsmall_ref, dal_ref, ddt_ref, dgn_ref, q_ref, ds_scr, send_sems, recv_sems = rest
            step = pl.program_id(0) * nc + pl.program_id(1)

            @pl.when(step == 0)
            def _():
                for cp in _exchange_copies(p_ref, q_ref, send_sems, recv_sems):
                    cp.start()

            @pl.when(step == steps - 1)
            def _():
                for cp in _exchange_copies(p_ref, q_ref, send_sems, recv_sems):
                    cp.wait()

        @pl.when(pl.program_id(1) == 0)
        def _():
            ds_scr[...] = jnp.zeros_like(ds_scr)

        @pl.when((pl.program_id(0) == 0) & (pl.program_id(1) == 0))
        def _():
            dal_ref[...] = jnp.zeros_like(dal_ref)
            ddt_ref[...] = jnp.zeros_like(ddt_ref)
            dgn_ref[...] = jnp.zeros_like(dgn_ref)

        heads = range(N_LIN)
        tinvs = [ti_ref[0, 0, h] for h in heads]

        def chunk(qs, ks, vs, gates, states_in, small, al, dt, gn):
            outs, new_states, _ = _gdn_chunk(qs, ks, vs, gates, states_in, small, al, dt, gn, tinvs)
            return tuple(outs), tuple(new_states)

        prim = (tuple(_head(qkv_ref, h) for h in heads),
                tuple(_head(qkv_ref, N_LIN + h) for h in heads),
                tuple(_head(qkv_ref, 2 * N_LIN + h) for h in heads),
                tuple(_head(gate_ref, h) for h in heads),
                tuple(st_ref[0, 0, h] for h in heads),
                small_ref[...], al_ref[...], dt_ref[...], gn_ref[...])
        _, vjp = jax.vjp(chunk, *prim)
        cot = (tuple(_head(dmix_ref, h) for h in heads), tuple(ds_scr[h] for h in heads))
        dq, dk, dv, dgate, dstate, dsmall, dal, ddt, dgn = vjp(cot)
        dqkv_ref[...] = jnp.concatenate(list(dq) + list(dk) + list(dv), axis=1)
        dgate_ref[...] = jnp.concatenate(list(dgate), axis=1).astype(BF16)
        dsmall_ref[...] = dsmall.astype(BF16)
        for h in heads:
            ds_scr[h] = dstate[h]
        dal_ref[...] += dal
        ddt_ref[...] += ddt
        dgn_ref[...] += dgn

    row = lambda b, n: b * nc + (nc - 1 - n)
    vec = pl.BlockSpec((1, 128), lambda b, n: (0, 0))
    extra = [] if exchange is None else [exchange]
    return pl.pallas_call(
        body, name="gdn_bwd", grid=(n_batch, nc),
        in_specs=[pl.BlockSpec((CHUNK, MIX_W), lambda b, n: (row(b, n), 0)),
                  pl.BlockSpec((CHUNK, 3 * MIX_W), lambda b, n: (row(b, n), 0)),
                  pl.BlockSpec((CHUNK, 128), lambda b, n: (row(b, n), SMALL_COL)),
                  pl.BlockSpec((CHUNK, MIX_W), lambda b, n: (row(b, n), 3)),
                  pl.BlockSpec((1, 1, N_LIN, LIN_DH, LIN_DH), lambda b, n: (b, nc - 1 - n, 0, 0, 0)),
                  pl.BlockSpec((1, 1, N_LIN, CHUNK, CHUNK), lambda b, n: (b, nc - 1 - n, 0, 0, 0)),
                  vec, vec, vec] + [ANY] * len(extra),
        out_specs=[pl.BlockSpec((CHUNK, 3 * MIX_W), lambda b, n: (row(b, n), 0)),
                   pl.BlockSpec((CHUNK, MIX_W), lambda b, n: (row(b, n), 3)),
                   pl.BlockSpec((CHUNK, 128), lambda b, n: (row(b, n), 0)),
                   vec, vec, vec] + [ANY] * len(extra),
        out_shape=[jax.ShapeDtypeStruct((t, 3 * MIX_W), F32),
                   jax.ShapeDtypeStruct((t, IN_A_PAD), BF16),
                   jax.ShapeDtypeStruct((t, 128), BF16),
                   jax.ShapeDtypeStruct((1, 128), F32),
                   jax.ShapeDtypeStruct((1, 128), F32),
                   jax.ShapeDtypeStruct((1, 128), F32)]
                  + [jax.ShapeDtypeStruct((3,) + p.shape[1:], p.dtype) for p in extra],
        scratch_shapes=[pltpu.VMEM((N_LIN, LIN_DH, LIN_DH), F32)] + (EXCHANGE_SEMS if extra else []),
        compiler_params=_cparams(("arbitrary", "arbitrary")),
    )(dcat, qkv, proj, proj, states, tinvs, alog_row, dtb_row, gain_row, *extra)


SB_T = 256


def _sb_masks():
    r, c = _iota((SB_T, SB_T), 0), _iota((SB_T, SB_T), 1)
    return r, c


def _staggered(chains):
    pending, live = list(chains), []
    while pending or live:
        if pending:
            live.append(pending.pop(0))
        for g in list(live):
            try:
                next(g)
            except StopIteration:
                live.remove(g)


def _sb_rows(kb):
    start = kb * SB_T
    return pl.ds(start if isinstance(kb, int) else pl.multiple_of(start, SB_T), SB_T)


def _sb_fwd(proj, n_batch):
    nq = SEQ // SB_T
    t = n_batch * SEQ
    scale = SB_DH ** -0.5
    both = range(2)

    def body(q_ref, k_ref, v_ref, o_ref, tot_ref, acc_scr, run_scr):
        qi = pl.program_id(2)
        lane = _iota((SB_T, 128), 1)
        r, c = _sb_masks()
        upper = (r > c).astype(BF16)
        q = q_ref[...] * scale
        qm = [jnp.where((lane < SB_DH) == (hh == 0), q, jnp.zeros_like(q)) for hh in both]

        def blocks(kbs, diag=None):
            first = diag is not None
            k_blk = [k_ref[_sb_rows(kb), :] for kb in kbs]
            v_blk = [v_ref[_sb_rows(kb), :] for kb in kbs]
            run = [None if first else run_scr[hh][:, 0:1] for hh in both]
            pv = {hh: [] for hh in both}
            rowsums = {hh: [] for hh in both}

            def chain(n, hh):
                z = lax.dot_general(qm[hh], k_blk[n], NT, preferred_element_type=F32)
                yield
                lb = _log_sigmoid(z)
                l1m = lb - z
                if n == diag:
                    l1m = jnp.where(r > c, l1m, 0.0)
                parts = _split(l1m)
                terms = ([] if first else [run[hh]]) + rowsums[hh]
                before = sum(terms[1:], terms[0]) if terms else None
                rowsums[hh].append(jnp.sum(l1m, axis=1, keepdims=True))
                yield
                tail = _dot_mask(parts, upper)
                yield
                a = jnp.exp(lb + (tail if before is None else before + tail))
                if n == diag:
                    a = jnp.where(r > c, a, 0.0)
                a = a.astype(BF16)
                yield
                pv[hh].append(lax.dot_general(a, v_blk[n], NN, preferred_element_type=F32))

            _staggered([chain(n, hh) for n in range(len(kbs)) for hh in both])
            for hh in both:
                if first:
                    acc_scr[hh] = sum(pv[hh][1:], pv[hh][0])
                    run_scr[hh] = jnp.broadcast_to(sum(rowsums[hh][1:], rowsums[hh][0]), (SB_T, 128))
                else:
                    acc_scr[hh] += sum(pv[hh][1:], pv[hh][0])
                    run_scr[hh] += sum(rowsums[hh][1:], rowsums[hh][0])

        @pl.when(qi == 0)
        def _():
            blocks([0], diag=0)

        @pl.when((qi & 1) == 1)
        def _():
            blocks([qi, qi - 1], diag=0)

        @pl.when((qi >= 2) & ((qi & 1) == 0))
        def _():
            blocks([qi, qi - 1, qi - 2], diag=0)

        rest = jnp.where(qi == 0, 0, ((qi - 1) >> 1) << 1)

        def step(it, carry):
            kb = rest - 1 - 2 * it
            blocks([kb, kb - 1])
            return carry

        lax.fori_loop(0, rest >> 1, step, 0)
        first = lane < SB_DH
        o_ref[...] = jnp.where(first, acc_scr[0], acc_scr[1]).astype(BF16)
        tot_ref[...] = jnp.where(first, run_scr[0], run_scr[1])

    nq_blocks = lambda b, p, i: (b * nq + i, p)
    seq_spec = lambda which: pl.BlockSpec((SEQ, 128), lambda b, p, i: (b, 3 * p + which))
    return pl.pallas_call(
        body, name="sb_fwd", grid=(n_batch, SB_PAIRS, nq),
        in_specs=[pl.BlockSpec((SB_T, 128), lambda b, p, i: (b * nq + i, 3 * p)), seq_spec(1), seq_spec(2)],
        out_specs=[pl.BlockSpec((SB_T, 128), nq_blocks), pl.BlockSpec((SB_T, 128), nq_blocks)],
        out_shape=[jax.ShapeDtypeStruct((t, D_MODEL), BF16),
                   jax.ShapeDtypeStruct((t, MIX_W), F32)],
        scratch_shapes=[pltpu.VMEM((2, SB_T, 128), F32), pltpu.VMEM((2, SB_T, 128), F32)],
        compiler_params=_cparams(("parallel", "parallel", "arbitrary")),
    )(proj, proj, proj)


def _sb_bwd(dcat, proj, totals, n_batch):
    nq = SEQ // SB_T
    t = n_batch * SEQ
    scale = SB_DH ** -0.5
    both = range(2)

    def body(do_ref, q_ref, k_ref, v_ref, tot_ref, dp_ref, dq_scr, run_scr, grun_scr, dk_ref, dv_ref):
        qi = pl.program_id(2)

        @pl.when(qi == 0)
        def _():
            dk_ref[...] = jnp.zeros_like(dk_ref)
            dv_ref[...] = jnp.zeros_like(dv_ref)

        lane = _iota((SB_T, 128), 1)
        r, c = _sb_masks()
        incl = (r <= c).astype(BF16)
        earlier = (r < c).astype(BF16)
        dq_scr[...] = jnp.zeros_like(dq_scr)
        run_scr[...] = jnp.zeros_like(run_scr)
        grun_scr[...] = jnp.zeros_like(grun_scr)
        q, do, tot = q_ref[...] * scale, do_ref[...], tot_ref[...]
        sel = [(lane < SB_DH) == (hh == 0) for hh in both]
        qm = [jnp.where(sel[hh], q, jnp.zeros_like(q)) for hh in both]
        dom = [jnp.where(sel[hh], do, 0.0).astype(BF16) for hh in both]
        total = [jnp.sum(jnp.where(lane == hh * SB_DH, tot, 0.0), axis=1, keepdims=True) for hh in both]

        def blocks(kbs, diag=None):
            k_blk = [k_ref[_sb_rows(kb), :] for kb in kbs]
            v_blk = [v_ref[_sb_rows(kb), :] for kb in kbs]
            run = [run_scr[hh][:, 0:1] for hh in both]
            grun = [grun_scr[hh][:, 0:1] for hh in both]
            rs_l, rs_e, dqp = ({hh: [] for hh in both} for _ in range(3))
            dk, dv = ([[] for _ in kbs] for _ in range(2))

            def plus(base, terms):
                return base if not terms else base + sum(terms[1:], terms[0])

            def chain(n, hh):
                z = lax.dot_general(qm[hh], k_blk[n], NT, preferred_element_type=F32)
                da = lax.dot_general(dom[hh], v_blk[n], NT, preferred_element_type=F32)
                yield
                lb = _log_sigmoid(z)
                sig = jnp.exp(lb)
                l1m = lb - z
                if n == diag:
                    l1m = jnp.where(r > c, l1m, 0.0)
                parts = _split(l1m)
                run_before = plus(run[hh], rs_l[hh])
                rs_l[hh].append(jnp.sum(l1m, axis=1, keepdims=True))
                yield
                prefix = run_before + _dot_mask(parts, incl)
                yield
                a = jnp.exp(lb + (total[hh] - prefix))
                if n == diag:
                    a = jnp.where(r > c, a, 0.0)
                de = a * da
                a = a.astype(BF16)
                parts = _split(de)
                grun_before = plus(grun[hh], rs_e[hh])
                rs_e[hh].append(jnp.sum(de, axis=1, keepdims=True))
                yield
                dv[n].append(lax.dot_general(a, dom[hh], TN, preferred_element_type=F32))
                dl1m = grun_before + _dot_mask(parts, earlier)
                yield
                if n == diag:
                    dl1m = jnp.where(r > c, dl1m, 0.0)
                dz = (de * (1.0 - sig) - dl1m * sig).astype(BF16)
                yield
                dqp[hh].append(lax.dot_general(dz, k_blk[n], NN, preferred_element_type=F32))
                dk[n].append(lax.dot_general(dz, qm[hh], TN, preferred_element_type=F32))

            _staggered([chain(n, hh) for n in range(len(kbs)) for hh in both])
            for hh in both:
                dq_scr[hh] += sum(dqp[hh][1:], dqp[hh][0])
                run_scr[hh] += sum(rs_l[hh][1:], rs_l[hh][0])
                grun_scr[hh] += sum(rs_e[hh][1:], rs_e[hh][0])
            for n, kb in enumerate(kbs):
                dk_ref[_sb_rows(kb), :] += dk[n][0] + dk[n][1]
                dv_ref[_sb_rows(kb), :] += dv[n][0] + dv[n][1]

        rest = jnp.where(qi == 0, 0, ((qi - 1) >> 1) << 1)

        def step(it, carry):
            blocks([2 * it, 2 * it + 1])
            return carry

        lax.fori_loop(0, rest >> 1, step, 0)

        @pl.when(qi == 0)
        def _():
            blocks([0], diag=0)

        @pl.when((qi & 1) == 1)
        def _():
            blocks([qi - 1, qi], diag=1)

        @pl.when((qi >= 2) & ((qi & 1) == 0))
        def _():
            blocks([qi - 2, qi - 1, qi], diag=2)

        dq = (jnp.where(sel[0], dq_scr[0], dq_scr[1]) * scale).astype(BF16)
        dp_ref[pl.ds(pl.multiple_of(qi * SB_T, SB_T), SB_T), 0:128] = dq

        @pl.when(qi == nq - 1)
        def _():
            dp_ref[:, 128:256] = dk_ref[...].astype(BF16)
            dp_ref[:, 256:384] = dv_ref[...].astype(BF16)

    q_blocks = lambda b, p, i: (b * nq + i, p)
    seq_spec = lambda which: pl.BlockSpec((SEQ, 128), lambda b, p, i: (b, 3 * p + which))
    return pl.pallas_call(
        body, name="sb_bwd", grid=(n_batch, SB_PAIRS, nq),
        in_specs=[pl.BlockSpec((SB_T, 128), q_blocks),
                  pl.BlockSpec((SB_T, 128), lambda b, p, i: (b * nq + i, 3 * p)),
                  seq_spec(1), seq_spec(2), pl.BlockSpec((SB_T, 128), q_blocks)],
        out_specs=pl.BlockSpec((SEQ, 384), lambda b, p, i: (b, p)),
        out_shape=jax.ShapeDtypeStruct((t, IN_B), BF16),
        scratch_shapes=[pltpu.VMEM((2, SB_T, 128), F32), pltpu.VMEM((2, SB_T, 128), F32),
                        pltpu.VMEM((2, SB_T, 128), F32), pltpu.VMEM((SEQ, 128), F32), pltpu.VMEM((SEQ, 128), F32)],
        compiler_params=_cparams(("parallel", "arbitrary", "arbitrary")),
    )(dcat, proj, proj, proj, totals)


MEM_TQ = 512


def _mem_heads(lane):
    return [(lane >= X_HEAD_DIM * h) & (lane < X_HEAD_DIM * (h + 1)) for h in range(N_X_HEADS)]


def _mem_attn_fwd(proj, q_col, memkv, cat, n_batch):
    nq = SEQ // MEM_TQ
    scale = X_HEAD_DIM ** -0.5

    def body(q_ref, kv_ref, _, o_ref):
        q = q_ref[...]
        k = kv_ref[:, :X_WIDTH].astype(BF16)
        v = kv_ref[:, X_WIDTH:].astype(BF16)
        out = jnp.zeros((MEM_TQ, X_WIDTH), F32)
        for sel in _mem_heads(_iota((MEM_TQ, X_WIDTH), 1)):
            s = lax.dot_general(jnp.where(sel, q, 0.0).astype(BF16), k, NT, preferred_element_type=F32) * scale
            e = jnp.exp(s - jnp.max(s, axis=-1, keepdims=True))
            p = e / jnp.sum(e, axis=-1, keepdims=True)
            out = out + jnp.where(sel, lax.dot_general(p.astype(BF16), v, NN, preferred_element_type=F32), 0.0)
        o_ref[...] = out.astype(BF16)

    return pl.pallas_call(
        body, name="mem_attn_fwd", grid=(n_batch, nq),
        in_specs=[pl.BlockSpec((MEM_TQ, X_WIDTH), lambda b, i: (b * nq + i, q_col)),
                  pl.BlockSpec((N_MEM, 2 * X_WIDTH), lambda b, i: (b, 0)), ANY],
        out_specs=pl.BlockSpec((MEM_TQ, X_WIDTH), lambda b, i: (b * nq + i, MIX_W // X_WIDTH)),
        out_shape=jax.ShapeDtypeStruct(cat.shape, BF16),
        input_output_aliases={2: 0},
        compiler_params=_cparams(("parallel", "parallel")),
    )(proj, memkv, cat)


def _mem_attn_bwd(dcat, proj, q_col, memkv, dproj, n_batch, tail=None):
    nq = SEQ // MEM_TQ
    scale = X_HEAD_DIM ** -0.5
    width = X_WIDTH + (0 if tail is None else 128)
    assert (q_col * X_WIDTH) % width == 0

    def body(do_ref, q_ref, kv_ref, *rest):
        dq_ref, dkv_ref = rest[-2:]

        @pl.when(pl.program_id(1) == 0)
        def _():
            dkv_ref[...] = jnp.zeros_like(dkv_ref)

        q, do = q_ref[...], do_ref[...]
        k = kv_ref[:, :X_WIDTH].astype(BF16)
        v = kv_ref[:, X_WIDTH:].astype(BF16)
        dq = jnp.zeros((MEM_TQ, X_WIDTH), F32)
        dk = jnp.zeros((N_MEM, X_WIDTH), F32)
        dv = jnp.zeros((N_MEM, X_WIDTH), F32)
        for sel in _mem_heads(_iota((MEM_TQ, X_WIDTH), 1)):
            qm = jnp.where(sel, q, 0.0).astype(BF16)
            dom = jnp.where(sel, do, 0.0).astype(BF16)
            s = lax.dot_general(qm, k, NT, preferred_element_type=F32) * scale
            e = jnp.exp(s - jnp.max(s, axis=-1, keepdims=True))
            p = e / jnp.sum(e, axis=-1, keepdims=True)
            dp = lax.dot_general(dom, v, NT, preferred_element_type=F32)
            ds = ((p * (dp - jnp.sum(dp * p, axis=-1, keepdims=True))) * scale).astype(BF16)
            dv = dv + lax.dot_general(p.astype(BF16), dom, TN, preferred_element_type=F32)
            dk = dk + lax.dot_general(ds, qm, TN, preferred_element_type=F32)
            dq = dq + jnp.where(sel, lax.dot_general(ds, k, NN, preferred_element_type=F32), 0.0)
        if tail is None:
            dq_ref[...] = dq.astype(BF16)
        else:
            dq_ref[...] = jnp.concatenate([dq.astype(BF16), rest[0][...]], axis=1)
        dkv_ref[...] += jnp.concatenate([dk, dv], axis=1)

    rows = lambda b, i: b * nq + i
    extra = [] if tail is None else [tail]
    return pl.pallas_call(
        body, name="mem_attn_bwd", grid=(n_batch, nq),
        in_specs=[pl.BlockSpec((MEM_TQ, X_WIDTH), lambda b, i: (rows(b, i), MIX_W // X_WIDTH)),
                  pl.BlockSpec((MEM_TQ, X_WIDTH), lambda b, i: (rows(b, i), q_col)),
                  pl.BlockSpec((N_MEM, 2 * X_WIDTH), lambda b, i: (b, 0))]
                 + [pl.BlockSpec((MEM_TQ, 128), lambda b, i: (rows(b, i), 0))] * len(extra) + [ANY],
        out_specs=[pl.BlockSpec((MEM_TQ, width), lambda b, i: (rows(b, i), q_col * X_WIDTH // width)),
                   pl.BlockSpec((N_MEM, 2 * X_WIDTH), lambda b, i: (b, 0))],
        out_shape=[jax.ShapeDtypeStruct(dproj.shape, BF16),
                   jax.ShapeDtypeStruct((n_batch * N_MEM, 2 * X_WIDTH), F32)],
        input_output_aliases={3 + len(extra): 0},
        compiler_params=_cparams(("parallel", "arbitrary")),
    )(dcat, proj, memkv, *extra, dproj)


def _relu2_epilogue(acc):
    r = jnp.maximum(acc, 0.0)
    return (r * r,)


def _relu2_bwd_epilogue(acc, a):
    return (acc * (2.0 * jnp.sqrt(a.astype(F32))),)


def _pad_in_a(w_in_a):
    w = 3 * MIX_W
    parts = [w_in_a[:, :w], w_in_a[:, w:w + MIX_W], w_in_a[:, IN_A - X_WIDTH:],
             w_in_a[:, w + MIX_W:w + MIX_W + 2 * N_LIN]]
    pad = jnp.zeros((D_MODEL, IN_A_PAD - IN_A), w_in_a.dtype)
    return jnp.concatenate(parts + [pad], axis=1)


def _unpad_in_a(g):
    w = 3 * MIX_W
    return jnp.concatenate([g[:, :w + MIX_W], g[:, w + MIX_W + X_WIDTH:w + MIX_W + X_WIDTH + 2 * N_LIN],
                            g[:, w + MIX_W:w + MIX_W + X_WIDTH]], axis=1)


def _qkv_to_pairs(w):
    w3 = 3 * MIX_W
    qkv = w[:, :w3].reshape(-1, 3, SB_PAIRS, 128).transpose(0, 2, 1, 3).reshape(-1, w3)
    return jnp.concatenate([qkv, w[:, w3:]], axis=1)


def _pairs_to_qkv(w):
    w3 = 3 * MIX_W
    qkv = w[:, :w3].reshape(-1, SB_PAIRS, 3, 128).transpose(0, 2, 1, 3).reshape(-1, w3)
    return jnp.concatenate([qkv, w[:, w3:]], axis=1)


def _local_step(x, mem, target, wts, small, comm=None):
    wts = dict(wts)
    t = x.shape[0]
    nb = t // SEQ
    npre, npost, mpre, mpost = small["norm_pre_mix"], small["norm_post_mix"], small["norm_pre_mlp"], small["norm_post_mlp"]
    alog_row, dtb_row, gain_row = _gdn_param_rows(small["a_log_a"][0], small["dt_bias_a"][0], small["onorm_a"][0])
    conv_w = small["conv_w"]

    mem_n = _rms_fwd(mem, small["mem_norm"], name="mem_norm_fwd", tile=256)
    saved = []
    h = _rms_fwd(x, npre[0], name="pre_mix_norm0")
    big = min(1024, t)
    for i in range(DEPTH):
        s = {"x_in": x, "h1": h}
        if i == 0:
            proj = _matmul(h, wts["in_a", None], mode="nn", tm=big, tn=1152, tk=1024, name="proj_a")
            qkv = _conv_silu_fwd(proj, conv_w, nb)
            if comm is None:
                mix, states, tinvs = _gdn_fwd(qkv, proj, alog_row, dtb_row, gain_row, nb)
            else:
                mix, states, tinvs, second = _gdn_fwd(qkv, proj, alog_row, dtb_row, gain_row, nb, gather=comm["slabs"])
                second = second.reshape(N_CHIPS, REGION_ROWS[W_SECOND], D_MODEL)
                wts.update(_as_operands(_unpack_region_full(W_SECOND, second)))
            s.update(qkv=qkv, states=states, tinvs=tinvs)
            q_col = (3 * MIX_W + MIX_W) // X_WIDTH
        else:
            proj = _matmul(h, wts["in_b", None], mode="nn", tm=big, tn=1280, tk=1024, name="proj_b", out_dtypes=(BF16,))
            mix, totals = _sb_fwd(proj, nb)
            s.update(totals=totals)
            q_col = 3 * MIX_W // X_WIDTH
        memkv = _matmul(mem_n, wts["mem_kv", i], mode="nn", tm=256, tn=512, tk=1024, name=f"memkv{i}")
        cat = _mem_attn_fwd(proj, q_col, memkv, mix, nb)
        y = _matmul(cat, wts["out", i], mode="nn", tm=big, tn=1024, tk=1024, name=f"out_proj{i}")
        x2, h2 = _post_norm_add(x, y, npost[i], mpre[i], name=f"post_mix{i}")
        a = _matmul(h2, wts["up", i], mode="nn", tm=big, tn=2048, tk=1024, name=f"up{i}",
                    out_dtypes=(BF16,), epilogue=_relu2_epilogue, n_outer=True)
        y2 = _matmul(a, wts["down", i], mode="nn", tm=big, tn=1024, tk=2048, name=f"down{i}")
        s.update(proj=proj, q_col=q_col, memkv=memkv, cat=cat, y=y, x2=x2, h2=h2, a=a, y2=y2)
        saved.append(s)
        if i + 1 < DEPTH:
            x, h = _post_norm_add(x2, y2, mpost[i], npre[i + 1], name=f"post_mlp{i}")
        else:
            loss_row, dx = _post_norm_loss(x2, y2, mpost[i], target, name="loss_head")

    gw = {}
    gs = {k: [None] * DEPTH for k in ("norm_pre_mix", "norm_post_mix", "norm_pre_mlp", "norm_post_mlp")}
    dmem_n, early = None, None
    for i in reversed(range(DEPTH)):
        s = saved[i]
        if i == DEPTH - 1:
            dy2, gs["norm_post_mlp"][i] = _rms_bwd(dx, s["y2"], mpost[i], name=f"post_mlp_bwd{i}", out_dtype=BF16)
        du = _matmul(dy2, wts["down", i], mode="nt", tm=big, tn=2048, tk=1024, name=f"down_dx{i}",
                     out_dtypes=(BF16,), epilogue=_relu2_bwd_epilogue, extras=(s["a"],), n_outer=True)
        gw["down", i] = _matmul(s["a"], dy2, mode="tn", tm=1024, tn=1024, tk=big, name=f"down_dw{i}")
        dh2 = _matmul(du, wts["up", i], mode="nt", tm=big, tn=1024, tk=2048, name=f"up_dx{i}")
        gw["up", i] = _matmul(s["h2"], du, mode="tn", tm=1024, tn=2048, tk=512, name=f"up_dw{i}")
        dx2, dy, gs["norm_pre_mlp"][i], gs["norm_post_mix"][i] = _rms_bwd_pair(
            dh2, s["x2"], mpre[i], dx, s["y"], npost[i], name=f"mlp_norms_bwd{i}")
        dcat = _matmul(dy, wts["out", i], mode="nt", tm=big, tn=1024, tk=1024, name=f"out_dx{i}")
        gw["out", i] = _matmul(s["cat"], dy, mode="tn", tm=1024, tn=1024, tk=big, name=f"out_dw{i}")
        if i == 0:
            exchange = None
            if comm is not None:
                own, exchange = _reduce_in_chip(_pack_region_full(G_EARLY, gw), comm["core"])
            res = _gdn_bwd(dcat, s["qkv"], s["proj"], s["states"], s["tinvs"], alog_row, dtb_row, gain_row, nb,
                           exchange=exchange)
            dqkv, dproj, dsmall, dalog, ddtb, dgain = res[:6]
            if comm is not None:
                early = (own, res[6])
            dproj, dconv = _conv_silu_bwd(dqkv, s["proj"], conv_w, dproj, nb)
            dproj, dmemkv = _mem_attn_bwd(dcat, s["proj"], s["q_col"], s["memkv"], dproj, nb, tail=dsmall)
            w_in, tile = wts["in_a", None], 1152
        else:
            dproj = _sb_bwd(dcat, s["proj"], s["totals"], nb)
            dproj, dmemkv = _mem_attn_bwd(dcat, s["proj"], s["q_col"], s["memkv"], dproj, nb)
            w_in, tile = wts["in_b", None], 1280
        dmemkv = dmemkv.astype(BF16)
        gw["mem_kv", i] = _matmul(mem_n, dmemkv, mode="tn", tm=1024, tn=512, tk=256, name=f"memkv_dw{i}")
        dmn = _matmul(dmemkv, wts["mem_kv", i], mode="nt", tm=256, tn=1024, tk=512, name=f"memkv_dx{i}")
        dmem_n = dmn if dmem_n is None else dmem_n + dmn
        dh1 = _matmul(dproj, w_in, mode="nt", tm=big, tn=1024, tk=tile, name=f"proj_dx{i}")
        g_in = _matmul(s["h1"], dproj, mode="tn", tm=1024, tn=tile, tk=big, name=f"proj_dw{i}")
        if i == 0:
            gw["in_a", None] = _unpad_in_a(g_in)
        else:
            gw["in_b", None] = _pairs_to_qkv(g_in)
        if i > 0:
            dx, dy2, gs["norm_pre_mix"][i], gs["norm_post_mlp"][i - 1] = _rms_bwd_pair(
                dh1, s["x_in"], npre[i], dx2, saved[i - 1]["y2"], mpost[i - 1], name=f"mix_norms_bwd{i}")
        else:
            dx, gs["norm_pre_mix"][i] = _rms_bwd(dh1, s["x_in"], npre[i], name=f"pre_mix_bwd{i}", res=dx2)

    _, g_mem_norm = _rms_bwd(dmem_n, mem, small["mem_norm"], name="mem_norm_bwd", tile=256)
    gsmall = {k: jnp.concatenate(v, axis=0) for k, v in gs.items()}
    gsmall.update(mem_norm=g_mem_norm[0], a_log_a=dalog[:, N_LIN:2 * N_LIN], dt_bias_a=ddtb[:, N_LIN:2 * N_LIN],
                  onorm_a=dgain, conv_w=dconv)
    return loss_row[0, 0], dx, gw, gsmall, early


SUM_TILE = 640


def _position():
    x, y, c = lax.axis_index("x"), lax.axis_index("y"), lax.axis_index("c")
    others = [(1 - x, y), (x, 1 - y), (1 - x, 1 - y)]
    return x, y, c, others


class _Gather:
    def __init__(self, w_ref, out_ref, send_sems, recv_sems):
        self.w, self.out, self.send, self.recv = w_ref, out_ref, send_sems, recv_sems
        self.x, self.y, self.c, self.others = _position()
        self.me = 2 * self.x + self.y

    def _copy(self, k, src, dst, to):
        return pltpu.make_async_remote_copy(src_ref=src, dst_ref=dst, send_sem=self.send.at[k],
                                            recv_sem=self.recv.at[k], device_id=to, device_id_type=MESH)

    def _first(self):
        return [self._copy(j, self.w.at[self.me, self.c], self.out.at[self.me, self.c], (ox, oy, self.c))
                for j, (ox, oy) in enumerate(self.others)]

    def _passed(self):
        sibling = (self.x, self.y, 1 - self.c)
        return [self._copy(3 + j, self.out.at[2 * ox + oy, self.c], self.out.at[2 * ox + oy, self.c], sibling)
                for j, (ox, oy) in enumerate(self.others)]

    def start(self):
        for cp in self._first():
            cp.start()

    def forward(self):
        passed = self._passed()
        for j, (ox, oy) in enumerate(self.others):
            self._copy(j, self.w.at[self.me, self.c], self.out.at[2 * ox + oy, self.c], (self.x, self.y, self.c)).wait_recv()
            passed[j].start()

    def finish(self):
        for j, (ox, oy) in enumerate(self.others):
            self._copy(3 + j, self.w.at[self.me, self.c], self.out.at[2 * ox + oy, 1 - self.c],
                       (self.x, self.y, self.c)).wait_recv()
        for cp in self._first() + self._passed():
            cp.wait_send()


GATHER_SEMS = [pltpu.SemaphoreType.DMA((6,)), pltpu.SemaphoreType.DMA((6,))]


def _gather_chips(wflat):
    def body(w_ref, out_ref, send_sems, recv_sems):
        g = _Gather(w_ref, out_ref, send_sems, recv_sems)
        g.start()
        g.forward()
        g.finish()

    return pl.pallas_call(
        body, name="gather_weights",
        in_specs=[ANY], out_specs=ANY, input_output_aliases={0: 0},
        out_shape=jax.ShapeDtypeStruct(wflat.shape, wflat.dtype),
        scratch_shapes=GATHER_SEMS,
    )(wflat)


def _gather_all(v, *, name):
    rows, n = v.shape

    def body(x_ref, out_ref, send_sems, recv_sems, local_sem):
        x, y, c, others = _position()
        me, sibling = (x, y, c), (x, y, 1 - c)

        def blk(px, py, pc):
            return out_ref.at[pl.ds((4 * px + 2 * py + pc) * rows, rows), :]

        def copy(k, block, to, src=None):
            return pltpu.make_async_remote_copy(src_ref=blk(*block) if src is None else src, dst_ref=blk(*block),
                                                send_sem=send_sems.at[k], recv_sem=recv_sems.at[k],
                                                device_id=to, device_id_type=MESH)

        mine = pltpu.make_async_copy(x_ref, blk(*me), local_sem)
        mine.start()
        first = [copy(0, me, sibling, src=x_ref)]
        first += [copy(1 + j, me, (*chip, c), src=x_ref) for j, chip in enumerate(others)]
        for cp in first:
            cp.start()
        passed = [copy(4 + j, (*chip, c), sibling) for j, chip in enumerate(others)]
        for j, chip in enumerate(others):
            copy(1 + j, (*chip, c), me).wait_recv()
            passed[j].start()
        copy(0, sibling, me).wait_recv()
        for j, chip in enumerate(others):
            copy(4 + j, (*chip, 1 - c), me).wait_recv()
        for cp in first + passed:
            cp.wait_send()
        mine.wait()

    vmem = pl.BlockSpec(memory_space=pltpu.VMEM)
    return pl.pallas_call(
        body, name=name, in_specs=[vmem], out_specs=vmem,
        out_shape=jax.ShapeDtypeStruct((8 * rows, n), v.dtype),
        scratch_shapes=[pltpu.SemaphoreType.DMA((7,)), pltpu.SemaphoreType.DMA((7,)), pltpu.SemaphoreType.DMA],
    )(v)


def _swap_halves(g5):
    def body(g_ref, out_ref, send_sem, recv_sem):
        x, y, c, _ = _position()
        cp = pltpu.make_async_remote_copy(src_ref=g_ref.at[:, 1 - c], dst_ref=out_ref, send_sem=send_sem,
                                          recv_sem=recv_sem, device_id=(x, y, 1 - c), device_id_type=MESH)
        cp.start()
        cp.wait()

    return pl.pallas_call(
        body, name="grad_swap_halves", in_specs=[ANY], out_specs=ANY,
        out_shape=jax.ShapeDtypeStruct((N_CHIPS, g5.shape[2], D_MODEL), g5.dtype),
        scratch_shapes=[pltpu.SemaphoreType.DMA, pltpu.SemaphoreType.DMA],
    )(g5)


def _add_halves(core, g5, got):
    def body(c_ref, a_ref, b_ref, o_ref, ob_ref):
        s = a_ref[0] + b_ref[...]
        o_ref[...] = s
        ob_ref[...] = s.astype(BF16)

    half = g5.shape[2]
    nt = half // SUM_TILE
    spec = pl.BlockSpec((1, SUM_TILE, D_MODEL), lambda s, i, c_ref: (s, i, 0))
    return pl.pallas_call(
        body, name="grad_add_halves",
        grid_spec=pltpu.PrefetchScalarGridSpec(
            num_scalar_prefetch=1, grid=(N_CHIPS, nt),
            in_specs=[pl.BlockSpec((1, 1, SUM_TILE, D_MODEL), lambda s, i, c_ref: (s, c_ref[0], i, 0)), spec],
            out_specs=[spec, spec]),
        out_shape=[jax.ShapeDtypeStruct((N_CHIPS, half, D_MODEL), F32),
                   jax.ShapeDtypeStruct((N_CHIPS, half, D_MODEL), BF16)],
        compiler_params=_cparams(("parallel", "parallel")),
    )(core, g5, got)


def _exchange_copies(p_ref, q_ref, send_sems, recv_sems):
    x, y, c, others = _position()
    return [pltpu.make_async_remote_copy(src_ref=p_ref.at[2 * ox + oy], dst_ref=q_ref.at[j],
                                         send_sem=send_sems.at[j], recv_sem=recv_sems.at[j],
                                         device_id=(ox, oy, c), device_id_type=MESH)
            for j, (ox, oy) in enumerate(others)]


EXCHANGE_SEMS = [pltpu.SemaphoreType.DMA((3,)), pltpu.SemaphoreType.DMA((3,))]


def _exchange_chips(p):
    def body(p_ref, q_ref, send_sems, recv_sems):
        copies = _exchange_copies(p_ref, q_ref, send_sems, recv_sems)
        for cp in copies:
            cp.start()
        for cp in copies:
            cp.wait()

    return pl.pallas_call(
        body, name="grad_exchange_chips", in_specs=[ANY], out_specs=ANY,
        out_shape=jax.ShapeDtypeStruct((3,) + p.shape[1:], p.dtype),
        scratch_shapes=EXCHANGE_SEMS,
    )(p)


def _add_chips(chip_core, p, q):
    def body(kc_ref, p_ref, q_ref, o_ref):
        o_ref[0] = ((p_ref[0] + q_ref[0].astype(F32)) + q_ref[1].astype(F32)) + q_ref[2].astype(F32)

    half = p.shape[1]
    nt = half // SUM_TILE
    return pl.pallas_call(
        body, name="grad_add_chips",
        grid_spec=pltpu.PrefetchScalarGridSpec(
            num_scalar_prefetch=1, grid=(nt,),
            in_specs=[pl.BlockSpec((1, SUM_TILE, D_MODEL), lambda i, kc_ref: (kc_ref[0], i, 0)),
                      pl.BlockSpec((3, SUM_TILE, D_MODEL), lambda i, kc_ref: (0, i, 0))],
            out_specs=pl.BlockSpec((1, SUM_TILE, D_MODEL), lambda i, kc_ref: (kc_ref[1], i, 0))),
        out_shape=jax.ShapeDtypeStruct((2, half, D_MODEL), F32),
        compiler_params=_cparams(("parallel",)),
    )(chip_core, p, q)


def _share_halves(halves):
    def body(h_ref, out_ref, send_sem, recv_sem):
        x, y, c, _ = _position()
        cp = pltpu.make_async_remote_copy(src_ref=h_ref.at[c], dst_ref=out_ref.at[c], send_sem=send_sem,
                                          recv_sem=recv_sem, device_id=(x, y, 1 - c), device_id_type=MESH)
        cp.start()
        pltpu.make_async_remote_copy(src_ref=h_ref.at[c], dst_ref=out_ref.at[1 - c], send_sem=send_sem,
                                     recv_sem=recv_sem, device_id=(x, y, c), device_id_type=MESH).wait_recv()
        cp.wait_send()

    return pl.pallas_call(
        body, name="grad_share_halves", in_specs=[ANY], out_specs=ANY, input_output_aliases={0: 0},
        out_shape=jax.ShapeDtypeStruct(halves.shape, halves.dtype),
        scratch_shapes=[pltpu.SemaphoreType.DMA, pltpu.SemaphoreType.DMA],
    )(halves)


def _reduce_in_chip(g_packed, core):
    rows = g_packed.shape[1]
    g5 = g_packed.reshape(N_CHIPS, 2, rows // 2, D_MODEL)
    return _add_halves(core.reshape(1), g5, _swap_halves(g5))


def _reduce_across_chips(p, q, chip, core):
    halves = _share_halves(_add_chips(jnp.stack([chip, core]), p, q))
    return halves.reshape(2 * halves.shape[1], D_MODEL)


def _reduce_scatter(g_packed, chip, core):
    p, p_bf = _reduce_in_chip(g_packed, core)
    return _reduce_across_chips(p, _exchange_chips(p_bf), chip, core)


def _slot(n):
    return -(-n // 16) * 16


def _pad_rows(a, axis):
    n = a.shape[axis]
    widths = [(0, 0)] * a.ndim
    widths[axis] = (0, _slot(n) - n)
    return jnp.pad(a, widths) if _slot(n) != n else a


W_FIRST = (("in_a", None),)
W_SECOND = (("mem_kv", 0), ("out", 0), ("up", 0), ("down", 0),
            ("in_b", None), ("mem_kv", 1), ("out", 1), ("up", 1), ("down", 1))
G_LATE = (("in_a", None), ("mem_kv", 0), ("out", 0))
G_EARLY = (("up", 0), ("down", 0), ("in_b", None), ("mem_kv", 1), ("out", 1), ("up", 1), ("down", 1))
REGION_ROWS = {W_FIRST: 896, W_SECOND: 5504, G_LATE: 1280, G_EARLY: 5120}
FULL_SHAPE = {"in_a": (D_MODEL, IN_A), "in_b": (D_MODEL, IN_B), "mem_kv": (D_MODEL, 2 * X_WIDTH),
              "out": (D_MODEL, D_MODEL), "up": (D_MODEL, D_FF), "down": (D_FF, D_MODEL)}
COLUMN_SHARDED = ("in_a", "in_b", "up")


def _shard_shape(name):
    r, c = FULL_SHAPE[name]
    return (r, c // N_CHIPS) if name in COLUMN_SHARDED else (r // N_CHIPS, c)


def _part_rows(name):
    r, c = _shard_shape(name)
    return r * c // D_MODEL


def _pack_region(region, part, dtype):
    rows = [_pad_rows(part(name, layer).reshape(-1, D_MODEL).astype(dtype), 0) for name, layer in region]
    used = sum(r.shape[0] for r in rows)
    return jnp.concatenate(rows + [jnp.zeros((REGION_ROWS[region] - used, D_MODEL), dtype)], axis=0)


def _unpack_region(region, flat):
    out, off = {}, 0
    for name, layer in region:
        n = _part_rows(name)
        out[name, layer] = flat[off:off + n].reshape(_shard_shape(name))
        off += _slot(n)
    return out


def _unpack_region_full(region, g):
    out, off = {}, 0
    for name, layer in region:
        n = _part_rows(name)
        piece = g[:, off:off + n].reshape((N_CHIPS,) + _shard_shape(name))
        if name in COLUMN_SHARDED:
            piece = piece.transpose(1, 0, 2)
        out[name, layer] = piece.reshape(FULL_SHAPE[name])
        off += _slot(n)
    return out


def _pack_region_full(region, full):
    s = N_CHIPS
    parts = []
    for name, layer in region:
        g = full[name, layer]
        if name in COLUMN_SHARDED:
            g = g.reshape(g.shape[0], s, -1).transpose(1, 0, 2)
        parts.append(_pad_rows(g.reshape(s, -1, D_MODEL), 1))
    used = sum(p.shape[1] for p in parts)
    return jnp.concatenate(parts + [jnp.zeros((s, REGION_ROWS[region] - used, D_MODEL), F32)], axis=1)


def _as_operands(full):
    out = dict(full)
    if ("in_a", None) in out:
        out["in_a", None] = _pad_in_a(out["in_a", None])
    if ("in_b", None) in out:
        out["in_b", None] = _qkv_to_pairs(out["in_b", None])
    return out


def _place(packed, chip):
    rows = packed.shape[0]
    slabs = lax.dynamic_update_slice(jnp.zeros((N_CHIPS, rows, D_MODEL), packed.dtype), packed[None], (chip, 0, 0))
    return slabs.reshape(N_CHIPS, 2, rows // 2, D_MODEL)


def _adamw_math(w, g, m, v):
    m = ADAM_B1 * m + (1.0 - ADAM_B1) * g
    v = ADAM_B2 * v + (1.0 - ADAM_B2) * (g * g)
    m_hat = m / (1.0 - ADAM_B1 ** ADAM_STEP)
    v_hat = v / (1.0 - ADAM_B2 ** ADAM_STEP)
    delta = -ADAM_LR * (m_hat / (jnp.sqrt(v_hat) + ADAM_EPS) + ADAM_WD * w)
    return delta, m, v


ADAM_TILE = 256


def _adamw(w, g, m, v, *, name):
    shape = w.shape
    cols = shape[-1]
    rows = w.size // cols
    tile = min(rows, ADAM_TILE)
    assert rows % tile == 0, (name, shape)

    def body(w_ref, g_ref, m_ref, v_ref, d_ref, nm_ref, nv_ref):
        d_ref[...], nm_ref[...], nv_ref[...] = _adamw_math(w_ref[...], g_ref[...], m_ref[...], v_ref[...])

    spec = pl.BlockSpec((tile, cols), lambda i: (i, 0))
    outs = pl.pallas_call(
        body, name=name, grid=(rows // tile,), in_specs=[spec] * 4, out_specs=[spec] * 3,
        out_shape=[jax.ShapeDtypeStruct((rows, cols), F32)] * 3,
        compiler_params=_cparams(("parallel",)),
    )(*[a.reshape(rows, cols) for a in (w, g, m, v)])
    return [o.reshape(shape) for o in outs]


SMALL_NAMES = (("mem_norm", 8), ("norm_pre_mix", 16), ("norm_post_mix", 16), ("norm_pre_mlp", 16),
               ("norm_post_mlp", 16), ("a_log_a", 1), ("dt_bias_a", 1), ("onorm_a", 1))
SMALL_ROWS = 80
CONV_ROWS = CONV_K * 3 * MIX_W // 128
SMALL_GRAD_ROWS = SMALL_ROWS + CONV_ROWS


def _pack_small(vals):
    rows = []
    for name, n in SMALL_NAMES:
        flat = vals[name].reshape(-1)
        rows.append(jnp.pad(flat, (0, n * 128 - flat.size)).reshape(n, 128))
    used = sum(n for _, n in SMALL_NAMES)
    return jnp.concatenate(rows + [jnp.zeros((SMALL_ROWS - used, 128), F32)], axis=0)


def _unpack_small(packed, like):
    out, off = {}, 0
    for name, n in SMALL_NAMES:
        size = like[name].size
        out[name] = packed[off:off + n].reshape(-1)[:size].reshape(like[name].shape)
        off += n
    return out


def _small_update(gathered, w, m, v):
    def body(g_ref, w_ref, m_ref, v_ref, gs_ref, d_ref, nm_ref, nv_ref):
        g = g_ref[0]
        for dev in range(1, 8):
            g = g + g_ref[dev]
        gs_ref[...] = g
        d_ref[...], nm_ref[...], nv_ref[...] = _adamw_math(w_ref[...], g[:SMALL_ROWS], m_ref[...], v_ref[...])

    small = jax.ShapeDtypeStruct((SMALL_ROWS, 128), F32)
    return pl.pallas_call(
        body, name="small_update",
        out_shape=[jax.ShapeDtypeStruct((SMALL_GRAD_ROWS, 128), F32), small, small, small],
    )(gathered.reshape(8, SMALL_GRAD_ROWS, 128), w, m, v)


def kernel(x, mem, mem_norm, norm_pre_mix, norm_post_mix, norm_pre_mlp, norm_post_mlp, w_in_a, conv_w_a, a_log_a, dt_bias_a, onorm_a, w_in_b, w_mem_kv, w_out, w_up, w_down, loss_target, m_mem_norm, m_norm_pre_mix, m_norm_post_mix, m_norm_pre_mlp, m_norm_post_mlp, m_w_in_a, m_conv_w_a, m_a_log_a, m_dt_bias_a, m_onorm_a, m_w_in_b, m_w_mem_kv, m_w_out, m_w_up, m_w_down, v_mem_norm, v_norm_pre_mix, v_norm_post_mix, v_norm_pre_mlp, v_norm_post_mlp, v_w_in_a, v_conv_w_a, v_a_log_a, v_dt_bias_a, v_onorm_a, v_w_in_b, v_w_mem_kv, v_w_out, v_w_up, v_w_down):
    nb = x.shape[0]
    chip = (2 * lax.axis_index("x") + lax.axis_index("y")).astype(jnp.int32)
    core = lax.axis_index("c").astype(jnp.int32)
    shards = {"in_a": w_in_a, "in_b": w_in_b, "mem_kv": w_mem_kv, "out": w_out, "up": w_up, "down": w_down}
    moments_m = {"in_a": m_w_in_a, "in_b": m_w_in_b, "mem_kv": m_w_mem_kv, "out": m_w_out, "up": m_w_up, "down": m_w_down}
    moments_v = {"in_a": v_w_in_a, "in_b": v_w_in_b, "mem_kv": v_w_mem_kv, "out": v_w_out, "up": v_w_up, "down": v_w_down}
    small_w = {"mem_norm": mem_norm, "norm_pre_mix": norm_pre_mix, "norm_post_mix": norm_post_mix,
               "norm_pre_mlp": norm_pre_mlp, "norm_post_mlp": norm_post_mlp, "a_log_a": a_log_a,
               "dt_bias_a": dt_bias_a, "onorm_a": onorm_a}
    small_m = {"mem_norm": m_mem_norm, "norm_pre_mix": m_norm_pre_mix, "norm_post_mix": m_norm_post_mix,
               "norm_pre_mlp": m_norm_pre_mlp, "norm_post_mlp": m_norm_post_mlp, "a_log_a": m_a_log_a,
               "dt_bias_a": m_dt_bias_a, "onorm_a": m_onorm_a}
    small_v = {"mem_norm": v_mem_norm, "norm_pre_mix": v_norm_pre_mix, "norm_post_mix": v_norm_post_mix,
               "norm_pre_mlp": v_norm_pre_mlp, "norm_post_mlp": v_norm_post_mlp, "a_log_a": v_a_log_a,
               "dt_bias_a": v_dt_bias_a, "onorm_a": v_onorm_a}

    def shard_part(name, layer):
        return shards[name][0 if layer is None else layer]

    first = _gather_chips(_place(_pack_region(W_FIRST, shard_part, BF16), chip))
    wts = _as_operands(_unpack_region_full(W_FIRST, first.reshape(N_CHIPS, REGION_ROWS[W_FIRST], D_MODEL)))
    comm = {"slabs": _place(_pack_region(W_SECOND, shard_part, BF16), chip), "core": core}
    conv_rows = CONV_ROWS // N_CHIPS
    conv_blk = jnp.pad(conv_w_a.reshape(conv_rows, 128), ((0, 24 - conv_rows), (0, 0)))
    conv_all = _gather_all(conv_blk, name="gather_conv").reshape(N_CHIPS, 2, 24, 128)[:, 0, :conv_rows]
    conv_full = conv_all.reshape(N_CHIPS, CONV_K, 3 * MIX_W // N_CHIPS).transpose(1, 0, 2).reshape(CONV_K, 3 * MIX_W)

    loss_local, dx, gw, gsmall, (own_early, others_early) = _local_step(
        x.reshape(nb * SEQ, D_MODEL), mem.reshape(nb * N_MEM, D_MODEL), loss_target.reshape(nb * SEQ, D_MODEL),
        wts, dict(small_w, conv_w=conv_full), comm)
    loss = lax.psum(loss_local, ("x", "y", "c"))
    grad_x = dx.reshape(nb, SEQ, D_MODEL)

    g_part = _unpack_region(G_EARLY, _reduce_across_chips(own_early, others_early, chip, core))
    g_part.update(_unpack_region(G_LATE, _reduce_scatter(_pack_region_full(G_LATE, gw), chip, core)))
    g_shard = {k: (g_part[k, None][None] if (k, None) in g_part else jnp.stack([g_part[k, i] for i in range(DEPTH)]))
               for k in shards}
    upd = {k: _adamw(shards[k], g_shard[k], moments_m[k], moments_v[k], name=f"adamw_{k}") for k in shards}

    g_rows = jnp.concatenate([_pack_small(gsmall), gsmall["conv_w"].reshape(CONV_ROWS, 128)], axis=0)
    g_all = _gather_all(g_rows, name="gather_small_grads")
    g_sum, d_small, nm_small, nv_small = _small_update(g_all, _pack_small(small_w), _pack_small(small_m), _pack_small(small_v))
    gs = _unpack_small(g_sum, small_w)
    ds, nms, nvs = (_unpack_small(p, small_w) for p in (d_small, nm_small, nv_small))
    cw = 3 * MIX_W // N_CHIPS
    g_conv = lax.dynamic_slice(g_sum[SMALL_ROWS:].reshape(CONV_K, 3 * MIX_W), (0, chip * cw), (CONV_K, cw)).reshape(conv_w_a.shape)
    d_conv, nm_conv, nv_conv = _adamw(conv_w_a, g_conv, m_conv_w_a, v_conv_w_a, name="adamw_conv")

    order = ("mem_norm", "norm_pre_mix", "norm_post_mix", "norm_pre_mlp", "norm_post_mlp", "in_a", "conv", "a_log_a",
             "dt_bias_a", "onorm_a", "in_b", "mem_kv", "out", "up", "down")
    grads = dict(gs, conv=g_conv, **g_shard)
    deltas = dict(ds, conv=d_conv, **{k: u[0] for k, u in upd.items()})
    new_m = dict(nms, conv=nm_conv, **{k: u[1] for k, u in upd.items()})
    new_v = dict(nvs, conv=nv_conv, **{k: u[2] for k, u in upd.items()})
    return (loss, grad_x, *[grads[k] for k in order], *[deltas[k] for k in order],
            *[new_m[k] for k in order], *[new_v[k] for k in order])
```

```python
import functools

import jax
import jax.numpy as jnp
from jax import lax
from jax.experimental import pallas as pl
from jax.experimental.pallas import tpu as pltpu

F32 = jnp.float32
BF16 = jnp.bfloat16
HIGHEST = lax.Precision.HIGHEST
MESH = pl.DeviceIdType.MESH

D_MODEL = 1024
SEQ = 2048
DEPTH = 2
X_WIDTH = 256
N_X_HEADS = 4
X_HEAD_DIM = 64
MIX_W = 768
LIN_DH = 128
N_LIN = 6
CONV_K = 4
CHUNK = 64
SB_DH = 64
SB_PAIRS = 6
N_MEM = 256
D_FF = 4096
EPS = 1e-6
IN_A = 3340
IN_A_PAD = 3456
IN_B = 2560
SMALL_COL = 26
N_CHIPS = 4

ADAM_LR, ADAM_B1, ADAM_B2, ADAM_EPS, ADAM_WD, ADAM_STEP = 0.001, 0.9, 0.999, 1e-08, 0.01, 10

VMEM_LIMIT = 48 * 1024 * 1024

ANY = pl.BlockSpec(memory_space=pl.ANY)

NN = (((1,), (0,)), ((), ()))
NT = (((1,), (1,)), ((), ()))
TN = (((0,), (0,)), ((), ()))


def _cparams(sem):
    return pltpu.CompilerParams(dimension_semantics=sem, vmem_limit_bytes=VMEM_LIMIT)


def _dotbf(a, b, dn=NN):
    return lax.dot_general(a.astype(BF16), b.astype(BF16), dn, preferred_element_type=F32)


def _split(a):
    hi = a.astype(BF16)
    lo = (a - hi.astype(F32)).astype(BF16)
    return hi, lo


def _dot3(a, b, dn=NN):
    ah, al = _split(a)
    bh, bl = _split(b)
    d = functools.partial(lax.dot_general, dimension_numbers=dn, preferred_element_type=F32)
    return d(ah, bh) + (d(ah, bl) + d(al, bh))


def _exact01(a, b, dn, mask_left):
    m, x = (a, b) if mask_left else (b, a)
    m = m.astype(BF16)
    hi = x.astype(BF16)
    r1 = x - hi.astype(F32)
    mid = r1.astype(BF16)
    lo = (r1 - mid.astype(F32)).astype(BF16)
    d = functools.partial(lax.dot_general, dimension_numbers=dn, preferred_element_type=F32)
    pair = (lambda p: d(m, p)) if mask_left else (lambda p: d(p, m))
    return pair(hi) + (pair(mid) + pair(lo))


@functools.partial(jax.custom_vjp, nondiff_argnums=(2,))
def _dot01(m01, x, dn):
    return _exact01(m01, x, dn, True)


def _dot01_fwd(m01, x, dn):
    return _exact01(m01, x, dn, True), m01


def _dot01_bwd(dn, m01, ct):
    dx = _exact01(m01, ct, TN, True) if dn == NN else _exact01(ct, m01, TN, False)
    return jnp.zeros_like(m01), dx


_dot01.defvjp(_dot01_fwd, _dot01_bwd)


def _dot_mask(parts, m01):
    d = functools.partial(lax.dot_general, dimension_numbers=NN, preferred_element_type=F32)
    return d(parts[0], m01) + d(parts[1], m01)


def _iota(shape, dim):
    return lax.broadcasted_iota(jnp.int32, shape, dim)


def _softplus(x):
    return jnp.maximum(x, 0.0) + jnp.log(1.0 + jnp.exp(-jnp.abs(x)))


def _log_sigmoid(z):
    return jnp.minimum(z, 0.0) - jnp.log(1.0 + jnp.exp(-jnp.abs(z)))


def _rms(x, g):
    r = lax.rsqrt(jnp.mean(x * x, axis=-1, keepdims=True) + EPS)
    return (x * r) * g


def _matmul(a, b, *, mode, tm, tn, tk, name, out_dtypes=(F32,), epilogue=None, extras=(), n_outer=False):
    if n_outer:
        ix = lambda f: (lambda j, i, kk: f(i, j, kk))
    else:
        ix = lambda f: f
    if mode == "nn":
        (m, k), (k2, n) = a.shape, b.shape
        a_spec = pl.BlockSpec((tm, tk), ix(lambda i, j, kk: (i, kk)))
        b_spec = pl.BlockSpec((tk, tn), ix(lambda i, j, kk: (kk, j)))
        dn = NN
    elif mode == "nt":
        (m, k), (n, k2) = a.shape, b.shape
        a_spec = pl.BlockSpec((tm, tk), ix(lambda i, j, kk: (i, kk)))
        b_spec = pl.BlockSpec((tn, tk), ix(lambda i, j, kk: (j, kk)))
        dn = NT
    else:
        (k, m), (k2, n) = a.shape, b.shape
        a_spec = pl.BlockSpec((tk, tm), ix(lambda i, j, kk: (kk, i)))
        b_spec = pl.BlockSpec((tk, tn), ix(lambda i, j, kk: (kk, j)))
        dn = TN
    assert k == k2 and m % tm == 0 and n % tn == 0 and k % tk == 0, (name, a.shape, b.shape)
    assert a.dtype == BF16 and b.dtype == BF16, name
    nk = k // tk
    n_extra, n_out = len(extras), len(out_dtypes)

    def finish(acc, extra_refs, out_refs):
        outs = (acc,) if epilogue is None else epilogue(acc, *[r[...] for r in extra_refs])
        for o_ref, o in zip(out_refs, outs):
            o_ref[...] = o.astype(o_ref.dtype)

    def body_single(a_ref, b_ref, *rest):
        acc = lax.dot_general(a_ref[...], b_ref[...], dn, preferred_element_type=F32)
        finish(acc, rest[:n_extra], rest[n_extra:n_extra + n_out])

    def body_tiled(a_ref, b_ref, *rest):
        extra_refs, out_refs, acc_ref = rest[:n_extra], rest[n_extra:n_extra + n_out], rest[-1]
        kk = pl.program_id(2)

        @pl.when(kk == 0)
        def _():
            acc_ref[...] = jnp.zeros_like(acc_ref)

        acc_ref[...] += lax.dot_general(a_ref[...], b_ref[...], dn, preferred_element_type=F32)

        @pl.when(kk == nk - 1)
        def _():
            finish(acc_ref[...], extra_refs, out_refs)

    mn_spec = pl.BlockSpec((tm, tn), ix(lambda i, j, kk: (i, j)))
    grid = (n // tn, m // tm, nk) if n_outer else (m // tm, n // tn, nk)
    outs = pl.pallas_call(
        body_single if nk == 1 else body_tiled,
        name=name,
        grid=grid,
        in_specs=[a_spec, b_spec] + [mn_spec] * n_extra,
        out_specs=[mn_spec] * n_out,
        out_shape=[jax.ShapeDtypeStruct((m, n), dt) for dt in out_dtypes],
        scratch_shapes=[] if nk == 1 else [pltpu.VMEM((tm, tn), F32)],
        compiler_params=_cparams(("parallel", "parallel", "arbitrary")),
    )(a, b, *extras)
    return outs[0] if n_out == 1 else outs


ROW_TILE = 512


def _row_spec(width=D_MODEL, tile=ROW_TILE):
    return pl.BlockSpec((tile, width), lambda i: (i, 0))


def _vec_spec(width=D_MODEL):
    return pl.BlockSpec((1, width), lambda i: (0, 0))


def _rms_fwd(x, g, *, name, tile=ROW_TILE):
    t = x.shape[0]

    def body(x_ref, g_ref, h_ref):
        h_ref[...] = _rms(x_ref[...], g_ref[...]).astype(BF16)

    return pl.pallas_call(
        body, name=name, grid=(t // tile,),
        in_specs=[_row_spec(tile=tile), _vec_spec()], out_specs=_row_spec(tile=tile),
        out_shape=jax.ShapeDtypeStruct((t, D_MODEL), BF16),
        compiler_params=_cparams(("parallel",)),
    )(x, g.reshape(1, D_MODEL))


def _post_norm_add(xres, y, g_post, g_next, *, name):
    t = xres.shape[0]

    def body(x_ref, y_ref, gp_ref, gn_ref, xo_ref, h_ref):
        xo = x_ref[...] + _rms(y_ref[...], gp_ref[...])
        xo_ref[...] = xo
        h_ref[...] = _rms(xo, gn_ref[...]).astype(BF16)

    return pl.pallas_call(
        body, name=name, grid=(t // ROW_TILE,),
        in_specs=[_row_spec(), _row_spec(), _vec_spec(), _vec_spec()],
        out_specs=[_row_spec(), _row_spec()],
        out_shape=[jax.ShapeDtypeStruct((t, D_MODEL), F32), jax.ShapeDtypeStruct((t, D_MODEL), BF16)],
        compiler_params=_cparams(("parallel",)),
    )(xres, y, g_post.reshape(1, D_MODEL), g_next.reshape(1, D_MODEL))


def _post_norm_loss(xres, y, g_post, target, *, name):
    t = xres.shape[0]

    def body(x_ref, y_ref, gp_ref, t_ref, loss_ref, dx_ref):
        @pl.when(pl.program_id(0) == 0)
        def _():
            loss_ref[...] = jnp.zeros_like(loss_ref)

        err = (x_ref[...] + _rms(y_ref[...], gp_ref[...])) - t_ref[...]
        per_tok = jnp.mean(err * err, axis=-1, keepdims=True)
        loss_ref[...] += 0.5 * jnp.sum(per_tok, axis=0, keepdims=True)
        dx_ref[...] = err * (1.0 / D_MODEL)

    return pl.pallas_call(
        body, name=name, grid=(t // ROW_TILE,),
        in_specs=[_row_spec(), _row_spec(), _vec_spec(), _row_spec()],
        out_specs=[pl.BlockSpec((1, 128), lambda i: (0, 0)), _row_spec()],
        out_shape=[jax.ShapeDtypeStruct((1, 128), F32), jax.ShapeDtypeStruct((t, D_MODEL), F32)],
        compiler_params=_cparams(("arbitrary",)),
    )(xres, y, g_post.reshape(1, D_MODEL), target)


def _rms_bwd(dy, x, g, *, name, res=None, out_dtype=F32, tile=ROW_TILE):
    t = x.shape[0]
    has_res = res is not None

    def body(dy_ref, x_ref, g_ref, *rest):
        res_ref = rest[0] if has_res else None
        dx_ref, dg_ref = rest[-2], rest[-1]

        @pl.when(pl.program_id(0) == 0)
        def _():
            dg_ref[...] = jnp.zeros_like(dg_ref)

        xf = x_ref[...]
        dyf = dy_ref[...].astype(F32)
        r = lax.rsqrt(jnp.mean(xf * xf, axis=-1, keepdims=True) + EPS)
        xhat = xf * r
        dg_ref[...] += jnp.sum(dyf * xhat, axis=0, keepdims=True)
        dxh = dyf * g_ref[...]
        dx = r * (dxh - xhat * jnp.mean(dxh * xhat, axis=-1, keepdims=True))
        if has_res:
            dx = dx + res_ref[...]
        dx_ref[...] = dx.astype(dx_ref.dtype)

    args = [dy, x, g.reshape(1, D_MODEL)] + ([res] if has_res else [])
    return pl.pallas_call(
        body, name=name, grid=(t // tile,),
        in_specs=[_row_spec(tile=tile), _row_spec(tile=tile), _vec_spec()] + ([_row_spec(tile=tile)] if has_res else []),
        out_specs=[_row_spec(tile=tile), _vec_spec()],
        out_shape=[jax.ShapeDtypeStruct((t, D_MODEL), out_dtype), jax.ShapeDtypeStruct((1, D_MODEL), F32)],
        compiler_params=_cparams(("arbitrary",)),
    )(*args)


def _rms_bwd_pair(dh, x, g_pre, res, y, g_post, *, name):
    t = x.shape[0]

    def norm_bwd(dy, xf, g):
        r = lax.rsqrt(jnp.mean(xf * xf, axis=-1, keepdims=True) + EPS)
        xhat = xf * r
        dxh = dy * g
        dx = r * (dxh - xhat * jnp.mean(dxh * xhat, axis=-1, keepdims=True))
        return dx, jnp.sum(dy * xhat, axis=0, keepdims=True)

    def body(dh_ref, x_ref, gp_ref, res_ref, y_ref, gq_ref, dx_ref, dy_ref, dgp_ref, dgq_ref):
        @pl.when(pl.program_id(0) == 0)
        def _():
            dgp_ref[...] = jnp.zeros_like(dgp_ref)
            dgq_ref[...] = jnp.zeros_like(dgq_ref)

        dx, dgp = norm_bwd(dh_ref[...], x_ref[...], gp_ref[...])
        dx = dx + res_ref[...]
        dx_ref[...] = dx
        dy, dgq = norm_bwd(dx, y_ref[...], gq_ref[...])
        dy_ref[...] = dy.astype(BF16)
        dgp_ref[...] += dgp
        dgq_ref[...] += dgq

    return pl.pallas_call(
        body, name=name, grid=(t // ROW_TILE,),
        in_specs=[_row_spec(), _row_spec(), _vec_spec(), _row_spec(), _row_spec(), _vec_spec()],
        out_specs=[_row_spec(), _row_spec(), _vec_spec(), _vec_spec()],
        out_shape=[jax.ShapeDtypeStruct((t, D_MODEL), F32), jax.ShapeDtypeStruct((t, D_MODEL), BF16),
                   jax.ShapeDtypeStruct((1, D_MODEL), F32), jax.ShapeDtypeStruct((1, D_MODEL), F32)],
        compiler_params=_cparams(("arbitrary",)),
    )(dh, x, g_pre.reshape(1, D_MODEL), res, y, g_post.reshape(1, D_MODEL))


CONV_COLS = 256
N_CONV_BLOCKS = 3 * MIX_W // CONV_COLS
CONV_STRIP = 128


def _shift_down(x, k):
    if k == 0:
        return x
    return jnp.where(_iota(x.shape, 0) >= k, pltpu.roll(x, k, 0), 0.0)


def _shift_up(x, k):
    if k == 0:
        return x
    s = x.shape[0]
    return jnp.where(_iota(x.shape, 0) < s - k, pltpu.roll(x, s - k, 0), 0.0)


def _conv_pre(x, w_ref):
    c = w_ref[CONV_K - 1:CONV_K, :] * x
    for i in range(CONV_K - 1):
        c = c + w_ref[i:i + 1, :] * _shift_down(x, CONV_K - 1 - i)
    return c


def _conv_silu_fwd(proj, conv_w, n_batch):
    strip, halo = CONV_STRIP, 8

    def body(x_ref, w_ref, y_ref, xpad):
        xpad[0:halo, :] = jnp.zeros((halo, CONV_COLS), F32)
        xpad[halo:, :] = x_ref[...]
        taps = [w_ref[i:i + 1, :] for i in range(CONV_K)]

        def one(s, carry):
            a = pl.multiple_of(s * strip, strip)
            win = xpad[pl.ds(a, strip + halo), :]
            c = taps[CONV_K - 1] * win[halo:]
            for i in range(CONV_K - 1):
                c = c + taps[i] * pltpu.roll(win, CONV_K - 1 - i, 0)[halo:]
            y_ref[pl.ds(a, strip), :] = c * jax.nn.sigmoid(c)
            return carry

        lax.fori_loop(0, SEQ // strip, one, 0)

    return pl.pallas_call(
        body, name="conv_silu_fwd", grid=(n_batch, N_CONV_BLOCKS),
        scratch_shapes=[pltpu.VMEM((SEQ + 8, CONV_COLS), F32)],
        in_specs=[pl.BlockSpec((SEQ, CONV_COLS), lambda b, j: (b, j)),
                  pl.BlockSpec((CONV_K, CONV_COLS), lambda b, j: (0, j))],
        out_specs=pl.BlockSpec((SEQ, CONV_COLS), lambda b, j: (b, j)),
        out_shape=jax.ShapeDtypeStruct((n_batch * SEQ, 3 * MIX_W), F32),
        compiler_params=_cparams(("parallel", "parallel")),
    )(proj, conv_w)


def _conv_silu_bwd(dy, proj, conv_w, dproj, n_batch):
    strip, halo = CONV_STRIP, 8
    n_strips = SEQ // strip

    def body(dy_ref, x_ref, w_ref, _, dx_ref, dw_ref, xpad, dcpad):
        @pl.when(pl.program_id(1) == 0)
        def _():
            dw_ref[...] = jnp.zeros_like(dw_ref)

        xpad[0:halo, :] = jnp.zeros((halo, CONV_COLS), F32)
        xpad[halo:, :] = x_ref[...]
        dcpad[SEQ:, :] = jnp.zeros((halo, CONV_COLS), F32)
        taps = [w_ref[i:i + 1, :] for i in range(CONV_K)]

        def first(s, dw):
            a = pl.multiple_of(s * strip, strip)
            win = xpad[pl.ds(a, strip + halo), :]
            xs = [(win if i == CONV_K - 1 else pltpu.roll(win, CONV_K - 1 - i, 0))[halo:] for i in range(CONV_K)]
            c = taps[0] * xs[0]
            for i in range(1, CONV_K):
                c = c + taps[i] * xs[i]
            sig = jax.nn.sigmoid(c)
            dc = dy_ref[pl.ds(a, strip), :] * (sig * (1.0 + c * (1.0 - sig)))
            dcpad[pl.ds(a, strip), :] = dc
            return tuple(dw[i] + jnp.sum(dc * xs[i], axis=0, keepdims=True) for i in range(CONV_K))

        dw = lax.fori_loop(0, n_strips, first, tuple(jnp.zeros((1, CONV_COLS), F32) for _ in range(CONV_K)))
        for i in range(CONV_K):
            dw_ref[i:i + 1, :] += dw[i]

        def second(s, carry):
            a = pl.multiple_of(s * strip, strip)
            win = dcpad[pl.ds(a, strip + halo), :]
            dx = taps[CONV_K - 1] * win[:strip]
            for i in range(CONV_K - 1):
                dx = dx + taps[i] * pltpu.roll(win, strip + halo - (CONV_K - 1 - i), 0)[:strip]
            dx_ref[pl.ds(a, strip), :] = dx.astype(BF16)
            return carry

        lax.fori_loop(0, n_strips, second, 0)

    return pl.pallas_call(
        body, name="conv_silu_bwd", grid=(N_CONV_BLOCKS, n_batch),
        in_specs=[pl.BlockSpec((SEQ, CONV_COLS), lambda j, b: (b, j)),
                  pl.BlockSpec((SEQ, CONV_COLS), lambda j, b: (b, j)),
                  pl.BlockSpec((CONV_K, CONV_COLS), lambda j, b: (0, j)), ANY],
        out_specs=[pl.BlockSpec((SEQ, CONV_COLS), lambda j, b: (b, j)),
                   pl.BlockSpec((CONV_K, CONV_COLS), lambda j, b: (0, j))],
        out_shape=[jax.ShapeDtypeStruct(dproj.shape, BF16),
                   jax.ShapeDtypeStruct((CONV_K, 3 * MIX_W), F32)],
        input_output_aliases={3: 0},
        scratch_shapes=[pltpu.VMEM((SEQ + 8, CONV_COLS), F32), pltpu.VMEM((SEQ + 8, CONV_COLS), F32)],
        compiler_params=_cparams(("parallel", "arbitrary")),
    )(dy, proj, conv_w, dproj)


@jax.custom_vjp
def _solve_apply(low, rhs, tinv):
    return _dot3(tinv, rhs)


def _solve_apply_fwd(low, rhs, tinv):
    sol = _dot3(tinv, rhs)
    return sol, (tinv, sol)


def _solve_apply_bwd(resid, g):
    tinv, sol = resid
    y = _dotbf(tinv, g, TN)
    return -_dotbf(y, sol, NT), y, jnp.zeros_like(tinv)


_solve_apply.defvjp(_solve_apply_fwd, _solve_apply_bwd)


def _inv_unit_lower(lows):
    c = lows[0].shape[0]
    eye = (_iota((c, c), 0) == _iota((c, c), 1)).astype(F32)
    ms = [-low for low in lows]
    ps = [eye + m for m in ms]
    for _ in range(5):
        ms = [_dot3(m, m) for m in ms]
        ps = [p + _dot3(p, m) for p, m in zip(ps, ms)]
    return ps


def _gdn_chunk(qs, ks, vs, gates, states, small, alog_row, dtb_row, gain_row, tinvs):
    c = small.shape[0]
    heads = range(N_LIN)
    lane = _iota((c, 128), 1)
    row, col = _iota((c, c), 0), _iota((c, c), 1)
    causal, strict = row >= col, row > col
    last = _iota((c, 1), 0) == c - 1

    beta_all = jax.nn.sigmoid(small)
    g_all = -jnp.exp(alog_row) * _softplus(small + dtb_row)
    gc_all = _dot01((col <= row).astype(F32), g_all, NN)

    beta = [jnp.sum(jnp.where(lane == h, beta_all, 0.0), axis=1, keepdims=True) for h in heads]
    gc = [jnp.sum(jnp.where(lane == N_LIN + h, gc_all, 0.0), axis=1, keepdims=True) for h in heads]
    gc_j = [_dot01((lane == N_LIN + h).astype(F32), gc_all, NT) for h in heads]
    decay = [jnp.where(causal, jnp.exp(jnp.where(causal, gc[h] - gc_j[h], 0.0)), 0.0) for h in heads]
    gc_last = [jnp.sum(jnp.where(last, gc[h], 0.0), axis=0, keepdims=True) for h in heads]
    egc = [jnp.exp(g) for g in gc]
    qn = [q * lax.rsqrt(jnp.sum(q * q, axis=-1, keepdims=True) + EPS) * (LIN_DH ** -0.5) for q in qs]
    kn = [k * lax.rsqrt(jnp.sum(k * k, axis=-1, keepdims=True) + EPS) for k in ks]
    kb = [kn[h] * beta[h] for h in heads]
    low = [jnp.where(strict, _dotbf(kb[h], kn[h], NT) * decay[h], 0.0) for h in heads]
    if tinvs is None:
        tinvs = _inv_unit_lower(low)
    u = [_solve_apply(low[h], vs[h] * beta[h], tinvs[h]) for h in heads]
    w = [_solve_apply(low[h], kb[h] * egc[h], tinvs[h]) for h in heads]
    intra = [_dotbf(qn[h], kn[h], NT) * decay[h] for h in heads]
    v_new = [u[h] - _dotbf(w[h], states[h]) for h in heads]
    o = [_dotbf(qn[h] * egc[h], states[h]) + _dotbf(intra[h], v_new[h]) for h in heads]
    new_states = [states[h] * jnp.exp(gc_last[h]) + _dotbf(kn[h] * jnp.exp(gc_last[h] - gc[h]), v_new[h], TN)
                  for h in heads]
    o = [x * lax.rsqrt(jnp.mean(x * x, axis=-1, keepdims=True) + EPS) * gain_row for x in o]
    outs = [o[h] * (gates[h] * jax.nn.sigmoid(gates[h])) for h in heads]
    return outs, new_states, tinvs


def _gdn_param_rows(a_log, dt_bias, onorm):
    row = lambda v: jnp.pad(v.reshape(1, N_LIN), ((0, 0), (N_LIN, 128 - 2 * N_LIN)))
    return row(a_log), row(dt_bias), onorm.reshape(1, LIN_DH)


def _head(ref_or_val, h):
    return ref_or_val[:, LIN_DH * h:LIN_DH * (h + 1)]


def _gdn_fwd(qkv, proj, alog_row, dtb_row, gain_row, n_batch, gather=None):
    nc = SEQ // CHUNK
    t = n_batch * SEQ
    steps = n_batch * nc

    def body(qkv_ref, small_ref, gate_ref, al_ref, dt_ref, gn_ref, *rest):
        if gather is None:
            mix_ref, st_ref, ti_ref, s_scr = rest
        else:
            w_ref, mix_ref, st_ref, ti_ref, out_ref, s_scr, send_sems, recv_sems = rest
            step = pl.program_id(0) * nc + pl.program_id(1)
            for at, phase in ((0, "start"), (3 * steps // 4, "forward"), (steps - 1, "finish")):
                @pl.when(step == at)
                def _(phase=phase):
                    getattr(_Gather(w_ref, out_ref, send_sems, recv_sems), phase)()

        @pl.when(pl.program_id(1) == 0)
        def _():
            s_scr[...] = jnp.zeros_like(s_scr)

        heads = range(N_LIN)
        states = [s_scr[h] for h in heads]
        outs, new_states, tinvs = _gdn_chunk(
            [_head(qkv_ref, h) for h in heads], [_head(qkv_ref, N_LIN + h) for h in heads],
            [_head(qkv_ref, 2 * N_LIN + h) for h in heads], [_head(gate_ref, h) for h in heads],
            states, small_ref[...], al_ref[...], dt_ref[...], gn_ref[...], None)
        mix_ref[...] = jnp.concatenate(outs, axis=1).astype(BF16)
        for h in heads:
            st_ref[0, 0, h] = states[h]
            s_scr[h] = new_states[h]
            ti_ref[0, 0, h] = tinvs[h]

    row = lambda b, n: b * nc + n
    vec = pl.BlockSpec((1, 128), lambda b, n: (0, 0))
    extra = [] if gather is None else [gather]
    return pl.pallas_call(
        body, name="gdn_fwd", grid=(n_batch, nc),
        in_specs=[pl.BlockSpec((CHUNK, 3 * MIX_W), lambda b, n: (row(b, n), 0)),
                  pl.BlockSpec((CHUNK, 128), lambda b, n: (row(b, n), SMALL_COL)),
                  pl.BlockSpec((CHUNK, MIX_W), lambda b, n: (row(b, n), 3)),
                  vec, vec, vec] + [ANY] * len(extra),
        out_specs=[pl.BlockSpec((CHUNK, MIX_W), lambda b, n: (row(b, n), 0)),
                   pl.BlockSpec((1, 1, N_LIN, LIN_DH, LIN_DH), lambda b, n: (b, n, 0, 0, 0)),
                   pl.BlockSpec((1, 1, N_LIN, CHUNK, CHUNK), lambda b, n: (b, n, 0, 0, 0))] + [ANY] * len(extra),
        out_shape=[jax.ShapeDtypeStruct((t, D_MODEL), BF16),
                   jax.ShapeDtypeStruct((n_batch, nc, N_LIN, LIN_DH, LIN_DH), F32),
                   jax.ShapeDtypeStruct((n_batch, nc, N_LIN, CHUNK, CHUNK), F32)]
                  + [jax.ShapeDtypeStruct(g.shape, g.dtype) for g in extra],
        input_output_aliases={6: 3} if extra else {},
        scratch_shapes=[pltpu.VMEM((N_LIN, LIN_DH, LIN_DH), F32)] + (GATHER_SEMS if extra else []),
        compiler_params=_cparams(("arbitrary", "arbitrary")),
    )(qkv, proj, proj, alog_row, dtb_row, gain_row, *extra)


def _gdn_bwd(dcat, qkv, proj, states, tinvs, alog_row, dtb_row, gain_row, n_batch, exchange=None):
    nc = SEQ // CHUNK
    t = n_batch * SEQ
    steps = n_batch * nc

    def body(dmix_ref, qkv_ref, small_ref, gate_ref, st_ref, ti_ref, al_ref, dt_ref, gn_ref, *rest):
        if exchange is None:
            dqkv_ref, dgate_ref, dsmall_ref, dal_ref, ddt_ref, dgn_ref, ds_scr = rest
        else:
            p_ref, dqkv_ref, dgate_ref, dsmall_ref, dal_ref, ddt_ref, dgn_ref, q_ref, ds_scr, send_sems, recv_sems = rest
            step = pl.program_id(0) * nc + pl.program_id(1)

            @pl.when(step == 0)
            def _():
                for cp in _exchange_copies(p_ref, q_ref, send_sems, recv_sems):
                    cp.start()

            @pl.when(step == steps - 1)
            def _():
                for cp in _exchange_copies(p_ref, q_ref, send_sems, recv_sems):
                    cp.wait()

        @pl.when(pl.program_id(1) == 0)
        def _():
            ds_scr[...] = jnp.zeros_like(ds_scr)

        @pl.when((pl.program_id(0) == 0) & (pl.program_id(1) == 0))
        def _():
            dal_ref[...] = jnp.zeros_like(dal_ref)
            ddt_ref[...] = jnp.zeros_like(ddt_ref)
            dgn_ref[...] = jnp.zeros_like(dgn_ref)

        heads = range(N_LIN)
        tinvs = [ti_ref[0, 0, h] for h in heads]

        def chunk(qs, ks, vs, gates, states_in, small, al, dt, gn):
            outs, new_states, _ = _gdn_chunk(qs, ks, vs, gates, states_in, small, al, dt, gn, tinvs)
            return tuple(outs), tuple(new_states)

        prim = (tuple(_head(qkv_ref, h) for h in heads),
                tuple(_head(qkv_ref, N_LIN + h) for h in heads),
                tuple(_head(qkv_ref, 2 * N_LIN + h) for h in heads),
                tuple(_head(gate_ref, h) for h in heads),
                tuple(st_ref[0, 0, h] for h in heads),
                small_ref[...], al_ref[...], dt_ref[...], gn_ref[...])
        _, vjp = jax.vjp(chunk, *prim)
        cot = (tuple(_head(dmix_ref, h) for h in heads), tuple(ds_scr[h] for h in heads))
        dq, dk, dv, dgate, dstate, dsmall, dal, ddt, dgn = vjp(cot)
        dqkv_ref[...] = jnp.concatenate(list(dq) + list(dk) + list(dv), axis=1)
        dgate_ref[...] = jnp.concatenate(list(dgate), axis=1).astype(BF16)
        dsmall_ref[...] = dsmall.astype(BF16)
        for h in heads:
            ds_scr[h] = dstate[h]
        dal_ref[...] += dal
        ddt_ref[...] += ddt
        dgn_ref[...] += dgn

    row = lambda b, n: b * nc + (nc - 1 - n)
    vec = pl.BlockSpec((1, 128), lambda b, n: (0, 0))
    extra = [] if exchange is None else [exchange]
    return pl.pallas_call(
        body, name="gdn_bwd", grid=(n_batch, nc),
        in_specs=[pl.BlockSpec((CHUNK, MIX_W), lambda b, n: (row(b, n), 0)),
                  pl.BlockSpec((CHUNK, 3 * MIX_W), lambda b, n: (row(b, n), 0)),
                  pl.BlockSpec((CHUNK, 128), lambda b, n: (row(b, n), SMALL_COL)),
                  pl.BlockSpec((CHUNK, MIX_W), lambda b, n: (row(b, n), 3)),
                  pl.BlockSpec((1, 1, N_LIN, LIN_DH, LIN_DH), lambda b, n: (b, nc - 1 - n, 0, 0, 0)),
                  pl.BlockSpec((1, 1, N_LIN, CHUNK, CHUNK), lambda b, n: (b, nc - 1 - n, 0, 0, 0)),
                  vec, vec, vec] + [ANY] * len(extra),
        out_specs=[pl.BlockSpec((CHUNK, 3 * MIX_W), lambda b, n: (row(b, n), 0)),
                   pl.BlockSpec((CHUNK, MIX_W), lambda b, n: (row(b, n), 3)),
                   pl.BlockSpec((CHUNK, 128), lambda b, n: (row(b, n), 0)),
                   vec, vec, vec] + [ANY] * len(extra),
        out_shape=[jax.ShapeDtypeStruct((t, 3 * MIX_W), F32),
                   jax.ShapeDtypeStruct((t, IN_A_PAD), BF16),
                   jax.ShapeDtypeStruct((t, 128), BF16),
                   jax.ShapeDtypeStruct((1, 128), F32),
                   jax.ShapeDtypeStruct((1, 128), F32),
                   jax.ShapeDtypeStruct((1, 128), F32)]
                  + [jax.ShapeDtypeStruct((3,) + p.shape[1:], p.dtype) for p in extra],
        scratch_shapes=[pltpu.VMEM((N_LIN, LIN_DH, LIN_DH), F32)] + (EXCHANGE_SEMS if extra else []),
        compiler_params=_cparams(("arbitrary", "arbitrary")),
    )(dcat, qkv, proj, proj, states, tinvs, alog_row, dtb_row, gain_row, *extra)


SB_T = 256


def _sb_masks():
    r, c = _iota((SB_T, SB_T), 0), _iota((SB_T, SB_T), 1)
    return r, c


def _staggered(chains):
    pending, live = list(chains), []
    while pending or live:
        if pending:
            live.append(pending.pop(0))
        for g in list(live):
            try:
                next(g)
            except StopIteration:
                live.remove(g)


def _sb_rows(kb):
    start = kb * SB_T
    return pl.ds(start if isinstance(kb, int) else pl.multiple_of(start, SB_T), SB_T)


def _sb_fwd(proj, n_batch):
    nq = SEQ // SB_T
    t = n_batch * SEQ
    scale = SB_DH ** -0.5
    both = range(2)

    def body(q_ref, k_ref, v_ref, o_ref, tot_ref, acc_scr, run_scr):
        qi = pl.program_id(2)
        lane = _iota((SB_T, 128), 1)
        r, c = _sb_masks()
        upper = (r > c).astype(BF16)
        q = q_ref[...] * scale
        qm = [jnp.where((lane < SB_DH) == (hh == 0), q, jnp.zeros_like(q)) for hh in both]

        def blocks(kbs, diag=None):
            first = diag is not None
            k_blk = [k_ref[_sb_rows(kb), :] for kb in kbs]
            v_blk = [v_ref[_sb_rows(kb), :] for kb in kbs]
            run = [None if first else run_scr[hh][:, 0:1] for hh in both]
            pv = {hh: [] for hh in both}
            rowsums = {hh: [] for hh in both}

            def chain(n, hh):
                z = lax.dot_general(qm[hh], k_blk[n], NT, preferred_element_type=F32)
                yield
                lb = _log_sigmoid(z)
                l1m = lb - z
                if n == diag:
                    l1m = jnp.where(r > c, l1m, 0.0)
                parts = _split(l1m)
                terms = ([] if first else [run[hh]]) + rowsums[hh]
                before = sum(terms[1:], terms[0]) if terms else None
                rowsums[hh].append(jnp.sum(l1m, axis=1, keepdims=True))
                yield
                tail = _dot_mask(parts, upper)
                yield
                a = jnp.exp(lb + (tail if before is None else before + tail))
                if n == diag:
                    a = jnp.where(r > c, a, 0.0)
                a = a.astype(BF16)
                yield
                pv[hh].append(lax.dot_general(a, v_blk[n], NN, preferred_element_type=F32))

            _staggered([chain(n, hh) for n in range(len(kbs)) for hh in both])
            for hh in both:
                if first:
                    acc_scr[hh] = sum(pv[hh][1:], pv[hh][0])
                    run_scr[hh] = jnp.broadcast_to(sum(rowsums[hh][1:], rowsums[hh][0]), (SB_T, 128))
                else:
                    acc_scr[hh] += sum(pv[hh][1:], pv[hh][0])
                    run_scr[hh] += sum(rowsums[hh][1:], rowsums[hh][0])

        @pl.when(qi == 0)
        def _():
            blocks([0], diag=0)

        @pl.when((qi & 1) == 1)
        def _():
            blocks([qi, qi - 1], diag=0)

        @pl.when((qi >= 2) & ((qi & 1) == 0))
        def _():
            blocks([qi, qi - 1, qi - 2], diag=0)

        rest = jnp.where(qi == 0, 0, ((qi - 1) >> 1) << 1)

        def step(it, carry):
            kb = rest - 1 - 2 * it
            blocks([kb, kb - 1])
            return carry

        lax.fori_loop(0, rest >> 1, step, 0)
        first = lane < SB_DH
        o_ref[...] = jnp.where(first, acc_scr[0], acc_scr[1]).astype(BF16)
        tot_ref[...] = jnp.where(first, run_scr[0], run_scr[1])

    nq_blocks = lambda b, p, i: (b * nq + i, p)
    seq_spec = lambda which: pl.BlockSpec((SEQ, 128), lambda b, p, i: (b, 3 * p + which))
    return pl.pallas_call(
        body, name="sb_fwd", grid=(n_batch, SB_PAIRS, nq),
        in_specs=[pl.BlockSpec((SB_T, 128), lambda b, p, i: (b * nq + i, 3 * p)), seq_spec(1), seq_spec(2)],
        out_specs=[pl.BlockSpec((SB_T, 128), nq_blocks), pl.BlockSpec((SB_T, 128), nq_blocks)],
        out_shape=[jax.ShapeDtypeStruct((t, D_MODEL), BF16),
                   jax.ShapeDtypeStruct((t, MIX_W), F32)],
        scratch_shapes=[pltpu.VMEM((2, SB_T, 128), F32), pltpu.VMEM((2, SB_T, 128), F32)],
        compiler_params=_cparams(("parallel", "parallel", "arbitrary")),
    )(proj, proj, proj)


def _sb_bwd(dcat, proj, totals, n_batch):
    nq = SEQ // SB_T
    t = n_batch * SEQ
    scale = SB_DH ** -0.5
    both = range(2)

    def body(do_ref, q_ref, k_ref, v_ref, tot_ref, dp_ref, dq_scr, run_scr, grun_scr, dk_ref, dv_ref):
        qi = pl.program_id(2)

        @pl.when(qi == 0)
        def _():
            dk_ref[...] = jnp.zeros_like(dk_ref)
            dv_ref[...] = jnp.zeros_like(dv_ref)

        lane = _iota((SB_T, 128), 1)
        r, c = _sb_masks()
        incl = (r <= c).astype(BF16)
        earlier = (r < c).astype(BF16)
        dq_scr[...] = jnp.zeros_like(dq_scr)
        run_scr[...] = jnp.zeros_like(run_scr)
        grun_scr[...] = jnp.zeros_like(grun_scr)
        q, do, tot = q_ref[...] * scale, do_ref[...], tot_ref[...]
        sel = [(lane < SB_DH) == (hh == 0) for hh in both]
        qm = [jnp.where(sel[hh], q, jnp.zeros_like(q)) for hh in both]
        dom = [jnp.where(sel[hh], do, 0.0).astype(BF16) for hh in both]
        total = [jnp.sum(jnp.where(lane == hh * SB_DH, tot, 0.0), axis=1, keepdims=True) for hh in both]

        def blocks(kbs, diag=None):
            k_blk = [k_ref[_sb_rows(kb), :] for kb in kbs]
            v_blk = [v_ref[_sb_rows(kb), :] for kb in kbs]
            run = [run_scr[hh][:, 0:1] for hh in both]
            grun = [grun_scr[hh][:, 0:1] for hh in both]
            rs_l, rs_e, dqp = ({hh: [] for hh in both} for _ in range(3))
            dk, dv = ([[] for _ in kbs] for _ in range(2))

            def plus(base, terms):
                return base if not terms else base + sum(terms[1:], terms[0])

            def chain(n, hh):
                z = lax.dot_general(qm[hh], k_blk[n], NT, preferred_element_type=F32)
                da = lax.dot_general(dom[hh], v_blk[n], NT, preferred_element_type=F32)
                yield
                lb = _log_sigmoid(z)
                sig = jnp.exp(lb)
                l1m = lb - z
                if n == diag:
                    l1m = jnp.where(r > c, l1m, 0.0)
                parts = _split(l1m)
                run_before = plus(run[hh], rs_l[hh])
                rs_l[hh].append(jnp.sum(l1m, axis=1, keepdims=True))
                yield
                prefix = run_before + _dot_mask(parts, incl)
                yield
                a = jnp.exp(lb + (total[hh] - prefix))
                if n == diag:
                    a = jnp.where(r > c, a, 0.0)
                de = a * da
                a = a.astype(BF16)
                parts = _split(de)
                grun_before = plus(grun[hh], rs_e[hh])
                rs_e[hh].append(jnp.sum(de, axis=1, keepdims=True))
                yield
                dv[n].append(lax.dot_general(a, dom[hh], TN, preferred_element_type=F32))
                dl1m = grun_before + _dot_mask(parts, earlier)
                yield
                if n == diag:
                    dl1m = jnp.where(r > c, dl1m, 0.0)
                dz = (de * (1.0 - sig) - dl1m * sig).astype(BF16)
                yield
                dqp[hh].append(lax.dot_general(dz, k_blk[n], NN, preferred_element_type=F32))
                dk[n].append(lax.dot_general(dz, qm[hh], TN, preferred_element_type=F32))

            _staggered([chain(n, hh) for n in range(len(kbs)) for hh in both])
            for hh in both:
                dq_scr[hh] += sum(dqp[hh][1:], dqp[hh][0])
                run_scr[hh] += sum(rs_l[hh][1:], rs_l[hh][0])
                grun_scr[hh] += sum(rs_e[hh][1:], rs_e[hh][0])
            for n, kb in enumerate(kbs):
                dk_ref[_sb_rows(kb), :] += dk[n][0] + dk[n][1]
                dv_ref[_sb_rows(kb), :] += dv[n][0] + dv[n][1]

        rest = jnp.where(qi == 0, 0, ((qi - 1) >> 1) << 1)

        def step(it, carry):
            blocks([2 * it, 2 * it + 1])
            return carry

        lax.fori_loop(0, rest >> 1, step, 0)

        @pl.when(qi == 0)
        def _():
            blocks([0], diag=0)

        @pl.when((qi & 1) == 1)
        def _():
            blocks([qi - 1, qi], diag=1)

        @pl.when((qi >= 2) & ((qi & 1) == 0))
        def _():
            blocks([qi - 2, qi - 1, qi], diag=2)

        dq = (jnp.where(sel[0], dq_scr[0], dq_scr[1]) * scale).astype(BF16)
        dp_ref[pl.ds(pl.multiple_of(qi * SB_T, SB_T), SB_T), 0:128] = dq

        @pl.when(qi == nq - 1)
        def _():
            dp_ref[:, 128:256] = dk_ref[...].astype(BF16)
            dp_ref[:, 256:384] = dv_ref[...].astype(BF16)

    q_blocks = lambda b, p, i: (b * nq + i, p)
    seq_spec = lambda which: pl.BlockSpec((SEQ, 128), lambda b, p, i: (b, 3 * p + which))
    return pl.pallas_call(
        body, name="sb_bwd", grid=(n_batch, SB_PAIRS, nq),
        in_specs=[pl.BlockSpec((SB_T, 128), q_blocks),
                  pl.BlockSpec((SB_T, 128), lambda b, p, i: (b * nq + i, 3 * p)),
                  seq_spec(1), seq_spec(2), pl.BlockSpec((SB_T, 128), q_blocks)],
        out_specs=pl.BlockSpec((SEQ, 384), lambda b, p, i: (b, p)),
        out_shape=jax.ShapeDtypeStruct((t, IN_B), BF16),
        scratch_shapes=[pltpu.VMEM((2, SB_T, 128), F32), pltpu.VMEM((2, SB_T, 128), F32),
                        pltpu.VMEM((2, SB_T, 128), F32), pltpu.VMEM((SEQ, 128), F32), pltpu.VMEM((SEQ, 128), F32)],
        compiler_params=_cparams(("parallel", "arbitrary", "arbitrary")),
    )(dcat, proj, proj, proj, totals)


MEM_TQ = 512


def _mem_heads(lane):
    return [(lane >= X_HEAD_DIM * h) & (lane < X_HEAD_DIM * (h + 1)) for h in range(N_X_HEADS)]


def _mem_attn_fwd(proj, q_col, memkv, cat, n_batch):
    nq = SEQ // MEM_TQ
    scale = X_HEAD_DIM ** -0.5

    def body(q_ref, kv_ref, _, o_ref):
        q = q_ref[...]
        k = kv_ref[:, :X_WIDTH].astype(BF16)
        v = kv_ref[:, X_WIDTH:].astype(BF16)
        out = jnp.zeros((MEM_TQ, X_WIDTH), F32)
        for sel in _mem_heads(_iota((MEM_TQ, X_WIDTH), 1)):
            s = lax.dot_general(jnp.where(sel, q, 0.0).astype(BF16), k, NT, preferred_element_type=F32) * scale
            e = jnp.exp(s - jnp.max(s, axis=-1, keepdims=True))
            p = e / jnp.sum(e, axis=-1, keepdims=True)
            out = out + jnp.where(sel, lax.dot_general(p.astype(BF16), v, NN, preferred_element_type=F32), 0.0)
        o_ref[...] = out.astype(BF16)

    return pl.pallas_call(
        body, name="mem_attn_fwd", grid=(n_batch, nq),
        in_specs=[pl.BlockSpec((MEM_TQ, X_WIDTH), lambda b, i: (b * nq + i, q_col)),
                  pl.BlockSpec((N_MEM, 2 * X_WIDTH), lambda b, i: (b, 0)), ANY],
        out_specs=pl.BlockSpec((MEM_TQ, X_WIDTH), lambda b, i: (b * nq + i, MIX_W // X_WIDTH)),
        out_shape=jax.ShapeDtypeStruct(cat.shape, BF16),
        input_output_aliases={2: 0},
        compiler_params=_cparams(("parallel", "parallel")),
    )(proj, memkv, cat)


def _mem_attn_bwd(dcat, proj, q_col, memkv, dproj, n_batch, tail=None):
    nq = SEQ // MEM_TQ
    scale = X_HEAD_DIM ** -0.5
    width = X_WIDTH + (0 if tail is None else 128)
    assert (q_col * X_WIDTH) % width == 0

    def body(do_ref, q_ref, kv_ref, *rest):
        dq_ref, dkv_ref = rest[-2:]

        @pl.when(pl.program_id(1) == 0)
        def _():
            dkv_ref[...] = jnp.zeros_like(dkv_ref)

        q, do = q_ref[...], do_ref[...]
        k = kv_ref[:, :X_WIDTH].astype(BF16)
        v = kv_ref[:, X_WIDTH:].astype(BF16)
        dq = jnp.zeros((MEM_TQ, X_WIDTH), F32)
        dk = jnp.zeros((N_MEM, X_WIDTH), F32)
        dv = jnp.zeros((N_MEM, X_WIDTH), F32)
        for sel in _mem_heads(_iota((MEM_TQ, X_WIDTH), 1)):
            qm = jnp.where(sel, q, 0.0).astype(BF16)
            dom = jnp.where(sel, do, 0.0).astype(BF16)
            s = lax.dot_general(qm, k, NT, preferred_element_type=F32) * scale
            e = jnp.exp(s - jnp.max(s, axis=-1, keepdims=True))
            p = e / jnp.sum(e, axis=-1, keepdims=True)
            dp = lax.dot_general(dom, v, NT, preferred_element_type=F32)
            ds = ((p * (dp - jnp.sum(dp * p, axis=-1, keepdims=True))) * scale).astype(BF16)
            dv = dv + lax.dot_general(p.astype(BF16), dom, TN, preferred_element_type=F32)
            dk = dk + lax.dot_general(ds, qm, TN, preferred_element_type=F32)
            dq = dq + jnp.where(sel, lax.dot_general(ds, k, NN, preferred_element_type=F32), 0.0)
        if tail is None:
            dq_ref[...] = dq.astype(BF16)
        else:
            dq_ref[...] = jnp.concatenate([dq.astype(BF16), rest[0][...]], axis=1)
        dkv_ref[...] += jnp.concatenate([dk, dv], axis=1)

    rows = lambda b, i: b * nq + i
    extra = [] if tail is None else [tail]
    return pl.pallas_call(
        body, name="mem_attn_bwd", grid=(n_batch, nq),
        in_specs=[pl.BlockSpec((MEM_TQ, X_WIDTH), lambda b, i: (rows(b, i), MIX_W // X_WIDTH)),
                  pl.BlockSpec((MEM_TQ, X_WIDTH), lambda b, i: (rows(b, i), q_col)),
                  pl.BlockSpec((N_MEM, 2 * X_WIDTH), lambda b, i: (b, 0))]
                 + [pl.BlockSpec((MEM_TQ, 128), lambda b, i: (rows(b, i), 0))] * len(extra) + [ANY],
        out_specs=[pl.BlockSpec((MEM_TQ, width), lambda b, i: (rows(b, i), q_col * X_WIDTH // width)),
                   pl.BlockSpec((N_MEM, 2 * X_WIDTH), lambda b, i: (b, 0))],
        out_shape=[jax.ShapeDtypeStruct(dproj.shape, BF16),
                   jax.ShapeDtypeStruct((n_batch * N_MEM, 2 * X_WIDTH), F32)],
        input_output_aliases={3 + len(extra): 0},
        compiler_params=_cparams(("parallel", "arbitrary")),
    )(dcat, proj, memkv, *extra, dproj)


def _relu2_epilogue(acc):
    r = jnp.maximum(acc, 0.0)
    return (r * r,)


def _relu2_bwd_epilogue(acc, a):
    return (acc * (2.0 * jnp.sqrt(a.astype(F32))),)


def _pad_in_a(w_in_a):
    w = 3 * MIX_W
    parts = [w_in_a[:, :w], w_in_a[:, w:w + MIX_W], w_in_a[:, IN_A - X_WIDTH:],
             w_in_a[:, w + MIX_W:w + MIX_W + 2 * N_LIN]]
    pad = jnp.zeros((D_MODEL, IN_A_PAD - IN_A), w_in_a.dtype)
    return jnp.concatenate(parts + [pad], axis=1)


def _unpad_in_a(g):
    w = 3 * MIX_W
    return jnp.concatenate([g[:, :w + MIX_W], g[:, w + MIX_W + X_WIDTH:w + MIX_W + X_WIDTH + 2 * N_LIN],
                            g[:, w + MIX_W:w + MIX_W + X_WIDTH]], axis=1)


def _qkv_to_pairs(w):
    w3 = 3 * MIX_W
    qkv = w[:, :w3].reshape(-1, 3, SB_PAIRS, 128).transpose(0, 2, 1, 3).reshape(-1, w3)
    return jnp.concatenate([qkv, w[:, w3:]], axis=1)


def _pairs_to_qkv(w):
    w3 = 3 * MIX_W
    qkv = w[:, :w3].reshape(-1, SB_PAIRS, 3, 128).transpose(0, 2, 1, 3).reshape(-1, w3)
    return jnp.concatenate([qkv, w[:, w3:]], axis=1)


def _local_step(x, mem, target, wts, small, comm=None):
    wts = dict(wts)
    t = x.shape[0]
    nb = t // SEQ
    npre, npost, mpre, mpost = small["norm_pre_mix"], small["norm_post_mix"], small["norm_pre_mlp"], small["norm_post_mlp"]
    alog_row, dtb_row, gain_row = _gdn_param_rows(small["a_log_a"][0], small["dt_bias_a"][0], small["onorm_a"][0])
    conv_w = small["conv_w"]

    mem_n = _rms_fwd(mem, small["mem_norm"], name="mem_norm_fwd", tile=256)
    saved = []
    h = _rms_fwd(x, npre[0], name="pre_mix_norm0")
    big = min(1024, t)
    for i in range(DEPTH):
        s = {"x_in": x, "h1": h}
        if i == 0:
            proj = _matmul(h, wts["in_a", None], mode="nn", tm=big, tn=1152, tk=1024, name="proj_a")
            qkv = _conv_silu_fwd(proj, conv_w, nb)
            if comm is None:
                mix, states, tinvs = _gdn_fwd(qkv, proj, alog_row, dtb_row, gain_row, nb)
            else:
                mix, states, tinvs, second = _gdn_fwd(qkv, proj, alog_row, dtb_row, gain_row, nb, gather=comm["slabs"])
                second = second.reshape(N_CHIPS, REGION_ROWS[W_SECOND], D_MODEL)
                wts.update(_as_operands(_unpack_region_full(W_SECOND, second)))
            s.update(qkv=qkv, states=states, tinvs=tinvs)
            q_col = (3 * MIX_W + MIX_W) // X_WIDTH
        else:
            proj = _matmul(h, wts["in_b", None], mode="nn", tm=big, tn=1280, tk=1024, name="proj_b", out_dtypes=(BF16,))
            mix, totals = _sb_fwd(proj, nb)
            s.update(totals=totals)
            q_col = 3 * MIX_W // X_WIDTH
        memkv = _matmul(mem_n, wts["mem_kv", i], mode="nn", tm=256, tn=512, tk=1024, name=f"memkv{i}")
        cat = _mem_attn_fwd(proj, q_col, memkv, mix, nb)
        y = _matmul(cat, wts["out", i], mode="nn", tm=big, tn=1024, tk=1024, name=f"out_proj{i}")
        x2, h2 = _post_norm_add(x, y, npost[i], mpre[i], name=f"post_mix{i}")
        a = _matmul(h2, wts["up", i], mode="nn", tm=big, tn=2048, tk=1024, name=f"up{i}",
                    out_dtypes=(BF16,), epilogue=_relu2_epilogue, n_outer=True)
        y2 = _matmul(a, wts["down", i], mode="nn", tm=big, tn=1024, tk=2048, name=f"down{i}")
        s.update(proj=proj, q_col=q_col, memkv=memkv, cat=cat, y=y, x2=x2, h2=h2, a=a, y2=y2)
        saved.append(s)
        if i + 1 < DEPTH:
            x, h = _post_norm_add(x2, y2, mpost[i], npre[i + 1], name=f"post_mlp{i}")
        else:
            loss_row, dx = _post_norm_loss(x2, y2, mpost[i], target, name="loss_head")

    gw = {}
    gs = {k: [None] * DEPTH for k in ("norm_pre_mix", "norm_post_mix", "norm_pre_mlp", "norm_post_mlp")}
    dmem_n, early = None, None
    for i in reversed(range(DEPTH)):
        s = saved[i]
        if i == DEPTH - 1:
            dy2, gs["norm_post_mlp"][i] = _rms_bwd(dx, s["y2"], mpost[i], name=f"post_mlp_bwd{i}", out_dtype=BF16)
        du = _matmul(dy2, wts["down", i], mode="nt", tm=big, tn=2048, tk=1024, name=f"down_dx{i}",
                     out_dtypes=(BF16,), epilogue=_relu2_bwd_epilogue, extras=(s["a"],), n_outer=True)
        gw["down", i] = _matmul(s["a"], dy2, mode="tn", tm=1024, tn=1024, tk=big, name=f"down_dw{i}")
        dh2 = _matmul(du, wts["up", i], mode="nt", tm=big, tn=1024, tk=2048, name=f"up_dx{i}")
        gw["up", i] = _matmul(s["h2"], du, mode="tn", tm=1024, tn=2048, tk=512, name=f"up_dw{i}")
        dx2, dy, gs["norm_pre_mlp"][i], gs["norm_post_mix"][i] = _rms_bwd_pair(
            dh2, s["x2"], mpre[i], dx, s["y"], npost[i], name=f"mlp_norms_bwd{i}")
        dcat = _matmul(dy, wts["out", i], mode="nt", tm=big, tn=1024, tk=1024, name=f"out_dx{i}")
        gw["out", i] = _matmul(s["cat"], dy, mode="tn", tm=1024, tn=1024, tk=big, name=f"out_dw{i}")
        if i == 0:
            exchange = None
            if comm is not None:
                own, exchange = _reduce_in_chip(_pack_region_full(G_EARLY, gw), comm["core"])
            res = _gdn_bwd(dcat, s["qkv"], s["proj"], s["states"], s["tinvs"], alog_row, dtb_row, gain_row, nb,
                           exchange=exchange)
            dqkv, dproj, dsmall, dalog, ddtb, dgain = res[:6]
            if comm is not None:
                early = (own, res[6])
            dproj, dconv = _conv_silu_bwd(dqkv, s["proj"], conv_w, dproj, nb)
            dproj, dmemkv = _mem_attn_bwd(dcat, s["proj"], s["q_col"], s["memkv"], dproj, nb, tail=dsmall)
            w_in, tile = wts["in_a", None], 1152
        else:
            dproj = _sb_bwd(dcat, s["proj"], s["totals"], nb)
            dproj, dmemkv = _mem_attn_bwd(dcat, s["proj"], s["q_col"], s["memkv"], dproj, nb)
            w_in, tile = wts["in_b", None], 1280
        dmemkv = dmemkv.astype(BF16)
        gw["mem_kv", i] = _matmul(mem_n, dmemkv, mode="tn", tm=1024, tn=512, tk=256, name=f"memkv_dw{i}")
        dmn = _matmul(dmemkv, wts["mem_kv", i], mode="nt", tm=256, tn=1024, tk=512, name=f"memkv_dx{i}")
        dmem_n = dmn if dmem_n is None else dmem_n + dmn
        dh1 = _matmul(dproj, w_in, mode="nt", tm=big, tn=1024, tk=tile, name=f"proj_dx{i}")
        g_in = _matmul(s["h1"], dproj, mode="tn", tm=1024, tn=tile, tk=big, name=f"proj_dw{i}")
        if i == 0:
            gw["in_a", None] = _unpad_in_a(g_in)
        else:
            gw["in_b", None] = _pairs_to_qkv(g_in)
        if i > 0:
            dx, dy2, gs["norm_pre_mix"][i], gs["norm_post_mlp"][i - 1] = _rms_bwd_pair(
                dh1, s["x_in"], npre[i], dx2, saved[i - 1]["y2"], mpost[i - 1], name=f"mix_norms_bwd{i}")
        else:
            dx, gs["norm_pre_mix"][i] = _rms_bwd(dh1, s["x_in"], npre[i], name=f"pre_mix_bwd{i}", res=dx2)

    _, g_mem_norm = _rms_bwd(dmem_n, mem, small["mem_norm"], name="mem_norm_bwd", tile=256)
    gsmall = {k: jnp.concatenate(v, axis=0) for k, v in gs.items()}
    gsmall.update(mem_norm=g_mem_norm[0], a_log_a=dalog[:, N_LIN:2 * N_LIN], dt_bias_a=ddtb[:, N_LIN:2 * N_LIN],
                  onorm_a=dgain, conv_w=dconv)
    return loss_row[0, 0], dx, gw, gsmall, early


SUM_TILE = 640


def _position():
    x, y, c = lax.axis_index("x"), lax.axis_index("y"), lax.axis_index("c")
    others = [(1 - x, y), (x, 1 - y), (1 - x, 1 - y)]
    return x, y, c, others


class _Gather:
    def __init__(self, w_ref, out_ref, send_sems, recv_sems):
        self.w, self.out, self.send, self.recv = w_ref, out_ref, send_sems, recv_sems
        self.x, self.y, self.c, self.others = _position()
        self.me = 2 * self.x + self.y

    def _copy(self, k, src, dst, to):
        return pltpu.make_async_remote_copy(src_ref=src, dst_ref=dst, send_sem=self.send.at[k],
                                            recv_sem=self.recv.at[k], device_id=to, device_id_type=MESH)

    def _first(self):
        return [self._copy(j, self.w.at[self.me, self.c], self.out.at[self.me, self.c], (ox, oy, self.c))
                for j, (ox, oy) in enumerate(self.others)]

    def _passed(self):
        sibling = (self.x, self.y, 1 - self.c)
        return [self._copy(3 + j, self.out.at[2 * ox + oy, self.c], self.out.at[2 * ox + oy, self.c], sibling)
                for j, (ox, oy) in enumerate(self.others)]

    def start(self):
        for cp in self._first():
            cp.start()

    def forward(self):
        passed = self._passed()
        for j, (ox, oy) in enumerate(self.others):
            self._copy(j, self.w.at[self.me, self.c], self.out.at[2 * ox + oy, self.c], (self.x, self.y, self.c)).wait_recv()
            passed[j].start()

    def finish(self):
        for j, (ox, oy) in enumerate(self.others):
            self._copy(3 + j, self.w.at[self.me, self.c], self.out.at[2 * ox + oy, 1 - self.c],
                       (self.x, self.y, self.c)).wait_recv()
        for cp in self._first() + self._passed():
            cp.wait_send()


GATHER_SEMS = [pltpu.SemaphoreType.DMA((6,)), pltpu.SemaphoreType.DMA((6,))]


def _gather_chips(wflat):
    def body(w_ref, out_ref, send_sems, recv_sems):
        g = _Gather(w_ref, out_ref, send_sems, recv_sems)
        g.start()
        g.forward()
        g.finish()

    return pl.pallas_call(
        body, name="gather_weights",
        in_specs=[ANY], out_specs=ANY, input_output_aliases={0: 0},
        out_shape=jax.ShapeDtypeStruct(wflat.shape, wflat.dtype),
        scratch_shapes=GATHER_SEMS,
    )(wflat)


def _gather_all(v, *, name):
    rows, n = v.shape

    def body(x_ref, out_ref, send_sems, recv_sems, local_sem):
        x, y, c, others = _position()
        me, sibling = (x, y, c), (x, y, 1 - c)

        def blk(px, py, pc):
            return out_ref.at[pl.ds((4 * px + 2 * py + pc) * rows, rows), :]

        def copy(k, block, to, src=None):
            return pltpu.make_async_remote_copy(src_ref=blk(*block) if src is None else src, dst_ref=blk(*block),
                                                send_sem=send_sems.at[k], recv_sem=recv_sems.at[k],
                                                device_id=to, device_id_type=MESH)

        mine = pltpu.make_async_copy(x_ref, blk(*me), local_sem)
        mine.start()
        first = [copy(0, me, sibling, src=x_ref)]
        first += [copy(1 + j, me, (*chip, c), src=x_ref) for j, chip in enumerate(others)]
        for cp in first:
            cp.start()
        passed = [copy(4 + j, (*chip, c), sibling) for j, chip in enumerate(others)]
        for j, chip in enumerate(others):
            copy(1 + j, (*chip, c), me).wait_recv()
            passed[j].start()
        copy(0, sibling, me).wait_recv()
        for j, chip in enumerate(others):
            copy(4 + j, (*chip, 1 - c), me).wait_recv()
        for cp in first + passed:
            cp.wait_send()
        mine.wait()

    vmem = pl.BlockSpec(memory_space=pltpu.VMEM)
    return pl.pallas_call(
        body, name=name, in_specs=[vmem], out_specs=vmem,
        out_shape=jax.ShapeDtypeStruct((8 * rows, n), v.dtype),
        scratch_shapes=[pltpu.SemaphoreType.DMA((7,)), pltpu.SemaphoreType.DMA((7,)), pltpu.SemaphoreType.DMA],
    )(v)


def _swap_halves(g5):
    def body(g_ref, out_ref, send_sem, recv_sem):
        x, y, c, _ = _position()
        cp = pltpu.make_async_remote_copy(src_ref=g_ref.at[:, 1 - c], dst_ref=out_ref, send_sem=send_sem,
                                          recv_sem=recv_sem, device_id=(x, y, 1 - c), device_id_type=MESH)
        cp.start()
        cp.wait()

    return pl.pallas_call(
        body, name="grad_swap_halves", in_specs=[ANY], out_specs=ANY,
        out_shape=jax.ShapeDtypeStruct((N_CHIPS, g5.shape[2], D_MODEL), g5.dtype),
        scratch_shapes=[pltpu.SemaphoreType.DMA, pltpu.SemaphoreType.DMA],
    )(g5)


def _add_halves(core, g5, got):
    def body(c_ref, a_ref, b_ref, o_ref, ob_ref):
        s = a_ref[0] + b_ref[...]
        o_ref[...] = s
        ob_ref[...] = s.astype(BF16)

    half = g5.shape[2]
    nt = half // SUM_TILE
    spec = pl.BlockSpec((1, SUM_TILE, D_MODEL), lambda s, i, c_ref: (s, i, 0))
    return pl.pallas_call(
        body, name="grad_add_halves",
        grid_spec=pltpu.PrefetchScalarGridSpec(
            num_scalar_prefetch=1, grid=(N_CHIPS, nt),
            in_specs=[pl.BlockSpec((1, 1, SUM_TILE, D_MODEL), lambda s, i, c_ref: (s, c_ref[0], i, 0)), spec],
            out_specs=[spec, spec]),
        out_shape=[jax.ShapeDtypeStruct((N_CHIPS, half, D_MODEL), F32),
                   jax.ShapeDtypeStruct((N_CHIPS, half, D_MODEL), BF16)],
        compiler_params=_cparams(("parallel", "parallel")),
    )(core, g5, got)


def _exchange_copies(p_ref, q_ref, send_sems, recv_sems):
    x, y, c, others = _position()
    return [pltpu.make_async_remote_copy(src_ref=p_ref.at[2 * ox + oy], dst_ref=q_ref.at[j],
                                         send_sem=send_sems.at[j], recv_sem=recv_sems.at[j],
                                         device_id=(ox, oy, c), device_id_type=MESH)
            for j, (ox, oy) in enumerate(others)]


EXCHANGE_SEMS = [pltpu.SemaphoreType.DMA((3,)), pltpu.SemaphoreType.DMA((3,))]


def _exchange_chips(p):
    def body(p_ref, q_ref, send_sems, recv_sems):
        copies = _exchange_copies(p_ref, q_ref, send_sems, recv_sems)
        for cp in copies:
            cp.start()
        for cp in copies:
            cp.wait()

    return pl.pallas_call(
        body, name="grad_exchange_chips", in_specs=[ANY], out_specs=ANY,
        out_shape=jax.ShapeDtypeStruct((3,) + p.shape[1:], p.dtype),
        scratch_shapes=EXCHANGE_SEMS,
    )(p)


def _add_chips(chip_core, p, q):
    def body(kc_ref, p_ref, q_ref, o_ref):
        o_ref[0] = ((p_ref[0] + q_ref[0].astype(F32)) + q_ref[1].astype(F32)) + q_ref[2].astype(F32)

    half = p.shape[1]
    nt = half // SUM_TILE
    return pl.pallas_call(
        body, name="grad_add_chips",
        grid_spec=pltpu.PrefetchScalarGridSpec(
            num_scalar_prefetch=1, grid=(nt,),
            in_specs=[pl.BlockSpec((1, SUM_TILE, D_MODEL), lambda i, kc_ref: (kc_ref[0], i, 0)),
                      pl.BlockSpec((3, SUM_TILE, D_MODEL), lambda i, kc_ref: (0, i, 0))],
            out_specs=pl.BlockSpec((1, SUM_TILE, D_MODEL), lambda i, kc_ref: (kc_ref[1], i, 0))),
        out_shape=jax.ShapeDtypeStruct((2, half, D_MODEL), F32),
        compiler_params=_cparams(("parallel",)),
    )(chip_core, p, q)


def _share_halves(halves):
    def body(h_ref, out_ref, send_sem, recv_sem):
        x, y, c, _ = _position()
        cp = pltpu.make_async_remote_copy(src_ref=h_ref.at[c], dst_ref=out_ref.at[c], send_sem=send_sem,
                                          recv_sem=recv_sem, device_id=(x, y, 1 - c), device_id_type=MESH)
        cp.start()
        pltpu.make_async_remote_copy(src_ref=h_ref.at[c], dst_ref=out_ref.at[1 - c], send_sem=send_sem,
                                     recv_sem=recv_sem, device_id=(x, y, c), device_id_type=MESH).wait_recv()
        cp.wait_send()

    return pl.pallas_call(
        body, name="grad_share_halves", in_specs=[ANY], out_specs=ANY, input_output_aliases={0: 0},
        out_shape=jax.ShapeDtypeStruct(halves.shape, halves.dtype),
        scratch_shapes=[pltpu.SemaphoreType.DMA, pltpu.SemaphoreType.DMA],
    )(halves)


def _reduce_in_chip(g_packed, core):
    rows = g_packed.shape[1]
    g5 = g_packed.reshape(N_CHIPS, 2, rows // 2, D_MODEL)
    return _add_halves(core.reshape(1), g5, _swap_halves(g5))


def _reduce_across_chips(p, q, chip, core):
    halves = _share_halves(_add_chips(jnp.stack([chip, core]), p, q))
    return halves.reshape(2 * halves.shape[1], D_MODEL)


def _reduce_scatter(g_packed, chip, core):
    p, p_bf = _reduce_in_chip(g_packed, core)
    return _reduce_across_chips(p, _exchange_chips(p_bf), chip, core)


def _slot(n):
    return -(-n // 16) * 16


def _pad_rows(a, axis):
    n = a.shape[axis]
    widths = [(0, 0)] * a.ndim
    widths[axis] = (0, _slot(n) - n)
    return jnp.pad(a, widths) if _slot(n) != n else a


W_FIRST = (("in_a", None),)
W_SECOND = (("mem_kv", 0), ("out", 0), ("up", 0), ("down", 0),
            ("in_b", None), ("mem_kv", 1), ("out", 1), ("up", 1), ("down", 1))
G_LATE = (("in_a", None), ("mem_kv", 0), ("out", 0))
G_EARLY = (("up", 0), ("down", 0), ("in_b", None), ("mem_kv", 1), ("out", 1), ("up", 1), ("down", 1))
REGION_ROWS = {W_FIRST: 896, W_SECOND: 5504, G_LATE: 1280, G_EARLY: 5120}
FULL_SHAPE = {"in_a": (D_MODEL, IN_A), "in_b": (D_MODEL, IN_B), "mem_kv": (D_MODEL, 2 * X_WIDTH),
              "out": (D_MODEL, D_MODEL), "up": (D_MODEL, D_FF), "down": (D_FF, D_MODEL)}
COLUMN_SHARDED = ("in_a", "in_b", "up")


def _shard_shape(name):
    r, c = FULL_SHAPE[name]
    return (r, c // N_CHIPS) if name in COLUMN_SHARDED else (r // N_CHIPS, c)


def _part_rows(name):
    r, c = _shard_shape(name)
    return r * c // D_MODEL


def _pack_region(region, part, dtype):
    rows = [_pad_rows(part(name, layer).reshape(-1, D_MODEL).astype(dtype), 0) for name, layer in region]
    used = sum(r.shape[0] for r in rows)
    return jnp.concatenate(rows + [jnp.zeros((REGION_ROWS[region] - used, D_MODEL), dtype)], axis=0)


def _unpack_region(region, flat):
    out, off = {}, 0
    for name, layer in region:
        n = _part_rows(name)
        out[name, layer] = flat[off:off + n].reshape(_shard_shape(name))
        off += _slot(n)
    return out


def _unpack_region_full(region, g):
    out, off = {}, 0
    for name, layer in region:
        n = _part_rows(name)
        piece = g[:, off:off + n].reshape((N_CHIPS,) + _shard_shape(name))
        if name in COLUMN_SHARDED:
            piece = piece.transpose(1, 0, 2)
        out[name, layer] = piece.reshape(FULL_SHAPE[name])
        off += _slot(n)
    return out


def _pack_region_full(region, full):
    s = N_CHIPS
    parts = []
    for name, layer in region:
        g = full[name, layer]
        if name in COLUMN_SHARDED:
            g = g.reshape(g.shape[0], s, -1).transpose(1, 0, 2)
        parts.append(_pad_rows(g.reshape(s, -1, D_MODEL), 1))
    used = sum(p.shape[1] for p in parts)
    return jnp.concatenate(parts + [jnp.zeros((s, REGION_ROWS[region] - used, D_MODEL), F32)], axis=1)


def _as_operands(full):
    out = dict(full)
    if ("in_a", None) in out:
        out["in_a", None] = _pad_in_a(out["in_a", None])
    if ("in_b", None) in out:
        out["in_b", None] = _qkv_to_pairs(out["in_b", None])
    return out


def _place(packed, chip):
    rows = packed.shape[0]
    slabs = lax.dynamic_update_slice(jnp.zeros((N_CHIPS, rows, D_MODEL), packed.dtype), packed[None], (chip, 0, 0))
    return slabs.reshape(N_CHIPS, 2, rows // 2, D_MODEL)


def _adamw_math(w, g, m, v):
    m = ADAM_B1 * m + (1.0 - ADAM_B1) * g
    v = ADAM_B2 * v + (1.0 - ADAM_B2) * (g * g)
    m_hat = m / (1.0 - ADAM_B1 ** ADAM_STEP)
    v_hat = v / (1.0 - ADAM_B2 ** ADAM_STEP)
    delta = -ADAM_LR * (m_hat / (jnp.sqrt(v_hat) + ADAM_EPS) + ADAM_WD * w)
    return delta, m, v


ADAM_TILE = 256


def _adamw(w, g, m, v, *, name):
    shape = w.shape
    cols = shape[-1]
    rows = w.size // cols
    tile = min(rows, ADAM_TILE)
    assert rows % tile == 0, (name, shape)

    def body(w_ref, g_ref, m_ref, v_ref, d_ref, nm_ref, nv_ref):
        d_ref[...], nm_ref[...], nv_ref[...] = _adamw_math(w_ref[...], g_ref[...], m_ref[...], v_ref[...])

    spec = pl.BlockSpec((tile, cols), lambda i: (i, 0))
    outs = pl.pallas_call(
        body, name=name, grid=(rows // tile,), in_specs=[spec] * 4, out_specs=[spec] * 3,
        out_shape=[jax.ShapeDtypeStruct((rows, cols), F32)] * 3,
        compiler_params=_cparams(("parallel",)),
    )(*[a.reshape(rows, cols) for a in (w, g, m, v)])
    return [o.reshape(shape) for o in outs]


SMALL_NAMES = (("mem_norm", 8), ("norm_pre_mix", 16), ("norm_post_mix", 16), ("norm_pre_mlp", 16),
               ("norm_post_mlp", 16), ("a_log_a", 1), ("dt_bias_a", 1), ("onorm_a", 1))
SMALL_ROWS = 80
CONV_ROWS = CONV_K * 3 * MIX_W // 128
SMALL_GRAD_ROWS = SMALL_ROWS + CONV_ROWS


def _pack_small(vals):
    rows = []
    for name, n in SMALL_NAMES:
        flat = vals[name].reshape(-1)
        rows.append(jnp.pad(flat, (0, n * 128 - flat.size)).reshape(n, 128))
    used = sum(n for _, n in SMALL_NAMES)
    return jnp.concatenate(rows + [jnp.zeros((SMALL_ROWS - used, 128), F32)], axis=0)


def _unpack_small(packed, like):
    out, off = {}, 0
    for name, n in SMALL_NAMES:
        size = like[name].size
        out[name] = packed[off:off + n].reshape(-1)[:size].reshape(like[name].shape)
        off += n
    return out


def _small_update(gathered, w, m, v):
    def body(g_ref, w_ref, m_ref, v_ref, gs_ref, d_ref, nm_ref, nv_ref):
        g = g_ref[0]
        for dev in range(1, 8):
            g = g + g_ref[dev]
        gs_ref[...] = g
        d_ref[...], nm_ref[...], nv_ref[...] = _adamw_math(w_ref[...], g[:SMALL_ROWS], m_ref[...], v_ref[...])

    small = jax.ShapeDtypeStruct((SMALL_ROWS, 128), F32)
    return pl.pallas_call(
        body, name="small_update",
        out_shape=[jax.ShapeDtypeStruct((SMALL_GRAD_ROWS, 128), F32), small, small, small],
    )(gathered.reshape(8, SMALL_GRAD_ROWS, 128), w, m, v)


def kernel(x, mem, mem_norm, norm_pre_mix, norm_post_mix, norm_pre_mlp, norm_post_mlp, w_in_a, conv_w_a, a_log_a, dt_bias_a, onorm_a, w_in_b, w_mem_kv, w_out, w_up, w_down, loss_target, m_mem_norm, m_norm_pre_mix, m_norm_post_mix, m_norm_pre_mlp, m_norm_post_mlp, m_w_in_a, m_conv_w_a, m_a_log_a, m_dt_bias_a, m_onorm_a, m_w_in_b, m_w_mem_kv, m_w_out, m_w_up, m_w_down, v_mem_norm, v_norm_pre_mix, v_norm_post_mix, v_norm_pre_mlp, v_norm_post_mlp, v_w_in_a, v_conv_w_a, v_a_log_a, v_dt_bias_a, v_onorm_a, v_w_in_b, v_w_mem_kv, v_w_out, v_w_up, v_w_down):
    nb = x.shape[0]
    chip = (2 * lax.axis_index("x") + lax.axis_index("y")).astype(jnp.int32)
    core = lax.axis_index("c").astype(jnp.int32)
    shards = {"in_a": w_in_a, "in_b": w_in_b, "mem_kv": w_mem_kv, "out": w_out, "up": w_up, "down": w_down}
    moments_m = {"in_a": m_w_in_a, "in_b": m_w_in_b, "mem_kv": m_w_mem_kv, "out": m_w_out, "up": m_w_up, "down": m_w_down}
    moments_v = {"in_a": v_w_in_a, "in_b": v_w_in_b, "mem_kv": v_w_mem_kv, "out": v_w_out, "up": v_w_up, "down": v_w_down}
    small_w = {"mem_norm": mem_norm, "norm_pre_mix": norm_pre_mix, "norm_post_mix": norm_post_mix,
               "norm_pre_mlp": norm_pre_mlp, "norm_post_mlp": norm_post_mlp, "a_log_a": a_log_a,
               "dt_bias_a": dt_bias_a, "onorm_a": onorm_a}
    small_m = {"mem_norm": m_mem_norm, "norm_pre_mix": m_norm_pre_mix, "norm_post_mix": m_norm_post_mix,
               "norm_pre_mlp": m_norm_pre_mlp, "norm_post_mlp": m_norm_post_mlp, "a_log_a": m_a_log_a,
               "dt_bias_a": m_dt_bias_a, "onorm_a": m_onorm_a}
    small_v = {"mem_norm": v_mem_norm, "norm_pre_mix": v_norm_pre_mix, "norm_post_mix": v_norm_post_mix,
               "norm_pre_mlp": v_norm_pre_mlp, "norm_post_mlp": v_norm_post_mlp, "a_log_a": v_a_log_a,
               "dt_bias_a": v_dt_bias_a, "onorm_a": v_onorm_a}

    def shard_part(name, layer):
        return shards[name][0 if layer is None else layer]

    first = _gather_chips(_place(_pack_region(W_FIRST, shard_part, BF16), chip))
    wts = _as_operands(_unpack_region_full(W_FIRST, first.reshape(N_CHIPS, REGION_ROWS[W_FIRST], D_MODEL)))
    comm = {"slabs": _place(_pack_region(W_SECOND, shard_part, BF16), chip), "core": core}
    conv_rows = CONV_ROWS // N_CHIPS
    conv_blk = jnp.pad(conv_w_a.reshape(conv_rows, 128), ((0, 24 - conv_rows), (0, 0)))
    conv_all = _gather_all(conv_blk, name="gather_conv").reshape(N_CHIPS, 2, 24, 128)[:, 0, :conv_rows]
    conv_full = conv_all.reshape(N_CHIPS, CONV_K, 3 * MIX_W // N_CHIPS).transpose(1, 0, 2).reshape(CONV_K, 3 * MIX_W)

    loss_local, dx, gw, gsmall, (own_early, others_early) = _local_step(
        x.reshape(nb * SEQ, D_MODEL), mem.reshape(nb * N_MEM, D_MODEL), loss_target.reshape(nb * SEQ, D_MODEL),
        wts, dict(small_w, conv_w=conv_full), comm)
    loss = lax.psum(loss_local, ("x", "y", "c"))
    grad_x = dx.reshape(nb, SEQ, D_MODEL)

    g_part = _unpack_region(G_EARLY, _reduce_across_chips(own_early, others_early, chip, core))
    g_part.update(_unpack_region(G_LATE, _reduce_scatter(_pack_region_full(G_LATE, gw), chip, core)))
    g_shard = {k: (g_part[k, None][None] if (k, None) in g_part else jnp.stack([g_part[k, i] for i in range(DEPTH)]))
               for k in shards}
    upd = {k: _adamw(shards[k], g_shard[k], moments_m[k], moments_v[k], name=f"adamw_{k}") for k in shards}

    g_rows = jnp.concatenate([_pack_small(gsmall), gsmall["conv_w"].reshape(CONV_ROWS, 128)], axis=0)
    g_all = _gather_all(g_rows, name="gather_small_grads")
    g_sum, d_small, nm_small, nv_small = _small_update(g_all, _pack_small(small_w), _pack_small(small_m), _pack_small(small_v))
    gs = _unpack_small(g_sum, small_w)
    ds, nms, nvs = (_unpack_small(p, small_w) for p in (d_small, nm_small, nv_small))
    cw = 3 * MIX_W // N_CHIPS
    g_conv = lax.dynamic_slice(g_sum[SMALL_ROWS:].reshape(CONV_K, 3 * MIX_W), (0, chip * cw), (CONV_K, cw)).reshape(conv_w_a.shape)
    d_conv, nm_conv, nv_conv = _adamw(conv_w_a, g_conv, m_conv_w_a, v_conv_w_a, name="adamw_conv")

    order = ("mem_norm", "norm_pre_mix", "norm_post_mix", "norm_pre_mlp", "norm_post_mlp", "in_a", "conv", "a_log_a",
             "dt_bias_a", "onorm_a", "in_b", "mem_kv", "out", "up", "down")
    grads = dict(gs, conv=g_conv, **g_shard)
    deltas = dict(ds, conv=d_conv, **{k: u[0] for k, u in upd.items()})
    new_m = dict(nms, conv=nm_conv, **{k: u[1] for k, u in upd.items()})
    new_v = dict(nvs, conv=nv_conv, **{k: u[2] for k, u in upd.items()})
    return (loss, grad_x, *[grads[k] for k in order], *[deltas[k] for k in order],
            *[new_m[k] for k in order], *[new_v[k] for k in order])
```

```python
import functools

import jax
import jax.numpy as jnp
from jax import lax
from jax.experimental import pallas as pl
from jax.experimental.pallas import tpu as pltpu

F32 = jnp.float32
BF16 = jnp.bfloat16
HIGHEST = lax.Precision.HIGHEST
MESH = pl.DeviceIdType.MESH

D_MODEL = 1024
SEQ = 2048
DEPTH = 2
X_WIDTH = 256
N_X_HEADS = 4
X_HEAD_DIM = 64
MIX_W = 768
LIN_DH = 128
N_LIN = 6
CONV_K = 4
CHUNK = 64
SB_DH = 64
SB_PAIRS = 6
N_MEM = 256
D_FF = 4096
EPS = 1e-6
IN_A = 3340
IN_A_PAD = 3456
IN_B = 2560
SMALL_COL = 26
N_CHIPS = 4

ADAM_LR, ADAM_B1, ADAM_B2, ADAM_EPS, ADAM_WD, ADAM_STEP = 0.001, 0.9, 0.999, 1e-08, 0.01, 10

VMEM_LIMIT = 48 * 1024 * 1024

ANY = pl.BlockSpec(memory_space=pl.ANY)

NN = (((1,), (0,)), ((), ()))
NT = (((1,), (1,)), ((), ()))
TN = (((0,), (0,)), ((), ()))


def _cparams(sem):
    return pltpu.CompilerParams(dimension_semantics=sem, vmem_limit_bytes=VMEM_LIMIT)


def _dotbf(a, b, dn=NN):
    return lax.dot_general(a.astype(BF16), b.astype(BF16), dn, preferred_element_type=F32)


def _split(a):
    hi = a.astype(BF16)
    lo = (a - hi.astype(F32)).astype(BF16)
    return hi, lo


def _dot3(a, b, dn=NN):
    ah, al = _split(a)
    bh, bl = _split(b)
    d = functools.partial(lax.dot_general, dimension_numbers=dn, preferred_element_type=F32)
    return d(ah, bh) + (d(ah, bl) + d(al, bh))


def _exact01(a, b, dn, mask_left):
    m, x = (a, b) if mask_left else (b, a)
    m = m.astype(BF16)
    hi = x.astype(BF16)
    r1 = x - hi.astype(F32)
    mid = r1.astype(BF16)
    lo = (r1 - mid.astype(F32)).astype(BF16)
    d = functools.partial(lax.dot_general, dimension_numbers=dn, preferred_element_type=F32)
    pair = (lambda p: d(m, p)) if mask_left else (lambda p: d(p, m))
    return pair(hi) + (pair(mid) + pair(lo))


@functools.partial(jax.custom_vjp, nondiff_argnums=(2,))
def _dot01(m01, x, dn):
    return _exact01(m01, x, dn, True)


def _dot01_fwd(m01, x, dn):
    return _exact01(m01, x, dn, True), m01


def _dot01_bwd(dn, m01, ct):
    dx = _exact01(m01, ct, TN, True) if dn == NN else _exact01(ct, m01, TN, False)
    return jnp.zeros_like(m01), dx


_dot01.defvjp(_dot01_fwd, _dot01_bwd)


def _dot_mask(parts, m01):
    d = functools.partial(lax.dot_general, dimension_numbers=NN, preferred_element_type=F32)
    return d(parts[0], m01) + d(parts[1], m01)


def _iota(shape, dim):
    return lax.broadcasted_iota(jnp.int32, shape, dim)


def _softplus(x):
    return jnp.maximum(x, 0.0) + jnp.log(1.0 + jnp.exp(-jnp.abs(x)))


def _log_sigmoid(z):
    return jnp.minimum(z, 0.0) - jnp.log(1.0 + jnp.exp(-jnp.abs(z)))


def _rms(x, g):
    r = lax.rsqrt(jnp.mean(x * x, axis=-1, keepdims=True) + EPS)
    return (x * r) * g


def _matmul(a, b, *, mode, tm, tn, tk, name, out_dtypes=(F32,), epilogue=None, extras=(), n_outer=False):
    if n_outer:
        ix = lambda f: (lambda j, i, kk: f(i, j, kk))
    else:
        ix = lambda f: f
    if mode == "nn":
        (m, k), (k2, n) = a.shape, b.shape
        a_spec = pl.BlockSpec((tm, tk), ix(lambda i, j, kk: (i, kk)))
        b_spec = pl.BlockSpec((tk, tn), ix(lambda i, j, kk: (kk, j)))
        dn = NN
    elif mode == "nt":
        (m, k), (n, k2) = a.shape, b.shape
        a_spec = pl.BlockSpec((tm, tk), ix(lambda i, j, kk: (i, kk)))
        b_spec = pl.BlockSpec((tn, tk), ix(lambda i, j, kk: (j, kk)))
        dn = NT
    else:
        (k, m), (k2, n) = a.shape, b.shape
        a_spec = pl.BlockSpec((tk, tm), ix(lambda i, j, kk: (kk, i)))
        b_spec = pl.BlockSpec((tk, tn), ix(lambda i, j, kk: (kk, j)))
        dn = TN
    assert k == k2 and m % tm == 0 and n % tn == 0 and k % tk == 0, (name, a.shape, b.shape)
    assert a.dtype == BF16 and b.dtype == BF16, name
    nk = k // tk
    n_extra, n_out = len(extras), len(out_dtypes)

    def finish(acc, extra_refs, out_refs):
        outs = (acc,) if epilogue is None else epilogue(acc, *[r[...] for r in extra_refs])
        for o_ref, o in zip(out_refs, outs):
            o_ref[...] = o.astype(o_ref.dtype)

    def body_single(a_ref, b_ref, *rest):
        acc = lax.dot_general(a_ref[...], b_ref[...], dn, preferred_element_type=F32)
        finish(acc, rest[:n_extra], rest[n_extra:n_extra + n_out])

    def body_tiled(a_ref, b_ref, *rest):
        extra_refs, out_refs, acc_ref = rest[:n_extra], rest[n_extra:n_extra + n_out], rest[-1]
        kk = pl.program_id(2)

        @pl.when(kk == 0)
        def _():
            acc_ref[...] = jnp.zeros_like(acc_ref)

        acc_ref[...] += lax.dot_general(a_ref[...], b_ref[...], dn, preferred_element_type=F32)

        @pl.when(kk == nk - 1)
        def _():
            finish(acc_ref[...], extra_refs, out_refs)

    mn_spec = pl.BlockSpec((tm, tn), ix(lambda i, j, kk: (i, j)))
    grid = (n // tn, m // tm, nk) if n_outer else (m // tm, n // tn, nk)
    outs = pl.pallas_call(
        body_single if nk == 1 else body_tiled,
        name=name,
        grid=grid,
        in_specs=[a_spec, b_spec] + [mn_spec] * n_extra,
        out_specs=[mn_spec] * n_out,
        out_shape=[jax.ShapeDtypeStruct((m, n), dt) for dt in out_dtypes],
        scratch_shapes=[] if nk == 1 else [pltpu.VMEM((tm, tn), F32)],
        compiler_params=_cparams(("parallel", "parallel", "arbitrary")),
    )(a, b, *extras)
    return outs[0] if n_out == 1 else outs


ROW_TILE = 512


def _row_spec(width=D_MODEL, tile=ROW_TILE):
    return pl.BlockSpec((tile, width), lambda i: (i, 0))


def _vec_spec(width=D_MODEL):
    return pl.BlockSpec((1, width), lambda i: (0, 0))


def _rms_fwd(x, g, *, name, tile=ROW_TILE):
    t = x.shape[0]

    def body(x_ref, g_ref, h_ref):
        h_ref[...] = _rms(x_ref[...], g_ref[...]).astype(BF16)

    return pl.pallas_call(
        body, name=name, grid=(t // tile,),
        in_specs=[_row_spec(tile=tile), _vec_spec()], out_specs=_row_spec(tile=tile),
        out_shape=jax.ShapeDtypeStruct((t, D_MODEL), BF16),
        compiler_params=_cparams(("parallel",)),
    )(x, g.reshape(1, D_MODEL))


def _post_norm_add(xres, y, g_post, g_next, *, name):
    t = xres.shape[0]

    def body(x_ref, y_ref, gp_ref, gn_ref, xo_ref, h_ref):
        xo = x_ref[...] + _rms(y_ref[...], gp_ref[...])
        xo_ref[...] = xo
        h_ref[...] = _rms(xo, gn_ref[...]).astype(BF16)

    return pl.pallas_call(
        body, name=name, grid=(t // ROW_TILE,),
        in_specs=[_row_spec(), _row_spec(), _vec_spec(), _vec_spec()],
        out_specs=[_row_spec(), _row_spec()],
        out_shape=[jax.ShapeDtypeStruct((t, D_MODEL), F32), jax.ShapeDtypeStruct((t, D_MODEL), BF16)],
        compiler_params=_cparams(("parallel",)),
    )(xres, y, g_post.reshape(1, D_MODEL), g_next.reshape(1, D_MODEL))


def _post_norm_loss(xres, y, g_post, target, *, name):
    t = xres.shape[0]

    def body(x_ref, y_ref, gp_ref, t_ref, loss_ref, dx_ref):
        @pl.when(pl.program_id(0) == 0)
        def _():
            loss_ref[...] = jnp.zeros_like(loss_ref)

        err = (x_ref[...] + _rms(y_ref[...], gp_ref[...])) - t_ref[...]
        per_tok = jnp.mean(err * err, axis=-1, keepdims=True)
        loss_ref[...] += 0.5 * jnp.sum(per_tok, axis=0, keepdims=True)
        dx_ref[...] = err * (1.0 / D_MODEL)

    return pl.pallas_call(
        body, name=name, grid=(t // ROW_TILE,),
        in_specs=[_row_spec(), _row_spec(), _vec_spec(), _row_spec()],
        out_specs=[pl.BlockSpec((1, 128), lambda i: (0, 0)), _row_spec()],
        out_shape=[jax.ShapeDtypeStruct((1, 128), F32), jax.ShapeDtypeStruct((t, D_MODEL), F32)],
        compiler_params=_cparams(("arbitrary",)),
    )(xres, y, g_post.reshape(1, D_MODEL), target)


def _rms_bwd(dy, x, g, *, name, res=None, out_dtype=F32, tile=ROW_TILE):
    t = x.shape[0]
    has_res = res is not None

    def body(dy_ref, x_ref, g_ref, *rest):
        res_ref = rest[0] if has_res else None
        dx_ref, dg_ref = rest[-2], rest[-1]

        @pl.when(pl.program_id(0) == 0)
        def _():
            dg_ref[...] = jnp.zeros_like(dg_ref)

        xf = x_ref[...]
        dyf = dy_ref[...].astype(F32)
        r = lax.rsqrt(jnp.mean(xf * xf, axis=-1, keepdims=True) + EPS)
        xhat = xf * r
        dg_ref[...] += jnp.sum(dyf * xhat, axis=0, keepdims=True)
        dxh = dyf * g_ref[...]
        dx = r * (dxh - xhat * jnp.mean(dxh * xhat, axis=-1, keepdims=True))
        if has_res:
            dx = dx + res_ref[...]
        dx_ref[...] = dx.astype(dx_ref.dtype)

    args = [dy, x, g.reshape(1, D_MODEL)] + ([res] if has_res else [])
    return pl.pallas_call(
        body, name=name, grid=(t // tile,),
        in_specs=[_row_spec(tile=tile), _row_spec(tile=tile), _vec_spec()] + ([_row_spec(tile=tile)] if has_res else []),
        out_specs=[_row_spec(tile=tile), _vec_spec()],
        out_shape=[jax.ShapeDtypeStruct((t, D_MODEL), out_dtype), jax.ShapeDtypeStruct((1, D_MODEL), F32)],
        compiler_params=_cparams(("arbitrary",)),
    )(*args)


def _rms_bwd_pair(dh, x, g_pre, res, y, g_post, *, name):
    t = x.shape[0]

    def norm_bwd(dy, xf, g):
        r = lax.rsqrt(jnp.mean(xf * xf, axis=-1, keepdims=True) + EPS)
        xhat = xf * r
        dxh = dy * g
        dx = r * (dxh - xhat * jnp.mean(dxh * xhat, axis=-1, keepdims=True))
        return dx, jnp.sum(dy * xhat, axis=0, keepdims=True)

    def body(dh_ref, x_ref, gp_ref, res_ref, y_ref, gq_ref, dx_ref, dy_ref, dgp_ref, dgq_ref):
        @pl.when(pl.program_id(0) == 0)
        def _():
            dgp_ref[...] = jnp.zeros_like(dgp_ref)
            dgq_ref[...] = jnp.zeros_like(dgq_ref)

        dx, dgp = norm_bwd(dh_ref[...], x_ref[...], gp_ref[...])
        dx = dx + res_ref[...]
        dx_ref[...] = dx
        dy, dgq = norm_bwd(dx, y_ref[...], gq_ref[...])
        dy_ref[...] = dy.astype(BF16)
        dgp_ref[...] += dgp
        dgq_ref[...] += dgq

    return pl.pallas_call(
        body, name=name, grid=(t // ROW_TILE,),
        in_specs=[_row_spec(), _row_spec(), _vec_spec(), _row_spec(), _row_spec(), _vec_spec()],
        out_specs=[_row_spec(), _row_spec(), _vec_spec(), _vec_spec()],
        out_shape=[jax.ShapeDtypeStruct((t, D_MODEL), F32), jax.ShapeDtypeStruct((t, D_MODEL), BF16),
                   jax.ShapeDtypeStruct((1, D_MODEL), F32), jax.ShapeDtypeStruct((1, D_MODEL), F32)],
        compiler_params=_cparams(("arbitrary",)),
    )(dh, x, g_pre.reshape(1, D_MODEL), res, y, g_post.reshape(1, D_MODEL))


CONV_COLS = 256
N_CONV_BLOCKS = 3 * MIX_W // CONV_COLS
CONV_STRIP = 128


def _shift_down(x, k):
    if k == 0:
        return x
    return jnp.where(_iota(x.shape, 0) >= k, pltpu.roll(x, k, 0), 0.0)


def _shift_up(x, k):
    if k == 0:
        return x
    s = x.shape[0]
    return jnp.where(_iota(x.shape, 0) < s - k, pltpu.roll(x, s - k, 0), 0.0)


def _conv_pre(x, w_ref):
    c = w_ref[CONV_K - 1:CONV_K, :] * x
    for i in range(CONV_K - 1):
        c = c + w_ref[i:i + 1, :] * _shift_down(x, CONV_K - 1 - i)
    return c


def _conv_silu_fwd(proj, conv_w, n_batch):
    def body(x_ref, w_ref, y_ref):
        c = _conv_pre(x_ref[...], w_ref)
        y_ref[...] = c * jax.nn.sigmoid(c)

    return pl.pallas_call(
        body, name="conv_silu_fwd", grid=(n_batch, N_CONV_BLOCKS),
        in_specs=[pl.BlockSpec((SEQ, CONV_COLS), lambda b, j: (b, j)),
                  pl.BlockSpec((CONV_K, CONV_COLS), lambda b, j: (0, j))],
        out_specs=pl.BlockSpec((SEQ, CONV_COLS), lambda b, j: (b, j)),
        out_shape=jax.ShapeDtypeStruct((n_batch * SEQ, 3 * MIX_W), F32),
        compiler_params=_cparams(("parallel", "parallel")),
    )(proj, conv_w)


def _conv_silu_bwd(dy, proj, conv_w, dproj, n_batch):
    strip, halo = CONV_STRIP, 8
    n_strips = SEQ // strip

    def body(dy_ref, x_ref, w_ref, _, dx_ref, dw_ref, xpad, dcpad):
        @pl.when(pl.program_id(1) == 0)
        def _():
            dw_ref[...] = jnp.zeros_like(dw_ref)

        xpad[0:halo, :] = jnp.zeros((halo, CONV_COLS), F32)
        xpad[halo:, :] = x_ref[...]
        dcpad[SEQ:, :] = jnp.zeros((halo, CONV_COLS), F32)
        taps = [w_ref[i:i + 1, :] for i in range(CONV_K)]

        def first(s, dw):
            a = pl.multiple_of(s * strip, strip)
            win = xpad[pl.ds(a, strip + halo), :]
            xs = [(win if i == CONV_K - 1 else pltpu.roll(win, CONV_K - 1 - i, 0))[halo:] for i in range(CONV_K)]
            c = taps[0] * xs[0]
            for i in range(1, CONV_K):
                c = c + taps[i] * xs[i]
            sig = jax.nn.sigmoid(c)
            dc = dy_ref[pl.ds(a, strip), :] * (sig * (1.0 + c * (1.0 - sig)))
            dcpad[pl.ds(a, strip), :] = dc
            return tuple(dw[i] + jnp.sum(dc * xs[i], axis=0, keepdims=True) for i in range(CONV_K))

        dw = lax.fori_loop(0, n_strips, first, tuple(jnp.zeros((1, CONV_COLS), F32) for _ in range(CONV_K)))
        for i in range(CONV_K):
            dw_ref[i:i + 1, :] += dw[i]

        def second(s, carry):
            a = pl.multiple_of(s * strip, strip)
            win = dcpad[pl.ds(a, strip + halo), :]
            dx = taps[CONV_K - 1] * win[:strip]
            for i in range(CONV_K - 1):
                dx = dx + taps[i] * pltpu.roll(win, strip + halo - (CONV_K - 1 - i), 0)[:strip]
            dx_ref[pl.ds(a, strip), :] = dx.astype(BF16)
            return carry

        lax.fori_loop(0, n_strips, second, 0)

    return pl.pallas_call(
        body, name="conv_silu_bwd", grid=(N_CONV_BLOCKS, n_batch),
        in_specs=[pl.BlockSpec((SEQ, CONV_COLS), lambda j, b: (b, j)),
                  pl.BlockSpec((SEQ, CONV_COLS), lambda j, b: (b, j)),
                  pl.BlockSpec((CONV_K, CONV_COLS), lambda j, b: (0, j)), ANY],
        out_specs=[pl.BlockSpec((SEQ, CONV_COLS), lambda j, b: (b, j)),
                   pl.BlockSpec((CONV_K, CONV_COLS), lambda j, b: (0, j))],
        out_shape=[jax.ShapeDtypeStruct(dproj.shape, BF16),
                   jax.ShapeDtypeStruct((CONV_K, 3 * MIX_W), F32)],
        input_output_aliases={3: 0},
        scratch_shapes=[pltpu.VMEM((SEQ + 8, CONV_COLS), F32), pltpu.VMEM((SEQ + 8, CONV_COLS), F32)],
        compiler_params=_cparams(("parallel", "arbitrary")),
    )(dy, proj, conv_w, dproj)


@jax.custom_vjp
def _solve_apply(low, rhs, tinv):
    return _dot3(tinv, rhs)


def _solve_apply_fwd(low, rhs, tinv):
    sol = _dot3(tinv, rhs)
    return sol, (tinv, sol)


def _solve_apply_bwd(resid, g):
    tinv, sol = resid
    y = _dotbf(tinv, g, TN)
    return -_dotbf(y, sol, NT), y, jnp.zeros_like(tinv)


_solve_apply.defvjp(_solve_apply_fwd, _solve_apply_bwd)


def _inv_unit_lower(lows):
    c = lows[0].shape[0]
    eye = (_iota((c, c), 0) == _iota((c, c), 1)).astype(F32)
    ms = [-low for low in lows]
    ps = [eye + m for m in ms]
    for _ in range(5):
        ms = [_dot3(m, m) for m in ms]
        ps = [p + _dot3(p, m) for p, m in zip(ps, ms)]
    return ps


def _gdn_chunk(qs, ks, vs, gates, states, small, alog_row, dtb_row, gain_row, tinvs):
    c = small.shape[0]
    heads = range(N_LIN)
    lane = _iota((c, 128), 1)
    row, col = _iota((c, c), 0), _iota((c, c), 1)
    causal, strict = row >= col, row > col
    last = _iota((c, 1), 0) == c - 1

    beta_all = jax.nn.sigmoid(small)
    g_all = -jnp.exp(alog_row) * _softplus(small + dtb_row)
    gc_all = _dot01((col <= row).astype(F32), g_all, NN)

    beta = [jnp.sum(jnp.where(lane == h, beta_all, 0.0), axis=1, keepdims=True) for h in heads]
    gc = [jnp.sum(jnp.where(lane == N_LIN + h, gc_all, 0.0), axis=1, keepdims=True) for h in heads]
    gc_j = [_dot01((lane == N_LIN + h).astype(F32), gc_all, NT) for h in heads]
    decay = [jnp.where(causal, jnp.exp(jnp.where(causal, gc[h] - gc_j[h], 0.0)), 0.0) for h in heads]
    gc_last = [jnp.sum(jnp.where(last, gc[h], 0.0), axis=0, keepdims=True) for h in heads]
    egc = [jnp.exp(g) for g in gc]
    qn = [q * lax.rsqrt(jnp.sum(q * q, axis=-1, keepdims=True) + EPS) * (LIN_DH ** -0.5) for q in qs]
    kn = [k * lax.rsqrt(jnp.sum(k * k, axis=-1, keepdims=True) + EPS) for k in ks]
    kb = [kn[h] * beta[h] for h in heads]
    low = [jnp.where(strict, _dotbf(kb[h], kn[h], NT) * decay[h], 0.0) for h in heads]
    if tinvs is None:
        tinvs = _inv_unit_lower(low)
    u = [_solve_apply(low[h], vs[h] * beta[h], tinvs[h]) for h in heads]
    w = [_solve_apply(low[h], kb[h] * egc[h], tinvs[h]) for h in heads]
    intra = [_dotbf(qn[h], kn[h], NT) * decay[h] for h in heads]
    v_new = [u[h] - _dotbf(w[h], states[h]) for h in heads]
    o = [_dotbf(qn[h] * egc[h], states[h]) + _dotbf(intra[h], v_new[h]) for h in heads]
    new_states = [states[h] * jnp.exp(gc_last[h]) + _dotbf(kn[h] * jnp.exp(gc_last[h] - gc[h]), v_new[h], TN)
                  for h in heads]
    o = [x * lax.rsqrt(jnp.mean(x * x, axis=-1, keepdims=True) + EPS) * gain_row for x in o]
    outs = [o[h] * (gates[h] * jax.nn.sigmoid(gates[h])) for h in heads]
    return outs, new_states, tinvs


def _gdn_param_rows(a_log, dt_bias, onorm):
    row = lambda v: jnp.pad(v.reshape(1, N_LIN), ((0, 0), (N_LIN, 128 - 2 * N_LIN)))
    return row(a_log), row(dt_bias), onorm.reshape(1, LIN_DH)


def _head(ref_or_val, h):
    return ref_or_val[:, LIN_DH * h:LIN_DH * (h + 1)]


def _gdn_fwd(qkv, proj, alog_row, dtb_row, gain_row, n_batch, gather=None):
    nc = SEQ // CHUNK
    t = n_batch * SEQ
    steps = n_batch * nc

    def body(qkv_ref, small_ref, gate_ref, al_ref, dt_ref, gn_ref, *rest):
        if gather is None:
            mix_ref, st_ref, ti_ref, s_scr = rest
        else:
            w_ref, mix_ref, st_ref, ti_ref, out_ref, s_scr, send_sems, recv_sems = rest
            step = pl.program_id(0) * nc + pl.program_id(1)
            for at, phase in ((0, "start"), (3 * steps // 4, "forward"), (steps - 1, "finish")):
                @pl.when(step == at)
                def _(phase=phase):
                    getattr(_Gather(w_ref, out_ref, send_sems, recv_sems), phase)()

        @pl.when(pl.program_id(1) == 0)
        def _():
            s_scr[...] = jnp.zeros_like(s_scr)

        heads = range(N_LIN)
        states = [s_scr[h] for h in heads]
        outs, new_states, tinvs = _gdn_chunk(
            [_head(qkv_ref, h) for h in heads], [_head(qkv_ref, N_LIN + h) for h in heads],
            [_head(qkv_ref, 2 * N_LIN + h) for h in heads], [_head(gate_ref, h) for h in heads],
            states, small_ref[...], al_ref[...], dt_ref[...], gn_ref[...], None)
        mix_ref[...] = jnp.concatenate(outs, axis=1).astype(BF16)
        for h in heads:
            st_ref[0, 0, h] = states[h]
            s_scr[h] = new_states[h]
            ti_ref[0, 0, h] = tinvs[h]

    row = lambda b, n: b * nc + n
    vec = pl.BlockSpec((1, 128), lambda b, n: (0, 0))
    extra = [] if gather is None else [gather]
    return pl.pallas_call(
        body, name="gdn_fwd", grid=(n_batch, nc),
        in_specs=[pl.BlockSpec((CHUNK, 3 * MIX_W), lambda b, n: (row(b, n), 0)),
                  pl.BlockSpec((CHUNK, 128), lambda b, n: (row(b, n), SMALL_COL)),
                  pl.BlockSpec((CHUNK, MIX_W), lambda b, n: (row(b, n), 3)),
                  vec, vec, vec] + [ANY] * len(extra),
        out_specs=[pl.BlockSpec((CHUNK, MIX_W), lambda b, n: (row(b, n), 0)),
                   pl.BlockSpec((1, 1, N_LIN, LIN_DH, LIN_DH), lambda b, n: (b, n, 0, 0, 0)),
                   pl.BlockSpec((1, 1, N_LIN, CHUNK, CHUNK), lambda b, n: (b, n, 0, 0, 0))] + [ANY] * len(extra),
        out_shape=[jax.ShapeDtypeStruct((t, D_MODEL), BF16),
                   jax.ShapeDtypeStruct((n_batch, nc, N_LIN, LIN_DH, LIN_DH), F32),
                   jax.ShapeDtypeStruct((n_batch, nc, N_LIN, CHUNK, CHUNK), F32)]
                  + [jax.ShapeDtypeStruct(g.shape, g.dtype) for g in extra],
        input_output_aliases={6: 3} if extra else {},
        scratch_shapes=[pltpu.VMEM((N_LIN, LIN_DH, LIN_DH), F32)] + (GATHER_SEMS if extra else []),
        compiler_params=_cparams(("arbitrary", "arbitrary")),
    )(qkv, proj, proj, alog_row, dtb_row, gain_row, *extra)


def _gdn_bwd(dcat, qkv, proj, states, tinvs, alog_row, dtb_row, gain_row, n_batch, exchange=None):
    nc = SEQ // CHUNK
    t = n_batch * SEQ
    steps = n_batch * nc

    def body(dmix_ref, qkv_ref, small_ref, gate_ref, st_ref, ti_ref, al_ref, dt_ref, gn_ref, *rest):
        if exchange is None:
            dqkv_ref, dgate_ref, dsmall_ref, dal_ref, ddt_ref, dgn_ref, ds_scr = rest
        else:
            p_ref, dqkv_ref, dgate_ref, dsmall_ref, dal_ref, ddt_ref, dgn_ref, q_ref, ds_scr, send_sems, recv_sems = rest
            step = pl.program_id(0) * nc + pl.program_id(1)

            @pl.when(step == 0)
            def _():
                for cp in _exchange_copies(p_ref, q_ref, send_sems, recv_sems):
                    cp.start()

            @pl.when(step == steps - 1)
            def _():
                for cp in _exchange_copies(p_ref, q_ref, send_sems, recv_sems):
                    cp.wait()

        @pl.when(pl.program_id(1) == 0)
        def _():
            ds_scr[...] = jnp.zeros_like(ds_scr)

        @pl.when((pl.program_id(0) == 0) & (pl.program_id(1) == 0))
        def _():
            dal_ref[...] = jnp.zeros_like(dal_ref)
            ddt_ref[...] = jnp.zeros_like(ddt_ref)
            dgn_ref[...] = jnp.zeros_like(dgn_ref)

        heads = range(N_LIN)
        tinvs = [ti_ref[0, 0, h] for h in heads]

        def chunk(qs, ks, vs, gates, states_in, small, al, dt, gn):
            outs, new_states, _ = _gdn_chunk(qs, ks, vs, gates, states_in, small, al, dt, gn, tinvs)
            return tuple(outs), tuple(new_states)

        prim = (tuple(_head(qkv_ref, h) for h in heads),
                tuple(_head(qkv_ref, N_LIN + h) for h in heads),
                tuple(_head(qkv_ref, 2 * N_LIN + h) for h in heads),
                tuple(_head(gate_ref, h) for h in heads),
                tuple(st_ref[0, 0, h] for h in heads),
                small_ref[...], al_ref[...], dt_ref[...], gn_ref[...])
        _, vjp = jax.vjp(chunk, *prim)
        cot = (tuple(_head(dmix_ref, h) for h in heads), tuple(ds_scr[h] for h in heads))
        dq, dk, dv, dgate, dstate, dsmall, dal, ddt, dgn = vjp(cot)
        dqkv_ref[...] = jnp.concatenate(list(dq) + list(dk) + list(dv), axis=1)
        dgate_ref[...] = jnp.concatenate(list(dgate), axis=1).astype(BF16)
        dsmall_ref[...] = dsmall.astype(BF16)
        for h in heads:
            ds_scr[h] = dstate[h]
        dal_ref[...] += dal
        ddt_ref[...] += ddt
        dgn_ref[...] += dgn

    row = lambda b, n: b * nc + (nc - 1 - n)
    vec = pl.BlockSpec((1, 128), lambda b, n: (0, 0))
    extra = [] if exchange is None else [exchange]
    return pl.pallas_call(
        body, name="gdn_bwd", grid=(n_batch, nc),
        in_specs=[pl.BlockSpec((CHUNK, MIX_W), lambda b, n: (row(b, n), 0)),
                  pl.BlockSpec((CHUNK, 3 * MIX_W), lambda b, n: (row(b, n), 0)),
                  pl.BlockSpec((CHUNK, 128), lambda b, n: (row(b, n), SMALL_COL)),
                  pl.BlockSpec((CHUNK, MIX_W), lambda b, n: (row(b, n), 3)),
                  pl.BlockSpec((1, 1, N_LIN, LIN_DH, LIN_DH), lambda b, n: (b, nc - 1 - n, 0, 0, 0)),
                  pl.BlockSpec((1, 1, N_LIN, CHUNK, CHUNK), lambda b, n: (b, nc - 1 - n, 0, 0, 0)),
                  vec, vec, vec] + [ANY] * len(extra),
        out_specs=[pl.BlockSpec((CHUNK, 3 * MIX_W), lambda b, n: (row(b, n), 0)),
                   pl.BlockSpec((CHUNK, MIX_W), lambda b, n: (row(b, n), 3)),
                   pl.BlockSpec((CHUNK, 128), lambda b, n: (row(b, n), 0)),
                   vec, vec, vec] + [ANY] * len(extra),
        out_shape=[jax.ShapeDtypeStruct((t, 3 * MIX_W), F32),
                   jax.ShapeDtypeStruct((t, IN_A_PAD), BF16),
                   jax.ShapeDtypeStruct((t, 128), BF16),
                   jax.ShapeDtypeStruct((1, 128), F32),
                   jax.ShapeDtypeStruct((1, 128), F32),
                   jax.ShapeDtypeStruct((1, 128), F32)]
                  + [jax.ShapeDtypeStruct((3,) + p.shape[1:], p.dtype) for p in extra],
        scratch_shapes=[pltpu.VMEM((N_LIN, LIN_DH, LIN_DH), F32)] + (EXCHANGE_SEMS if extra else []),
        compiler_params=_cparams(("arbitrary", "arbitrary")),
    )(dcat, qkv, proj, proj, states, tinvs, alog_row, dtb_row, gain_row, *extra)


SB_T = 256


def _sb_masks():
    r, c = _iota((SB_T, SB_T), 0), _iota((SB_T, SB_T), 1)
    return r, c


def _staggered(chains):
    pending, live = list(chains), []
    while pending or live:
        if pending:
            live.append(pending.pop(0))
        for g in list(live):
            try:
                next(g)
            except StopIteration:
                live.remove(g)


def _sb_rows(kb):
    start = kb * SB_T
    return pl.ds(start if isinstance(kb, int) else pl.multiple_of(start, SB_T), SB_T)


def _sb_fwd(proj, n_batch):
    nq = SEQ // SB_T
    t = n_batch * SEQ
    scale = SB_DH ** -0.5
    both = range(2)

    def body(q_ref, k_ref, v_ref, o_ref, tot_ref, acc_scr, run_scr):
        qi = pl.program_id(2)
        lane = _iota((SB_T, 128), 1)
        r, c = _sb_masks()
        upper = (r > c).astype(BF16)
        q = q_ref[...] * scale
        qm = [jnp.where((lane < SB_DH) == (hh == 0), q, jnp.zeros_like(q)) for hh in both]

        def blocks(kbs, diag=None):
            first = diag is not None
            k_blk = [k_ref[_sb_rows(kb), :] for kb in kbs]
            v_blk = [v_ref[_sb_rows(kb), :] for kb in kbs]
            run = [None if first else run_scr[hh][:, 0:1] for hh in both]
            pv = {hh: [] for hh in both}
            rowsums = {hh: [] for hh in both}

            def chain(n, hh):
                z = lax.dot_general(qm[hh], k_blk[n], NT, preferred_element_type=F32)
                yield
                lb = _log_sigmoid(z)
                l1m = lb - z
                if n == diag:
                    l1m = jnp.where(r > c, l1m, 0.0)
                parts = _split(l1m)
                terms = ([] if first else [run[hh]]) + rowsums[hh]
                before = sum(terms[1:], terms[0]) if terms else None
                rowsums[hh].append(jnp.sum(l1m, axis=1, keepdims=True))
                yield
                tail = _dot_mask(parts, upper)
                yield
                a = jnp.exp(lb + (tail if before is None else before + tail))
                if n == diag:
                    a = jnp.where(r > c, a, 0.0)
                a = a.astype(BF16)
                yield
                pv[hh].append(lax.dot_general(a, v_blk[n], NN, preferred_element_type=F32))

            _staggered([chain(n, hh) for n in range(len(kbs)) for hh in both])
            for hh in both:
                if first:
                    acc_scr[hh] = sum(pv[hh][1:], pv[hh][0])
                    run_scr[hh] = jnp.broadcast_to(sum(rowsums[hh][1:], rowsums[hh][0]), (SB_T, 128))
                else:
                    acc_scr[hh] += sum(pv[hh][1:], pv[hh][0])
                    run_scr[hh] += sum(rowsums[hh][1:], rowsums[hh][0])

        @pl.when(qi == 0)
        def _():
            blocks([0], diag=0)

        @pl.when((qi & 1) == 1)
        def _():
            blocks([qi, qi - 1], diag=0)

        @pl.when((qi >= 2) & ((qi & 1) == 0))
        def _():
            blocks([qi, qi - 1, qi - 2], diag=0)

        rest = jnp.where(qi == 0, 0, ((qi - 1) >> 1) << 1)

        def step(it, carry):
            kb = rest - 1 - 2 * it
            blocks([kb, kb - 1])
            return carry

        lax.fori_loop(0, rest >> 1, step, 0)
        first = lane < SB_DH
        o_ref[...] = jnp.where(first, acc_scr[0], acc_scr[1]).astype(BF16)
        tot_ref[...] = jnp.where(first, run_scr[0], run_scr[1])

    nq_blocks = lambda b, p, i: (b * nq + i, p)
    seq_spec = lambda which: pl.BlockSpec((SEQ, 128), lambda b, p, i: (b, 3 * p + which))
    return pl.pallas_call(
        body, name="sb_fwd", grid=(n_batch, SB_PAIRS, nq),
        in_specs=[pl.BlockSpec((SB_T, 128), lambda b, p, i: (b * nq + i, 3 * p)), seq_spec(1), seq_spec(2)],
        out_specs=[pl.BlockSpec((SB_T, 128), nq_blocks), pl.BlockSpec((SB_T, 128), nq_blocks)],
        out_shape=[jax.ShapeDtypeStruct((t, D_MODEL), BF16),
                   jax.ShapeDtypeStruct((t, MIX_W), F32)],
        scratch_shapes=[pltpu.VMEM((2, SB_T, 128), F32), pltpu.VMEM((2, SB_T, 128), F32)],
        compiler_params=_cparams(("parallel", "parallel", "arbitrary")),
    )(proj, proj, proj)


def _sb_bwd(dcat, proj, totals, n_batch):
    nq = SEQ // SB_T
    t = n_batch * SEQ
    scale = SB_DH ** -0.5
    both = range(2)

    def body(do_ref, q_ref, k_ref, v_ref, tot_ref, dp_ref, dq_scr, run_scr, grun_scr, dk_ref, dv_ref):
        qi = pl.program_id(2)

        @pl.when(qi == 0)
        def _():
            dk_ref[...] = jnp.zeros_like(dk_ref)
            dv_ref[...] = jnp.zeros_like(dv_ref)

        lane = _iota((SB_T, 128), 1)
        r, c = _sb_masks()
        incl = (r <= c).astype(BF16)
        earlier = (r < c).astype(BF16)
        dq_scr[...] = jnp.zeros_like(dq_scr)
        run_scr[...] = jnp.zeros_like(run_scr)
        grun_scr[...] = jnp.zeros_like(grun_scr)
        q, do, tot = q_ref[...] * scale, do_ref[...], tot_ref[...]
        sel = [(lane < SB_DH) == (hh == 0) for hh in both]
        qm = [jnp.where(sel[hh], q, jnp.zeros_like(q)) for hh in both]
        dom = [jnp.where(sel[hh], do, 0.0).astype(BF16) for hh in both]
        total = [jnp.sum(jnp.where(lane == hh * SB_DH, tot, 0.0), axis=1, keepdims=True) for hh in both]

        def blocks(kbs, diag=None):
            k_blk = [k_ref[_sb_rows(kb), :] for kb in kbs]
            v_blk = [v_ref[_sb_rows(kb), :] for kb in kbs]
            run = [run_scr[hh][:, 0:1] for hh in both]
            grun = [grun_scr[hh][:, 0:1] for hh in both]
            rs_l, rs_e, dqp = ({hh: [] for hh in both} for _ in range(3))
            dk, dv = ([[] for _ in kbs] for _ in range(2))

            def plus(base, terms):
                return base if not terms else base + sum(terms[1:], terms[0])

            def chain(n, hh):
                z = lax.dot_general(qm[hh], k_blk[n], NT, preferred_element_type=F32)
                da = lax.dot_general(dom[hh], v_blk[n], NT, preferred_element_type=F32)
                yield
                lb = _log_sigmoid(z)
                sig = jnp.exp(lb)
                l1m = lb - z
                if n == diag:
                    l1m = jnp.where(r > c, l1m, 0.0)
                parts = _split(l1m)
                run_before = plus(run[hh], rs_l[hh])
                rs_l[hh].append(jnp.sum(l1m, axis=1, keepdims=True))
                yield
                prefix = run_before + _dot_mask(parts, incl)
                yield
                a = jnp.exp(lb + (total[hh] - prefix))
                if n == diag:
                    a = jnp.where(r > c, a, 0.0)
                de = a * da
                a = a.astype(BF16)
                parts = _split(de)
                grun_before = plus(grun[hh], rs_e[hh])
                rs_e[hh].append(jnp.sum(de, axis=1, keepdims=True))
                yield
                dv[n].append(lax.dot_general(a, dom[hh], TN, preferred_element_type=F32))
                dl1m = grun_before + _dot_mask(parts, earlier)
                yield
                if n == diag:
                    dl1m = jnp.where(r > c, dl1m, 0.0)
                dz = (de * (1.0 - sig) - dl1m * sig).astype(BF16)
                yield
                dqp[hh].append(lax.dot_general(dz, k_blk[n], NN, preferred_element_type=F32))
                dk[n].append(lax.dot_general(dz, qm[hh], TN, preferred_element_type=F32))

            _staggered([chain(n, hh) for n in range(len(kbs)) for hh in both])
            for hh in both:
                dq_scr[hh] += sum(dqp[hh][1:], dqp[hh][0])
                run_scr[hh] += sum(rs_l[hh][1:], rs_l[hh][0])
                grun_scr[hh] += sum(rs_e[hh][1:], rs_e[hh][0])
            for n, kb in enumerate(kbs):
                dk_ref[_sb_rows(kb), :] += dk[n][0] + dk[n][1]
                dv_ref[_sb_rows(kb), :] += dv[n][0] + dv[n][1]

        rest = jnp.where(qi == 0, 0, ((qi - 1) >> 1) << 1)

        def step(it, carry):
            blocks([2 * it, 2 * it + 1])
            return carry

        lax.fori_loop(0, rest >> 1, step, 0)

        @pl.when(qi == 0)
        def _():
            blocks([0], diag=0)

        @pl.when((qi & 1) == 1)
        def _():
            blocks([qi - 1, qi], diag=1)

        @pl.when((qi >= 2) & ((qi & 1) == 0))
        def _():
            blocks([qi - 2, qi - 1, qi], diag=2)

        dq = (jnp.where(sel[0], dq_scr[0], dq_scr[1]) * scale).astype(BF16)
        dp_ref[pl.ds(pl.multiple_of(qi * SB_T, SB_T), SB_T), 0:128] = dq

        @pl.when(qi == nq - 1)
        def _():
            dp_ref[:, 128:256] = dk_ref[...].astype(BF16)
            dp_ref[:, 256:384] = dv_ref[...].astype(BF16)

    q_blocks = lambda b, p, i: (b * nq + i, p)
    seq_spec = lambda which: pl.BlockSpec((SEQ, 128), lambda b, p, i: (b, 3 * p + which))
    return pl.pallas_call(
        body, name="sb_bwd", grid=(n_batch, SB_PAIRS, nq),
        in_specs=[pl.BlockSpec((SB_T, 128), q_blocks),
                  pl.BlockSpec((SB_T, 128), lambda b, p, i: (b * nq + i, 3 * p)),
                  seq_spec(1), seq_spec(2), pl.BlockSpec((SB_T, 128), q_blocks)],
        out_specs=pl.BlockSpec((SEQ, 384), lambda b, p, i: (b, p)),
        out_shape=jax.ShapeDtypeStruct((t, IN_B), BF16),
        scratch_shapes=[pltpu.VMEM((2, SB_T, 128), F32), pltpu.VMEM((2, SB_T, 128), F32),
                        pltpu.VMEM((2, SB_T, 128), F32), pltpu.VMEM((SEQ, 128), F32), pltpu.VMEM((SEQ, 128), F32)],
        compiler_params=_cparams(("parallel", "arbitrary", "arbitrary")),
    )(dcat, proj, proj, proj, totals)


MEM_TQ = 512


def _mem_heads(lane):
    return [(lane >= X_HEAD_DIM * h) & (lane < X_HEAD_DIM * (h + 1)) for h in range(N_X_HEADS)]


def _mem_attn_fwd(proj, q_col, memkv, cat, n_batch):
    nq = SEQ // MEM_TQ
    scale = X_HEAD_DIM ** -0.5

    def body(q_ref, kv_ref, _, o_ref):
        q = q_ref[...]
        k = kv_ref[:, :X_WIDTH].astype(BF16)
        v = kv_ref[:, X_WIDTH:].astype(BF16)
        out = jnp.zeros((MEM_TQ, X_WIDTH), F32)
        for sel in _mem_heads(_iota((MEM_TQ, X_WIDTH), 1)):
            s = lax.dot_general(jnp.where(sel, q, 0.0).astype(BF16), k, NT, preferred_element_type=F32) * scale
            e = jnp.exp(s - jnp.max(s, axis=-1, keepdims=True))
            p = e / jnp.sum(e, axis=-1, keepdims=True)
            out = out + jnp.where(sel, lax.dot_general(p.astype(BF16), v, NN, preferred_element_type=F32), 0.0)
        o_ref[...] = out.astype(BF16)

    return pl.pallas_call(
        body, name="mem_attn_fwd", grid=(n_batch, nq),
        in_specs=[pl.BlockSpec((MEM_TQ, X_WIDTH), lambda b, i: (b * nq + i, q_col)),
                  pl.BlockSpec((N_MEM, 2 * X_WIDTH), lambda b, i: (b, 0)), ANY],
        out_specs=pl.BlockSpec((MEM_TQ, X_WIDTH), lambda b, i: (b * nq + i, MIX_W // X_WIDTH)),
        out_shape=jax.ShapeDtypeStruct(cat.shape, BF16),
        input_output_aliases={2: 0},
        compiler_params=_cparams(("parallel", "parallel")),
    )(proj, memkv, cat)


def _mem_attn_bwd(dcat, proj, q_col, memkv, dproj, n_batch, tail=None):
    nq = SEQ // MEM_TQ
    scale = X_HEAD_DIM ** -0.5
    width = X_WIDTH + (0 if tail is None else 128)
    assert (q_col * X_WIDTH) % width == 0

    def body(do_ref, q_ref, kv_ref, *rest):
        dq_ref, dkv_ref = rest[-2:]

        @pl.when(pl.program_id(1) == 0)
        def _():
            dkv_ref[...] = jnp.zeros_like(dkv_ref)

        q, do = q_ref[...], do_ref[...]
        k = kv_ref[:, :X_WIDTH].astype(BF16)
        v = kv_ref[:, X_WIDTH:].astype(BF16)
        dq = jnp.zeros((MEM_TQ, X_WIDTH), F32)
        dk = jnp.zeros((N_MEM, X_WIDTH), F32)
        dv = jnp.zeros((N_MEM, X_WIDTH), F32)
        for sel in _mem_heads(_iota((MEM_TQ, X_WIDTH), 1)):
            qm = jnp.where(sel, q, 0.0).astype(BF16)
            dom = jnp.where(sel, do, 0.0).astype(BF16)
            s = lax.dot_general(qm, k, NT, preferred_element_type=F32) * scale
            e = jnp.exp(s - jnp.max(s, axis=-1, keepdims=True))
            p = e / jnp.sum(e, axis=-1, keepdims=True)
            dp = lax.dot_general(dom, v, NT, preferred_element_type=F32)
            ds = ((p * (dp - jnp.sum(dp * p, axis=-1, keepdims=True))) * scale).astype(BF16)
            dv = dv + lax.dot_general(p.astype(BF16), dom, TN, preferred_element_type=F32)
            dk = dk + lax.dot_general(ds, qm, TN, preferred_element_type=F32)
            dq = dq + jnp.where(sel, lax.dot_general(ds, k, NN, preferred_element_type=F32), 0.0)
        if tail is None:
            dq_ref[...] = dq.astype(BF16)
        else:
            dq_ref[...] = jnp.concatenate([dq.astype(BF16), rest[0][...]], axis=1)
        dkv_ref[...] += jnp.concatenate([dk, dv], axis=1)

    rows = lambda b, i: b * nq + i
    extra = [] if tail is None else [tail]
    return pl.pallas_call(
        body, name="mem_attn_bwd", grid=(n_batch, nq),
        in_specs=[pl.BlockSpec((MEM_TQ, X_WIDTH), lambda b, i: (rows(b, i), MIX_W // X_WIDTH)),
                  pl.BlockSpec((MEM_TQ, X_WIDTH), lambda b, i: (rows(b, i), q_col)),
                  pl.BlockSpec((N_MEM, 2 * X_WIDTH), lambda b, i: (b, 0))]
                 + [pl.BlockSpec((MEM_TQ, 128), lambda b, i: (rows(b, i), 0))] * len(extra) + [ANY],
        out_specs=[pl.BlockSpec((MEM_TQ, width), lambda b, i: (rows(b, i), q_col * X_WIDTH // width)),
                   pl.BlockSpec((N_MEM, 2 * X_WIDTH), lambda b, i: (b, 0))],
        out_shape=[jax.ShapeDtypeStruct(dproj.shape, BF16),
                   jax.ShapeDtypeStruct((n_batch * N_MEM, 2 * X_WIDTH), F32)],
        input_output_aliases={3 + len(extra): 0},
        compiler_params=_cparams(("parallel", "arbitrary")),
    )(dcat, proj, memkv, *extra, dproj)


def _relu2_epilogue(acc):
    r = jnp.maximum(acc, 0.0)
    return (r * r,)


def _relu2_bwd_epilogue(acc, a):
    return (acc * (2.0 * jnp.sqrt(a.astype(F32))),)


def _pad_in_a(w_in_a):
    w = 3 * MIX_W
    parts = [w_in_a[:, :w], w_in_a[:, w:w + MIX_W], w_in_a[:, IN_A - X_WIDTH:],
             w_in_a[:, w + MIX_W:w + MIX_W + 2 * N_LIN]]
    pad = jnp.zeros((D_MODEL, IN_A_PAD - IN_A), w_in_a.dtype)
    return jnp.concatenate(parts + [pad], axis=1)


def _unpad_in_a(g):
    w = 3 * MIX_W
    return jnp.concatenate([g[:, :w + MIX_W], g[:, w + MIX_W + X_WIDTH:w + MIX_W + X_WIDTH + 2 * N_LIN],
                            g[:, w + MIX_W:w + MIX_W + X_WIDTH]], axis=1)


def _qkv_to_pairs(w):
    w3 = 3 * MIX_W
    qkv = w[:, :w3].reshape(-1, 3, SB_PAIRS, 128).transpose(0, 2, 1, 3).reshape(-1, w3)
    return jnp.concatenate([qkv, w[:, w3:]], axis=1)


def _pairs_to_qkv(w):
    w3 = 3 * MIX_W
    qkv = w[:, :w3].reshape(-1, SB_PAIRS, 3, 128).transpose(0, 2, 1, 3).reshape(-1, w3)
    return jnp.concatenate([qkv, w[:, w3:]], axis=1)


def _local_step(x, mem, target, wts, small, comm=None):
    wts = dict(wts)
    t = x.shape[0]
    nb = t // SEQ
    npre, npost, mpre, mpost = small["norm_pre_mix"], small["norm_post_mix"], small["norm_pre_mlp"], small["norm_post_mlp"]
    alog_row, dtb_row, gain_row = _gdn_param_rows(small["a_log_a"][0], small["dt_bias_a"][0], small["onorm_a"][0])
    conv_w = small["conv_w"]

    mem_n = _rms_fwd(mem, small["mem_norm"], name="mem_norm_fwd", tile=256)
    saved = []
    h = _rms_fwd(x, npre[0], name="pre_mix_norm0")
    big = min(1024, t)
    for i in range(DEPTH):
        s = {"x_in": x, "h1": h}
        if i == 0:
            proj = _matmul(h, wts["in_a", None], mode="nn", tm=big, tn=1152, tk=1024, name="proj_a")
            qkv = _conv_silu_fwd(proj, conv_w, nb)
            if comm is None:
                mix, states, tinvs = _gdn_fwd(qkv, proj, alog_row, dtb_row, gain_row, nb)
            else:
                mix, states, tinvs, second = _gdn_fwd(qkv, proj, alog_row, dtb_row, gain_row, nb, gather=comm["slabs"])
                second = second.reshape(N_CHIPS, REGION_ROWS[W_SECOND], D_MODEL)
                wts.update(_as_operands(_unpack_region_full(W_SECOND, second)))
            s.update(qkv=qkv, states=states, tinvs=tinvs)
            q_col = (3 * MIX_W + MIX_W) // X_WIDTH
        else:
            proj = _matmul(h, wts["in_b", None], mode="nn", tm=big, tn=1280, tk=1024, name="proj_b", out_dtypes=(BF16,))
            mix, totals = _sb_fwd(proj, nb)
            s.update(totals=totals)
            q_col = 3 * MIX_W // X_WIDTH
        memkv = _matmul(mem_n, wts["mem_kv", i], mode="nn", tm=256, tn=512, tk=1024, name=f"memkv{i}")
        cat = _mem_attn_fwd(proj, q_col, memkv, mix, nb)
        y = _matmul(cat, wts["out", i], mode="nn", tm=big, tn=1024, tk=1024, name=f"out_proj{i}")
        x2, h2 = _post_norm_add(x, y, npost[i], mpre[i], name=f"post_mix{i}")
        a = _matmul(h2, wts["up", i], mode="nn", tm=big, tn=2048, tk=1024, name=f"up{i}",
                    out_dtypes=(BF16,), epilogue=_relu2_epilogue, n_outer=True)
        y2 = _matmul(a, wts["down", i], mode="nn", tm=big, tn=1024, tk=D_FF, name=f"down{i}")
        s.update(proj=proj, q_col=q_col, memkv=memkv, cat=cat, y=y, x2=x2, h2=h2, a=a, y2=y2)
        saved.append(s)
        if i + 1 < DEPTH:
            x, h = _post_norm_add(x2, y2, mpost[i], npre[i + 1], name=f"post_mlp{i}")
        else:
            loss_row, dx = _post_norm_loss(x2, y2, mpost[i], target, name="loss_head")

    gw = {}
    gs = {k: [None] * DEPTH for k in ("norm_pre_mix", "norm_post_mix", "norm_pre_mlp", "norm_post_mlp")}
    dmem_n, early = None, None
    for i in reversed(range(DEPTH)):
        s = saved[i]
        if i == DEPTH - 1:
            dy2, gs["norm_post_mlp"][i] = _rms_bwd(dx, s["y2"], mpost[i], name=f"post_mlp_bwd{i}", out_dtype=BF16)
        du = _matmul(dy2, wts["down", i], mode="nt", tm=big, tn=2048, tk=1024, name=f"down_dx{i}",
                     out_dtypes=(BF16,), epilogue=_relu2_bwd_epilogue, extras=(s["a"],), n_outer=True)
        gw["down", i] = _matmul(s["a"], dy2, mode="tn", tm=1024, tn=1024, tk=big, name=f"down_dw{i}")
        dh2 = _matmul(du, wts["up", i], mode="nt", tm=big, tn=1024, tk=D_FF, name=f"up_dx{i}")
        gw["up", i] = _matmul(s["h2"], du, mode="tn", tm=1024, tn=2048, tk=512, name=f"up_dw{i}")
        dx2, dy, gs["norm_pre_mlp"][i], gs["norm_post_mix"][i] = _rms_bwd_pair(
            dh2, s["x2"], mpre[i], dx, s["y"], npost[i], name=f"mlp_norms_bwd{i}")
        dcat = _matmul(dy, wts["out", i], mode="nt", tm=big, tn=1024, tk=1024, name=f"out_dx{i}")
        gw["out", i] = _matmul(s["cat"], dy, mode="tn", tm=1024, tn=1024, tk=big, name=f"out_dw{i}")
        if i == 0:
            exchange = None
            if comm is not None:
                own, exchange = _reduce_in_chip(_pack_region_full(G_EARLY, gw), comm["core"])
            res = _gdn_bwd(dcat, s["qkv"], s["proj"], s["states"], s["tinvs"], alog_row, dtb_row, gain_row, nb,
                           exchange=exchange)
            dqkv, dproj, dsmall, dalog, ddtb, dgain = res[:6]
            if comm is not None:
                early = (own, res[6])
            dproj, dconv = _conv_silu_bwd(dqkv, s["proj"], conv_w, dproj, nb)
            dproj, dmemkv = _mem_attn_bwd(dcat, s["proj"], s["q_col"], s["memkv"], dproj, nb, tail=dsmall)
            w_in, tile = wts["in_a", None], 1152
        else:
            dproj = _sb_bwd(dcat, s["proj"], s["totals"], nb)
            dproj, dmemkv = _mem_attn_bwd(dcat, s["proj"], s["q_col"], s["memkv"], dproj, nb)
            w_in, tile = wts["in_b", None], 1280
        dmemkv = dmemkv.astype(BF16)
        gw["mem_kv", i] = _matmul(mem_n, dmemkv, mode="tn", tm=1024, tn=512, tk=256, name=f"memkv_dw{i}")
        dmn = _matmul(dmemkv, wts["mem_kv", i], mode="nt", tm=256, tn=1024, tk=512, name=f"memkv_dx{i}")
        dmem_n = dmn if dmem_n is None else dmem_n + dmn
        dh1 = _matmul(dproj, w_in, mode="nt", tm=big, tn=1024, tk=tile, name=f"proj_dx{i}")
        g_in = _matmul(s["h1"], dproj, mode="tn", tm=1024, tn=tile, tk=big, name=f"proj_dw{i}")
        if i == 0:
            gw["in_a", None] = _unpad_in_a(g_in)
        else:
            gw["in_b", None] = _pairs_to_qkv(g_in)
        if i > 0:
            dx, dy2, gs["norm_pre_mix"][i], gs["norm_post_mlp"][i - 1] = _rms_bwd_pair(
                dh1, s["x_in"], npre[i], dx2, saved[i - 1]["y2"], mpost[i - 1], name=f"mix_norms_bwd{i}")
        else:
            dx, gs["norm_pre_mix"][i] = _rms_bwd(dh1, s["x_in"], npre[i], name=f"pre_mix_bwd{i}", res=dx2)

    _, g_mem_norm = _rms_bwd(dmem_n, mem, small["mem_norm"], name="mem_norm_bwd", tile=256)
    gsmall = {k: jnp.concatenate(v, axis=0) for k, v in gs.items()}
    gsmall.update(mem_norm=g_mem_norm[0], a_log_a=dalog[:, N_LIN:2 * N_LIN], dt_bias_a=ddtb[:, N_LIN:2 * N_LIN],
                  onorm_a=dgain, conv_w=dconv)
    return loss_row[0, 0], dx, gw, gsmall, early


SUM_TILE = 640


def _position():
    x, y, c = lax.axis_index("x"), lax.axis_index("y"), lax.axis_index("c")
    others = [(1 - x, y), (x, 1 - y), (1 - x, 1 - y)]
    return x, y, c, others


class _Gather:
    def __init__(self, w_ref, out_ref, send_sems, recv_sems):
        self.w, self.out, self.send, self.recv = w_ref, out_ref, send_sems, recv_sems
        self.x, self.y, self.c, self.others = _position()
        self.me = 2 * self.x + self.y

    def _copy(self, k, src, dst, to):
        return pltpu.make_async_remote_copy(src_ref=src, dst_ref=dst, send_sem=self.send.at[k],
                                            recv_sem=self.recv.at[k], device_id=to, device_id_type=MESH)

    def _first(self):
        return [self._copy(j, self.w.at[self.me, self.c], self.out.at[self.me, self.c], (ox, oy, self.c))
                for j, (ox, oy) in enumerate(self.others)]

    def _passed(self):
        sibling = (self.x, self.y, 1 - self.c)
        return [self._copy(3 + j, self.out.at[2 * ox + oy, self.c], self.out.at[2 * ox + oy, self.c], sibling)
                for j, (ox, oy) in enumerate(self.others)]

    def start(self):
        for cp in self._first():
            cp.start()

    def forward(self):
        passed = self._passed()
        for j, (ox, oy) in enumerate(self.others):
            self._copy(j, self.w.at[self.me, self.c], self.out.at[2 * ox + oy, self.c], (self.x, self.y, self.c)).wait_recv()
            passed[j].start()

    def finish(self):
        for j, (ox, oy) in enumerate(self.others):
            self._copy(3 + j, self.w.at[self.me, self.c], self.out.at[2 * ox + oy, 1 - self.c],
                       (self.x, self.y, self.c)).wait_recv()
        for cp in self._first() + self._passed():
            cp.wait_send()


GATHER_SEMS = [pltpu.SemaphoreType.DMA((6,)), pltpu.SemaphoreType.DMA((6,))]


def _gather_chips(wflat):
    def body(w_ref, out_ref, send_sems, recv_sems):
        g = _Gather(w_ref, out_ref, send_sems, recv_sems)
        g.start()
        g.forward()
        g.finish()

    return pl.pallas_call(
        body, name="gather_weights",
        in_specs=[ANY], out_specs=ANY, input_output_aliases={0: 0},
        out_shape=jax.ShapeDtypeStruct(wflat.shape, wflat.dtype),
        scratch_shapes=GATHER_SEMS,
    )(wflat)


def _gather_all(v, *, name):
    rows, n = v.shape

    def body(x_ref, out_ref, send_sems, recv_sems, local_sem):
        x, y, c, others = _position()
        me, sibling = (x, y, c), (x, y, 1 - c)

        def blk(px, py, pc):
            return out_ref.at[pl.ds((4 * px + 2 * py + pc) * rows, rows), :]

        def copy(k, block, to, src=None):
            return pltpu.make_async_remote_copy(src_ref=blk(*block) if src is None else src, dst_ref=blk(*block),
                                                send_sem=send_sems.at[k], recv_sem=recv_sems.at[k],
                                                device_id=to, device_id_type=MESH)

        mine = pltpu.make_async_copy(x_ref, blk(*me), local_sem)
        mine.start()
        first = [copy(0, me, sibling, src=x_ref)]
        first += [copy(1 + j, me, (*chip, c), src=x_ref) for j, chip in enumerate(others)]
        for cp in first:
            cp.start()
        passed = [copy(4 + j, (*chip, c), sibling) for j, chip in enumerate(others)]
        for j, chip in enumerate(others):
            copy(1 + j, (*chip, c), me).wait_recv()
            passed[j].start()
        copy(0, sibling, me).wait_recv()
        for j, chip in enumerate(others):
            copy(4 + j, (*chip, 1 - c), me).wait_recv()
        for cp in first + passed:
            cp.wait_send()
        mine.wait()

    vmem = pl.BlockSpec(memory_space=pltpu.VMEM)
    return pl.pallas_call(
        body, name=name, in_specs=[vmem], out_specs=vmem,
        out_shape=jax.ShapeDtypeStruct((8 * rows, n), v.dtype),
        scratch_shapes=[pltpu.SemaphoreType.DMA((7,)), pltpu.SemaphoreType.DMA((7,)), pltpu.SemaphoreType.DMA],
    )(v)


def _swap_halves(g5):
    def body(g_ref, out_ref, send_sem, recv_sem):
        x, y, c, _ = _position()
        cp = pltpu.make_async_remote_copy(src_ref=g_ref.at[:, 1 - c], dst_ref=out_ref, send_sem=send_sem,
                                          recv_sem=recv_sem, device_id=(x, y, 1 - c), device_id_type=MESH)
        cp.start()
        cp.wait()

    return pl.pallas_call(
        body, name="grad_swap_halves", in_specs=[ANY], out_specs=ANY,
        out_shape=jax.ShapeDtypeStruct((N_CHIPS, g5.shape[2], D_MODEL), g5.dtype),
        scratch_shapes=[pltpu.SemaphoreType.DMA, pltpu.SemaphoreType.DMA],
    )(g5)


def _add_halves(core, g5, got):
    def body(c_ref, a_ref, b_ref, o_ref, ob_ref):
        s = a_ref[0] + b_ref[...]
        o_ref[...] = s
        ob_ref[...] = s.astype(BF16)

    half = g5.shape[2]
    nt = half // SUM_TILE
    spec = pl.BlockSpec((1, SUM_TILE, D_MODEL), lambda s, i, c_ref: (s, i, 0))
    return pl.pallas_call(
        body, name="grad_add_halves",
        grid_spec=pltpu.PrefetchScalarGridSpec(
            num_scalar_prefetch=1, grid=(N_CHIPS, nt),
            in_specs=[pl.BlockSpec((1, 1, SUM_TILE, D_MODEL), lambda s, i, c_ref: (s, c_ref[0], i, 0)), spec],
            out_specs=[spec, spec]),
        out_shape=[jax.ShapeDtypeStruct((N_CHIPS, half, D_MODEL), F32),
                   jax.ShapeDtypeStruct((N_CHIPS, half, D_MODEL), BF16)],
        compiler_params=_cparams(("parallel", "parallel")),
    )(core, g5, got)


def _exchange_copies(p_ref, q_ref, send_sems, recv_sems):
    x, y, c, others = _position()
    return [pltpu.make_async_remote_copy(src_ref=p_ref.at[2 * ox + oy], dst_ref=q_ref.at[j],
                                         send_sem=send_sems.at[j], recv_sem=recv_sems.at[j],
                                         device_id=(ox, oy, c), device_id_type=MESH)
            for j, (ox, oy) in enumerate(others)]


EXCHANGE_SEMS = [pltpu.SemaphoreType.DMA((3,)), pltpu.SemaphoreType.DMA((3,))]


def _exchange_chips(p):
    def body(p_ref, q_ref, send_sems, recv_sems):
        copies = _exchange_copies(p_ref, q_ref, send_sems, recv_sems)
        for cp in copies:
            cp.start()
        for cp in copies:
            cp.wait()

    return pl.pallas_call(
        body, name="grad_exchange_chips", in_specs=[ANY], out_specs=ANY,
        out_shape=jax.ShapeDtypeStruct((3,) + p.shape[1:], p.dtype),
        scratch_shapes=EXCHANGE_SEMS,
    )(p)


def _add_chips(chip_core, p, q):
    def body(kc_ref, p_ref, q_ref, o_ref):
        o_ref[0] = ((p_ref[0] + q_ref[0].astype(F32)) + q_ref[1].astype(F32)) + q_ref[2].astype(F32)

    half = p.shape[1]
    nt = half // SUM_TILE
    return pl.pallas_call(
        body, name="grad_add_chips",
        grid_spec=pltpu.PrefetchScalarGridSpec(
            num_scalar_prefetch=1, grid=(nt,),
            in_specs=[pl.BlockSpec((1, SUM_TILE, D_MODEL), lambda i, kc_ref: (kc_ref[0], i, 0)),
                      pl.BlockSpec((3, SUM_TILE, D_MODEL), lambda i, kc_ref: (0, i, 0))],
            out_specs=pl.BlockSpec((1, SUM_TILE, D_MODEL), lambda i, kc_ref: (kc_ref[1], i, 0))),
        out_shape=jax.ShapeDtypeStruct((2, half, D_MODEL), F32),
        compiler_params=_cparams(("parallel",)),
    )(chip_core, p, q)


def _share_halves(halves):
    def body(h_ref, out_ref, send_sem, recv_sem):
        x, y, c, _ = _position()
        cp = pltpu.make_async_remote_copy(src_ref=h_ref.at[c], dst_ref=out_ref.at[c], send_sem=send_sem,
                                          recv_sem=recv_sem, device_id=(x, y, 1 - c), device_id_type=MESH)
        cp.start()
        pltpu.make_async_remote_copy(src_ref=h_ref.at[c], dst_ref=out_ref.at[1 - c], send_sem=send_sem,
                                     recv_sem=recv_sem, device_id=(x, y, c), device_id_type=MESH).wait_recv()
        cp.wait_send()

    return pl.pallas_call(
        body, name="grad_share_halves", in_specs=[ANY], out_specs=ANY, input_output_aliases={0: 0},
        out_shape=jax.ShapeDtypeStruct(halves.shape, halves.dtype),
        scratch_shapes=[pltpu.SemaphoreType.DMA, pltpu.SemaphoreType.DMA],
    )(halves)


def _reduce_in_chip(g_packed, core):
    rows = g_packed.shape[1]
    g5 = g_packed.reshape(N_CHIPS, 2, rows // 2, D_MODEL)
    return _add_halves(core.reshape(1), g5, _swap_halves(g5))


def _reduce_across_chips(p, q, chip, core):
    halves = _share_halves(_add_chips(jnp.stack([chip, core]), p, q))
    return halves.reshape(2 * halves.shape[1], D_MODEL)


def _reduce_scatter(g_packed, chip, core):
    p, p_bf = _reduce_in_chip(g_packed, core)
    return _reduce_across_chips(p, _exchange_chips(p_bf), chip, core)


def _slot(n):
    return -(-n // 16) * 16


def _pad_rows(a, axis):
    n = a.shape[axis]
    widths = [(0, 0)] * a.ndim
    widths[axis] = (0, _slot(n) - n)
    return jnp.pad(a, widths) if _slot(n) != n else a


W_FIRST = (("in_a", None),)
W_SECOND = (("mem_kv", 0), ("out", 0), ("up", 0), ("down", 0),
            ("in_b", None), ("mem_kv", 1), ("out", 1), ("up", 1), ("down", 1))
G_LATE = (("in_a", None), ("mem_kv", 0), ("out", 0))
G_EARLY = (("up", 0), ("down", 0), ("in_b", None), ("mem_kv", 1), ("out", 1), ("up", 1), ("down", 1))
REGION_ROWS = {W_FIRST: 896, W_SECOND: 5504, G_LATE: 1280, G_EARLY: 5120}
FULL_SHAPE = {"in_a": (D_MODEL, IN_A), "in_b": (D_MODEL, IN_B), "mem_kv": (D_MODEL, 2 * X_WIDTH),
              "out": (D_MODEL, D_MODEL), "up": (D_MODEL, D_FF), "down": (D_FF, D_MODEL)}
COLUMN_SHARDED = ("in_a", "in_b", "up")


def _shard_shape(name):
    r, c = FULL_SHAPE[name]
    return (r, c // N_CHIPS) if name in COLUMN_SHARDED else (r // N_CHIPS, c)


def _part_rows(name):
    r, c = _shard_shape(name)
    return r * c // D_MODEL


def _pack_region(region, part, dtype):
    rows = [_pad_rows(part(name, layer).reshape(-1, D_MODEL).astype(dtype), 0) for name, layer in region]
    used = sum(r.shape[0] for r in rows)
    return jnp.concatenate(rows + [jnp.zeros((REGION_ROWS[region] - used, D_MODEL), dtype)], axis=0)


def _unpack_region(region, flat):
    out, off = {}, 0
    for name, layer in region:
        n = _part_rows(name)
        out[name, layer] = flat[off:off + n].reshape(_shard_shape(name))
        off += _slot(n)
    return out


def _unpack_region_full(region, g):
    out, off = {}, 0
    for name, layer in region:
        n = _part_rows(name)
        piece = g[:, off:off + n].reshape((N_CHIPS,) + _shard_shape(name))
        if name in COLUMN_SHARDED:
            piece = piece.transpose(1, 0, 2)
        out[name, layer] = piece.reshape(FULL_SHAPE[name])
        off += _slot(n)
    return out


def _pack_region_full(region, full):
    s = N_CHIPS
    parts = []
    for name, layer in region:
        g = full[name, layer]
        if name in COLUMN_SHARDED:
            g = g.reshape(g.shape[0], s, -1).transpose(1, 0, 2)
        parts.append(_pad_rows(g.reshape(s, -1, D_MODEL), 1))
    used = sum(p.shape[1] for p in parts)
    return jnp.concatenate(parts + [jnp.zeros((s, REGION_ROWS[region] - used, D_MODEL), F32)], axis=1)


def _as_operands(full):
    out = dict(full)
    if ("in_a", None) in out:
        out["in_a", None] = _pad_in_a(out["in_a", None])
    if ("in_b", None) in out:
        out["in_b", None] = _qkv_to_pairs(out["in_b", None])
    return out


def _place(packed, chip):
    rows = packed.shape[0]
    slabs = lax.dynamic_update_slice(jnp.zeros((N_CHIPS, rows, D_MODEL), packed.dtype), packed[None], (chip, 0, 0))
    return slabs.reshape(N_CHIPS, 2, rows // 2, D_MODEL)


def _adamw_math(w, g, m, v):
    m = ADAM_B1 * m + (1.0 - ADAM_B1) * g
    v = ADAM_B2 * v + (1.0 - ADAM_B2) * (g * g)
    m_hat = m / (1.0 - ADAM_B1 ** ADAM_STEP)
    v_hat = v / (1.0 - ADAM_B2 ** ADAM_STEP)
    delta = -ADAM_LR * (m_hat / (jnp.sqrt(v_hat) + ADAM_EPS) + ADAM_WD * w)
    return delta, m, v


ADAM_TILE = 256


def _adamw(w, g, m, v, *, name):
    shape = w.shape
    cols = shape[-1]
    rows = w.size // cols
    tile = min(rows, ADAM_TILE)
    assert rows % tile == 0, (name, shape)

    def body(w_ref, g_ref, m_ref, v_ref, d_ref, nm_ref, nv_ref):
        d_ref[...], nm_ref[...], nv_ref[...] = _adamw_math(w_ref[...], g_ref[...], m_ref[...], v_ref[...])

    spec = pl.BlockSpec((tile, cols), lambda i: (i, 0))
    outs = pl.pallas_call(
        body, name=name, grid=(rows // tile,), in_specs=[spec] * 4, out_specs=[spec] * 3,
        out_shape=[jax.ShapeDtypeStruct((rows, cols), F32)] * 3,
        compiler_params=_cparams(("parallel",)),
    )(*[a.reshape(rows, cols) for a in (w, g, m, v)])
    return [o.reshape(shape) for o in outs]


SMALL_NAMES = (("mem_norm", 8), ("norm_pre_mix", 16), ("norm_post_mix", 16), ("norm_pre_mlp", 16),
               ("norm_post_mlp", 16), ("a_log_a", 1), ("dt_bias_a", 1), ("onorm_a", 1))
SMALL_ROWS = 80
CONV_ROWS = CONV_K * 3 * MIX_W // 128
SMALL_GRAD_ROWS = SMALL_ROWS + CONV_ROWS


def _pack_small(vals):
    rows = []
    for name, n in SMALL_NAMES:
        flat = vals[name].reshape(-1)
        rows.append(jnp.pad(flat, (0, n * 128 - flat.size)).reshape(n, 128))
    used = sum(n for _, n in SMALL_NAMES)
    return jnp.concatenate(rows + [jnp.zeros((SMALL_ROWS - used, 128), F32)], axis=0)


def _unpack_small(packed, like):
    out, off = {}, 0
    for name, n in SMALL_NAMES:
        size = like[name].size
        out[name] = packed[off:off + n].reshape(-1)[:size].reshape(like[name].shape)
        off += n
    return out


def _small_update(gathered, w, m, v):
    def body(g_ref, w_ref, m_ref, v_ref, gs_ref, d_ref, nm_ref, nv_ref):
        g = g_ref[0]
        for dev in range(1, 8):
            g = g + g_ref[dev]
        gs_ref[...] = g
        d_ref[...], nm_ref[...], nv_ref[...] = _adamw_math(w_ref[...], g[:SMALL_ROWS], m_ref[...], v_ref[...])

    small = jax.ShapeDtypeStruct((SMALL_ROWS, 128), F32)
    return pl.pallas_call(
        body, name="small_update",
        out_shape=[jax.ShapeDtypeStruct((SMALL_GRAD_ROWS, 128), F32), small, small, small],
    )(gathered.reshape(8, SMALL_GRAD_ROWS, 128), w, m, v)


def kernel(x, mem, mem_norm, norm_pre_mix, norm_post_mix, norm_pre_mlp, norm_post_mlp, w_in_a, conv_w_a, a_log_a, dt_bias_a, onorm_a, w_in_b, w_mem_kv, w_out, w_up, w_down, loss_target, m_mem_norm, m_norm_pre_mix, m_norm_post_mix, m_norm_pre_mlp, m_norm_post_mlp, m_w_in_a, m_conv_w_a, m_a_log_a, m_dt_bias_a, m_onorm_a, m_w_in_b, m_w_mem_kv, m_w_out, m_w_up, m_w_down, v_mem_norm, v_norm_pre_mix, v_norm_post_mix, v_norm_pre_mlp, v_norm_post_mlp, v_w_in_a, v_conv_w_a, v_a_log_a, v_dt_bias_a, v_onorm_a, v_w_in_b, v_w_mem_kv, v_w_out, v_w_up, v_w_down):
    nb = x.shape[0]
    chip = (2 * lax.axis_index("x") + lax.axis_index("y")).astype(jnp.int32)
    core = lax.axis_index("c").astype(jnp.int32)
    shards = {"in_a": w_in_a, "in_b": w_in_b, "mem_kv": w_mem_kv, "out": w_out, "up": w_up, "down": w_down}
    moments_m = {"in_a": m_w_in_a, "in_b": m_w_in_b, "mem_kv": m_w_mem_kv, "out": m_w_out, "up": m_w_up, "down": m_w_down}
    moments_v = {"in_a": v_w_in_a, "in_b": v_w_in_b, "mem_kv": v_w_mem_kv, "out": v_w_out, "up": v_w_up, "down": v_w_down}
    small_w = {"mem_norm": mem_norm, "norm_pre_mix": norm_pre_mix, "norm_post_mix": norm_post_mix,
               "norm_pre_mlp": norm_pre_mlp, "norm_post_mlp": norm_post_mlp, "a_log_a": a_log_a,
               "dt_bias_a": dt_bias_a, "onorm_a": onorm_a}
    small_m = {"mem_norm": m_mem_norm, "norm_pre_mix": m_norm_pre_mix, "norm_post_mix": m_norm_post_mix,
               "norm_pre_mlp": m_norm_pre_mlp, "norm_post_mlp": m_norm_post_mlp, "a_log_a": m_a_log_a,
               "dt_bias_a": m_dt_bias_a, "onorm_a": m_onorm_a}
    small_v = {"mem_norm": v_mem_norm, "norm_pre_mix": v_norm_pre_mix, "norm_post_mix": v_norm_post_mix,
               "norm_pre_mlp": v_norm_pre_mlp, "norm_post_mlp": v_norm_post_mlp, "a_log_a": v_a_log_a,
               "dt_bias_a": v_dt_bias_a, "onorm_a": v_onorm_a}

    def shard_part(name, layer):
        return shards[name][0 if layer is None else layer]

    first = _gather_chips(_place(_pack_region(W_FIRST, shard_part, BF16), chip))
    wts = _as_operands(_unpack_region_full(W_FIRST, first.reshape(N_CHIPS, REGION_ROWS[W_FIRST], D_MODEL)))
    comm = {"slabs": _place(_pack_region(W_SECOND, shard_part, BF16), chip), "core": core}
    conv_rows = CONV_ROWS // N_CHIPS
    conv_blk = jnp.pad(conv_w_a.reshape(conv_rows, 128), ((0, 24 - conv_rows), (0, 0)))
    conv_all = _gather_all(conv_blk, name="gather_conv").reshape(N_CHIPS, 2, 24, 128)[:, 0, :conv_rows]
    conv_full = conv_all.reshape(N_CHIPS, CONV_K, 3 * MIX_W // N_CHIPS).transpose(1, 0, 2).reshape(CONV_K, 3 * MIX_W)

    loss_local, dx, gw, gsmall, (own_early, others_early) = _local_step(
        x.reshape(nb * SEQ, D_MODEL), mem.reshape(nb * N_MEM, D_MODEL), loss_target.reshape(nb * SEQ, D_MODEL),
        wts, dict(small_w, conv_w=conv_full), comm)
    loss = lax.psum(loss_local, ("x", "y", "c"))
    grad_x = dx.reshape(nb, SEQ, D_MODEL)

    g_part = _unpack_region(G_EARLY, _reduce_across_chips(own_early, others_early, chip, core))
    g_part.update(_unpack_region(G_LATE, _reduce_scatter(_pack_region_full(G_LATE, gw), chip, core)))
    g_shard = {k: (g_part[k, None][None] if (k, None) in g_part else jnp.stack([g_part[k, i] for i in range(DEPTH)]))
               for k in shards}
    upd = {k: _adamw(shards[k], g_shard[k], moments_m[k], moments_v[k], name=f"adamw_{k}") for k in shards}

    g_rows = jnp.concatenate([_pack_small(gsmall), gsmall["conv_w"].reshape(CONV_ROWS, 128)], axis=0)
    g_all = _gather_all(g_rows, name="gather_small_grads")
    g_sum, d_small, nm_small, nv_small = _small_update(g_all, _pack_small(small_w), _pack_small(small_m), _pack_small(small_v))
    gs = _unpack_small(g_sum, small_w)
    ds, nms, nvs = (_unpack_small(p, small_w) for p in (d_small, nm_small, nv_small))
    cw = 3 * MIX_W // N_CHIPS
    g_conv = lax.dynamic_slice(g_sum[SMALL_ROWS:].reshape(CONV_K, 3 * MIX_W), (0, chip * cw), (CONV_K, cw)).reshape(conv_w_a.shape)
    d_conv, nm_conv, nv_conv = _adamw(conv_w_a, g_conv, m_conv_w_a, v_conv_w_a, name="adamw_conv")

    order = ("mem_norm", "norm_pre_mix", "norm_post_mix", "norm_pre_mlp", "norm_post_mlp", "in_a", "conv", "a_log_a",
             "dt_bias_a", "onorm_a", "in_b", "mem_kv", "out", "up", "down")
    grads = dict(gs, conv=g_conv, **g_shard)
    deltas = dict(ds, conv=d_conv, **{k: u[0] for k, u in upd.items()})
    new_m = dict(nms, conv=nm_conv, **{k: u[1] for k, u in upd.items()})
    new_v = dict(nvs, conv=nv_conv, **{k: u[2] for k, u in upd.items()})
    return (loss, grad_x, *[grads[k] for k in order], *[deltas[k] for k in order],
            *[new_m[k] for k in order], *[new_v[k] for k in order])
```
